```python
import math
import jax
import jax.numpy as jnp
from jax import lax
import numpy as np

D_MODEL = 1024
BATCH = 4
SEQ = 4096
DEPTH = 4

F32 = jnp.float32
QB = 128
NEG = -1e30
BIG = 1e9
LN_EPS = 1e-5
RMS_EPS = 1e-6

ALPHA = (2 * DEPTH) ** 0.25
BETA = (8 * DEPTH) ** -0.25

N_BUCKETS = 32
REL_MAX_DIST = 2048

H_A = 12
NOPE = 64
ROPE_DIM = 32
MLA_V = 64
Q_LORA = 256
KV_LORA = 128
ROPE_THETA = 10000.0

DIL_PAIRS = ((128, 1), (512, 4), (2048, 16))
H_B_GROUP = 4
H_B = H_B_GROUP * 3
HD_B = 64

H_C = 8
G_C = 2
R_C = 4
DK_C = 64
DV_C = 64
CMP_LEN = 32
CMP_STRIDE = 16
CMP_HID = 64
SEL_BLOCK = 64
SEL_TOP = 16
SEL_QB = 64
WIN = 512

H_D = 4
DD = 64

REL_B0 = 0
REL_C0 = H_B
REL_D0 = H_B + H_C
N_REL_HEADS = H_B + H_C + H_D

N_EXPERTS = 64
TOP_K = 8
N_EXPERT_GROUPS = 8
TOPK_GROUPS = 4
D_EXPERT = 256
D_SHARED = 256
ROUTED_SCALE = 2.5
EBLK = 128

AB_SIZES = (Q_LORA, KV_LORA, ROPE_DIM, H_B * HD_B, H_B * HD_B, H_B * HD_B)
AB_IN = sum(AB_SIZES)
AB_OUT = H_A * MLA_V + H_B_GROUP * HD_B
CD_SIZES = (H_C * DK_C,) + (G_C * DK_C, G_C * DV_C) * 3 + (3 * H_C, H_D * 2 * DD, H_D * 2 * DD, H_D * 2 * DD)
CD_IN = sum(CD_SIZES)
CD_OUT = H_C * DV_C + H_D * 2 * DD

kernel_name = 'hybrid_mla_dilated_nsa_diff_moe_deepnorm'


def split_cols(h, sizes):
    out, start = [], 0
    for n in sizes:
        out.append(h[..., start:start + n])
        start += n
    return out


def layer_norm(x, g, b):
    xf = x.astype(F32)
    mu = xf.mean(-1, keepdims=True)
    var = jnp.square(xf - mu).mean(-1, keepdims=True)
    return ((xf - mu) * lax.rsqrt(var + LN_EPS) * g + b).astype(x.dtype)


def rms_norm(x, g):
    xf = x.astype(F32)
    return (xf * lax.rsqrt(jnp.square(xf).mean(-1, keepdims=True) + RMS_EPS) * g).astype(x.dtype)


def rope(t, pos):
    half = t.shape[-1] // 2
    freqs = ROPE_THETA ** (-jnp.arange(half, dtype=F32) / half)
    ang = pos.astype(F32)[:, None] * freqs
    cos = jnp.cos(ang)[:, None, :]
    sin = jnp.sin(ang)[:, None, :]
    t1 = t[..., :half].astype(F32)
    t2 = t[..., half:].astype(F32)
    return jnp.concatenate([t1 * cos - t2 * sin, t1 * sin + t2 * cos], -1).astype(t.dtype)


def rel_bucket(dist):
    n = jnp.maximum(dist, 0)
    exact = N_BUCKETS // 2
    log_ratio = jnp.log(jnp.maximum(n, 1).astype(F32) / exact) / math.log(REL_MAX_DIST / exact)
    large = exact + (log_ratio * (N_BUCKETS - exact)).astype(jnp.int32)
    return jnp.where(n < exact, n, jnp.minimum(large, N_BUCKETS - 1))


def rel_bias_lookup(tab, dist):
    return tab[rel_bucket(dist)].astype(F32)


def banded_attention(q, k, v, max_dist, bias_tab, dist_scale):
    N, G, R, L, dk = q.shape
    dv = v.shape[-1]
    nb = L // QB
    nback = -(-max_dist // QB)
    kw = (nback + 1) * QB

    def windows(t):
        tp = jnp.pad(t, ((0, 0), (0, 0), (nback * QB, 0), (0, 0))).reshape(N, G, nb + nback, QB, t.shape[-1])
        return jnp.concatenate([tp[:, :, j:j + nb] for j in range(nback + 1)], axis=3)

    kwin, vwin = windows(k), windows(v)
    a = jnp.arange(QB)
    c = jnp.arange(kw)
    dist = a[:, None] + nback * QB - c[None, :]
    kpos = (jnp.arange(nb)[:, None] - nback) * QB + c[None, :]
    mask = ((dist >= 0) & (dist <= max_dist))[None] & (kpos >= 0)[:, None, :]
    bias = rel_bias_lookup(bias_tab, dist * dist_scale).transpose(2, 0, 1).reshape(G, R, 1, QB, kw)
    qb = q.reshape(N, G, R, nb, QB, dk)
    s = jnp.einsum('ngrbqd,ngbkd->ngrbqk', qb, kwin).astype(F32) * dk ** -0.5 + bias
    s = jnp.where(mask, s, NEG)
    m = s.max(-1, keepdims=True)
    p = jnp.exp(s - m)
    l = p.sum(-1)
    o = jnp.einsum('ngrbqk,ngbkd->ngrbqd', p.astype(v.dtype), vwin).astype(F32) / l[..., None]
    lse = m[..., 0] + jnp.log(l)
    return o.reshape(N, G, R, L, dv).astype(v.dtype), lse.reshape(N, G, R, L)


def to_strided(t, dil, lp):
    B, H, S, d = t.shape
    L = S // dil
    t = t.reshape(B, H, L, dil, d).transpose(0, 3, 1, 2, 4).reshape(B * dil, H, L, d)
    return jnp.pad(t, ((0, 0), (0, 0), (0, lp - L), (0, 0)))


def from_strided(t, B, S, dil):
    L = S // dil
    t = t[:, :, :L]
    rest = t.shape[3:]
    t = jnp.moveaxis(t.reshape((B, dil) + t.shape[1:]), 1, 3)
    return t.reshape((B, t.shape[1], S) + rest)


def dilated_attention(q, k, v, tab):
    B, _, H, S, hd = q.shape
    outs, lses = [], []
    for g, (window, dil) in enumerate(DIL_PAIRS):
        L = S // dil
        lp = -(-L // QB) * QB
        o, lse = banded_attention(to_strided(q[:, g], dil, lp)[:, :, None], to_strided(k[:, g], dil, lp),
                                  to_strided(v[:, g], dil, lp), window // dil,
                                  tab[:, g * H:(g + 1) * H], dil)
        outs.append(from_strided(o[:, :, 0], B, S, dil).astype(F32))
        lses.append(from_strided(lse[:, :, 0], B, S, dil))
    w = jax.nn.softmax(jnp.stack(lses), axis=0)
    return jnp.sum(w[..., None] * jnp.stack(outs), axis=0).astype(v.dtype)


def mla_attention(q, k, v):
    B, H, S, dqk = q.shape
    nb = S // QB
    qb = q.reshape(B, H, nb, QB, dqk).transpose(2, 0, 1, 3, 4)
    kpos = jnp.arange(S)

    def body(args):
        qq, i = args
        causal = kpos[None, :] <= (i * QB + jnp.arange(QB))[:, None]
        s = jnp.einsum('bhqd,bhkd->bhqk', qq, k).astype(F32) * dqk ** -0.5
        p = jax.nn.softmax(jnp.where(causal, s, NEG), axis=-1)
        return jnp.einsum('bhqk,bhkd->bhqd', p.astype(v.dtype), v)

    o = lax.map(body, (qb, jnp.arange(nb)))
    return o.transpose(1, 2, 0, 3, 4).reshape(B, H, S, -1)


def mixer_ab(x, w_in, q_norm, w_uq, kv_norm, w_ukv, w_out, tab_b):
    B, S, _ = x.shape
    pos = jnp.arange(S)
    c_q, c_kv, k_r, q_b, k_b, v_b = split_cols(x @ w_in, AB_SIZES)
    q = (rms_norm(c_q, q_norm) @ w_uq).reshape(B, S, H_A, NOPE + ROPE_DIM)
    kv = (rms_norm(c_kv, kv_norm) @ w_ukv).reshape(B, S, H_A, NOPE + MLA_V)
    k_r = jnp.broadcast_to(rope(k_r.reshape(B, S, 1, ROPE_DIM), pos), (B, S, H_A, ROPE_DIM))
    q_a = jnp.concatenate([q[..., :NOPE], rope(q[..., NOPE:], pos)], -1).transpose(0, 2, 1, 3)
    k_a = jnp.concatenate([kv[..., :NOPE], k_r], -1).transpose(0, 2, 1, 3)
    v_a = kv[..., NOPE:].transpose(0, 2, 1, 3)
    o_a = mla_attention(q_a, k_a, v_a).transpose(0, 2, 1, 3).reshape(B, S, H_A * MLA_V)

    def heads_b(t):
        return t.reshape(B, S, len(DIL_PAIRS), H_B_GROUP, HD_B).transpose(0, 2, 3, 1, 4)

    o_b = dilated_attention(heads_b(q_b), heads_b(k_b), heads_b(v_b), tab_b)
    o_b = o_b.transpose(0, 2, 1, 3).reshape(B, S, H_B_GROUP * HD_B)
    return jnp.concatenate([o_a, o_b], -1) @ w_out


def compress_blocks(t, pos_emb, w1, w2):
    B, G, S, d = t.shape
    nc = (S - CMP_LEN) // CMP_STRIDE + 1
    idx = jnp.arange(nc)[:, None] * CMP_STRIDE + jnp.arange(CMP_LEN)[None, :]
    blk = (t[:, :, idx] + pos_emb).reshape(B, G, nc, CMP_LEN * d)
    return jax.nn.gelu(blk @ w1) @ w2


def nsa_attention(q, k_cmp, v_cmp, k_slc, v_slc, k_win, v_win, gates, pos_k, k_w1, k_w2, pos_v, v_w1, v_w2, tab):
    B, G, R, S, dk = q.shape
    scale = dk ** -0.5
    pos = jnp.arange(S)
    kc = compress_blocks(k_cmp, pos_k, k_w1, k_w2)
    vc = compress_blocks(v_cmp, pos_v, v_w1, v_w2)
    nc = kc.shape[2]
    blk_end = jnp.arange(nc) * CMP_STRIDE + CMP_LEN - 1
    dist_c = pos[:, None] - blk_end[None, :]
    valid_c = dist_c >= 0
    bias_c = rel_bias_lookup(tab, dist_c).transpose(2, 0, 1).reshape(G, R, S, nc)
    s_c = jnp.einsum('bgrsd,bgcd->bgrsc', q, kc).astype(F32) * scale + bias_c
    s_c = jnp.where(valid_c, s_c, NEG)
    e_c = jnp.exp(s_c - s_c.max(-1, keepdims=True)) * valid_c
    p_c = e_c / jnp.maximum(e_c.sum(-1, keepdims=True), 1e-30)
    o_cmp = jnp.einsum('bgrsc,bgcd->bgrsd', p_c.astype(vc.dtype), vc)
    n_sel = S // SEL_BLOCK
    c0 = jnp.arange(nc) * CMP_STRIDE
    s0 = jnp.arange(n_sel) * SEL_BLOCK
    overlap = jnp.maximum(jnp.minimum(c0[:, None] + CMP_LEN, s0[None, :] + SEL_BLOCK)
                          - jnp.maximum(c0[:, None], s0[None, :]), 0).astype(F32) / CMP_LEN
    imp = jnp.einsum('bgrsc,cj->bgsj', p_c, overlap)
    forced = (s0[None, :] == (pos // SEL_BLOCK * SEL_BLOCK)[:, None]) | (s0[None, :] == 0)
    imp = jnp.where(forced, BIG, jnp.where(s0[None, :] <= pos[:, None], imp, -BIG))
    n_top = min(SEL_TOP, n_sel)
    _, sel_idx = lax.top_k(imp, n_top)
    kblk = k_slc.reshape(B, G, n_sel, SEL_BLOCK, dk)
    vblk = v_slc.reshape(B, G, n_sel, SEL_BLOCK, -1)
    nq = S // SEL_QB
    tk = n_top * SEL_BLOCK
    b_ix = jnp.arange(B)[:, None, None]
    g_ix = jnp.arange(G)[None, :, None]
    tab_g = tab.reshape(N_BUCKETS, G, R).transpose(1, 0, 2)

    def sel_block(args):
        qq, ii, bi = args
        flat = ii.reshape(B, G, SEL_QB * n_top)
        kk = kblk[b_ix, g_ix, flat].reshape(B, G, SEL_QB, tk, dk)
        vv = vblk[b_ix, g_ix, flat].reshape(B, G, SEL_QB, tk, -1)
        kpos = (ii[..., None] * SEL_BLOCK + jnp.arange(SEL_BLOCK)).reshape(B, G, SEL_QB, tk)
        dist = (bi * SEL_QB + jnp.arange(SEL_QB))[:, None] - kpos
        bias = tab_g[g_ix[..., None], rel_bucket(dist)].astype(F32).transpose(0, 1, 4, 2, 3)
        s = jnp.einsum('bgrqd,bgqkd->bgrqk', qq, kk).astype(F32) * scale + bias
        p = jax.nn.softmax(jnp.where((dist >= 0)[:, :, None], s, NEG), axis=-1)
        return jnp.einsum('bgrqk,bgqkd->bgrqd', p.astype(vv.dtype), vv)

    q_b = q.reshape(B, G, R, nq, SEL_QB, dk).transpose(3, 0, 1, 2, 4, 5)
    i_b = sel_idx.reshape(B, G, nq, SEL_QB, n_top).transpose(2, 0, 1, 3, 4)
    o_sel = lax.map(sel_block, (q_b, i_b, jnp.arange(nq)))
    o_sel = o_sel.transpose(1, 2, 3, 0, 4, 5).reshape(B, G, R, S, -1)
    o_win, _ = banded_attention(q, k_win, v_win, WIN - 1, tab, 1)
    return gates[0] * o_cmp + gates[1] * o_sel + gates[2] * o_win


def diff_attention(q, k, v, lam, tab):
    B, H, _, S, dd = q.shape
    nb = S // QB
    kpos = jnp.arange(S)
    q_b = q.reshape(B, H, 2, nb, QB, dd).transpose(3, 0, 1, 2, 4, 5)

    def body(args):
        qq, i = args
        dist = (i * QB + jnp.arange(QB))[:, None] - kpos[None, :]
        bias = rel_bias_lookup(tab, dist).transpose(2, 0, 1)[:, None]
        s = jnp.einsum('bhmqd,bhmkd->bhmqk', qq, k).astype(F32) * dd ** -0.5 + bias
        p = jax.nn.softmax(jnp.where(dist >= 0, s, NEG), axis=-1)
        a = p[:, :, 0] - lam * p[:, :, 1]
        return jnp.einsum('bhqk,bhkd->bhqd', a.astype(v.dtype), v)

    o = lax.map(body, (q_b, jnp.arange(nb)))
    return o.transpose(1, 2, 0, 3, 4).reshape(B, H, S, -1)


def mixer_cd(x, w_in, pos_k, k_w1, k_w2, pos_v, v_w1, v_w2, lq1, lk1, lq2, lk2, d_norm, w_out,
             tab_c, tab_d, lam_init):
    B, S, _ = x.shape
    (q_c, k_cmp, v_cmp, k_slc, v_slc, k_win, v_win, g_c, q_d, k_d, v_d) = split_cols(x @ w_in, CD_SIZES)

    def heads_q(t):
        return t.reshape(B, S, G_C, R_C, DK_C).transpose(0, 2, 3, 1, 4)

    def heads_kv(t):
        return t.reshape(B, S, G_C, -1).transpose(0, 2, 1, 3)

    gates = jax.nn.sigmoid(g_c).reshape(B, S, 3, G_C, R_C).transpose(2, 0, 3, 4, 1)[..., None]
    o_c = nsa_attention(heads_q(q_c), heads_kv(k_cmp), heads_kv(v_cmp), heads_kv(k_slc), heads_kv(v_slc),
                        heads_kv(k_win), heads_kv(v_win), gates, pos_k, k_w1, k_w2, pos_v, v_w1, v_w2, tab_c)
    o_c = o_c.transpose(0, 3, 1, 2, 4).reshape(B, S, H_C * DV_C)
    lam = (jnp.exp(jnp.sum(lq1.astype(F32) * lk1.astype(F32)))
           - jnp.exp(jnp.sum(lq2.astype(F32) * lk2.astype(F32))) + lam_init)
    qd = q_d.reshape(B, S, H_D, 2, DD).transpose(0, 2, 3, 1, 4)
    kd = k_d.reshape(B, S, H_D, 2, DD).transpose(0, 2, 3, 1, 4)
    vd = v_d.reshape(B, S, H_D, 2 * DD).transpose(0, 2, 1, 3)
    o_d = diff_attention(qd, kd, vd, lam, tab_d)
    o_d = (rms_norm(o_d, d_norm) * (1.0 - lam_init)).transpose(0, 2, 1, 3).reshape(B, S, H_D * 2 * DD)
    return jnp.concatenate([o_c, o_d], -1) @ w_out


def swiglu(x, wg, wu, wd):
    return (jax.nn.silu(x @ wg) * (x @ wu)) @ wd


def routed_experts(xf, e_idx, gate, w_gate, w_up, w_down):
    N, D = xf.shape
    E = w_gate.shape[0]
    A = N * TOP_K
    flat_e = e_idx.reshape(A)
    order = jnp.argsort(flat_e)
    e_sorted = flat_e[order]
    tok_sorted = (order // TOP_K).astype(jnp.int32)
    w_sorted = gate.reshape(A)[order]
    counts = jnp.bincount(flat_e, length=E)
    padded = (counts + EBLK - 1) // EBLK * EBLK
    pad_end = jnp.cumsum(padded)
    pad_start = pad_end - padded
    start = jnp.cumsum(counts) - counts
    dest = pad_start[e_sorted] + jnp.arange(A) - start[e_sorted]
    n_blocks = (A + E * (EBLK - 1) + EBLK - 1) // EBLK
    P = n_blocks * EBLK
    rows = jnp.full((P,), N, jnp.int32).at[dest].set(tok_sorted)
    row_w = jnp.zeros((P,), F32).at[dest].set(w_sorted)
    blk_e = jnp.minimum(jnp.searchsorted(pad_end, jnp.arange(n_blocks) * EBLK, side='right'), E - 1)
    xpad = jnp.concatenate([xf, jnp.zeros((1, D), xf.dtype)], 0)

    def body(args):
        r, w, e = args
        xb = xpad[r]
        hb = jax.nn.silu(xb @ w_gate[e]) * (xb @ w_up[e])
        return (hb @ w_down[e]) * w[:, None].astype(xb.dtype)

    yb = lax.map(body, (rows.reshape(n_blocks, EBLK), row_w.reshape(n_blocks, EBLK), blk_e))
    return jax.ops.segment_sum(yb.reshape(P, D), rows, num_segments=N + 1)[:N]


def moe(x, router_w, router_b, w_gate, w_up, w_down, sh_gate, sh_up, sh_down):
    B, S, D = x.shape
    xf = x.reshape(B * S, D)
    scores = jax.nn.sigmoid(xf.astype(F32) @ router_w.astype(F32))
    sel = scores + router_b.astype(F32)
    per_grp = N_EXPERTS // N_EXPERT_GROUPS
    grp_score = lax.top_k(sel.reshape(-1, N_EXPERT_GROUPS, per_grp), 2)[0].sum(-1)
    _, grp_idx = lax.top_k(grp_score, TOPK_GROUPS)
    grp_mask = jax.nn.one_hot(grp_idx, N_EXPERT_GROUPS, dtype=F32).sum(-2) > 0
    sel = jnp.where(jnp.repeat(grp_mask, per_grp, axis=-1), sel, NEG)
    _, e_idx = lax.top_k(sel, TOP_K)
    gate = jnp.take_along_axis(scores, e_idx, axis=-1)
    gate = gate / gate.sum(-1, keepdims=True) * ROUTED_SCALE
    y = routed_experts(xf, e_idx, gate, w_gate, w_up, w_down) + swiglu(xf, sh_gate, sh_up, sh_down)
    return y.reshape(B, S, D)


def setup_inputs(seed: int = 0) -> dict:
    key = jax.random.key(seed)
    ks = iter(jax.random.split(key, 48))
    NE = (DEPTH + 1) // 2
    NO = DEPTH // 2

    def nrm(shape, scale):
        return jax.random.normal(next(ks), shape, F32) * scale

    def gain(shape):
        return 1.0 + nrm(shape, 0.02)

    return {
        'x': nrm((BATCH, SEQ, D_MODEL), 1.0),
        'rel_bias': nrm((N_BUCKETS, N_REL_HEADS), 0.1),
        'ab_w_in': nrm((NE, D_MODEL, AB_IN), D_MODEL ** -0.5),
        'mla_q_norm': gain((NE, Q_LORA)),
        'mla_w_uq': nrm((NE, Q_LORA, H_A * (NOPE + ROPE_DIM)), Q_LORA ** -0.5),
        'mla_kv_norm': gain((NE, KV_LORA)),
        'mla_w_ukv': nrm((NE, KV_LORA, H_A * (NOPE + MLA_V)), KV_LORA ** -0.5),
        'ab_w_out': nrm((NE, AB_OUT, D_MODEL), AB_OUT ** -0.5 * BETA),
        'cd_w_in': nrm((NO, D_MODEL, CD_IN), D_MODEL ** -0.5),
        'nsa_cmp_pos_k': nrm((NO, CMP_LEN, DK_C), 0.1),
        'nsa_cmp_k_w1': nrm((NO, CMP_LEN * DK_C, CMP_HID), (CMP_LEN * DK_C) ** -0.5),
        'nsa_cmp_k_w2': nrm((NO, CMP_HID, DK_C), CMP_HID ** -0.5),
        'nsa_cmp_pos_v': nrm((NO, CMP_LEN, DV_C), 0.1),
        'nsa_cmp_v_w1': nrm((NO, CMP_LEN * DV_C, CMP_HID), (CMP_LEN * DV_C) ** -0.5),
        'nsa_cmp_v_w2': nrm((NO, CMP_HID, DV_C), CMP_HID ** -0.5),
        'diff_lambda_q1': nrm((NO, DD), 0.1),
        'diff_lambda_k1': nrm((NO, DD), 0.1),
        'diff_lambda_q2': nrm((NO, DD), 0.1),
        'diff_lambda_k2': nrm((NO, DD), 0.1),
        'diff_norm': gain((NO, 2 * DD)),
        'cd_w_out': nrm((NO, CD_OUT, D_MODEL), CD_OUT ** -0.5 * BETA),
        'ln1_g': gain((DEPTH, D_MODEL)),
        'ln1_b': nrm((DEPTH, D_MODEL), 0.02),
        'ln2_g': gain((DEPTH, D_MODEL)),
        'ln2_b': nrm((DEPTH, D_MODEL), 0.02),
        'router_w': nrm((DEPTH, D_MODEL, N_EXPERTS), D_MODEL ** -0.5),
        'router_b': nrm((DEPTH, N_EXPERTS), 0.01),
        'exp_w_gate': nrm((DEPTH, N_EXPERTS, D_MODEL, D_EXPERT), D_MODEL ** -0.5),
        'exp_w_up': nrm((DEPTH, N_EXPERTS, D_MODEL, D_EXPERT), D_MODEL ** -0.5),
        'exp_w_down': nrm((DEPTH, N_EXPERTS, D_EXPERT, D_MODEL), D_EXPERT ** -0.5 * BETA),
        'sh_w_gate': nrm((DEPTH, D_MODEL, D_SHARED), D_MODEL ** -0.5),
        'sh_w_up': nrm((DEPTH, D_MODEL, D_SHARED), D_MODEL ** -0.5),
        'sh_w_down': nrm((DEPTH, D_SHARED, D_MODEL), D_SHARED ** -0.5 * BETA),
    }


def reference(x, rel_bias, ab_w_in, mla_q_norm, mla_w_uq, mla_kv_norm, mla_w_ukv, ab_w_out,
              cd_w_in, nsa_cmp_pos_k, nsa_cmp_k_w1, nsa_cmp_k_w2, nsa_cmp_pos_v, nsa_cmp_v_w1, nsa_cmp_v_w2,
              diff_lambda_q1, diff_lambda_k1, diff_lambda_q2, diff_lambda_k2, diff_norm, cd_w_out,
              ln1_g, ln1_b, ln2_g, ln2_b, router_w, router_b, exp_w_gate, exp_w_up, exp_w_down,
              sh_w_gate, sh_w_up, sh_w_down):
    tab_b = rel_bias[:, REL_B0:REL_B0 + H_B]
    tab_c = rel_bias[:, REL_C0:REL_C0 + H_C]
    tab_d = rel_bias[:, REL_D0:REL_D0 + H_D]
    for l in range(DEPTH):
        i = l // 2
        if l % 2 == 0:
            y = mixer_ab(x, ab_w_in[i], mla_q_norm[i], mla_w_uq[i], mla_kv_norm[i], mla_w_ukv[i],
                         ab_w_out[i], tab_b)
        else:
            lam_init = 0.8 - 0.6 * math.exp(-0.3 * l)
            y = mixer_cd(x, cd_w_in[i], nsa_cmp_pos_k[i], nsa_cmp_k_w1[i], nsa_cmp_k_w2[i],
                         nsa_cmp_pos_v[i], nsa_cmp_v_w1[i], nsa_cmp_v_w2[i],
                         diff_lambda_q1[i], diff_lambda_k1[i], diff_lambda_q2[i], diff_lambda_k2[i],
                         diff_norm[i], cd_w_out[i], tab_c, tab_d, lam_init)
        x = layer_norm(ALPHA * x + y, ln1_g[l], ln1_b[l])
        y = moe(x, router_w[l], router_b[l], exp_w_gate[l], exp_w_up[l], exp_w_down[l],
                sh_w_gate[l], sh_w_up[l], sh_w_down[l])
        x = layer_norm(ALPHA * x + y, ln2_g[l], ln2_b[l])
    return x
```

```python
import functools
import math

import jax
import jax.numpy as jnp
from jax import lax
from jax.experimental import pallas as pl
from jax.experimental.pallas import tpu as pltpu

F32 = jnp.float32
BF16 = jnp.bfloat16
HI = lax.Precision.HIGHEST

DEPTH = 4
NEG = -1e30
BIG = 1e9
LN_EPS = 1e-5
RMS_EPS = 1e-6
ALPHA = (2 * DEPTH) ** 0.25

N_BUCKETS = 32
REL_MAX_DIST = 2048

H_A = 12
NOPE = 64
ROPE_DIM = 32
MLA_V = 64
Q_LORA = 256
KV_LORA = 128
ROPE_THETA = 10000.0

DIL_PAIRS = ((128, 1), (512, 4), (2048, 16))
H_B_GROUP = 4
H_B = 12
HD_B = 64

H_C = 8
G_C = 2
R_C = 4
DK_C = 64
CMP_LEN = 32
CMP_STRIDE = 16
CMP_HID = 64
SEL_BLOCK = 64
SEL_TOP = 16
WIN = 512

H_D = 4
DD = 64

N_EXPERTS = 64
TOP_K = 8
N_EXPERT_GROUPS = 8
TOPK_GROUPS = 4
D_EXPERT = 256
ROUTED_SCALE = 2.5

LANE = 128
MOE_BM = 256
VMEM_LIMIT = 56 * 1024 * 1024

_NT = (((1,), (1,)), ((), ()))


def _cparams(n_axes, vmem=None):
    return pltpu.CompilerParams(dimension_semantics=("arbitrary",) * n_axes, vmem_limit_bytes=vmem)


def _mm_body(x_ref, w_ref, o_ref, *, act, precision):
    y = jnp.dot(x_ref[...], w_ref[...], preferred_element_type=F32, precision=precision)
    if act == "sigmoid":
        y = jax.nn.sigmoid(y)
    o_ref[...] = y.astype(o_ref.dtype)


def _mm(x, w, out_dtype, tn, tm=512, act=None, precision=None):
    m, k = x.shape
    nc = w.shape[1]
    return pl.pallas_call(
        functools.partial(_mm_body, act=act, precision=precision),
        grid=(m // tm, nc // tn),
        in_specs=[pl.BlockSpec((tm, k), lambda i, j: (i, 0)),
                  pl.BlockSpec((k, tn), lambda i, j: (0, j))],
        out_specs=pl.BlockSpec((tm, tn), lambda i, j: (i, j)),
        out_shape=jax.ShapeDtypeStruct((m, nc), out_dtype),
        compiler_params=_cparams(2),
    )(x, w)


def _layer_norm(z, g, b):
    mu = jnp.mean(z, axis=-1, keepdims=True)
    zc = z - mu
    var = jnp.mean(zc * zc, axis=-1, keepdims=True)
    return zc * lax.rsqrt(var + LN_EPS) * g + b


def _out_ln_body(*refs, n_sum):
    a0 = refs[0][...].astype(F32)
    for r in refs[1:n_sum]:
        a0 = a0 + r[...].astype(F32)
    a1_ref, w0_ref, w1_ref, r_ref, g_ref, b_ref, of_ref, ob_ref = refs[n_sum:]
    y = jnp.dot(a0.astype(BF16), w0_ref[...], preferred_element_type=F32)
    y = y + jnp.dot(a1_ref[...], w1_ref[...], preferred_element_type=F32)
    out = _layer_norm(ALPHA * r_ref[...] + y, g_ref[...], b_ref[...])
    of_ref[...] = out
    ob_ref[...] = out.astype(BF16)


def _out_ln(a0s, a1, w0, w1, resid, g, b, tm=256):
    n, d = resid.shape
    k0, k1 = w0.shape[0], w1.shape[0]
    row = lambda i: (i, 0)
    fixed = lambda i: (0, 0)
    return pl.pallas_call(
        functools.partial(_out_ln_body, n_sum=len(a0s)),
        grid=(n // tm,),
        in_specs=[pl.BlockSpec((tm, k0), row)] * len(a0s) + [
            pl.BlockSpec((tm, k1), row), pl.BlockSpec((k0, d), fixed), pl.BlockSpec((k1, d), fixed),
            pl.BlockSpec((tm, d), row), pl.BlockSpec((1, d), fixed), pl.BlockSpec((1, d), fixed)],
        out_specs=[pl.BlockSpec((tm, d), row), pl.BlockSpec((tm, d), row)],
        out_shape=[jax.ShapeDtypeStruct((n, d), F32), jax.ShapeDtypeStruct((n, d), BF16)],
        compiler_params=_cparams(1),
    )(*a0s, a1, w0, w1, resid, g, b)


def _rms(x, g):
    return x * lax.rsqrt(jnp.mean(x * x, axis=-1, keepdims=True) + RMS_EPS) * g


def _mla_q_body(c_ref, g_ref, w_ref, wr_ref, cos_ref, sin_ref, o_ref):
    cn = _rms(c_ref[...], g_ref[...]).astype(BF16)
    a = jnp.dot(cn, w_ref[...], preferred_element_type=F32)
    r = jnp.dot(cn, wr_ref[...], preferred_element_type=F32)
    cos, sin = cos_ref[...], sin_ref[...]
    for h in range(H_A):
        sl = slice(h * LANE, (h + 1) * LANE)
        o_ref[:, sl] = (a[:, sl] * cos + r[:, sl] * sin).astype(o_ref.dtype)


def _mla_kv_body(c_ref, kr_ref, krr_ref, g_ref, wk_ref, wv_ref, cos_ref, sin_ref, k_ref, v_ref):
    cn = _rms(c_ref[...], g_ref[...]).astype(BF16)
    kn = jnp.dot(cn, wk_ref[...], preferred_element_type=F32)
    rope = kr_ref[...] * cos_ref[...] + krr_ref[...] * sin_ref[...]
    for h in range(H_A):
        sl = slice(h * LANE, (h + 1) * LANE)
        k_ref[:, sl] = (kn[:, sl] + rope).astype(k_ref.dtype)
    v_ref[...] = jnp.dot(cn, wv_ref[...], preferred_element_type=F32).astype(v_ref.dtype)


def _mla_up(h1, q_norm, wq, wq_rot, kv_norm, wk, wv, cos_q, sin_q, cos_k, sin_k, seq, tm=512):
    n = h1.shape[0]
    spt = seq // tm
    row = lambda c: (lambda i: (i, c))
    pos = lambda i: (i % spt, 0)
    fixed = lambda i: (0, 0)
    q_a = pl.pallas_call(
        _mla_q_body, grid=(n // tm,),
        in_specs=[pl.BlockSpec((tm, Q_LORA), row(0)), pl.BlockSpec((1, Q_LORA), fixed),
                  pl.BlockSpec(wq.shape, fixed), pl.BlockSpec(wq_rot.shape, fixed),
                  pl.BlockSpec((tm, LANE), pos), pl.BlockSpec((tm, LANE), pos)],
        out_specs=pl.BlockSpec((tm, H_A * LANE), row(0)),
        out_shape=jax.ShapeDtypeStruct((n, H_A * LANE), BF16),
        compiler_params=_cparams(1),
    )(h1, q_norm, wq, wq_rot, cos_q, sin_q)
    k_a, v_a = pl.pallas_call(
        _mla_kv_body, grid=(n // tm,),
        in_specs=[pl.BlockSpec((tm, LANE), row(2)), pl.BlockSpec((tm, LANE), row(3)),
                  pl.BlockSpec((tm, LANE), row(4)), pl.BlockSpec((1, KV_LORA), fixed),
                  pl.BlockSpec(wk.shape, fixed), pl.BlockSpec(wv.shape, fixed),
                  pl.BlockSpec((tm, LANE), pos), pl.BlockSpec((tm, LANE), pos)],
        out_specs=[pl.BlockSpec((tm, H_A * LANE), row(0)), pl.BlockSpec((tm, H_A * MLA_V), row(0))],
        out_shape=[jax.ShapeDtypeStruct((n, H_A * LANE), BF16), jax.ShapeDtypeStruct((n, H_A * MLA_V), BF16)],
        compiler_params=_cparams(1),
    )(h1, h1, h1, kv_norm, wk, wv, cos_k, sin_k)
    return q_a, k_a, v_a


def _flash_body(*refs, T, dq, dv, q_offs, k_offs, v_offs, scale, bias_mode, has_sel, has_gate,
                epilogue, want_lse, seg_tiles, nback, lam_init):
    it = iter(refs)
    q_ref, k_ref, v_ref = next(it), next(it), next(it)
    bias_ref = next(it) if bias_mode else None
    sel_ref, e_ref = (next(it), next(it)) if has_sel else (None, None)
    gate_ref = next(it) if has_gate else None
    lam_ref, dn_ref = (next(it), next(it)) if epilogue == "diff" else (None, None)
    o_ref = next(it)
    lse_ref = next(it) if want_lse else None
    m_scr, l_scr, acc_scr = next(it), next(it), next(it)

    qi = pl.program_id(2)
    m_scr[...] = jnp.full(m_scr.shape, NEG, F32)
    l_scr[...] = jnp.zeros(l_scr.shape, F32)
    acc_scr[...] = jnp.zeros(acc_scr.shape, F32)
    qfull = q_ref[0]
    qs = [qfull[:, off:off + dq] for off in q_offs]
    selt = sel_ref[0, 0] if has_sel else None

    def chunk(kc, diag):
        start = pl.multiple_of(kc * T, T)
        keep = None
        if has_sel:
            keep = jnp.dot(selt, e_ref[:, pl.ds(start, T)], preferred_element_type=F32) > 0.5
        kfull = k_ref[0, pl.ds(start, T), :]
        vfull = v_ref[0, pl.ds(start, T), :]
        for u in range(2):
            k = kfull[:, k_offs[u]:k_offs[u] + dq]
            v = vfull[:, v_offs[u]:v_offs[u] + dv]
            s = lax.dot_general(qs[u], k, _NT, preferred_element_type=F32) * scale
            if bias_mode:
                s = s + bias_ref[u if bias_mode == "pair" else 0, qi - kc]
            elif diag:
                row = lax.broadcasted_iota(jnp.int32, (T, T), 0)
                col = lax.broadcasted_iota(jnp.int32, (T, T), 1)
                s = jnp.where(row >= col, s, NEG)
            if has_sel:
                s = jnp.where(keep, s, NEG)
            m_prev = m_scr[u]
            m_new = jnp.maximum(m_prev, jnp.max(s, axis=-1, keepdims=True))
            alpha = jnp.exp(m_prev - m_new)
            p = jnp.exp(s - m_new)
            l_scr[u] = alpha * l_scr[u] + jnp.sum(p, axis=-1, keepdims=True)
            acc_scr[u] = alpha * acc_scr[u] + jnp.dot(p.astype(BF16), v, preferred_element_type=F32)
            m_scr[u] = m_new

    def loop_body(kc, carry):
        chunk(kc, False)
        return carry

    if bias_mode:
        if nback is None:
            lo = 0
        elif seg_tiles is not None:
            lo = jnp.where(qi % seg_tiles == 0, qi, qi - nback)
        else:
            lo = jnp.maximum(qi - nback, 0)
        lax.fori_loop(lo, qi + 1, loop_body, 0)
    else:
        lax.fori_loop(0, qi, loop_body, 0)
        chunk(qi, True)

    outs = [acc_scr[u] / l_scr[u] for u in range(2)]
    if epilogue == "diff":
        a = outs[0] - lam_ref[0, 0] * outs[1]
        o = _rms(a, dn_ref[...]) * (1.0 - lam_init)
    else:
        o = jnp.concatenate(outs, axis=-1)
        if has_gate:
            o = o * gate_ref[0]
    o_ref[0] = o.astype(o_ref.dtype)
    if want_lse:
        lse = [jnp.broadcast_to(m_scr[u] + jnp.log(l_scr[u]), (T, dv)) for u in range(2)]
        lse_ref[0] = jnp.concatenate(lse, axis=-1)


def _flash(q, k, v, *, n_outer, T, dq, dv, q_col, k_col, v_col, q_offs, k_offs, v_offs, scale,
           out_cols, out_col, bias=None, bias_mode=None, bias_idx=None, sel=None, emat=None, sel_idx=None,
           gate=None, gate_col=None, lam=None, dnorm=None, lam_init=0.0, epilogue="plain", want_lse=False,
           seg_tiles=None, nback=None, vmem=None, k_w=None, v_w=None):
    bsz, seq, _ = q.shape
    nq = seq // T
    qw = max(o + dq for o in q_offs)
    kw = k_w or max(o + dq for o in k_offs)
    vw = v_w or max(o + dv for o in v_offs)
    ow = dv if epilogue == "diff" else 2 * dv
    in_specs = [pl.BlockSpec((1, T, qw), lambda g, b, i: (b, i, q_col(g))),
                pl.BlockSpec((1, seq, kw), lambda g, b, i: (b, 0, k_col(g))),
                pl.BlockSpec((1, seq, vw), lambda g, b, i: (b, 0, v_col(g)))]
    args = [q, k, v]
    if bias_mode:
        nb = 2 if bias_mode == "pair" else 1
        in_specs.append(pl.BlockSpec((nb,) + bias.shape[1:], lambda g, b, i: (bias_idx(g), 0, 0, 0)))
        args.append(bias)
    if sel is not None:
        in_specs.append(pl.BlockSpec((1, 1, T, sel.shape[-1]), lambda g, b, i: (b, sel_idx(g), i, 0)))
        in_specs.append(pl.BlockSpec(emat.shape, lambda g, b, i: (0, 0)))
        args += [sel, emat]
    if gate is not None:
        in_specs.append(pl.BlockSpec((1, T, ow), lambda g, b, i: (b, i, gate_col(g))))
        args.append(gate)
    if epilogue == "diff":
        in_specs.append(pl.BlockSpec(memory_space=pltpu.SMEM))
        in_specs.append(pl.BlockSpec((1, dv), lambda g, b, i: (0, 0)))
        args += [lam, dnorm]
    out_specs = [pl.BlockSpec((1, T, ow), lambda g, b, i: (b, i, out_col(g)))]
    out_shape = [jax.ShapeDtypeStruct((bsz, seq, out_cols), BF16)]
    if want_lse:
        out_specs.append(pl.BlockSpec((1, T, ow), lambda g, b, i: (b, i, out_col(g))))
        out_shape.append(jax.ShapeDtypeStruct((bsz, seq, out_cols), F32))
    body = functools.partial(
        _flash_body, T=T, dq=dq, dv=dv, q_offs=q_offs, k_offs=k_offs, v_offs=v_offs, scale=scale,
        bias_mode=bias_mode, has_sel=sel is not None, has_gate=gate is not None, epilogue=epilogue,
        want_lse=want_lse, seg_tiles=seg_tiles, nback=nback, lam_init=lam_init)
    res = pl.pallas_call(
        body, grid=(n_outer, bsz, nq), in_specs=in_specs, out_specs=out_specs, out_shape=out_shape,
        scratch_shapes=[pltpu.VMEM((2, T, 1), F32), pltpu.VMEM((2, T, 1), F32), pltpu.VMEM((2, T, dv), F32)],
        compiler_params=_cparams(3, vmem),
    )(*args)
    return res if want_lse else res[0]


def _rel_bucket(dist):
    n = jnp.maximum(dist, 0)
    exact = N_BUCKETS // 2
    log_ratio = jnp.log(jnp.maximum(n, 1).astype(F32) / exact) / math.log(REL_MAX_DIST / exact)
    large = exact + (log_ratio * (N_BUCKETS - exact)).astype(jnp.int32)
    return jnp.where(n < exact, n, jnp.minimum(large, N_BUCKETS - 1))


def _toeplitz_bias(tab, T, n_d, dist_scale, max_dist):
    i = jnp.arange(T)[:, None]
    j = jnp.arange(T)[None, :]
    dist = jnp.arange(n_d)[:, None, None] * T + (i - j)[None]
    vec = tab[_rel_bucket(jnp.arange(n_d * T + T) * dist_scale)].astype(F32)
    b = jnp.moveaxis(vec[jnp.maximum(dist, 0)], -1, 0)
    return jnp.where(((dist >= 0) & (dist <= max_dist))[None], b, NEG)


def _compress_body(u_ref, pe_ref, w1_ref, w2_ref, o_ref, *, ncp):
    outs = []
    for a in range(2):
        u = u_ref[0, 0, a].astype(F32)
        p1 = jnp.dot(u + pe_ref[a, 0], w1_ref[a, 0], preferred_element_type=F32, precision=HI)
        p2 = jnp.dot(u + pe_ref[a, 1], w1_ref[a, 1], preferred_element_type=F32, precision=HI)
        hid = jax.nn.gelu(p1 + pltpu.roll(p2, ncp - 1, 0))
        outs.append(jnp.dot(hid, w2_ref[a], preferred_element_type=F32, precision=HI))
    o_ref[0, 0] = jnp.concatenate(outs, axis=-1)


def _compress(u, pe, w1, w2):
    bsz, g, _, ncp, width = u.shape
    return pl.pallas_call(
        functools.partial(_compress_body, ncp=ncp), grid=(bsz, g),
        in_specs=[pl.BlockSpec((1, 1, 2, ncp, width), lambda b, gg: (b, gg, 0, 0, 0)),
                  pl.BlockSpec(pe.shape, lambda b, gg: (0, 0, 0, 0)),
                  pl.BlockSpec(w1.shape, lambda b, gg: (0, 0, 0, 0)),
                  pl.BlockSpec(w2.shape, lambda b, gg: (0, 0, 0))],
        out_specs=pl.BlockSpec((1, 1, ncp, 2 * DK_C), lambda b, gg: (b, gg, 0, 0)),
        out_shape=jax.ShapeDtypeStruct((bsz, g, ncp, 2 * DK_C), F32),
        compiler_params=_cparams(2),
    )(u, pe, w1, w2)


def _cmp_attn_body(q_ref, kv_ref, bias_ref, ov_ref, gate_ref, o_ref, sel_ref, *, T, ncp, n_sel, n_top, scale):
    qi = pl.program_id(2)
    kc = kv_ref[0, 0, :, :DK_C]
    vc = kv_ref[0, 0, :, DK_C:]
    t = qi * T + lax.broadcasted_iota(jnp.int32, (T, ncp), 0)
    c = lax.broadcasted_iota(jnp.int32, (T, ncp), 1)
    valid = t >= c * CMP_STRIDE + (CMP_LEN - 1)
    validf = valid.astype(F32)
    psum = jnp.zeros((T, ncp), F32)
    outs = []
    for r in range(R_C):
        q = q_ref[0, :, r * DK_C:(r + 1) * DK_C].astype(F32)
        s = lax.dot_general(q, kc, _NT, preferred_element_type=F32, precision=HI) * scale + bias_ref[r]
        s = jnp.where(valid, s, NEG)
        e = jnp.exp(s - jnp.max(s, axis=-1, keepdims=True)) * validf
        p = e / jnp.maximum(jnp.sum(e, axis=-1, keepdims=True), 1e-30)
        outs.append(jnp.dot(p, vc, preferred_element_type=F32, precision=HI))
        psum = psum + p
    o_ref[0] = (jnp.concatenate(outs, axis=-1) * gate_ref[0]).astype(o_ref.dtype)

    imp = jnp.dot(psum, ov_ref[...], preferred_element_type=F32, precision=HI)
    tq = qi * T + lax.broadcasted_iota(jnp.int32, (T, n_sel), 0)
    j = lax.broadcasted_iota(jnp.int32, (T, n_sel), 1)
    forced = (j == tq // SEL_BLOCK) | (j == 0)
    work = jnp.where(forced, BIG, jnp.where(j * SEL_BLOCK <= tq, imp, -BIG))
    sel = jnp.zeros((T, n_sel), F32)
    jf = j.astype(F32)
    for _ in range(n_top):
        mx = jnp.max(work, axis=-1, keepdims=True)
        first = jnp.min(jnp.where(work == mx, jf, float(n_sel)), axis=-1, keepdims=True)
        pick = jf == first
        sel = jnp.where(pick, 1.0, sel)
        work = jnp.where(pick, -3e38, work)
    sel_ref[0, 0] = sel.astype(sel_ref.dtype)


def _cmp_attn(h, kvc, bias_c, overlap, gates, T=256):
    bsz, seq, _ = h.shape
    ncp = kvc.shape[2]
    n_sel = seq // SEL_BLOCK
    n_top = min(SEL_TOP, n_sel)
    qw = R_C * DK_C
    return pl.pallas_call(
        functools.partial(_cmp_attn_body, T=T, ncp=ncp, n_sel=n_sel, n_top=n_top, scale=DK_C ** -0.5),
        grid=(G_C, bsz, seq // T),
        in_specs=[pl.BlockSpec((1, T, qw), lambda g, b, i: (b, i, g)),
                  pl.BlockSpec((1, 1, ncp, 2 * DK_C), lambda g, b, i: (b, g, 0, 0)),
                  pl.BlockSpec((R_C, T, ncp), lambda g, b, i: (g, i, 0)),
                  pl.BlockSpec(overlap.shape, lambda g, b, i: (0, 0)),
                  pl.BlockSpec((1, T, qw), lambda g, b, i: (b, i, g))],
        out_specs=[pl.BlockSpec((1, T, qw), lambda g, b, i: (b, i, g)),
                   pl.BlockSpec((1, 1, T, n_sel), lambda g, b, i: (b, g, i, 0))],
        out_shape=[jax.ShapeDtypeStruct((bsz, seq, H_C * DK_C), BF16),
                   jax.ShapeDtypeStruct((bsz, G_C, seq, n_sel), BF16)],
        compiler_params=_cparams(3),
    )(h, kvc, bias_c, overlap, gates)


def _moe_ffn_body(be_ref, nb_ref, x_ref, wg_ref, wu_ref, wd_ref, o_ref):
    i = pl.program_id(0)

    @pl.when(i < nb_ref[0])
    def _():
        x = x_ref[...]
        hg = jnp.dot(x, wg_ref[0], preferred_element_type=F32)
        hu = jnp.dot(x, wu_ref[0], preferred_element_type=F32)
        hb = (hg * jax.nn.sigmoid(hg) * hu).astype(BF16)
        o_ref[...] = jnp.dot(hb, wd_ref[0], preferred_element_type=F32).astype(o_ref.dtype)

    @pl.when(i >= nb_ref[0])
    def _():
        o_ref[...] = jnp.zeros(o_ref.shape, o_ref.dtype)


def _moe_ffn(xs, blk_e, n_used, wg, wu, wd):
    p, d = xs.shape
    n_blocks = p // MOE_BM
    grid_spec = pltpu.PrefetchScalarGridSpec(
        num_scalar_prefetch=2, grid=(n_blocks,),
        in_specs=[pl.BlockSpec((MOE_BM, d), lambda i, be, nb: (i, 0)),
                  pl.BlockSpec((1, d, D_EXPERT), lambda i, be, nb: (be[i], 0, 0)),
                  pl.BlockSpec((1, d, D_EXPERT), lambda i, be, nb: (be[i], 0, 0)),
                  pl.BlockSpec((1, D_EXPERT, d), lambda i, be, nb: (be[i], 0, 0))],
        out_specs=pl.BlockSpec((MOE_BM, d), lambda i, be, nb: (i, 0)))
    return pl.pallas_call(
        _moe_ffn_body, grid_spec=grid_spec, out_shape=jax.ShapeDtypeStruct((p, d), BF16),
        compiler_params=_cparams(1),
    )(blk_e, n_used, xs, wg, wu, wd)


def _moe_out_body(xb_ref, xf_ref, yr_ref, sg_ref, su_ref, sd_ref, g_ref, b_ref, of_ref, ob_ref):
    x = xb_ref[...]
    hg = jnp.dot(x, sg_ref[...], preferred_element_type=F32)
    hu = jnp.dot(x, su_ref[...], preferred_element_type=F32)
    hb = (hg * jax.nn.sigmoid(hg) * hu).astype(BF16)
    y = yr_ref[...] + jnp.dot(hb, sd_ref[...], preferred_element_type=F32)
    out = _layer_norm(ALPHA * xf_ref[...] + y, g_ref[...], b_ref[...])
    of_ref[...] = out
    ob_ref[...] = out.astype(BF16)


def _moe_out(xb, xf, yr, sg, su, sd, g, b, tm=256):
    n, d = xf.shape
    row = lambda i: (i, 0)
    fixed = lambda i: (0, 0)
    return pl.pallas_call(
        _moe_out_body, grid=(n // tm,),
        in_specs=[pl.BlockSpec((tm, d), row), pl.BlockSpec((tm, d), row), pl.BlockSpec((tm, d), row),
                  pl.BlockSpec(sg.shape, fixed), pl.BlockSpec(su.shape, fixed), pl.BlockSpec(sd.shape, fixed),
                  pl.BlockSpec((1, d), fixed), pl.BlockSpec((1, d), fixed)],
        out_specs=[pl.BlockSpec((tm, d), row), pl.BlockSpec((tm, d), row)],
        out_shape=[jax.ShapeDtypeStruct((n, d), F32), jax.ShapeDtypeStruct((n, d), BF16)],
        compiler_params=_cparams(1),
    )(xb, xf, yr, sg, su, sd, g, b)


def _moe(xf, xb, router_w, router_b, w_gate, w_up, w_down, sh_gate, sh_up, sh_down, ln_g, ln_b):
    n, d = xf.shape
    rw = jnp.pad(router_w, ((0, 0), (0, LANE - N_EXPERTS)))
    scores = _mm(xf, rw, F32, tn=LANE, act="sigmoid", precision=HI)[:, :N_EXPERTS]
    sel = scores + router_b.astype(F32)
    per_grp = N_EXPERTS // N_EXPERT_GROUPS
    grp_score = lax.top_k(sel.reshape(-1, N_EXPERT_GROUPS, per_grp), 2)[0].sum(-1)
    _, grp_idx = lax.top_k(grp_score, TOPK_GROUPS)
    grp_mask = jax.nn.one_hot(grp_idx, N_EXPERT_GROUPS, dtype=F32).sum(-2) > 0
    sel = jnp.where(jnp.repeat(grp_mask, per_grp, axis=-1), sel, NEG)
    _, e_idx = lax.top_k(sel, TOP_K)
    gate = jnp.take_along_axis(scores, e_idx, axis=-1)
    gate = gate / gate.sum(-1, keepdims=True) * ROUTED_SCALE

    onehot = jnp.sum(jax.nn.one_hot(e_idx, N_EXPERTS, dtype=jnp.int32), axis=1)
    rank = jnp.cumsum(onehot, axis=0) - onehot
    counts = jnp.sum(onehot, axis=0)
    padded = (counts + MOE_BM - 1) // MOE_BM * MOE_BM
    pad_end = jnp.cumsum(padded)
    pad_start = pad_end - padded
    pos = pad_start[e_idx] + jnp.take_along_axis(rank, e_idx, axis=1)
    n_blocks = (n * TOP_K + N_EXPERTS * (MOE_BM - 1) + MOE_BM - 1) // MOE_BM
    p = n_blocks * MOE_BM
    rows = jnp.full((p,), n, jnp.int32).at[pos.reshape(-1)].set(
        jnp.repeat(jnp.arange(n, dtype=jnp.int32), TOP_K))
    blk_e = jnp.minimum(jnp.searchsorted(pad_end, jnp.arange(n_blocks) * MOE_BM, side="right"),
                        N_EXPERTS - 1).astype(jnp.int32)
    n_used = (pad_end[-1] // MOE_BM).astype(jnp.int32).reshape(1)
    xs = jnp.concatenate([xb, jnp.zeros((1, d), xb.dtype)], 0)[rows]
    y = _moe_ffn(xs, blk_e, n_used, w_gate, w_up, w_down)
    yr = jnp.sum(y[pos].astype(F32) * gate[..., None], axis=1)
    return _moe_out(xb, xf, yr, sh_gate, sh_up, sh_down, ln_g[None], ln_b[None])


def _rope_tables(seq):
    half = ROPE_DIM // 2
    freqs = ROPE_THETA ** (-jnp.arange(half, dtype=F32) / half)
    ang = jnp.arange(seq).astype(F32)[:, None] * freqs
    cos = jnp.concatenate([jnp.cos(ang)] * 2, -1)
    sin = jnp.concatenate([jnp.sin(ang)] * 2, -1)
    z = lambda w: jnp.zeros((seq, w), F32)
    pad = LANE - NOPE - ROPE_DIM
    cos_q = jnp.concatenate([jnp.ones((seq, NOPE), F32), cos, z(pad)], -1)
    sin_q = jnp.concatenate([z(NOPE), sin, z(pad)], -1)
    cos_k = jnp.concatenate([z(NOPE), cos, z(pad)], -1)
    return cos_q, sin_q, cos_k, sin_q


def _rot_cols(w):
    half = w.shape[-1] // 2
    return jnp.concatenate([-w[..., half:], w[..., :half]], -1)


def _mixer_ab(xb, xf, bsz, seq, w_in, q_norm, w_uq, kv_norm, w_ukv, w_out, ln_g, ln_b, rope_tabs, dil_bias):
    n = bsz * seq
    d = w_in.shape[0]
    c0 = Q_LORA + KV_LORA
    w_kr = w_in[:, c0:c0 + ROPE_DIM]
    zc = lambda w: jnp.zeros((d, w), F32)
    pad = LANE - NOPE - ROPE_DIM
    w1 = jnp.concatenate([w_in[:, :c0], zc(NOPE), w_kr, zc(pad), zc(NOPE), _rot_cols(w_kr), zc(pad)], 1)
    h1 = _mm(xb, w1.astype(BF16), F32, tn=LANE)
    h2 = _mm(xb, w_in[:, c0 + ROPE_DIM:].astype(BF16), BF16, tn=256)

    wq = w_uq.reshape(Q_LORA, H_A, NOPE + ROPE_DIM)
    zq = jnp.zeros((Q_LORA, H_A, pad), F32)
    wq_main = jnp.concatenate([wq, zq], -1).reshape(Q_LORA, H_A * LANE)
    wq_rot = jnp.concatenate([jnp.zeros((Q_LORA, H_A, NOPE), F32), _rot_cols(wq[..., NOPE:]), zq], -1)
    wq_rot = wq_rot.reshape(Q_LORA, H_A * LANE)
    wkv = w_ukv.reshape(KV_LORA, H_A, NOPE + MLA_V)
    wk = jnp.concatenate([wkv[..., :NOPE], jnp.zeros((KV_LORA, H_A, LANE - NOPE), F32)], -1)
    wk = wk.reshape(KV_LORA, H_A * LANE)
    wv = wkv[..., NOPE:].reshape(KV_LORA, H_A * MLA_V)
    q_a, k_a, v_a = _mla_up(h1, q_norm[None], wq_main.astype(BF16), wq_rot.astype(BF16), kv_norm[None],
                            wk.astype(BF16), wv.astype(BF16), *rope_tabs, seq)
    o_a = _flash(q_a.reshape(bsz, seq, -1), k_a.reshape(bsz, seq, -1), v_a.reshape(bsz, seq, -1),
                 n_outer=H_A // 2, T=256, dq=LANE, dv=MLA_V,
                 q_col=lambda g: g, k_col=lambda g: g, v_col=lambda g: g,
                 q_offs=(0, LANE), k_offs=(0, LANE), v_offs=(0, MLA_V),
                 scale=(NOPE + ROPE_DIM) ** -0.5, out_cols=H_A * MLA_V, out_col=lambda g: g)

    gw = H_B_GROUP * HD_B
    hq = h2.reshape(bsz, seq, 3, len(DIL_PAIRS), gw)
    outs, lses = [], []
    for gi, (window, dil) in enumerate(DIL_PAIRS):
        L = seq // dil
        t = hq[:, :, :, gi].reshape(bsz, L, dil, 3 * gw).transpose(0, 2, 1, 3).reshape(bsz, seq, 3 * gw)
        T = min(256, L)
        o, lse = _flash(t, t, t, n_outer=2, T=T, dq=HD_B, dv=HD_B,
                        q_col=lambda g: g, k_col=lambda g: 2 + g, v_col=lambda g: 4 + g,
                        q_offs=(0, HD_B), k_offs=(0, HD_B), v_offs=(0, HD_B), scale=HD_B ** -0.5,
                        out_cols=gw, out_col=lambda g: g, bias=dil_bias[gi], bias_mode="pair",
                        bias_idx=lambda g: g, want_lse=True, seg_tiles=L // T, nback=1)
        unstride = lambda a: a.reshape(bsz, dil, L, -1).transpose(0, 2, 1, 3).reshape(bsz, seq, -1)
        outs.append(unstride(o).astype(F32))
        lses.append(unstride(lse))
    w = jax.nn.softmax(jnp.stack(lses), axis=0)
    o_b = jnp.sum(w * jnp.stack(outs), axis=0).astype(BF16).reshape(n, gw)
    na = H_A * MLA_V
    return _out_ln([o_a.reshape(n, na)], o_b, w_out[:na].astype(BF16), w_out[na:].astype(BF16),
                   xf, ln_g[None], ln_b[None])


def _mixer_cd(xb, xf, bsz, seq, w_in, pos_k, k_w1, k_w2, pos_v, v_w1, v_w2, lq1, lk1, lq2, lk2, d_norm,
              w_out, ln_g, ln_b, lam_init, tabs):
    n = bsz * seq
    qc_w = H_C * DK_C
    kv_w = G_C * DK_C
    off = qc_w
    kvs = []
    for _ in range(3):
        wk_ = w_in[:, off:off + kv_w].reshape(-1, G_C, DK_C)
        wv_ = w_in[:, off + kv_w:off + 2 * kv_w].reshape(-1, G_C, DK_C)
        kvs.append(jnp.concatenate([wk_, wv_], -1).reshape(-1, 2 * kv_w))
        off += 2 * kv_w
    g_off = off
    d_off = off + 3 * H_C
    w_main = jnp.concatenate([w_in[:, :qc_w]] + kvs + [w_in[:, d_off:]], 1)
    h = _mm(xb, w_main.astype(BF16), BF16, tn=256).reshape(bsz, seq, -1)
    w_g = jnp.repeat(w_in[:, g_off:d_off], DK_C, axis=1)
    gates = _mm(xb, w_g.astype(BF16), F32, tn=256, act="sigmoid").reshape(bsz, seq, -1)

    ncp = seq // CMP_STRIDE
    half = CMP_STRIDE * DK_C
    kv_cmp = h[:, :, qc_w:qc_w + 2 * kv_w].reshape(bsz, ncp, CMP_STRIDE, G_C, 2, DK_C)
    u = kv_cmp.transpose(0, 3, 4, 1, 2, 5).reshape(bsz, G_C, 2, ncp, half)
    pe = jnp.stack([pos_k.reshape(2, 1, half), pos_v.reshape(2, 1, half)])
    w1 = jnp.stack([k_w1.reshape(2, half, CMP_HID), v_w1.reshape(2, half, CMP_HID)])
    w2 = jnp.stack([k_w2, v_w2])
    kvc = _compress(u, pe, w1, w2)
    o_cmp, sel = _cmp_attn(h, kvc, tabs["bias_c"], tabs["overlap"], gates)

    cb = qc_w // LANE
    scale = DK_C ** -0.5
    n_pairs = H_C // 2
    nsa = dict(n_outer=n_pairs, dq=DK_C, dv=DK_C, q_col=lambda g: g, q_offs=(0, DK_C), k_offs=(0, 0),
               v_offs=(DK_C, DK_C), scale=scale, out_cols=qc_w, out_col=lambda g: g, bias_mode="pair",
               bias_idx=lambda g: g, gate=gates, vmem=VMEM_LIMIT, k_w=LANE, v_w=LANE)
    o_sel = _flash(h, h, h, T=256, k_col=lambda g: cb + 2 + g // 2, v_col=lambda g: cb + 2 + g // 2,
                   bias=tabs["bias_sel"], sel=sel, emat=tabs["emat"], sel_idx=lambda g: g // 2,
                   gate_col=lambda g: n_pairs + g, **nsa)
    o_win = _flash(h, h, h, T=WIN, k_col=lambda g: cb + 4 + g // 2, v_col=lambda g: cb + 4 + g // 2,
                   bias=tabs["bias_win"], gate_col=lambda g: 2 * n_pairs + g, nback=1, **nsa)

    lam = (jnp.exp(jnp.sum(lq1.astype(F32) * lk1.astype(F32)))
           - jnp.exp(jnp.sum(lq2.astype(F32) * lk2.astype(F32))) + lam_init).reshape(1, 1)
    db = cb + 6
    o_d = _flash(h, h, h, n_outer=H_D, T=256, dq=DD, dv=2 * DD,
                 q_col=lambda g: db + g, k_col=lambda g: db + H_D + g, v_col=lambda g: db + 2 * H_D + g,
                 q_offs=(0, DD), k_offs=(0, DD), v_offs=(0, 0), scale=DD ** -0.5,
                 out_cols=H_D * 2 * DD, out_col=lambda g: g, bias=tabs["bias_d"], bias_mode="shared",
                 bias_idx=lambda g: g, lam=lam, dnorm=d_norm[None], lam_init=lam_init, epilogue="diff",
                 vmem=VMEM_LIMIT)
    r2 = lambda a: a.reshape(n, -1)
    return _out_ln([r2(o_cmp), r2(o_sel), r2(o_win)], r2(o_d), w_out[:qc_w].astype(BF16),
                   w_out[qc_w:].astype(BF16), xf, ln_g[None], ln_b[None])


def _nsa_tables(rel_bias, seq):
    tab_c = rel_bias[:, H_B:H_B + H_C]
    tab_d = rel_bias[:, H_B + H_C:H_B + H_C + H_D]
    ncp = seq // CMP_STRIDE
    n_sel = seq // SEL_BLOCK
    pos = jnp.arange(seq)
    blk_end = jnp.arange(ncp) * CMP_STRIDE + CMP_LEN - 1
    bias_c = tab_c[_rel_bucket(pos[:, None] - blk_end[None, :])].astype(F32).transpose(2, 0, 1)
    c0 = jnp.arange(ncp) * CMP_STRIDE
    s0 = jnp.arange(n_sel) * SEL_BLOCK
    overlap = jnp.maximum(jnp.minimum(c0[:, None] + CMP_LEN, s0[None, :] + SEL_BLOCK)
                          - jnp.maximum(c0[:, None], s0[None, :]), 0).astype(F32) / CMP_LEN
    emat = (jnp.arange(n_sel)[:, None] == (pos // SEL_BLOCK)[None, :]).astype(BF16)
    return {
        "bias_c": bias_c, "overlap": overlap, "emat": emat,
        "bias_sel": _toeplitz_bias(tab_c, 256, seq // 256, 1, seq),
        "bias_win": _toeplitz_bias(tab_c, WIN, 2, 1, WIN - 1),
        "bias_d": _toeplitz_bias(tab_d, 256, seq // 256, 1, seq),
    }


def kernel(x, rel_bias, ab_w_in, mla_q_norm, mla_w_uq, mla_kv_norm, mla_w_ukv, ab_w_out, cd_w_in, nsa_cmp_pos_k, nsa_cmp_k_w1, nsa_cmp_k_w2, nsa_cmp_pos_v, nsa_cmp_v_w1, nsa_cmp_v_w2, diff_lambda_q1, diff_lambda_k1, diff_lambda_q2, diff_lambda_k2, diff_norm, cd_w_out, ln1_g, ln1_b, ln2_g, ln2_b, router_w, router_b, exp_w_gate, exp_w_up, exp_w_down, sh_w_gate, sh_w_up, sh_w_down):
    bsz, seq, d = x.shape
    n = bsz * seq
    depth = ln1_g.shape[0]
    rope_tabs = _rope_tables(seq)
    dil_bias = [_toeplitz_bias(rel_bias[:, gi * H_B_GROUP:(gi + 1) * H_B_GROUP], min(256, seq // dil), 2, dil,
                               window // dil) for gi, (window, dil) in enumerate(DIL_PAIRS)]
    nsa_tabs = _nsa_tables(rel_bias, seq)
    xf = x.reshape(n, d)
    xb = xf.astype(BF16)
    for l in range(depth):
        i = l // 2
        if l % 2 == 0:
            xf, xb = _mixer_ab(xb, xf, bsz, seq, ab_w_in[i], mla_q_norm[i], mla_w_uq[i], mla_kv_norm[i],
                               mla_w_ukv[i], ab_w_out[i], ln1_g[l], ln1_b[l], rope_tabs, dil_bias)
        else:
            lam_init = 0.8 - 0.6 * math.exp(-0.3 * l)
            xf, xb = _mixer_cd(xb, xf, bsz, seq, cd_w_in[i], nsa_cmp_pos_k[i], nsa_cmp_k_w1[i],
                               nsa_cmp_k_w2[i], nsa_cmp_pos_v[i], nsa_cmp_v_w1[i], nsa_cmp_v_w2[i],
                               diff_lambda_q1[i], diff_lambda_k1[i], diff_lambda_q2[i], diff_lambda_k2[i],
                               diff_norm[i], cd_w_out[i], ln1_g[l], ln1_b[l], lam_init, nsa_tabs)
        xf, xb = _moe(xf, xb, router_w[l], router_b[l], exp_w_gate[l].astype(BF16), exp_w_up[l].astype(BF16),
                      exp_w_down[l].astype(BF16), sh_w_gate[l].astype(BF16), sh_w_up[l].astype(BF16),
                      sh_w_down[l].astype(BF16), ln2_g[l], ln2_b[l])
    return xf.reshape(bsz, seq, d)
```

```python
import functools
import math

import jax
import jax.numpy as jnp
from jax import lax
from jax.experimental import pallas as pl
from jax.experimental.pallas import tpu as pltpu

F32 = jnp.float32
BF16 = jnp.bfloat16
HI = lax.Precision.HIGHEST

DEPTH = 4
NEG = -1e30
BIG = 1e9
LN_EPS = 1e-5
RMS_EPS = 1e-6
ALPHA = (2 * DEPTH) ** 0.25

N_BUCKETS = 32
REL_MAX_DIST = 2048

H_A = 12
NOPE = 64
ROPE_DIM = 32
MLA_V = 64
Q_LORA = 256
KV_LORA = 128
ROPE_THETA = 10000.0

DIL_PAIRS = ((128, 1), (512, 4), (2048, 16))
H_B_GROUP = 4
H_B = 12
HD_B = 64

H_C = 8
G_C = 2
R_C = 4
DK_C = 64
CMP_LEN = 32
CMP_STRIDE = 16
CMP_HID = 64
SEL_BLOCK = 64
SEL_TOP = 16
WIN = 512

H_D = 4
DD = 64

N_EXPERTS = 64
TOP_K = 8
N_EXPERT_GROUPS = 8
TOPK_GROUPS = 4
D_EXPERT = 256
ROUTED_SCALE = 2.5

LANE = 128
MOE_BM = 256
VMEM_LIMIT = 56 * 1024 * 1024

_NT = (((1,), (1,)), ((), ()))


def _cparams(n_axes, vmem=None):
    return pltpu.CompilerParams(dimension_semantics=("arbitrary",) * n_axes, vmem_limit_bytes=vmem)


def _mm_body(x_ref, w_ref, o_ref, *, act, precision):
    y = jnp.dot(x_ref[...], w_ref[...], preferred_element_type=F32, precision=precision)
    if act == "sigmoid":
        y = jax.nn.sigmoid(y)
    o_ref[...] = y.astype(o_ref.dtype)


def _mm(x, w, out_dtype, tn, tm=512, act=None, precision=None):
    m, k = x.shape
    nc = w.shape[1]
    return pl.pallas_call(
        functools.partial(_mm_body, act=act, precision=precision),
        grid=(m // tm, nc // tn),
        in_specs=[pl.BlockSpec((tm, k), lambda i, j: (i, 0)),
                  pl.BlockSpec((k, tn), lambda i, j: (0, j))],
        out_specs=pl.BlockSpec((tm, tn), lambda i, j: (i, j)),
        out_shape=jax.ShapeDtypeStruct((m, nc), out_dtype),
        compiler_params=_cparams(2), name="proj",
    )(x, w)


def _layer_norm(z, g, b):
    mu = jnp.mean(z, axis=-1, keepdims=True)
    zc = z - mu
    var = jnp.mean(zc * zc, axis=-1, keepdims=True)
    return zc * lax.rsqrt(var + LN_EPS) * g + b


def _out_ln_body(*refs, n_sum):
    a0 = refs[0][...].astype(F32)
    for r in refs[1:n_sum]:
        a0 = a0 + r[...].astype(F32)
    a1_ref, w0_ref, w1_ref, r_ref, g_ref, b_ref, of_ref, ob_ref = refs[n_sum:]
    y = jnp.dot(a0.astype(BF16), w0_ref[...], preferred_element_type=F32)
    y = y + jnp.dot(a1_ref[...], w1_ref[...], preferred_element_type=F32)
    out = _layer_norm(ALPHA * r_ref[...] + y, g_ref[...], b_ref[...])
    of_ref[...] = out
    ob_ref[...] = out.astype(BF16)


def _out_ln(a0s, a1, w0, w1, resid, g, b, tm=256):
    n, d = resid.shape
    k0, k1 = w0.shape[0], w1.shape[0]
    row = lambda i: (i, 0)
    fixed = lambda i: (0, 0)
    return pl.pallas_call(
        functools.partial(_out_ln_body, n_sum=len(a0s)),
        grid=(n // tm,),
        in_specs=[pl.BlockSpec((tm, k0), row)] * len(a0s) + [
            pl.BlockSpec((tm, k1), row), pl.BlockSpec((k0, d), fixed), pl.BlockSpec((k1, d), fixed),
            pl.BlockSpec((tm, d), row), pl.BlockSpec((1, d), fixed), pl.BlockSpec((1, d), fixed)],
        out_specs=[pl.BlockSpec((tm, d), row), pl.BlockSpec((tm, d), row)],
        out_shape=[jax.ShapeDtypeStruct((n, d), F32), jax.ShapeDtypeStruct((n, d), BF16)],
        compiler_params=_cparams(1), name="out_proj_ln",
    )(*a0s, a1, w0, w1, resid, g, b)


def _rms(x, g):
    return x * lax.rsqrt(jnp.mean(x * x, axis=-1, keepdims=True) + RMS_EPS) * g


def _mla_q_body(c_ref, g_ref, w_ref, wr_ref, cos_ref, sin_ref, o_ref):
    cn = _rms(c_ref[...], g_ref[...]).astype(BF16)
    a = jnp.dot(cn, w_ref[...], preferred_element_type=F32)
    r = jnp.dot(cn, wr_ref[...], preferred_element_type=F32)
    cos, sin = cos_ref[...], sin_ref[...]
    for h in range(H_A):
        sl = slice(h * LANE, (h + 1) * LANE)
        o_ref[:, sl] = (a[:, sl] * cos + r[:, sl] * sin).astype(o_ref.dtype)


def _mla_kv_body(c_ref, kr_ref, krr_ref, g_ref, wk_ref, wv_ref, cos_ref, sin_ref, k_ref, v_ref):
    cn = _rms(c_ref[...], g_ref[...]).astype(BF16)
    kn = jnp.dot(cn, wk_ref[...], preferred_element_type=F32)
    rope = kr_ref[...] * cos_ref[...] + krr_ref[...] * sin_ref[...]
    for h in range(H_A):
        sl = slice(h * LANE, (h + 1) * LANE)
        k_ref[:, sl] = (kn[:, sl] + rope).astype(k_ref.dtype)
    v_ref[...] = jnp.dot(cn, wv_ref[...], preferred_element_type=F32).astype(v_ref.dtype)


def _mla_up(h1, q_norm, wq, wq_rot, kv_norm, wk, wv, cos_q, sin_q, cos_k, sin_k, seq, tm=512):
    n = h1.shape[0]
    spt = seq // tm
    row = lambda c: (lambda i: (i, c))
    pos = lambda i: (i % spt, 0)
    fixed = lambda i: (0, 0)
    q_a = pl.pallas_call(
        _mla_q_body, grid=(n // tm,),
        in_specs=[pl.BlockSpec((tm, Q_LORA), row(0)), pl.BlockSpec((1, Q_LORA), fixed),
                  pl.BlockSpec(wq.shape, fixed), pl.BlockSpec(wq_rot.shape, fixed),
                  pl.BlockSpec((tm, LANE), pos), pl.BlockSpec((tm, LANE), pos)],
        out_specs=pl.BlockSpec((tm, H_A * LANE), row(0)),
        out_shape=jax.ShapeDtypeStruct((n, H_A * LANE), BF16),
        compiler_params=_cparams(1), name="mla_q_up",
    )(h1, q_norm, wq, wq_rot, cos_q, sin_q)
    k_a, v_a = pl.pallas_call(
        _mla_kv_body, grid=(n // tm,),
        in_specs=[pl.BlockSpec((tm, LANE), row(2)), pl.BlockSpec((tm, LANE), row(3)),
                  pl.BlockSpec((tm, LANE), row(4)), pl.BlockSpec((1, KV_LORA), fixed),
                  pl.BlockSpec(wk.shape, fixed), pl.BlockSpec(wv.shape, fixed),
                  pl.BlockSpec((tm, LANE), pos), pl.BlockSpec((tm, LANE), pos)],
        out_specs=[pl.BlockSpec((tm, H_A * LANE), row(0)), pl.BlockSpec((tm, H_A * MLA_V), row(0))],
        out_shape=[jax.ShapeDtypeStruct((n, H_A * LANE), BF16), jax.ShapeDtypeStruct((n, H_A * MLA_V), BF16)],
        compiler_params=_cparams(1), name="mla_kv_up",
    )(h1, h1, h1, kv_norm, wk, wv, cos_k, sin_k)
    return q_a, k_a, v_a


def _flash_body(*refs, T, dq, dv, q_offs, k_offs, v_offs, scale, bias_mode, has_sel, has_gate,
                epilogue, want_lse, seg_tiles, nback, lam_init):
    it = iter(refs)
    q_ref, k_ref, v_ref = next(it), next(it), next(it)
    bias_ref = next(it) if bias_mode else None
    sel_ref, e_ref = (next(it), next(it)) if has_sel else (None, None)
    gate_ref = next(it) if has_gate else None
    lam_ref, dn_ref = (next(it), next(it)) if epilogue == "diff" else (None, None)
    o_ref = next(it)
    lse_ref = next(it) if want_lse else None
    m_scr, l_scr, acc_scr = next(it), next(it), next(it)

    qi = pl.program_id(2)
    m_scr[...] = jnp.full(m_scr.shape, NEG, F32)
    l_scr[...] = jnp.zeros(l_scr.shape, F32)
    acc_scr[...] = jnp.zeros(acc_scr.shape, F32)
    qfull = q_ref[0]
    qs = [qfull[:, off:off + dq] for off in q_offs]
    selt = sel_ref[0, 0] if has_sel else None

    def chunk(kc, diag):
        start = pl.multiple_of(kc * T, T)
        keep = None
        if has_sel:
            keep = jnp.dot(selt, e_ref[:, pl.ds(start, T)], preferred_element_type=F32) > 0.5
        kfull = k_ref[0, pl.ds(start, T), :]
        vfull = v_ref[0, pl.ds(start, T), :]
        for u in range(2):
            k = kfull[:, k_offs[u]:k_offs[u] + dq]
            v = vfull[:, v_offs[u]:v_offs[u] + dv]
            s = lax.dot_general(qs[u], k, _NT, preferred_element_type=F32) * scale
            if bias_mode:
                s = s + bias_ref[u if bias_mode == "pair" else 0, qi - kc]
            elif diag:
                row = lax.broadcasted_iota(jnp.int32, (T, T), 0)
                col = lax.broadcasted_iota(jnp.int32, (T, T), 1)
                s = jnp.where(row >= col, s, NEG)
            if has_sel:
                s = jnp.where(keep, s, NEG)
            m_prev = m_scr[u]
            m_new = jnp.maximum(m_prev, jnp.max(s, axis=-1, keepdims=True))
            alpha = jnp.exp(m_prev - m_new)
            p = jnp.exp(s - m_new)
            l_scr[u] = alpha * l_scr[u] + jnp.sum(p, axis=-1, keepdims=True)
            acc_scr[u] = alpha * acc_scr[u] + jnp.dot(p.astype(BF16), v, preferred_element_type=F32)
            m_scr[u] = m_new

    def loop_body(kc, carry):
        chunk(kc, False)
        return carry

    if bias_mode:
        if nback is None:
            lo = 0
        elif seg_tiles is not None:
            lo = jnp.where(qi % seg_tiles == 0, qi, qi - nback)
        else:
            lo = jnp.maximum(qi - nback, 0)
        lax.fori_loop(lo, qi + 1, loop_body, 0)
    else:
        lax.fori_loop(0, qi, loop_body, 0)
        chunk(qi, True)

    outs = [acc_scr[u] / l_scr[u] for u in range(2)]
    if epilogue == "diff":
        a = outs[0] - lam_ref[0, 0] * outs[1]
        o = _rms(a, dn_ref[...]) * (1.0 - lam_init)
    else:
        o = jnp.concatenate(outs, axis=-1)
        if has_gate:
            o = o * gate_ref[0]
    o_ref[0] = o.astype(o_ref.dtype)
    if want_lse:
        lse = [jnp.broadcast_to(m_scr[u] + jnp.log(l_scr[u]), (T, dv)) for u in range(2)]
        lse_ref[0] = jnp.concatenate(lse, axis=-1)


def _flash(q, k, v, *, n_outer, T, dq, dv, q_col, k_col, v_col, q_offs, k_offs, v_offs, scale,
           out_cols, out_col, bias=None, bias_mode=None, bias_idx=None, sel=None, emat=None, sel_idx=None,
           gate=None, gate_col=None, lam=None, dnorm=None, lam_init=0.0, epilogue="plain", want_lse=False,
           seg_tiles=None, nback=None, vmem=None, k_w=None, v_w=None, name="flash"):
    bsz, seq, _ = q.shape
    nq = seq // T
    qw = max(o + dq for o in q_offs)
    kw = k_w or max(o + dq for o in k_offs)
    vw = v_w or max(o + dv for o in v_offs)
    ow = dv if epilogue == "diff" else 2 * dv
    in_specs = [pl.BlockSpec((1, T, qw), lambda g, b, i: (b, i, q_col(g))),
                pl.BlockSpec((1, seq, kw), lambda g, b, i: (b, 0, k_col(g))),
                pl.BlockSpec((1, seq, vw), lambda g, b, i: (b, 0, v_col(g)))]
    args = [q, k, v]
    if bias_mode:
        nb = 2 if bias_mode == "pair" else 1
        in_specs.append(pl.BlockSpec((nb,) + bias.shape[1:], lambda g, b, i: (bias_idx(g), 0, 0, 0)))
        args.append(bias)
    if sel is not None:
        in_specs.append(pl.BlockSpec((1, 1, T, sel.shape[-1]), lambda g, b, i: (b, sel_idx(g), i, 0)))
        in_specs.append(pl.BlockSpec(emat.shape, lambda g, b, i: (0, 0)))
        args += [sel, emat]
    if gate is not None:
        in_specs.append(pl.BlockSpec((1, T, ow), lambda g, b, i: (b, i, gate_col(g))))
        args.append(gate)
    if epilogue == "diff":
        in_specs.append(pl.BlockSpec(memory_space=pltpu.SMEM))
        in_specs.append(pl.BlockSpec((1, dv), lambda g, b, i: (0, 0)))
        args += [lam, dnorm]
    out_specs = [pl.BlockSpec((1, T, ow), lambda g, b, i: (b, i, out_col(g)))]
    out_shape = [jax.ShapeDtypeStruct((bsz, seq, out_cols), BF16)]
    if want_lse:
        out_specs.append(pl.BlockSpec((1, T, ow), lambda g, b, i: (b, i, out_col(g))))
        out_shape.append(jax.ShapeDtypeStruct((bsz, seq, out_cols), F32))
    body = functools.partial(
        _flash_body, T=T, dq=dq, dv=dv, q_offs=q_offs, k_offs=k_offs, v_offs=v_offs, scale=scale,
        bias_mode=bias_mode, has_sel=sel is not None, has_gate=gate is not None, epilogue=epilogue,
        want_lse=want_lse, seg_tiles=seg_tiles, nback=nback, lam_init=lam_init)
    res = pl.pallas_call(
        body, grid=(n_outer, bsz, nq), in_specs=in_specs, out_specs=out_specs, out_shape=out_shape,
        scratch_shapes=[pltpu.VMEM((2, T, 1), F32), pltpu.VMEM((2, T, 1), F32), pltpu.VMEM((2, T, dv), F32)],
        compiler_params=_cparams(3, vmem), name=name,
    )(*args)
    return res if want_lse else res[0]


def _rel_bucket(dist):
    n = jnp.maximum(dist, 0)
    exact = N_BUCKETS // 2
    log_ratio = jnp.log(jnp.maximum(n, 1).astype(F32) / exact) / math.log(REL_MAX_DIST / exact)
    large = exact + (log_ratio * (N_BUCKETS - exact)).astype(jnp.int32)
    return jnp.where(n < exact, n, jnp.minimum(large, N_BUCKETS - 1))


def _toeplitz_bias(tab, T, n_d, dist_scale, max_dist):
    x = jnp.arange(2 * T)
    dist = jnp.arange(n_d)[:, None] * T - jnp.where(x < T, x, x - 2 * T)[None, :]
    w = tab[_rel_bucket(dist * dist_scale)].astype(F32)
    w = jnp.where(((dist >= 0) & (dist <= max_dist))[..., None], w, NEG)
    return _toeplitz(jnp.moveaxis(w, -1, 0), T)


def _toeplitz(w, t):
    tiled = jnp.tile(w, (1,) * (w.ndim - 1) + (t,))[..., :t * (2 * t - 1)]
    return tiled.reshape(w.shape[:-1] + (t, 2 * t - 1))[..., :t]


def _compress_body(u_ref, pe_ref, w1_ref, w2_ref, o_ref, *, ncp):
    outs = []
    for a in range(2):
        u = u_ref[0, 0, a].astype(F32)
        p1 = jnp.dot(u + pe_ref[a, 0], w1_ref[a, 0], preferred_element_type=F32, precision=HI)
        p2 = jnp.dot(u + pe_ref[a, 1], w1_ref[a, 1], preferred_element_type=F32, precision=HI)
        hid = jax.nn.gelu(p1 + pltpu.roll(p2, ncp - 1, 0))
        outs.append(jnp.dot(hid, w2_ref[a], preferred_element_type=F32, precision=HI))
    o_ref[0, 0] = jnp.concatenate(outs, axis=-1)


def _compress(u, pe, w1, w2):
    bsz, g, _, ncp, width = u.shape
    return pl.pallas_call(
        functools.partial(_compress_body, ncp=ncp), grid=(bsz, g),
        in_specs=[pl.BlockSpec((1, 1, 2, ncp, width), lambda b, gg: (b, gg, 0, 0, 0)),
                  pl.BlockSpec(pe.shape, lambda b, gg: (0, 0, 0, 0)),
                  pl.BlockSpec(w1.shape, lambda b, gg: (0, 0, 0, 0)),
                  pl.BlockSpec(w2.shape, lambda b, gg: (0, 0, 0))],
        out_specs=pl.BlockSpec((1, 1, ncp, 2 * DK_C), lambda b, gg: (b, gg, 0, 0)),
        out_shape=jax.ShapeDtypeStruct((bsz, g, ncp, 2 * DK_C), F32),
        compiler_params=_cparams(2), name="nsa_compress",
    )(u, pe, w1, w2)


def _cmp_attn_body(q_ref, kv_ref, bias_ref, ov_ref, gate_ref, o_ref, sel_ref, *, T, ncp, n_sel, n_top, scale):
    qi = pl.program_id(2)
    kc = kv_ref[0, 0, :, :DK_C]
    vc = kv_ref[0, 0, :, DK_C:]
    t = qi * T + lax.broadcasted_iota(jnp.int32, (T, ncp), 0)
    c = lax.broadcasted_iota(jnp.int32, (T, ncp), 1)
    valid = t >= c * CMP_STRIDE + (CMP_LEN - 1)
    validf = valid.astype(F32)
    psum = jnp.zeros((T, ncp), F32)
    outs = []
    for r in range(R_C):
        q = q_ref[0, :, r * DK_C:(r + 1) * DK_C].astype(F32)
        s = lax.dot_general(q, kc, _NT, preferred_element_type=F32, precision=HI) * scale + bias_ref[r]
        s = jnp.where(valid, s, NEG)
        e = jnp.exp(s - jnp.max(s, axis=-1, keepdims=True)) * validf
        p = e / jnp.maximum(jnp.sum(e, axis=-1, keepdims=True), 1e-30)
        outs.append(jnp.dot(p, vc, preferred_element_type=F32, precision=HI))
        psum = psum + p
    o_ref[0] = (jnp.concatenate(outs, axis=-1) * gate_ref[0]).astype(o_ref.dtype)

    imp = jnp.dot(psum, ov_ref[...], preferred_element_type=F32, precision=HI)
    tq = qi * T + lax.broadcasted_iota(jnp.int32, (T, n_sel), 0)
    j = lax.broadcasted_iota(jnp.int32, (T, n_sel), 1)
    forced = (j == tq // SEL_BLOCK) | (j == 0)
    work = jnp.where(forced, BIG, jnp.where(j * SEL_BLOCK <= tq, imp, -BIG))
    sel = jnp.zeros((T, n_sel), F32)
    jf = j.astype(F32)
    for _ in range(n_top):
        mx = jnp.max(work, axis=-1, keepdims=True)
        first = jnp.min(jnp.where(work == mx, jf, float(n_sel)), axis=-1, keepdims=True)
        pick = jf == first
        sel = jnp.where(pick, 1.0, sel)
        work = jnp.where(pick, -3e38, work)
    sel_ref[0, 0] = sel.astype(sel_ref.dtype)


def _cmp_attn(h, kvc, bias_c, overlap, gates, T=256):
    bsz, seq, _ = h.shape
    ncp = kvc.shape[2]
    n_sel = seq // SEL_BLOCK
    n_top = min(SEL_TOP, n_sel)
    qw = R_C * DK_C
    return pl.pallas_call(
        functools.partial(_cmp_attn_body, T=T, ncp=ncp, n_sel=n_sel, n_top=n_top, scale=DK_C ** -0.5),
        grid=(G_C, bsz, seq // T),
        in_specs=[pl.BlockSpec((1, T, qw), lambda g, b, i: (b, i, g)),
                  pl.BlockSpec((1, 1, ncp, 2 * DK_C), lambda g, b, i: (b, g, 0, 0)),
                  pl.BlockSpec((R_C, T, ncp), lambda g, b, i: (g, i, 0)),
                  pl.BlockSpec(overlap.shape, lambda g, b, i: (0, 0)),
                  pl.BlockSpec((1, T, qw), lambda g, b, i: (b, i, g))],
        out_specs=[pl.BlockSpec((1, T, qw), lambda g, b, i: (b, i, g)),
                   pl.BlockSpec((1, 1, T, n_sel), lambda g, b, i: (b, g, i, 0))],
        out_shape=[jax.ShapeDtypeStruct((bsz, seq, H_C * DK_C), BF16),
                   jax.ShapeDtypeStruct((bsz, G_C, seq, n_sel), BF16)],
        compiler_params=_cparams(3), name="nsa_cmp_attn",
    )(h, kvc, bias_c, overlap, gates)


def _first_max(work, idx, n):
    mx = jnp.max(work, axis=0, keepdims=True)
    first = jnp.min(jnp.where(work == mx, idx, float(n)), axis=0, keepdims=True)
    return mx, first, idx == first


def _router_body(x_ref, wt_ref, b_ref, tri_ref, e_ref, g_ref, r_ref, cnt_ref, carry_scr, *, tm):
    i = pl.program_id(0)

    @pl.when(i == 0)
    def _():
        carry_scr[...] = jnp.zeros(carry_scr.shape, F32)

    st = lax.dot_general(wt_ref[...], x_ref[...], _NT, preferred_element_type=F32, precision=HI)
    scores = jax.nn.sigmoid(st)
    sel = scores + b_ref[...]
    per = N_EXPERTS // N_EXPERT_GROUPS
    fiota = lambda rows: lax.broadcasted_iota(jnp.int32, (rows, tm), 0).astype(F32)
    i_per, i_grp, i_exp = fiota(per), fiota(N_EXPERT_GROUPS), fiota(N_EXPERTS)
    grp_scores = []
    for g in range(N_EXPERT_GROUPS):
        blk = sel[g * per:(g + 1) * per]
        m1, _, pick = _first_max(blk, i_per, per)
        grp_scores.append(m1 + jnp.max(jnp.where(pick, -jnp.inf, blk), axis=0, keepdims=True))
    work = jnp.concatenate(grp_scores, axis=0)
    gmask = jnp.zeros((N_EXPERT_GROUPS, tm), F32)
    for _ in range(TOPK_GROUPS):
        _, _, pick = _first_max(work, i_grp, N_EXPERT_GROUPS)
        gmask = jnp.where(pick, 1.0, gmask)
        work = jnp.where(pick, -jnp.inf, work)
    work = jnp.concatenate([jnp.where(gmask[g:g + 1] > 0.5, sel[g * per:(g + 1) * per], NEG)
                            for g in range(N_EXPERT_GROUPS)], axis=0)
    picks, firsts, vals = [], [], []
    for _ in range(TOP_K):
        _, first, pick = _first_max(work, i_exp, N_EXPERTS)
        picks.append(pick)
        firsts.append(first)
        vals.append(jnp.sum(jnp.where(pick, scores, 0.0), axis=0, keepdims=True))
        work = jnp.where(pick, -jnp.inf, work)
    val = jnp.concatenate(vals, axis=0)
    g_ref[...] = val / jnp.sum(val, axis=0, keepdims=True) * ROUTED_SCALE
    e_ref[...] = jnp.concatenate(firsts, axis=0).astype(jnp.int32)
    onehot = picks[0].astype(F32)
    for pick in picks[1:]:
        onehot = onehot + pick.astype(F32)
    before = jnp.dot(onehot.astype(BF16), tri_ref[...], preferred_element_type=F32) + carry_scr[...]
    r_ref[...] = jnp.concatenate([jnp.sum(jnp.where(pick, before, 0.0), axis=0, keepdims=True)
                                  for pick in picks], axis=0).astype(jnp.int32)
    carry = carry_scr[...] + jnp.sum(onehot, axis=1, keepdims=True)
    carry_scr[...] = carry
    cnt_ref[...] = jnp.broadcast_to(carry, cnt_ref.shape)


def _router(xf, router_w, router_b, tm=256):
    n, d = xf.shape
    tri = (jnp.arange(tm)[:, None] < jnp.arange(tm)[None, :]).astype(BF16)
    col = lambda i: (0, i)
    fixed = lambda i: (0, 0)
    return pl.pallas_call(
        functools.partial(_router_body, tm=tm), grid=(n // tm,),
        in_specs=[pl.BlockSpec((tm, d), lambda i: (i, 0)), pl.BlockSpec((N_EXPERTS, d), fixed),
                  pl.BlockSpec((N_EXPERTS, 1), fixed), pl.BlockSpec((tm, tm), fixed)],
        out_specs=[pl.BlockSpec((TOP_K, tm), col), pl.BlockSpec((TOP_K, tm), col),
                   pl.BlockSpec((TOP_K, tm), col), pl.BlockSpec((N_EXPERTS, LANE), fixed)],
        out_shape=[jax.ShapeDtypeStruct((TOP_K, n), jnp.int32), jax.ShapeDtypeStruct((TOP_K, n), F32),
                   jax.ShapeDtypeStruct((TOP_K, n), jnp.int32), jax.ShapeDtypeStruct((N_EXPERTS, LANE), F32)],
        scratch_shapes=[pltpu.VMEM((N_EXPERTS, 1), F32)],
        compiler_params=_cparams(1), name="router",
    )(xf, router_w.T, router_b.astype(F32)[:, None], tri)


def _dispatch_body(ps_ref, fill_ref, e_ref, r_ref, x_hbm, z_ref, xs_hbm, sem, *, tm, n_blocks):
    i = pl.program_id(0)

    def row_copy(src, dst):
        return pltpu.make_async_copy(x_hbm.at[pl.ds(src, 1)], xs_hbm.at[pl.ds(dst, 1)], sem)

    def zero_copy(dst):
        return pltpu.make_async_copy(z_ref.at[pl.ds(0, 1)], xs_hbm.at[pl.ds(dst, 1)], sem)

    def zero_block(blk):
        return pltpu.make_async_copy(z_ref, xs_hbm.at[pl.ds(blk * MOE_BM, MOE_BM)], sem)

    def issue(t, c):
        for k in range(TOP_K):
            row_copy(i * tm + t, ps_ref[e_ref[k, t]] + r_ref[k, t]).start()
        return c

    def drain(t, c):
        for k in range(TOP_K):
            row_copy(0, 0).wait()
        return c

    lax.fori_loop(0, tm, issue, 0)
    lax.fori_loop(0, tm, drain, 0)

    @pl.when(i == 0)
    def _():
        def fill(e, c):
            lo, hi = fill_ref[2 * e], fill_ref[2 * e + 1]

            def start(r, cc):
                zero_copy(r).start()
                return cc

            def wait(r, cc):
                zero_copy(0).wait()
                return cc

            lax.fori_loop(lo, hi, start, 0)
            lax.fori_loop(lo, hi, wait, 0)
            return c

        lax.fori_loop(0, N_EXPERTS, fill, 0)

        def tail_start(blk, c):
            zero_block(blk).start()
            return c

        def tail_wait(blk, c):
            zero_block(0).wait()
            return c

        lax.fori_loop(fill_ref[2 * N_EXPERTS], n_blocks, tail_start, 0)
        lax.fori_loop(fill_ref[2 * N_EXPERTS], n_blocks, tail_wait, 0)


def _dispatch(xf, e_idx, rank, pad_start, fill, p, tm=128):
    n, d = xf.shape
    smem_col = pl.BlockSpec((TOP_K, tm), lambda i, ps, fl: (0, i), memory_space=pltpu.SMEM)
    grid_spec = pltpu.PrefetchScalarGridSpec(
        num_scalar_prefetch=2, grid=(n // tm,),
        in_specs=[smem_col, smem_col, pl.BlockSpec(memory_space=pl.ANY),
                  pl.BlockSpec((MOE_BM, d), lambda i, ps, fl: (0, 0))],
        out_specs=pl.BlockSpec(memory_space=pl.ANY),
        scratch_shapes=[pltpu.SemaphoreType.DMA(())])
    return pl.pallas_call(
        functools.partial(_dispatch_body, tm=tm, n_blocks=p // MOE_BM), grid_spec=grid_spec,
        out_shape=jax.ShapeDtypeStruct((p, d), xf.dtype),
        compiler_params=_cparams(1), name="dispatch",
    )(pad_start, fill, e_idx, rank, xf, jnp.zeros((MOE_BM, d), xf.dtype))


def _moe_ffn_body(be_ref, nb_ref, x_ref, wg_ref, wu_ref, wd_ref, o_ref):
    i = pl.program_id(0)

    @pl.when(i < nb_ref[0])
    def _():
        x = x_ref[...].astype(BF16)
        hg = jnp.dot(x, wg_ref[0], preferred_element_type=F32)
        hu = jnp.dot(x, wu_ref[0], preferred_element_type=F32)
        hb = (hg * jax.nn.sigmoid(hg) * hu).astype(BF16)
        o_ref[...] = jnp.dot(hb, wd_ref[0], preferred_element_type=F32).astype(o_ref.dtype)

    @pl.when(i >= nb_ref[0])
    def _():
        o_ref[...] = jnp.zeros(o_ref.shape, o_ref.dtype)


def _moe_ffn(xs, blk_e, n_used, wg, wu, wd):
    p, d = xs.shape
    n_blocks = p // MOE_BM
    grid_spec = pltpu.PrefetchScalarGridSpec(
        num_scalar_prefetch=2, grid=(n_blocks,),
        in_specs=[pl.BlockSpec((MOE_BM, d), lambda i, be, nb: (i, 0)),
                  pl.BlockSpec((1, d, D_EXPERT), lambda i, be, nb: (be[i], 0, 0)),
                  pl.BlockSpec((1, d, D_EXPERT), lambda i, be, nb: (be[i], 0, 0)),
                  pl.BlockSpec((1, D_EXPERT, d), lambda i, be, nb: (be[i], 0, 0))],
        out_specs=pl.BlockSpec((MOE_BM, d), lambda i, be, nb: (i, 0)))
    return pl.pallas_call(
        _moe_ffn_body, grid_spec=grid_spec, out_shape=jax.ShapeDtypeStruct((p, d), F32),
        compiler_params=_cparams(1), name="expert_ffn",
    )(blk_e, n_used, xs, wg, wu, wd)


def _combine_body(ps_ref, e_ref, r_ref, gate_ref, xb_ref, xf_ref, y_hbm, sg_ref, su_ref, sd_ref, g_ref, b_ref,
                  of_ref, ob_ref, buf, sem, *, tm):
    def row_copy(k, t, src):
        return pltpu.make_async_copy(y_hbm.at[pl.ds(src, 1)], buf.at[k, pl.ds(t, 1)], sem)

    def issue(t, c):
        for k in range(TOP_K):
            row_copy(k, t, ps_ref[e_ref[k, t]] + r_ref[k, t]).start()
        return c

    def drain(t, c):
        for k in range(TOP_K):
            row_copy(k, t, 0).wait()
        return c

    lax.fori_loop(0, tm, issue, 0)
    x = xb_ref[...]
    hg = jnp.dot(x, sg_ref[...], preferred_element_type=F32)
    hu = jnp.dot(x, su_ref[...], preferred_element_type=F32)
    hb = (hg * jax.nn.sigmoid(hg) * hu).astype(BF16)
    y = jnp.dot(hb, sd_ref[...], preferred_element_type=F32)
    lax.fori_loop(0, tm, drain, 0)
    gate = gate_ref[...]
    for k in range(TOP_K):
        y = y + gate[:, k:k + 1] * buf[k]
    out = _layer_norm(ALPHA * xf_ref[...] + y, g_ref[...], b_ref[...])
    of_ref[...] = out
    ob_ref[...] = out.astype(BF16)


def _combine(pad_start, e_idx, rank, gate, xb, xf, y, sg, su, sd, g, b, tm=128):
    n, d = xf.shape
    row = lambda i, ps: (i, 0)
    fixed = lambda i, ps: (0, 0)
    smem_col = pl.BlockSpec((TOP_K, tm), lambda i, ps: (0, i), memory_space=pltpu.SMEM)
    grid_spec = pltpu.PrefetchScalarGridSpec(
        num_scalar_prefetch=1, grid=(n // tm,),
        in_specs=[smem_col, smem_col, pl.BlockSpec((tm, TOP_K), row), pl.BlockSpec((tm, d), row),
                  pl.BlockSpec((tm, d), row), pl.BlockSpec(memory_space=pl.ANY),
                  pl.BlockSpec(sg.shape, fixed), pl.BlockSpec(su.shape, fixed), pl.BlockSpec(sd.shape, fixed),
                  pl.BlockSpec((1, d), fixed), pl.BlockSpec((1, d), fixed)],
        out_specs=[pl.BlockSpec((tm, d), row), pl.BlockSpec((tm, d), row)],
        scratch_shapes=[pltpu.VMEM((TOP_K, tm, d), F32), pltpu.SemaphoreType.DMA(())])
    return pl.pallas_call(
        functools.partial(_combine_body, tm=tm), grid_spec=grid_spec,
        out_shape=[jax.ShapeDtypeStruct((n, d), F32), jax.ShapeDtypeStruct((n, d), BF16)],
        compiler_params=_cparams(1), name="combine",
    )(pad_start, e_idx, rank, gate, xb, xf, y, sg, su, sd, g, b)


def _moe(xf, xb, router_w, router_b, w_gate, w_up, w_down, sh_gate, sh_up, sh_down, ln_g, ln_b):
    n, d = xf.shape
    e_idx, gate, rank, cnt = _router(xf, router_w, router_b)
    counts = cnt[:, 0].astype(jnp.int32)
    padded = (counts + MOE_BM - 1) // MOE_BM * MOE_BM
    pad_end = jnp.cumsum(padded)
    pad_start = (pad_end - padded).astype(jnp.int32)
    n_blocks = (n * TOP_K + N_EXPERTS * (MOE_BM - 1) + MOE_BM - 1) // MOE_BM
    blk_e = jnp.minimum(jnp.sum(pad_end[None, :] <= (jnp.arange(n_blocks) * MOE_BM)[:, None], axis=1),
                        N_EXPERTS - 1).astype(jnp.int32)
    n_used = (pad_end[-1] // MOE_BM).astype(jnp.int32).reshape(1)
    fill = jnp.concatenate([jnp.stack([pad_start + counts, pad_end], axis=1).reshape(-1), n_used]).astype(jnp.int32)
    xs = _dispatch(xf, e_idx, rank, pad_start, fill, n_blocks * MOE_BM)
    y = _moe_ffn(xs, blk_e, n_used, w_gate, w_up, w_down)
    return _combine(pad_start, e_idx, rank, gate.T, xb, xf, y, sh_gate, sh_up, sh_down, ln_g[None], ln_b[None])


def _rope_tables(seq):
    half = ROPE_DIM // 2
    freqs = ROPE_THETA ** (-jnp.arange(half, dtype=F32) / half)
    ang = jnp.arange(seq).astype(F32)[:, None] * freqs
    cos = jnp.concatenate([jnp.cos(ang)] * 2, -1)
    sin = jnp.concatenate([jnp.sin(ang)] * 2, -1)
    z = lambda w: jnp.zeros((seq, w), F32)
    pad = LANE - NOPE - ROPE_DIM
    cos_q = jnp.concatenate([jnp.ones((seq, NOPE), F32), cos, z(pad)], -1)
    sin_q = jnp.concatenate([z(NOPE), sin, z(pad)], -1)
    cos_k = jnp.concatenate([z(NOPE), cos, z(pad)], -1)
    return cos_q, sin_q, cos_k, sin_q


def _rot_cols(w):
    half = w.shape[-1] // 2
    return jnp.concatenate([-w[..., half:], w[..., :half]], -1)


def _mixer_ab(xb, xf, bsz, seq, w_in, q_norm, w_uq, kv_norm, w_ukv, w_out, ln_g, ln_b, rope_tabs, dil_bias):
    n = bsz * seq
    d = w_in.shape[0]
    c0 = Q_LORA + KV_LORA
    w_kr = w_in[:, c0:c0 + ROPE_DIM]
    zc = lambda w: jnp.zeros((d, w), F32)
    pad = LANE - NOPE - ROPE_DIM
    w1 = jnp.concatenate([w_in[:, :c0], zc(NOPE), w_kr, zc(pad), zc(NOPE), _rot_cols(w_kr), zc(pad)], 1)
    h1 = _mm(xb, w1.astype(BF16), F32, tn=LANE)
    h2 = _mm(xb, w_in[:, c0 + ROPE_DIM:].astype(BF16), BF16, tn=256)

    wq = w_uq.reshape(Q_LORA, H_A, NOPE + ROPE_DIM)
    zq = jnp.zeros((Q_LORA, H_A, pad), F32)
    wq_main = jnp.concatenate([wq, zq], -1).reshape(Q_LORA, H_A * LANE)
    wq_rot = jnp.concatenate([jnp.zeros((Q_LORA, H_A, NOPE), F32), _rot_cols(wq[..., NOPE:]), zq], -1)
    wq_rot = wq_rot.reshape(Q_LORA, H_A * LANE)
    wkv = w_ukv.reshape(KV_LORA, H_A, NOPE + MLA_V)
    wk = jnp.concatenate([wkv[..., :NOPE], jnp.zeros((KV_LORA, H_A, LANE - NOPE), F32)], -1)
    wk = wk.reshape(KV_LORA, H_A * LANE)
    wv = wkv[..., NOPE:].reshape(KV_LORA, H_A * MLA_V)
    q_a, k_a, v_a = _mla_up(h1, q_norm[None], wq_main.astype(BF16), wq_rot.astype(BF16), kv_norm[None],
                            wk.astype(BF16), wv.astype(BF16), *rope_tabs, seq)
    o_a = _flash(q_a.reshape(bsz, seq, -1), k_a.reshape(bsz, seq, -1), v_a.reshape(bsz, seq, -1), name="mla_attn",
                 n_outer=H_A // 2, T=256, dq=LANE, dv=MLA_V,
                 q_col=lambda g: g, k_col=lambda g: g, v_col=lambda g: g,
                 q_offs=(0, LANE), k_offs=(0, LANE), v_offs=(0, MLA_V),
                 scale=(NOPE + ROPE_DIM) ** -0.5, out_cols=H_A * MLA_V, out_col=lambda g: g)

    gw = H_B_GROUP * HD_B
    hq = h2.reshape(bsz, seq, 3, len(DIL_PAIRS), gw)
    outs, lses = [], []
    for gi, (window, dil) in enumerate(DIL_PAIRS):
        L = seq // dil
        t = hq[:, :, :, gi].reshape(bsz, L, dil, 3 * gw).transpose(0, 2, 1, 3).reshape(bsz, seq, 3 * gw)
        T = min(256, L)
        o, lse = _flash(t, t, t, name="dilated_attn", n_outer=2, T=T, dq=HD_B, dv=HD_B,
                        q_col=lambda g: g, k_col=lambda g: 2 + g, v_col=lambda g: 4 + g,
                        q_offs=(0, HD_B), k_offs=(0, HD_B), v_offs=(0, HD_B), scale=HD_B ** -0.5,
                        out_cols=gw, out_col=lambda g: g, bias=dil_bias[gi], bias_mode="pair",
                        bias_idx=lambda g: g, want_lse=True, seg_tiles=L // T, nback=1)
        unstride = lambda a: a.reshape(bsz, dil, L, -1).transpose(0, 2, 1, 3).reshape(bsz, seq, -1)
        outs.append(unstride(o).astype(F32))
        lses.append(unstride(lse))
    w = jax.nn.softmax(jnp.stack(lses), axis=0)
    o_b = jnp.sum(w * jnp.stack(outs), axis=0).astype(BF16).reshape(n, gw)
    na = H_A * MLA_V
    return _out_ln([o_a.reshape(n, na)], o_b, w_out[:na].astype(BF16), w_out[na:].astype(BF16),
                   xf, ln_g[None], ln_b[None])


def _mixer_cd(xb, xf, bsz, seq, w_in, pos_k, k_w1, k_w2, pos_v, v_w1, v_w2, lq1, lk1, lq2, lk2, d_norm,
              w_out, ln_g, ln_b, lam_init, tabs):
    n = bsz * seq
    qc_w = H_C * DK_C
    kv_w = G_C * DK_C
    off = qc_w
    kvs = []
    for _ in range(3):
        wk_ = w_in[:, off:off + kv_w].reshape(-1, G_C, DK_C)
        wv_ = w_in[:, off + kv_w:off + 2 * kv_w].reshape(-1, G_C, DK_C)
        kvs.append(jnp.concatenate([wk_, wv_], -1).reshape(-1, 2 * kv_w))
        off += 2 * kv_w
    g_off = off
    d_off = off + 3 * H_C
    w_main = jnp.concatenate([w_in[:, :qc_w]] + kvs + [w_in[:, d_off:]], 1)
    h = _mm(xb, w_main.astype(BF16), BF16, tn=256).reshape(bsz, seq, -1)
    w_g = jnp.repeat(w_in[:, g_off:d_off], DK_C, axis=1)
    gates = _mm(xb, w_g.astype(BF16), F32, tn=256, act="sigmoid").reshape(bsz, seq, -1)

    ncp = seq // CMP_STRIDE
    half = CMP_STRIDE * DK_C
    kv_cmp = h[:, :, qc_w:qc_w + 2 * kv_w].reshape(bsz, ncp, CMP_STRIDE, G_C, 2, DK_C)
    u = kv_cmp.transpose(0, 3, 4, 1, 2, 5).reshape(bsz, G_C, 2, ncp, half)
    pe = jnp.stack([pos_k.reshape(2, 1, half), pos_v.reshape(2, 1, half)])
    w1 = jnp.stack([k_w1.reshape(2, half, CMP_HID), v_w1.reshape(2, half, CMP_HID)])
    w2 = jnp.stack([k_w2, v_w2])
    kvc = _compress(u, pe, w1, w2)
    o_cmp, sel = _cmp_attn(h, kvc, tabs["bias_c"], tabs["overlap"], gates)

    cb = qc_w // LANE
    scale = DK_C ** -0.5
    n_pairs = H_C // 2
    nsa = dict(n_outer=n_pairs, dq=DK_C, dv=DK_C, q_col=lambda g: g, q_offs=(0, DK_C), k_offs=(0, 0),
               v_offs=(DK_C, DK_C), scale=scale, out_cols=qc_w, out_col=lambda g: g, bias_mode="pair",
               bias_idx=lambda g: g, gate=gates, vmem=VMEM_LIMIT, k_w=LANE, v_w=LANE)
    o_sel = _flash(h, h, h, name="nsa_sel_attn", T=256, k_col=lambda g: cb + 2 + g // 2, v_col=lambda g: cb + 2 + g // 2,
                   bias=tabs["bias_sel"], sel=sel, emat=tabs["emat"], sel_idx=lambda g: g // 2,
                   gate_col=lambda g: n_pairs + g, **nsa)
    o_win = _flash(h, h, h, name="nsa_win_attn", T=WIN, k_col=lambda g: cb + 4 + g // 2, v_col=lambda g: cb + 4 + g // 2,
                   bias=tabs["bias_win"], gate_col=lambda g: 2 * n_pairs + g, nback=1, **nsa)

    lam = (jnp.exp(jnp.sum(lq1.astype(F32) * lk1.astype(F32)))
           - jnp.exp(jnp.sum(lq2.astype(F32) * lk2.astype(F32))) + lam_init).reshape(1, 1)
    db = cb + 6
    o_d = _flash(h, h, h, name="diff_attn", n_outer=H_D, T=256, dq=DD, dv=2 * DD,
                 q_col=lambda g: db + g, k_col=lambda g: db + H_D + g, v_col=lambda g: db + 2 * H_D + g,
                 q_offs=(0, DD), k_offs=(0, DD), v_offs=(0, 0), scale=DD ** -0.5,
                 out_cols=H_D * 2 * DD, out_col=lambda g: g, bias=tabs["bias_d"], bias_mode="shared",
                 bias_idx=lambda g: g, lam=lam, dnorm=d_norm[None], lam_init=lam_init, epilogue="diff",
                 vmem=VMEM_LIMIT)
    r2 = lambda a: a.reshape(n, -1)
    return _out_ln([r2(o_cmp), r2(o_sel), r2(o_win)], r2(o_d), w_out[:qc_w].astype(BF16),
                   w_out[qc_w:].astype(BF16), xf, ln_g[None], ln_b[None])


def _nsa_tables(rel_bias, seq):
    tab_c = rel_bias[:, H_B:H_B + H_C]
    tab_d = rel_bias[:, H_B + H_C:H_B + H_C + H_D]
    ncp = seq // CMP_STRIDE
    n_sel = seq // SEL_BLOCK
    pos = jnp.arange(seq)
    x = jnp.arange(2 * ncp)
    c_minus_a = jnp.where(x < ncp, x, x - 2 * ncp)
    dist = -CMP_STRIDE * c_minus_a[None, :] + jnp.arange(CMP_STRIDE)[:, None] - (CMP_LEN - 1)
    w = jnp.moveaxis(tab_c[_rel_bucket(dist)].astype(F32), -1, 0)
    bias_c = _toeplitz(w, ncp).transpose(0, 2, 1, 3).reshape(H_C, seq, ncp)
    c0 = jnp.arange(ncp) * CMP_STRIDE
    s0 = jnp.arange(n_sel) * SEL_BLOCK
    overlap = jnp.maximum(jnp.minimum(c0[:, None] + CMP_LEN, s0[None, :] + SEL_BLOCK)
                          - jnp.maximum(c0[:, None], s0[None, :]), 0).astype(F32) / CMP_LEN
    emat = (jnp.arange(n_sel)[:, None] == (pos // SEL_BLOCK)[None, :]).astype(BF16)
    return {
        "bias_c": bias_c, "overlap": overlap, "emat": emat,
        "bias_sel": _toeplitz_bias(tab_c, 256, seq // 256, 1, seq),
        "bias_win": _toeplitz_bias(tab_c, WIN, 2, 1, WIN - 1),
        "bias_d": _toeplitz_bias(tab_d, 256, seq // 256, 1, seq),
    }


def kernel(x, rel_bias, ab_w_in, mla_q_norm, mla_w_uq, mla_kv_norm, mla_w_ukv, ab_w_out, cd_w_in, nsa_cmp_pos_k, nsa_cmp_k_w1, nsa_cmp_k_w2, nsa_cmp_pos_v, nsa_cmp_v_w1, nsa_cmp_v_w2, diff_lambda_q1, diff_lambda_k1, diff_lambda_q2, diff_lambda_k2, diff_norm, cd_w_out, ln1_g, ln1_b, ln2_g, ln2_b, router_w, router_b, exp_w_gate, exp_w_up, exp_w_down, sh_w_gate, sh_w_up, sh_w_down):
    bsz, seq, d = x.shape
    n = bsz * seq
    depth = ln1_g.shape[0]
    rope_tabs = _rope_tables(seq)
    dil_bias = [_toeplitz_bias(rel_bias[:, gi * H_B_GROUP:(gi + 1) * H_B_GROUP], min(256, seq // dil), 2, dil,
                               window // dil) for gi, (window, dil) in enumerate(DIL_PAIRS)]
    nsa_tabs = _nsa_tables(rel_bias, seq)
    xf = x.reshape(n, d)
    xb = xf.astype(BF16)
    for l in range(depth):
        i = l // 2
        if l % 2 == 0:
            xf, xb = _mixer_ab(xb, xf, bsz, seq, ab_w_in[i], mla_q_norm[i], mla_w_uq[i], mla_kv_norm[i],
                               mla_w_ukv[i], ab_w_out[i], ln1_g[l], ln1_b[l], rope_tabs, dil_bias)
        else:
            lam_init = 0.8 - 0.6 * math.exp(-0.3 * l)
            xf, xb = _mixer_cd(xb, xf, bsz, seq, cd_w_in[i], nsa_cmp_pos_k[i], nsa_cmp_k_w1[i],
                               nsa_cmp_k_w2[i], nsa_cmp_pos_v[i], nsa_cmp_v_w1[i], nsa_cmp_v_w2[i],
                               diff_lambda_q1[i], diff_lambda_k1[i], diff_lambda_q2[i], diff_lambda_k2[i],
                               diff_norm[i], cd_w_out[i], ln1_g[l], ln1_b[l], lam_init, nsa_tabs)
        xf, xb = _moe(xf, xb, router_w[l], router_b[l], exp_w_gate[l].astype(BF16), exp_w_up[l].astype(BF16),
                      exp_w_down[l].astype(BF16), sh_w_gate[l].astype(BF16), sh_w_up[l].astype(BF16),
                      sh_w_down[l].astype(BF16), ln2_g[l], ln2_b[l])
    return xf.reshape(bsz, seq, d)
```

```python
import functools
import math

import jax
import jax.numpy as jnp
from jax import lax
from jax.experimental import pallas as pl
from jax.experimental.pallas import tpu as pltpu

F32 = jnp.float32
BF16 = jnp.bfloat16
HI = lax.Precision.HIGHEST

DEPTH = 4
NEG = -1e30
BIG = 1e9
LN_EPS = 1e-5
RMS_EPS = 1e-6
ALPHA = (2 * DEPTH) ** 0.25

N_BUCKETS = 32
REL_MAX_DIST = 2048

H_A = 12
NOPE = 64
ROPE_DIM = 32
MLA_V = 64
Q_LORA = 256
KV_LORA = 128
ROPE_THETA = 10000.0

DIL_PAIRS = ((128, 1), (512, 4), (2048, 16))
H_B_GROUP = 4
H_B = 12
HD_B = 64

H_C = 8
G_C = 2
R_C = 4
DK_C = 64
CMP_LEN = 32
CMP_STRIDE = 16
CMP_HID = 64
SEL_BLOCK = 64
SEL_TOP = 16
WIN = 512

H_D = 4
DD = 64

N_EXPERTS = 64
TOP_K = 8
N_EXPERT_GROUPS = 8
TOPK_GROUPS = 4
D_EXPERT = 256
ROUTED_SCALE = 2.5

LANE = 128
MOE_BM = 256
VMEM_LIMIT = 56 * 1024 * 1024

_NT = (((1,), (1,)), ((), ()))


def _cparams(n_axes, vmem=None):
    return pltpu.CompilerParams(dimension_semantics=("arbitrary",) * n_axes, vmem_limit_bytes=vmem)


def _mm_body(x_ref, w_ref, o_ref, *, act, precision):
    y = jnp.dot(x_ref[...], w_ref[...], preferred_element_type=F32, precision=precision)
    if act == "sigmoid":
        y = jax.nn.sigmoid(y)
    o_ref[...] = y.astype(o_ref.dtype)


def _mm(x, w, out_dtype, tn, tm=512, act=None, precision=None):
    m, k = x.shape
    nc = w.shape[1]
    return pl.pallas_call(
        functools.partial(_mm_body, act=act, precision=precision),
        grid=(m // tm, nc // tn),
        in_specs=[pl.BlockSpec((tm, k), lambda i, j: (i, 0)),
                  pl.BlockSpec((k, tn), lambda i, j: (0, j))],
        out_specs=pl.BlockSpec((tm, tn), lambda i, j: (i, j)),
        out_shape=jax.ShapeDtypeStruct((m, nc), out_dtype),
        compiler_params=_cparams(2), name="proj",
    )(x, w)


def _layer_norm(z, g, b):
    mu = jnp.mean(z, axis=-1, keepdims=True)
    zc = z - mu
    var = jnp.mean(zc * zc, axis=-1, keepdims=True)
    return zc * lax.rsqrt(var + LN_EPS) * g + b


def _out_ln_body(*refs, n_sum):
    a0 = refs[0][...].astype(F32)
    for r in refs[1:n_sum]:
        a0 = a0 + r[...].astype(F32)
    a1_ref, w0_ref, w1_ref, r_ref, g_ref, b_ref, of_ref, ob_ref = refs[n_sum:]
    y = jnp.dot(a0.astype(BF16), w0_ref[...], preferred_element_type=F32)
    y = y + jnp.dot(a1_ref[...], w1_ref[...], preferred_element_type=F32)
    out = _layer_norm(ALPHA * r_ref[...] + y, g_ref[...], b_ref[...])
    of_ref[...] = out
    ob_ref[...] = out.astype(BF16)


def _out_ln(a0s, a1, w0, w1, resid, g, b, tm=256):
    n, d = resid.shape
    k0, k1 = w0.shape[0], w1.shape[0]
    row = lambda i: (i, 0)
    fixed = lambda i: (0, 0)
    return pl.pallas_call(
        functools.partial(_out_ln_body, n_sum=len(a0s)),
        grid=(n // tm,),
        in_specs=[pl.BlockSpec((tm, k0), row)] * len(a0s) + [
            pl.BlockSpec((tm, k1), row), pl.BlockSpec((k0, d), fixed), pl.BlockSpec((k1, d), fixed),
            pl.BlockSpec((tm, d), row), pl.BlockSpec((1, d), fixed), pl.BlockSpec((1, d), fixed)],
        out_specs=[pl.BlockSpec((tm, d), row), pl.BlockSpec((tm, d), row)],
        out_shape=[jax.ShapeDtypeStruct((n, d), F32), jax.ShapeDtypeStruct((n, d), BF16)],
        compiler_params=_cparams(1), name="out_proj_ln",
    )(*a0s, a1, w0, w1, resid, g, b)


def _rms(x, g):
    return x * lax.rsqrt(jnp.mean(x * x, axis=-1, keepdims=True) + RMS_EPS) * g


def _mla_q_body(c_ref, g_ref, w_ref, wr_ref, cos_ref, sin_ref, o_ref):
    cn = _rms(c_ref[...], g_ref[...]).astype(BF16)
    a = jnp.dot(cn, w_ref[...], preferred_element_type=F32)
    r = jnp.dot(cn, wr_ref[...], preferred_element_type=F32)
    cos, sin = cos_ref[...], sin_ref[...]
    for h in range(H_A):
        sl = slice(h * LANE, (h + 1) * LANE)
        o_ref[:, sl] = (a[:, sl] * cos + r[:, sl] * sin).astype(o_ref.dtype)


def _mla_kv_body(c_ref, kr_ref, krr_ref, g_ref, wk_ref, wv_ref, cos_ref, sin_ref, k_ref, v_ref):
    cn = _rms(c_ref[...], g_ref[...]).astype(BF16)
    kn = jnp.dot(cn, wk_ref[...], preferred_element_type=F32)
    rope = kr_ref[...] * cos_ref[...] + krr_ref[...] * sin_ref[...]
    for h in range(H_A):
        sl = slice(h * LANE, (h + 1) * LANE)
        k_ref[:, sl] = (kn[:, sl] + rope).astype(k_ref.dtype)
    v_ref[...] = jnp.dot(cn, wv_ref[...], preferred_element_type=F32).astype(v_ref.dtype)


def _mla_up(h1, q_norm, wq, wq_rot, kv_norm, wk, wv, cos_q, sin_q, cos_k, sin_k, seq, tm=512):
    n = h1.shape[0]
    spt = seq // tm
    row = lambda c: (lambda i: (i, c))
    pos = lambda i: (i % spt, 0)
    fixed = lambda i: (0, 0)
    q_a = pl.pallas_call(
        _mla_q_body, grid=(n // tm,),
        in_specs=[pl.BlockSpec((tm, Q_LORA), row(0)), pl.BlockSpec((1, Q_LORA), fixed),
                  pl.BlockSpec(wq.shape, fixed), pl.BlockSpec(wq_rot.shape, fixed),
                  pl.BlockSpec((tm, LANE), pos), pl.BlockSpec((tm, LANE), pos)],
        out_specs=pl.BlockSpec((tm, H_A * LANE), row(0)),
        out_shape=jax.ShapeDtypeStruct((n, H_A * LANE), BF16),
        compiler_params=_cparams(1), name="mla_q_up",
    )(h1, q_norm, wq, wq_rot, cos_q, sin_q)
    k_a, v_a = pl.pallas_call(
        _mla_kv_body, grid=(n // tm,),
        in_specs=[pl.BlockSpec((tm, LANE), row(2)), pl.BlockSpec((tm, LANE), row(3)),
                  pl.BlockSpec((tm, LANE), row(4)), pl.BlockSpec((1, KV_LORA), fixed),
                  pl.BlockSpec(wk.shape, fixed), pl.BlockSpec(wv.shape, fixed),
                  pl.BlockSpec((tm, LANE), pos), pl.BlockSpec((tm, LANE), pos)],
        out_specs=[pl.BlockSpec((tm, H_A * LANE), row(0)), pl.BlockSpec((tm, H_A * MLA_V), row(0))],
        out_shape=[jax.ShapeDtypeStruct((n, H_A * LANE), BF16), jax.ShapeDtypeStruct((n, H_A * MLA_V), BF16)],
        compiler_params=_cparams(1), name="mla_kv_up",
    )(h1, h1, h1, kv_norm, wk, wv, cos_k, sin_k)
    return q_a, k_a, v_a


def _flash_body(*refs, T, dq, dv, q_offs, k_offs, v_offs, scale, bias_mode, has_sel, has_gate,
                epilogue, want_lse, seg_tiles, nback, lam_init):
    it = iter(refs)
    q_ref, k_ref, v_ref = next(it), next(it), next(it)
    bias_ref = next(it) if bias_mode else None
    sel_ref = next(it) if has_sel else None
    gate_ref = next(it) if has_gate else None
    lam_ref, dn_ref = (next(it), next(it)) if epilogue == "diff" else (None, None)
    o_ref = next(it)
    lse_ref = next(it) if want_lse else None
    vt_scr = next(it)

    qi = pl.program_id(2)
    seq = v_ref.shape[1]

    @pl.when(qi == 0)
    def _():
        for c in range(seq // T):
            vt_scr[:, c * T:(c + 1) * T] = v_ref[0, c * T:(c + 1) * T, :].astype(F32).T.astype(BF16)

    qfull = q_ref[0].astype(F32)
    fold_scale = math.frexp(scale)[0] == 0.5
    qts = [(qfull[:, off:off + dq] * (scale if fold_scale else 1.0)).T.astype(BF16) for off in q_offs]

    def qk(kc):
        kfull = k_ref[0, pl.ds(pl.multiple_of(kc * T, T), T), :]
        return tuple(jnp.dot(kfull[:, k_offs[u]:k_offs[u] + dq], qts[u], preferred_element_type=F32)
                     for u in range(2))

    def update(kc, state, scores, diag):
        start = pl.multiple_of(kc * T, T)
        new_state = []
        sel_add = None
        if has_sel:
            per = T // SEL_BLOCK
            rows = [sel_ref[0, 0, pl.ds(kc * per + a, 1), :] for a in range(per)]
            sel_add = jnp.concatenate([jnp.broadcast_to((r - 1.0) * (-NEG), (SEL_BLOCK, T)) for r in rows], axis=0)
        for u in range(2):
            vt = vt_scr[v_offs[u]:v_offs[u] + dv, pl.ds(start, T)]
            s = scores[u]
            if not fold_scale:
                s = s * scale
            if bias_mode:
                s = s + bias_ref[u if bias_mode == "pair" else 0, qi - kc]
            elif diag:
                key = lax.broadcasted_iota(jnp.int32, (T, T), 0)
                qry = lax.broadcasted_iota(jnp.int32, (T, T), 1)
                s = jnp.where(key <= qry, s, NEG)
            if has_sel:
                s = s + sel_add
            m_prev, l_prev, acc_prev = state[u]
            m_new = jnp.maximum(m_prev, jnp.max(s, axis=0, keepdims=True))
            alpha = jnp.exp(m_prev - m_new)
            p = jnp.exp(s - m_new)
            l_new = alpha * l_prev + jnp.sum(p, axis=0, keepdims=True)
            acc_new = alpha * acc_prev + jnp.dot(vt, p.astype(BF16), preferred_element_type=F32)
            new_state.append((m_new, l_new, acc_new))
        return tuple(new_state)

    init = tuple((jnp.full((1, T), NEG, F32), jnp.zeros((1, T), F32), jnp.zeros((dv, T), F32)) for _ in range(2))
    if nback is None:
        lo = 0
    elif seg_tiles is not None:
        lo = jnp.where(qi % seg_tiles == 0, qi, qi - nback)
    else:
        lo = jnp.maximum(qi - nback, 0)

    def step(kc, carry):
        state, scores = carry
        nxt = qk(kc + 1)
        return update(kc, state, scores, False), nxt

    state, scores = lax.fori_loop(lo, qi, step, (init, qk(lo)))
    state = update(qi, state, scores, True)

    outs = [acc / l for _, l, acc in state]
    if epilogue == "diff":
        a = outs[0] - lam_ref[0, 0] * outs[1]
        rinv = lax.rsqrt(jnp.mean(a * a, axis=0, keepdims=True) + RMS_EPS)
        o = (a * rinv * dn_ref[...] * (1.0 - lam_init)).T
    else:
        o = jnp.concatenate(outs, axis=0).T
        if has_gate:
            o = o * gate_ref[0]
    o_ref[0] = o.astype(o_ref.dtype)
    if want_lse:
        lse_ref[0, 0] = jnp.concatenate([m + jnp.log(l) for m, l, _ in state], axis=0)


def _flash(q, k, v, *, n_outer, T, dq, dv, q_col, k_col, v_col, q_offs, k_offs, v_offs, scale,
           out_cols, out_col, bias=None, bias_mode=None, bias_idx=None, sel=None, sel_idx=None,
           gate=None, gate_col=None, lam=None, dnorm=None, lam_init=0.0, epilogue="plain", want_lse=False,
           seg_tiles=None, nback=None, vmem=None, k_w=None, v_w=None, name="flash"):
    bsz, seq, _ = q.shape
    nq = seq // T
    qw = max(o + dq for o in q_offs)
    kw = k_w or max(o + dq for o in k_offs)
    vw = v_w or max(o + dv for o in v_offs)
    ow = dv if epilogue == "diff" else 2 * dv
    in_specs = [pl.BlockSpec((1, T, qw), lambda g, b, i: (b, i, q_col(g))),
                pl.BlockSpec((1, seq, kw), lambda g, b, i: (b, 0, k_col(g))),
                pl.BlockSpec((1, seq, vw), lambda g, b, i: (b, 0, v_col(g)))]
    args = [q, k, v]
    if bias_mode:
        nb = 2 if bias_mode == "pair" else 1
        in_specs.append(pl.BlockSpec((nb,) + bias.shape[1:], lambda g, b, i: (bias_idx(g), 0, 0, 0)))
        args.append(bias)
    if sel is not None:
        in_specs.append(pl.BlockSpec((1, 1, sel.shape[2], T), lambda g, b, i: (b, sel_idx(g), 0, i)))
        args.append(sel)
    if gate is not None:
        in_specs.append(pl.BlockSpec((1, T, ow), lambda g, b, i: (b, i, gate_col(g))))
        args.append(gate)
    if epilogue == "diff":
        in_specs.append(pl.BlockSpec(memory_space=pltpu.SMEM))
        in_specs.append(pl.BlockSpec((dv, 1), lambda g, b, i: (0, 0)))
        args += [lam, dnorm]
    out_specs = [pl.BlockSpec((1, T, ow), lambda g, b, i: (b, i, out_col(g)))]
    out_shape = [jax.ShapeDtypeStruct((bsz, seq, out_cols), BF16)]
    if want_lse:
        out_specs.append(pl.BlockSpec((1, 1, 2, T), lambda g, b, i: (b, g, 0, i)))
        out_shape.append(jax.ShapeDtypeStruct((bsz, n_outer, 2, seq), F32))
    body = functools.partial(
        _flash_body, T=T, dq=dq, dv=dv, q_offs=q_offs, k_offs=k_offs, v_offs=v_offs, scale=scale,
        bias_mode=bias_mode, has_sel=sel is not None, has_gate=gate is not None, epilogue=epilogue,
        want_lse=want_lse, seg_tiles=seg_tiles, nback=nback, lam_init=lam_init)
    res = pl.pallas_call(
        body, grid=(n_outer, bsz, nq), in_specs=in_specs, out_specs=out_specs, out_shape=out_shape,
        scratch_shapes=[pltpu.VMEM((vw, seq), BF16)],
        compiler_params=_cparams(3, vmem), name=name,
    )(*args)
    return res if want_lse else res[0]


def _rel_bucket(dist):
    n = jnp.maximum(dist, 0)
    exact = N_BUCKETS // 2
    log_ratio = jnp.log(jnp.maximum(n, 1).astype(F32) / exact) / math.log(REL_MAX_DIST / exact)
    large = exact + (log_ratio * (N_BUCKETS - exact)).astype(jnp.int32)
    return jnp.where(n < exact, n, jnp.minimum(large, N_BUCKETS - 1))


def _toeplitz_bias(tab, T, n_d, dist_scale, max_dist):
    x = jnp.arange(2 * T)
    dist = jnp.arange(n_d)[:, None] * T + jnp.where(x < T, x, x - 2 * T)[None, :]
    w = tab[_rel_bucket(dist * dist_scale)].astype(F32)
    w = jnp.where(((dist >= 0) & (dist <= max_dist))[..., None], w, NEG)
    return _toeplitz(jnp.moveaxis(w, -1, 0), T)


def _toeplitz(w, t):
    tiled = jnp.tile(w, (1,) * (w.ndim - 1) + (t,))[..., :t * (2 * t - 1)]
    return tiled.reshape(w.shape[:-1] + (t, 2 * t - 1))[..., :t]


def _compress_body(u_ref, pe_ref, w1_ref, w2_ref, o_ref, *, ncp):
    outs = []
    for a in range(2):
        u = u_ref[0, 0, a].astype(F32)
        p1 = jnp.dot(u + pe_ref[a, 0], w1_ref[a, 0], preferred_element_type=F32, precision=HI)
        p2 = jnp.dot(u + pe_ref[a, 1], w1_ref[a, 1], preferred_element_type=F32, precision=HI)
        hid = jax.nn.gelu(p1 + pltpu.roll(p2, ncp - 1, 0))
        outs.append(jnp.dot(hid, w2_ref[a], preferred_element_type=F32, precision=HI))
    o_ref[0, 0] = jnp.concatenate(outs, axis=-1)


def _compress(u, pe, w1, w2):
    bsz, g, _, ncp, width = u.shape
    return pl.pallas_call(
        functools.partial(_compress_body, ncp=ncp), grid=(bsz, g),
        in_specs=[pl.BlockSpec((1, 1, 2, ncp, width), lambda b, gg: (b, gg, 0, 0, 0)),
                  pl.BlockSpec(pe.shape, lambda b, gg: (0, 0, 0, 0)),
                  pl.BlockSpec(w1.shape, lambda b, gg: (0, 0, 0, 0)),
                  pl.BlockSpec(w2.shape, lambda b, gg: (0, 0, 0))],
        out_specs=pl.BlockSpec((1, 1, ncp, 2 * DK_C), lambda b, gg: (b, gg, 0, 0)),
        out_shape=jax.ShapeDtypeStruct((bsz, g, ncp, 2 * DK_C), F32),
        compiler_params=_cparams(2), name="nsa_compress",
    )(u, pe, w1, w2)


def _cmp_attn_body(q_ref, kv_ref, bias_ref, ov_ref, gate_ref, o_ref, sel_ref, *, T, ncp, n_sel, n_top, scale):
    qi = pl.program_id(2)
    kc = kv_ref[0, 0, :, :DK_C]
    vc = kv_ref[0, 0, :, DK_C:]
    t = qi * T + lax.broadcasted_iota(jnp.int32, (T, ncp), 0)
    c = lax.broadcasted_iota(jnp.int32, (T, ncp), 1)
    valid = t >= c * CMP_STRIDE + (CMP_LEN - 1)
    validf = valid.astype(F32)
    psum = jnp.zeros((T, ncp), F32)
    outs = []
    for r in range(R_C):
        q = q_ref[0, :, r * DK_C:(r + 1) * DK_C].astype(F32)
        s = lax.dot_general(q, kc, _NT, preferred_element_type=F32, precision=HI) * scale + bias_ref[r]
        s = jnp.where(valid, s, NEG)
        e = jnp.exp(s - jnp.max(s, axis=-1, keepdims=True)) * validf
        p = e / jnp.maximum(jnp.sum(e, axis=-1, keepdims=True), 1e-30)
        outs.append(jnp.dot(p, vc, preferred_element_type=F32, precision=HI))
        psum = psum + p
    o_ref[0] = (jnp.concatenate(outs, axis=-1) * gate_ref[0]).astype(o_ref.dtype)

    imp = jnp.dot(ov_ref[...], psum.T, preferred_element_type=F32, precision=HI)
    tq = qi * T + lax.broadcasted_iota(jnp.int32, (n_sel, T), 1)
    j = lax.broadcasted_iota(jnp.int32, (n_sel, T), 0)
    forced = (j == tq // SEL_BLOCK) | (j == 0)
    work = jnp.where(forced, BIG, jnp.where(j * SEL_BLOCK <= tq, imp, -BIG))
    sel = jnp.zeros((n_sel, T), F32)
    jf = j.astype(F32)
    for _ in range(n_top):
        _, _, pick = _first_max(work, jf, n_sel)
        sel = jnp.where(pick, 1.0, sel)
        work = jnp.where(pick, -jnp.inf, work)
    sel_ref[0, 0] = sel


def _cmp_attn(h, kvc, bias_c, overlap, gates, T=256):
    bsz, seq, _ = h.shape
    ncp = kvc.shape[2]
    n_sel = seq // SEL_BLOCK
    n_top = min(SEL_TOP, n_sel)
    qw = R_C * DK_C
    return pl.pallas_call(
        functools.partial(_cmp_attn_body, T=T, ncp=ncp, n_sel=n_sel, n_top=n_top, scale=DK_C ** -0.5),
        grid=(G_C, bsz, seq // T),
        in_specs=[pl.BlockSpec((1, T, qw), lambda g, b, i: (b, i, g)),
                  pl.BlockSpec((1, 1, ncp, 2 * DK_C), lambda g, b, i: (b, g, 0, 0)),
                  pl.BlockSpec((R_C, T, ncp), lambda g, b, i: (g, i, 0)),
                  pl.BlockSpec(overlap.shape, lambda g, b, i: (0, 0)),
                  pl.BlockSpec((1, T, qw), lambda g, b, i: (b, i, g))],
        out_specs=[pl.BlockSpec((1, T, qw), lambda g, b, i: (b, i, g)),
                   pl.BlockSpec((1, 1, n_sel, T), lambda g, b, i: (b, g, 0, i))],
        out_shape=[jax.ShapeDtypeStruct((bsz, seq, H_C * DK_C), BF16),
                   jax.ShapeDtypeStruct((bsz, G_C, n_sel, seq), F32)],
        compiler_params=_cparams(3), name="nsa_cmp_attn",
    )(h, kvc, bias_c, overlap, gates)


def _first_max(work, idx, n):
    mx = jnp.max(work, axis=0, keepdims=True)
    first = jnp.min(jnp.where(work == mx, idx, float(n)), axis=0, keepdims=True)
    return mx, first, idx == first


def _router_body(x_ref, wt_ref, b_ref, tri_ref, e_ref, g_ref, r_ref, cnt_ref, carry_scr, *, tm):
    i = pl.program_id(0)

    @pl.when(i == 0)
    def _():
        carry_scr[...] = jnp.zeros(carry_scr.shape, F32)

    st = lax.dot_general(wt_ref[...], x_ref[...], _NT, preferred_element_type=F32, precision=HI)
    scores = jax.nn.sigmoid(st)
    sel = scores + b_ref[...]
    per = N_EXPERTS // N_EXPERT_GROUPS
    fiota = lambda rows: lax.broadcasted_iota(jnp.int32, (rows, tm), 0).astype(F32)
    i_per, i_grp, i_exp = fiota(per), fiota(N_EXPERT_GROUPS), fiota(N_EXPERTS)
    grp_scores = []
    for g in range(N_EXPERT_GROUPS):
        blk = sel[g * per:(g + 1) * per]
        m1, _, pick = _first_max(blk, i_per, per)
        grp_scores.append(m1 + jnp.max(jnp.where(pick, -jnp.inf, blk), axis=0, keepdims=True))
    work = jnp.concatenate(grp_scores, axis=0)
    gmask = jnp.zeros((N_EXPERT_GROUPS, tm), F32)
    for _ in range(TOPK_GROUPS):
        _, _, pick = _first_max(work, i_grp, N_EXPERT_GROUPS)
        gmask = jnp.where(pick, 1.0, gmask)
        work = jnp.where(pick, -jnp.inf, work)
    work = jnp.concatenate([jnp.where(gmask[g:g + 1] > 0.5, sel[g * per:(g + 1) * per], NEG)
                            for g in range(N_EXPERT_GROUPS)], axis=0)
    picks, firsts, vals = [], [], []
    for _ in range(TOP_K):
        _, first, pick = _first_max(work, i_exp, N_EXPERTS)
        picks.append(pick)
        firsts.append(first)
        vals.append(jnp.sum(jnp.where(pick, scores, 0.0), axis=0, keepdims=True))
        work = jnp.where(pick, -jnp.inf, work)
    val = jnp.concatenate(vals, axis=0)
    g_ref[...] = val / jnp.sum(val, axis=0, keepdims=True) * ROUTED_SCALE
    e_ref[...] = jnp.concatenate(firsts, axis=0).astype(jnp.int32)
    onehot = picks[0].astype(F32)
    for pick in picks[1:]:
        onehot = onehot + pick.astype(F32)
    before = jnp.dot(onehot.astype(BF16), tri_ref[...], preferred_element_type=F32) + carry_scr[...]
    r_ref[...] = jnp.concatenate([jnp.sum(jnp.where(pick, before, 0.0), axis=0, keepdims=True)
                                  for pick in picks], axis=0).astype(jnp.int32)
    carry = carry_scr[...] + jnp.sum(onehot, axis=1, keepdims=True)
    carry_scr[...] = carry
    cnt_ref[...] = jnp.broadcast_to(carry, cnt_ref.shape)


def _router(xf, router_w, router_b, tm=256):
    n, d = xf.shape
    tri = (jnp.arange(tm)[:, None] < jnp.arange(tm)[None, :]).astype(BF16)
    col = lambda i: (0, i)
    fixed = lambda i: (0, 0)
    return pl.pallas_call(
        functools.partial(_router_body, tm=tm), grid=(n // tm,),
        in_specs=[pl.BlockSpec((tm, d), lambda i: (i, 0)), pl.BlockSpec((N_EXPERTS, d), fixed),
                  pl.BlockSpec((N_EXPERTS, 1), fixed), pl.BlockSpec((tm, tm), fixed)],
        out_specs=[pl.BlockSpec((TOP_K, tm), col), pl.BlockSpec((TOP_K, tm), col),
                   pl.BlockSpec((TOP_K, tm), col), pl.BlockSpec((N_EXPERTS, LANE), fixed)],
        out_shape=[jax.ShapeDtypeStruct((TOP_K, n), jnp.int32), jax.ShapeDtypeStruct((TOP_K, n), F32),
                   jax.ShapeDtypeStruct((TOP_K, n), jnp.int32), jax.ShapeDtypeStruct((N_EXPERTS, LANE), F32)],
        scratch_shapes=[pltpu.VMEM((N_EXPERTS, 1), F32)],
        compiler_params=_cparams(1), name="router",
    )(xf, router_w.T, router_b.astype(F32)[:, None], tri)


def _dispatch_body(ps_ref, fill_ref, e_ref, r_ref, x_ref, z_ref, xs_hbm, sem, *, tm, n_blocks):
    i = pl.program_id(0)

    def row_copy(src, dst):
        return pltpu.make_async_copy(x_ref.at[pl.ds(src, 1)], xs_hbm.at[pl.ds(dst, 1)], sem)

    def zero_copy(dst):
        return pltpu.make_async_copy(z_ref.at[pl.ds(0, 1)], xs_hbm.at[pl.ds(dst, 1)], sem)

    def zero_block(blk):
        return pltpu.make_async_copy(z_ref, xs_hbm.at[pl.ds(blk * MOE_BM, MOE_BM)], sem)

    def issue(t, c):
        for k in range(TOP_K):
            row_copy(t, ps_ref[e_ref[k, t]] + r_ref[k, t]).start()
        return c

    def drain(t, c):
        for k in range(TOP_K):
            row_copy(0, 0).wait()
        return c

    lax.fori_loop(0, tm, issue, 0)
    lax.fori_loop(0, tm, drain, 0)

    @pl.when(i == 0)
    def _():
        def fill(e, c):
            lo, hi = fill_ref[2 * e], fill_ref[2 * e + 1]

            def start(r, cc):
                zero_copy(r).start()
                return cc

            def wait(r, cc):
                zero_copy(0).wait()
                return cc

            lax.fori_loop(lo, hi, start, 0)
            lax.fori_loop(lo, hi, wait, 0)
            return c

        lax.fori_loop(0, N_EXPERTS, fill, 0)

        def tail_start(blk, c):
            zero_block(blk).start()
            return c

        def tail_wait(blk, c):
            zero_block(0).wait()
            return c

        lax.fori_loop(fill_ref[2 * N_EXPERTS], n_blocks, tail_start, 0)
        lax.fori_loop(fill_ref[2 * N_EXPERTS], n_blocks, tail_wait, 0)


def _dispatch(xf, e_idx, rank, pad_start, fill, p, tm=128):
    n, d = xf.shape
    smem_col = pl.BlockSpec((TOP_K, tm), lambda i, ps, fl: (0, i), memory_space=pltpu.SMEM)
    grid_spec = pltpu.PrefetchScalarGridSpec(
        num_scalar_prefetch=2, grid=(n // tm,),
        in_specs=[smem_col, smem_col, pl.BlockSpec((tm, d), lambda i, ps, fl: (i, 0)),
                  pl.BlockSpec((MOE_BM, d), lambda i, ps, fl: (0, 0))],
        out_specs=pl.BlockSpec(memory_space=pl.ANY),
        scratch_shapes=[pltpu.SemaphoreType.DMA(())])
    return pl.pallas_call(
        functools.partial(_dispatch_body, tm=tm, n_blocks=p // MOE_BM), grid_spec=grid_spec,
        out_shape=jax.ShapeDtypeStruct((p, d), xf.dtype),
        compiler_params=_cparams(1), name="dispatch",
    )(pad_start, fill, e_idx, rank, xf, jnp.zeros((MOE_BM, d), xf.dtype))


def _moe_ffn_body(be_ref, nb_ref, x_ref, wg_ref, wu_ref, wd_ref, o_ref):
    i = pl.program_id(0)

    @pl.when(i < nb_ref[0])
    def _():
        x = x_ref[...].astype(BF16)
        hg = jnp.dot(x, wg_ref[0], preferred_element_type=F32)
        hu = jnp.dot(x, wu_ref[0], preferred_element_type=F32)
        hb = (hg * jax.nn.sigmoid(hg) * hu).astype(BF16)
        o_ref[...] = jnp.dot(hb, wd_ref[0], preferred_element_type=F32).astype(o_ref.dtype)

    @pl.when(i >= nb_ref[0])
    def _():
        o_ref[...] = jnp.zeros(o_ref.shape, o_ref.dtype)


def _moe_ffn(xs, blk_e, n_used, wg, wu, wd):
    p, d = xs.shape
    n_blocks = p // MOE_BM
    grid_spec = pltpu.PrefetchScalarGridSpec(
        num_scalar_prefetch=2, grid=(n_blocks,),
        in_specs=[pl.BlockSpec((MOE_BM, d), lambda i, be, nb: (i, 0)),
                  pl.BlockSpec((1, d, D_EXPERT), lambda i, be, nb: (be[i], 0, 0)),
                  pl.BlockSpec((1, d, D_EXPERT), lambda i, be, nb: (be[i], 0, 0)),
                  pl.BlockSpec((1, D_EXPERT, d), lambda i, be, nb: (be[i], 0, 0))],
        out_specs=pl.BlockSpec((MOE_BM, d), lambda i, be, nb: (i, 0)))
    return pl.pallas_call(
        _moe_ffn_body, grid_spec=grid_spec, out_shape=jax.ShapeDtypeStruct((p, d), F32),
        compiler_params=_cparams(1), name="expert_ffn",
    )(blk_e, n_used, xs, wg, wu, wd)


def _combine_body(ps_ref, e_ref, r_ref, gate_ref, xb_ref, xf_ref, y_hbm, sg_ref, su_ref, sd_ref, g_ref, b_ref,
                  of_ref, ob_ref, buf, sem, *, tm):
    def row_copy(k, t, src):
        return pltpu.make_async_copy(y_hbm.at[pl.ds(src, 1)], buf.at[k, pl.ds(t, 1)], sem)

    def issue(t, c):
        for k in range(TOP_K):
            row_copy(k, t, ps_ref[e_ref[k, t]] + r_ref[k, t]).start()
        return c

    def drain(t, c):
        for k in range(TOP_K):
            row_copy(k, t, 0).wait()
        return c

    lax.fori_loop(0, tm, issue, 0)
    x = xb_ref[...]
    hg = jnp.dot(x, sg_ref[...], preferred_element_type=F32)
    hu = jnp.dot(x, su_ref[...], preferred_element_type=F32)
    hb = (hg * jax.nn.sigmoid(hg) * hu).astype(BF16)
    y = jnp.dot(hb, sd_ref[...], preferred_element_type=F32)
    lax.fori_loop(0, tm, drain, 0)
    gate = gate_ref[...]
    for k in range(TOP_K):
        y = y + gate[:, k:k + 1] * buf[k]
    out = _layer_norm(ALPHA * xf_ref[...] + y, g_ref[...], b_ref[...])
    of_ref[...] = out
    ob_ref[...] = out.astype(BF16)


def _combine(pad_start, e_idx, rank, gate, xb, xf, y, sg, su, sd, g, b, tm=128):
    n, d = xf.shape
    row = lambda i, ps: (i, 0)
    fixed = lambda i, ps: (0, 0)
    smem_col = pl.BlockSpec((TOP_K, tm), lambda i, ps: (0, i), memory_space=pltpu.SMEM)
    grid_spec = pltpu.PrefetchScalarGridSpec(
        num_scalar_prefetch=1, grid=(n // tm,),
        in_specs=[smem_col, smem_col, pl.BlockSpec((tm, TOP_K), row), pl.BlockSpec((tm, d), row),
                  pl.BlockSpec((tm, d), row), pl.BlockSpec(memory_space=pl.ANY),
                  pl.BlockSpec(sg.shape, fixed), pl.BlockSpec(su.shape, fixed), pl.BlockSpec(sd.shape, fixed),
                  pl.BlockSpec((1, d), fixed), pl.BlockSpec((1, d), fixed)],
        out_specs=[pl.BlockSpec((tm, d), row), pl.BlockSpec((tm, d), row)],
        scratch_shapes=[pltpu.VMEM((TOP_K, tm, d), F32), pltpu.SemaphoreType.DMA(())])
    return pl.pallas_call(
        functools.partial(_combine_body, tm=tm), grid_spec=grid_spec,
        out_shape=[jax.ShapeDtypeStruct((n, d), F32), jax.ShapeDtypeStruct((n, d), BF16)],
        compiler_params=_cparams(1), name="combine",
    )(pad_start, e_idx, rank, gate, xb, xf, y, sg, su, sd, g, b)


def _moe(xf, xb, router_w, router_b, w_gate, w_up, w_down, sh_gate, sh_up, sh_down, ln_g, ln_b):
    n, d = xf.shape
    e_idx, gate, rank, cnt = _router(xf, router_w, router_b)
    counts = cnt[:, 0].astype(jnp.int32)
    padded = (counts + MOE_BM - 1) // MOE_BM * MOE_BM
    pad_end = jnp.cumsum(padded)
    pad_start = (pad_end - padded).astype(jnp.int32)
    n_blocks = (n * TOP_K + N_EXPERTS * (MOE_BM - 1) + MOE_BM - 1) // MOE_BM
    blk_e = jnp.minimum(jnp.sum(pad_end[None, :] <= (jnp.arange(n_blocks) * MOE_BM)[:, None], axis=1),
                        N_EXPERTS - 1).astype(jnp.int32)
    n_used = (pad_end[-1] // MOE_BM).astype(jnp.int32).reshape(1)
    fill = jnp.concatenate([jnp.stack([pad_start + counts, pad_end], axis=1).reshape(-1), n_used]).astype(jnp.int32)
    xs = _dispatch(xf, e_idx, rank, pad_start, fill, n_blocks * MOE_BM)
    y = _moe_ffn(xs, blk_e, n_used, w_gate, w_up, w_down)
    return _combine(pad_start, e_idx, rank, gate.T, xb, xf, y, sh_gate, sh_up, sh_down, ln_g[None], ln_b[None])


def _rope_tables(seq):
    half = ROPE_DIM // 2
    freqs = ROPE_THETA ** (-jnp.arange(half, dtype=F32) / half)
    ang = jnp.arange(seq).astype(F32)[:, None] * freqs
    cos = jnp.concatenate([jnp.cos(ang)] * 2, -1)
    sin = jnp.concatenate([jnp.sin(ang)] * 2, -1)
    z = lambda w: jnp.zeros((seq, w), F32)
    pad = LANE - NOPE - ROPE_DIM
    cos_q = jnp.concatenate([jnp.ones((seq, NOPE), F32), cos, z(pad)], -1)
    sin_q = jnp.concatenate([z(NOPE), sin, z(pad)], -1)
    cos_k = jnp.concatenate([z(NOPE), cos, z(pad)], -1)
    return cos_q, sin_q, cos_k, sin_q


def _rot_cols(w):
    half = w.shape[-1] // 2
    return jnp.concatenate([-w[..., half:], w[..., :half]], -1)


def _mixer_ab(xb, xf, bsz, seq, w_in, q_norm, w_uq, kv_norm, w_ukv, w_out, ln_g, ln_b, rope_tabs, dil_bias):
    n = bsz * seq
    d = w_in.shape[0]
    c0 = Q_LORA + KV_LORA
    w_kr = w_in[:, c0:c0 + ROPE_DIM]
    zc = lambda w: jnp.zeros((d, w), F32)
    pad = LANE - NOPE - ROPE_DIM
    w1 = jnp.concatenate([w_in[:, :c0], zc(NOPE), w_kr, zc(pad), zc(NOPE), _rot_cols(w_kr), zc(pad)], 1)
    h1 = _mm(xb, w1.astype(BF16), F32, tn=LANE)
    h2 = _mm(xb, w_in[:, c0 + ROPE_DIM:].astype(BF16), BF16, tn=256)

    wq = w_uq.reshape(Q_LORA, H_A, NOPE + ROPE_DIM)
    zq = jnp.zeros((Q_LORA, H_A, pad), F32)
    wq_main = jnp.concatenate([wq, zq], -1).reshape(Q_LORA, H_A * LANE)
    wq_rot = jnp.concatenate([jnp.zeros((Q_LORA, H_A, NOPE), F32), _rot_cols(wq[..., NOPE:]), zq], -1)
    wq_rot = wq_rot.reshape(Q_LORA, H_A * LANE)
    wkv = w_ukv.reshape(KV_LORA, H_A, NOPE + MLA_V)
    wk = jnp.concatenate([wkv[..., :NOPE], jnp.zeros((KV_LORA, H_A, LANE - NOPE), F32)], -1)
    wk = wk.reshape(KV_LORA, H_A * LANE)
    wv = wkv[..., NOPE:].reshape(KV_LORA, H_A * MLA_V)
    q_a, k_a, v_a = _mla_up(h1, q_norm[None], wq_main.astype(BF16), wq_rot.astype(BF16), kv_norm[None],
                            wk.astype(BF16), wv.astype(BF16), *rope_tabs, seq)
    o_a = _flash(q_a.reshape(bsz, seq, -1), k_a.reshape(bsz, seq, -1), v_a.reshape(bsz, seq, -1), name="mla_attn",
                 n_outer=H_A // 2, T=256, dq=LANE, dv=MLA_V,
                 q_col=lambda g: g, k_col=lambda g: g, v_col=lambda g: g,
                 q_offs=(0, LANE), k_offs=(0, LANE), v_offs=(0, MLA_V),
                 scale=(NOPE + ROPE_DIM) ** -0.5, out_cols=H_A * MLA_V, out_col=lambda g: g)

    gw = H_B_GROUP * HD_B
    hq = h2.reshape(bsz, seq, 3, len(DIL_PAIRS), gw)
    outs, lses = [], []
    for gi, (window, dil) in enumerate(DIL_PAIRS):
        L = seq // dil
        t = hq[:, :, :, gi].reshape(bsz, L, dil, 3 * gw).transpose(0, 2, 1, 3).reshape(bsz, seq, 3 * gw)
        T = min(256, L)
        o, lse = _flash(t, t, t, name="dilated_attn", n_outer=2, T=T, dq=HD_B, dv=HD_B,
                        q_col=lambda g: g, k_col=lambda g: 2 + g, v_col=lambda g: 4 + g,
                        q_offs=(0, HD_B), k_offs=(0, HD_B), v_offs=(0, HD_B), scale=HD_B ** -0.5,
                        out_cols=gw, out_col=lambda g: g, bias=dil_bias[gi], bias_mode="pair",
                        bias_idx=lambda g: g, want_lse=True, seg_tiles=L // T, nback=1)
        outs.append(o.reshape(bsz, dil, L, gw).transpose(0, 2, 1, 3).reshape(bsz, seq, H_B_GROUP, HD_B).astype(F32))
        lses.append(lse.reshape(bsz, H_B_GROUP, dil, L).transpose(0, 3, 2, 1).reshape(bsz, seq, H_B_GROUP))
    w = jax.nn.softmax(jnp.stack(lses), axis=0)
    o_b = jnp.sum(w[..., None] * jnp.stack(outs), axis=0).astype(BF16).reshape(n, gw)
    na = H_A * MLA_V
    return _out_ln([o_a.reshape(n, na)], o_b, w_out[:na].astype(BF16), w_out[na:].astype(BF16),
                   xf, ln_g[None], ln_b[None])


def _mixer_cd(xb, xf, bsz, seq, w_in, pos_k, k_w1, k_w2, pos_v, v_w1, v_w2, lq1, lk1, lq2, lk2, d_norm,
              w_out, ln_g, ln_b, lam_init, tabs):
    n = bsz * seq
    qc_w = H_C * DK_C
    kv_w = G_C * DK_C
    off = qc_w
    kvs = []
    for _ in range(3):
        wk_ = w_in[:, off:off + kv_w].reshape(-1, G_C, DK_C)
        wv_ = w_in[:, off + kv_w:off + 2 * kv_w].reshape(-1, G_C, DK_C)
        kvs.append(jnp.concatenate([wk_, wv_], -1).reshape(-1, 2 * kv_w))
        off += 2 * kv_w
    g_off = off
    d_off = off + 3 * H_C
    w_main = jnp.concatenate([w_in[:, :qc_w]] + kvs + [w_in[:, d_off:]], 1)
    h = _mm(xb, w_main.astype(BF16), BF16, tn=256).reshape(bsz, seq, -1)
    w_g = jnp.repeat(w_in[:, g_off:d_off], DK_C, axis=1)
    gates = _mm(xb, w_g.astype(BF16), F32, tn=256, act="sigmoid").reshape(bsz, seq, -1)

    ncp = seq // CMP_STRIDE
    half = CMP_STRIDE * DK_C
    kv_cmp = h[:, :, qc_w:qc_w + 2 * kv_w].reshape(bsz, ncp, CMP_STRIDE, G_C, 2, DK_C)
    u = kv_cmp.transpose(0, 3, 4, 1, 2, 5).reshape(bsz, G_C, 2, ncp, half)
    pe = jnp.stack([pos_k.reshape(2, 1, half), pos_v.reshape(2, 1, half)])
    w1 = jnp.stack([k_w1.reshape(2, half, CMP_HID), v_w1.reshape(2, half, CMP_HID)])
    w2 = jnp.stack([k_w2, v_w2])
    kvc = _compress(u, pe, w1, w2)
    o_cmp, sel = _cmp_attn(h, kvc, tabs["bias_c"], tabs["overlap"], gates)

    cb = qc_w // LANE
    scale = DK_C ** -0.5
    n_pairs = H_C // 2
    nsa = dict(n_outer=n_pairs, dq=DK_C, dv=DK_C, q_col=lambda g: g, q_offs=(0, DK_C), k_offs=(0, 0),
               v_offs=(DK_C, DK_C), scale=scale, out_cols=qc_w, out_col=lambda g: g, bias_mode="pair",
               bias_idx=lambda g: g, gate=gates, vmem=VMEM_LIMIT, k_w=LANE, v_w=LANE)
    o_sel = _flash(h, h, h, name="nsa_sel_attn", T=256, k_col=lambda g: cb + 2 + g // 2, v_col=lambda g: cb + 2 + g // 2,
                   bias=tabs["bias_sel"], sel=sel, sel_idx=lambda g: g // 2,
                   gate_col=lambda g: n_pairs + g, **nsa)
    o_win = _flash(h, h, h, name="nsa_win_attn", T=WIN, k_col=lambda g: cb + 4 + g // 2, v_col=lambda g: cb + 4 + g // 2,
                   bias=tabs["bias_win"], gate_col=lambda g: 2 * n_pairs + g, nback=1, **nsa)

    lam = (jnp.exp(jnp.sum(lq1.astype(F32) * lk1.astype(F32)))
           - jnp.exp(jnp.sum(lq2.astype(F32) * lk2.astype(F32))) + lam_init).reshape(1, 1)
    db = cb + 6
    o_d = _flash(h, h, h, name="diff_attn", n_outer=H_D, T=256, dq=DD, dv=2 * DD,
                 q_col=lambda g: db + g, k_col=lambda g: db + H_D + g, v_col=lambda g: db + 2 * H_D + g,
                 q_offs=(0, DD), k_offs=(0, DD), v_offs=(0, 0), scale=DD ** -0.5,
                 out_cols=H_D * 2 * DD, out_col=lambda g: g, bias=tabs["bias_d"], bias_mode="shared",
                 bias_idx=lambda g: g, lam=lam, dnorm=d_norm[:, None], lam_init=lam_init, epilogue="diff",
                 vmem=VMEM_LIMIT)
    r2 = lambda a: a.reshape(n, -1)
    return _out_ln([r2(o_cmp), r2(o_sel), r2(o_win)], r2(o_d), w_out[:qc_w].astype(BF16),
                   w_out[qc_w:].astype(BF16), xf, ln_g[None], ln_b[None])


def _nsa_tables(rel_bias, seq):
    tab_c = rel_bias[:, H_B:H_B + H_C]
    tab_d = rel_bias[:, H_B + H_C:H_B + H_C + H_D]
    ncp = seq // CMP_STRIDE
    n_sel = seq // SEL_BLOCK
    pos = jnp.arange(seq)
    x = jnp.arange(2 * ncp)
    c_minus_a = jnp.where(x < ncp, x, x - 2 * ncp)
    dist = -CMP_STRIDE * c_minus_a[None, :] + jnp.arange(CMP_STRIDE)[:, None] - (CMP_LEN - 1)
    w = jnp.moveaxis(tab_c[_rel_bucket(dist)].astype(F32), -1, 0)
    bias_c = _toeplitz(w, ncp).transpose(0, 2, 1, 3).reshape(H_C, seq, ncp)
    c0 = jnp.arange(ncp) * CMP_STRIDE
    s0 = jnp.arange(n_sel) * SEL_BLOCK
    overlap = jnp.maximum(jnp.minimum(c0[:, None] + CMP_LEN, s0[None, :] + SEL_BLOCK)
                          - jnp.maximum(c0[:, None], s0[None, :]), 0).astype(F32) / CMP_LEN
    return {
        "bias_c": bias_c, "overlap": overlap.T,
        "bias_sel": _toeplitz_bias(tab_c, 256, seq // 256, 1, seq),
        "bias_win": _toeplitz_bias(tab_c, WIN, 2, 1, WIN - 1),
        "bias_d": _toeplitz_bias(tab_d, 256, seq // 256, 1, seq),
    }


def kernel(x, rel_bias, ab_w_in, mla_q_norm, mla_w_uq, mla_kv_norm, mla_w_ukv, ab_w_out, cd_w_in, nsa_cmp_pos_k, nsa_cmp_k_w1, nsa_cmp_k_w2, nsa_cmp_pos_v, nsa_cmp_v_w1, nsa_cmp_v_w2, diff_lambda_q1, diff_lambda_k1, diff_lambda_q2, diff_lambda_k2, diff_norm, cd_w_out, ln1_g, ln1_b, ln2_g, ln2_b, router_w, router_b, exp_w_gate, exp_w_up, exp_w_down, sh_w_gate, sh_w_up, sh_w_down):
    bsz, seq, d = x.shape
    n = bsz * seq
    depth = ln1_g.shape[0]
    rope_tabs = _rope_tables(seq)
    dil_bias = [_toeplitz_bias(rel_bias[:, gi * H_B_GROUP:(gi + 1) * H_B_GROUP], min(256, seq // dil), 2, dil,
                               window // dil) for gi, (window, dil) in enumerate(DIL_PAIRS)]
    nsa_tabs = _nsa_tables(rel_bias, seq)
    xf = x.reshape(n, d)
    xb = xf.astype(BF16)
    for l in range(depth):
        i = l // 2
        if l % 2 == 0:
            xf, xb = _mixer_ab(xb, xf, bsz, seq, ab_w_in[i], mla_q_norm[i], mla_w_uq[i], mla_kv_norm[i],
                               mla_w_ukv[i], ab_w_out[i], ln1_g[l], ln1_b[l], rope_tabs, dil_bias)
        else:
            lam_init = 0.8 - 0.6 * math.exp(-0.3 * l)
            xf, xb = _mixer_cd(xb, xf, bsz, seq, cd_w_in[i], nsa_cmp_pos_k[i], nsa_cmp_k_w1[i],
                               nsa_cmp_k_w2[i], nsa_cmp_pos_v[i], nsa_cmp_v_w1[i], nsa_cmp_v_w2[i],
                               diff_lambda_q1[i], diff_lambda_k1[i], diff_lambda_q2[i], diff_lambda_k2[i],
                               diff_norm[i], cd_w_out[i], ln1_g[l], ln1_b[l], lam_init, nsa_tabs)
        xf, xb = _moe(xf, xb, router_w[l], router_b[l], exp_w_gate[l].astype(BF16), exp_w_up[l].astype(BF16),
                      exp_w_down[l].astype(BF16), sh_w_gate[l].astype(BF16), sh_w_up[l].astype(BF16),
                      sh_w_down[l].astype(BF16), ln2_g[l], ln2_b[l])
    return xf.reshape(bsz, seq, d)
```

```python
import functools
import math

import jax
import jax.numpy as jnp
from jax import lax
from jax.experimental import pallas as pl
from jax.experimental.pallas import tpu as pltpu

F32 = jnp.float32
BF16 = jnp.bfloat16
HI = lax.Precision.HIGHEST

DEPTH = 4
NEG = -1e30
BIG = 1e9
LN_EPS = 1e-5
RMS_EPS = 1e-6
ALPHA = (2 * DEPTH) ** 0.25

N_BUCKETS = 32
REL_MAX_DIST = 2048

H_A = 12
NOPE = 64
ROPE_DIM = 32
MLA_V = 64
Q_LORA = 256
KV_LORA = 128
ROPE_THETA = 10000.0

DIL_PAIRS = ((128, 1), (512, 4), (2048, 16))
H_B_GROUP = 4
H_B = 12
HD_B = 64

H_C = 8
G_C = 2
R_C = 4
DK_C = 64
CMP_LEN = 32
CMP_STRIDE = 16
CMP_HID = 64
SEL_BLOCK = 64
SEL_TOP = 16
WIN = 512

H_D = 4
DD = 64

N_EXPERTS = 64
TOP_K = 8
N_EXPERT_GROUPS = 8
TOPK_GROUPS = 4
D_EXPERT = 256
ROUTED_SCALE = 2.5

LANE = 128
MOE_BM = 256
VMEM_LIMIT = 56 * 1024 * 1024

_NT = (((1,), (1,)), ((), ()))


def _cparams(n_axes, vmem=None):
    return pltpu.CompilerParams(dimension_semantics=("arbitrary",) * n_axes, vmem_limit_bytes=vmem)


def _mm_body(x_ref, w_ref, o_ref, *, act, precision):
    y = jnp.dot(x_ref[...], w_ref[...], preferred_element_type=F32, precision=precision)
    if act == "sigmoid":
        y = jax.nn.sigmoid(y)
    o_ref[...] = y.astype(o_ref.dtype)


def _mm(x, w, out_dtype, tn, tm=512, act=None, precision=None):
    m, k = x.shape
    nc = w.shape[1]
    return pl.pallas_call(
        functools.partial(_mm_body, act=act, precision=precision),
        grid=(m // tm, nc // tn),
        in_specs=[pl.BlockSpec((tm, k), lambda i, j: (i, 0)),
                  pl.BlockSpec((k, tn), lambda i, j: (0, j))],
        out_specs=pl.BlockSpec((tm, tn), lambda i, j: (i, j)),
        out_shape=jax.ShapeDtypeStruct((m, nc), out_dtype),
        compiler_params=_cparams(2), name="proj",
    )(x, w)


def _layer_norm(z, g, b):
    mu = jnp.mean(z, axis=-1, keepdims=True)
    zc = z - mu
    var = jnp.mean(zc * zc, axis=-1, keepdims=True)
    return zc * lax.rsqrt(var + LN_EPS) * g + b


def _out_ln_body(*refs, n_sum):
    a0 = refs[0][...].astype(F32)
    for r in refs[1:n_sum]:
        a0 = a0 + r[...].astype(F32)
    a1_ref, w0_ref, w1_ref, r_ref, g_ref, b_ref, of_ref, ob_ref = refs[n_sum:]
    y = jnp.dot(a0.astype(BF16), w0_ref[...], preferred_element_type=F32)
    y = y + jnp.dot(a1_ref[...], w1_ref[...], preferred_element_type=F32)
    out = _layer_norm(ALPHA * r_ref[...] + y, g_ref[...], b_ref[...])
    of_ref[...] = out
    ob_ref[...] = out.astype(BF16)


def _out_ln(a0s, a1, w0, w1, resid, g, b, tm=256):
    n, d = resid.shape
    k0, k1 = w0.shape[0], w1.shape[0]
    row = lambda i: (i, 0)
    fixed = lambda i: (0, 0)
    return pl.pallas_call(
        functools.partial(_out_ln_body, n_sum=len(a0s)),
        grid=(n // tm,),
        in_specs=[pl.BlockSpec((tm, k0), row)] * len(a0s) + [
            pl.BlockSpec((tm, k1), row), pl.BlockSpec((k0, d), fixed), pl.BlockSpec((k1, d), fixed),
            pl.BlockSpec((tm, d), row), pl.BlockSpec((1, d), fixed), pl.BlockSpec((1, d), fixed)],
        out_specs=[pl.BlockSpec((tm, d), row), pl.BlockSpec((tm, d), row)],
        out_shape=[jax.ShapeDtypeStruct((n, d), F32), jax.ShapeDtypeStruct((n, d), BF16)],
        compiler_params=_cparams(1), name="out_proj_ln",
    )(*a0s, a1, w0, w1, resid, g, b)


def _rms(x, g):
    return x * lax.rsqrt(jnp.mean(x * x, axis=-1, keepdims=True) + RMS_EPS) * g


def _mla_q_body(c_ref, g_ref, w_ref, wr_ref, cos_ref, sin_ref, o_ref):
    cn = _rms(c_ref[...], g_ref[...]).astype(BF16)
    a = jnp.dot(cn, w_ref[...], preferred_element_type=F32)
    r = jnp.dot(cn, wr_ref[...], preferred_element_type=F32)
    cos, sin = cos_ref[...], sin_ref[...]
    for h in range(H_A):
        sl = slice(h * LANE, (h + 1) * LANE)
        o_ref[:, sl] = (a[:, sl] * cos + r[:, sl] * sin).astype(o_ref.dtype)


def _mla_kv_body(c_ref, kr_ref, krr_ref, g_ref, wk_ref, wv_ref, cos_ref, sin_ref, k_ref, v_ref):
    cn = _rms(c_ref[...], g_ref[...]).astype(BF16)
    kn = jnp.dot(cn, wk_ref[...], preferred_element_type=F32)
    rope = kr_ref[...] * cos_ref[...] + krr_ref[...] * sin_ref[...]
    for h in range(H_A):
        sl = slice(h * LANE, (h + 1) * LANE)
        k_ref[:, sl] = (kn[:, sl] + rope).astype(k_ref.dtype)
    v_ref[...] = jnp.dot(cn, wv_ref[...], preferred_element_type=F32).astype(v_ref.dtype)


def _mla_up(h1, q_norm, wq, wq_rot, kv_norm, wk, wv, cos_q, sin_q, cos_k, sin_k, seq, tm=512):
    n = h1.shape[0]
    spt = seq // tm
    row = lambda c: (lambda i: (i, c))
    pos = lambda i: (i % spt, 0)
    fixed = lambda i: (0, 0)
    q_a = pl.pallas_call(
        _mla_q_body, grid=(n // tm,),
        in_specs=[pl.BlockSpec((tm, Q_LORA), row(0)), pl.BlockSpec((1, Q_LORA), fixed),
                  pl.BlockSpec(wq.shape, fixed), pl.BlockSpec(wq_rot.shape, fixed),
                  pl.BlockSpec((tm, LANE), pos), pl.BlockSpec((tm, LANE), pos)],
        out_specs=pl.BlockSpec((tm, H_A * LANE), row(0)),
        out_shape=jax.ShapeDtypeStruct((n, H_A * LANE), BF16),
        compiler_params=_cparams(1), name="mla_q_up",
    )(h1, q_norm, wq, wq_rot, cos_q, sin_q)
    k_a, v_a = pl.pallas_call(
        _mla_kv_body, grid=(n // tm,),
        in_specs=[pl.BlockSpec((tm, LANE), row(2)), pl.BlockSpec((tm, LANE), row(3)),
                  pl.BlockSpec((tm, LANE), row(4)), pl.BlockSpec((1, KV_LORA), fixed),
                  pl.BlockSpec(wk.shape, fixed), pl.BlockSpec(wv.shape, fixed),
                  pl.BlockSpec((tm, LANE), pos), pl.BlockSpec((tm, LANE), pos)],
        out_specs=[pl.BlockSpec((tm, H_A * LANE), row(0)), pl.BlockSpec((tm, H_A * MLA_V), row(0))],
        out_shape=[jax.ShapeDtypeStruct((n, H_A * LANE), BF16), jax.ShapeDtypeStruct((n, H_A * MLA_V), BF16)],
        compiler_params=_cparams(1), name="mla_kv_up",
    )(h1, h1, h1, kv_norm, wk, wv, cos_k, sin_k)
    return q_a, k_a, v_a


def _flash_body(*refs, T, dq, dv, q_offs, k_offs, v_offs, scale, bias_mode, has_sel, has_gate,
                epilogue, want_lse, seg_tiles, nback, lam_init):
    it = iter(refs)
    q_ref, k_ref, v_ref = next(it), next(it), next(it)
    bias_ref = next(it) if bias_mode else None
    sel_ref = next(it) if has_sel else None
    gate_ref = next(it) if has_gate else None
    lam_ref, dn_ref = (next(it), next(it)) if epilogue == "diff" else (None, None)
    o_ref = next(it)
    lse_ref = next(it) if want_lse else None
    vt_scr = next(it)

    qi = pl.program_id(2)
    seq = v_ref.shape[1]

    @pl.when(qi == 0)
    def _():
        for c in range(seq // T):
            vt_scr[:, c * T:(c + 1) * T] = v_ref[0, c * T:(c + 1) * T, :].astype(F32).T.astype(BF16)

    qfull = q_ref[0].astype(F32)
    fold_scale = math.frexp(scale)[0] == 0.5
    qts = [(qfull[:, off:off + dq] * (scale if fold_scale else 1.0)).T.astype(BF16) for off in q_offs]

    def qk(kc):
        kfull = k_ref[0, pl.ds(pl.multiple_of(kc * T, T), T), :]
        return tuple(jnp.dot(kfull[:, k_offs[u]:k_offs[u] + dq], qts[u], preferred_element_type=F32)
                     for u in range(2))

    def update(kc, state, scores, diag):
        start = pl.multiple_of(kc * T, T)
        new_state = []
        sel_add = None
        if has_sel:
            per = T // SEL_BLOCK
            rows = [sel_ref[0, 0, pl.ds(kc * per + a, 1), :] for a in range(per)]
            sel_add = jnp.concatenate([jnp.broadcast_to((r - 1.0) * (-NEG), (SEL_BLOCK, T)) for r in rows], axis=0)
        for u in range(2):
            vt = vt_scr[v_offs[u]:v_offs[u] + dv, pl.ds(start, T)]
            s = scores[u]
            if not fold_scale:
                s = s * scale
            if bias_mode:
                s = s + bias_ref[u if bias_mode == "pair" else 0, qi - kc]
            elif diag:
                key = lax.broadcasted_iota(jnp.int32, (T, T), 0)
                qry = lax.broadcasted_iota(jnp.int32, (T, T), 1)
                s = jnp.where(key <= qry, s, NEG)
            if has_sel:
                s = s + sel_add
            m_prev, l_prev, acc_prev = state[u]
            m_new = jnp.maximum(m_prev, jnp.max(s, axis=0, keepdims=True))
            alpha = jnp.exp(m_prev - m_new)
            p = jnp.exp(s - m_new)
            l_new = alpha * l_prev + jnp.sum(p, axis=0, keepdims=True)
            acc_new = alpha * acc_prev + jnp.dot(vt, p.astype(BF16), preferred_element_type=F32)
            new_state.append((m_new, l_new, acc_new))
        return tuple(new_state)

    init = tuple((jnp.full((1, T), NEG, F32), jnp.zeros((1, T), F32), jnp.zeros((dv, T), F32)) for _ in range(2))
    if nback is None:
        lo = 0
    elif seg_tiles is not None:
        lo = jnp.where(qi % seg_tiles == 0, qi, qi - nback)
    else:
        lo = jnp.maximum(qi - nback, 0)

    def step(kc, carry):
        state, scores = carry
        nxt = qk(kc + 1)
        return update(kc, state, scores, False), nxt

    state, scores = lax.fori_loop(lo, qi, step, (init, qk(lo)))
    state = update(qi, state, scores, True)

    outs = [acc / l for _, l, acc in state]
    if epilogue == "diff":
        a = outs[0] - lam_ref[0, 0] * outs[1]
        rinv = lax.rsqrt(jnp.mean(a * a, axis=0, keepdims=True) + RMS_EPS)
        o = (a * rinv * dn_ref[...] * (1.0 - lam_init)).T
    else:
        o = jnp.concatenate(outs, axis=0).T
        if has_gate:
            o = o * gate_ref[0]
    o_ref[0] = o.astype(o_ref.dtype)
    if want_lse:
        lse_ref[0, 0] = jnp.concatenate([m + jnp.log(l) for m, l, _ in state], axis=0)


def _flash(q, k, v, *, n_outer, T, dq, dv, q_col, k_col, v_col, q_offs, k_offs, v_offs, scale,
           out_cols, out_col, bias=None, bias_mode=None, bias_idx=None, sel=None, sel_idx=None,
           gate=None, gate_col=None, lam=None, dnorm=None, lam_init=0.0, epilogue="plain", want_lse=False,
           seg_tiles=None, nback=None, vmem=None, k_w=None, v_w=None, name="flash"):
    bsz, seq, _ = q.shape
    nq = seq // T
    qw = max(o + dq for o in q_offs)
    kw = k_w or max(o + dq for o in k_offs)
    vw = v_w or max(o + dv for o in v_offs)
    ow = dv if epilogue == "diff" else 2 * dv
    in_specs = [pl.BlockSpec((1, T, qw), lambda g, b, i: (b, i, q_col(g))),
                pl.BlockSpec((1, seq, kw), lambda g, b, i: (b, 0, k_col(g))),
                pl.BlockSpec((1, seq, vw), lambda g, b, i: (b, 0, v_col(g)))]
    args = [q, k, v]
    if bias_mode:
        nb = 2 if bias_mode == "pair" else 1
        in_specs.append(pl.BlockSpec((nb,) + bias.shape[1:], lambda g, b, i: (bias_idx(g), 0, 0, 0)))
        args.append(bias)
    if sel is not None:
        in_specs.append(pl.BlockSpec((1, 1, sel.shape[2], T), lambda g, b, i: (b, sel_idx(g), 0, i)))
        args.append(sel)
    if gate is not None:
        in_specs.append(pl.BlockSpec((1, T, ow), lambda g, b, i: (b, i, gate_col(g))))
        args.append(gate)
    if epilogue == "diff":
        in_specs.append(pl.BlockSpec(memory_space=pltpu.SMEM))
        in_specs.append(pl.BlockSpec((dv, 1), lambda g, b, i: (0, 0)))
        args += [lam, dnorm]
    out_specs = [pl.BlockSpec((1, T, ow), lambda g, b, i: (b, i, out_col(g)))]
    out_shape = [jax.ShapeDtypeStruct((bsz, seq, out_cols), BF16)]
    if want_lse:
        out_specs.append(pl.BlockSpec((1, 1, 2, T), lambda g, b, i: (b, g, 0, i)))
        out_shape.append(jax.ShapeDtypeStruct((bsz, n_outer, 2, seq), F32))
    body = functools.partial(
        _flash_body, T=T, dq=dq, dv=dv, q_offs=q_offs, k_offs=k_offs, v_offs=v_offs, scale=scale,
        bias_mode=bias_mode, has_sel=sel is not None, has_gate=gate is not None, epilogue=epilogue,
        want_lse=want_lse, seg_tiles=seg_tiles, nback=nback, lam_init=lam_init)
    res = pl.pallas_call(
        body, grid=(n_outer, bsz, nq), in_specs=in_specs, out_specs=out_specs, out_shape=out_shape,
        scratch_shapes=[pltpu.VMEM((vw, seq), BF16)],
        compiler_params=_cparams(3, vmem), name=name,
    )(*args)
    return res if want_lse else res[0]


def _rel_bucket(dist):
    n = jnp.maximum(dist, 0)
    exact = N_BUCKETS // 2
    log_ratio = jnp.log(jnp.maximum(n, 1).astype(F32) / exact) / math.log(REL_MAX_DIST / exact)
    large = exact + (log_ratio * (N_BUCKETS - exact)).astype(jnp.int32)
    return jnp.where(n < exact, n, jnp.minimum(large, N_BUCKETS - 1))


def _toeplitz_bias(tab, T, n_d, dist_scale, max_dist):
    x = jnp.arange(2 * T)
    dist = jnp.arange(n_d)[:, None] * T + jnp.where(x < T, x, x - 2 * T)[None, :]
    w = tab[_rel_bucket(dist * dist_scale)].astype(F32)
    w = jnp.where(((dist >= 0) & (dist <= max_dist))[..., None], w, NEG)
    return _toeplitz(jnp.moveaxis(w, -1, 0), T)


def _toeplitz(w, t):
    tiled = jnp.tile(w, (1,) * (w.ndim - 1) + (t,))[..., :t * (2 * t - 1)]
    return tiled.reshape(w.shape[:-1] + (t, 2 * t - 1))[..., :t]


def _compress_body(u_ref, pe_ref, w1_ref, w2_ref, o_ref, *, ncp):
    outs = []
    for a in range(2):
        u = u_ref[0, 0, a].astype(F32)
        p1 = jnp.dot(u + pe_ref[a, 0], w1_ref[a, 0], preferred_element_type=F32, precision=HI)
        p2 = jnp.dot(u + pe_ref[a, 1], w1_ref[a, 1], preferred_element_type=F32, precision=HI)
        hid = jax.nn.gelu(p1 + pltpu.roll(p2, ncp - 1, 0))
        outs.append(jnp.dot(hid, w2_ref[a], preferred_element_type=F32, precision=HI))
    o_ref[0, 0] = jnp.concatenate(outs, axis=-1)


def _compress(u, pe, w1, w2):
    bsz, g, _, ncp, width = u.shape
    return pl.pallas_call(
        functools.partial(_compress_body, ncp=ncp), grid=(bsz, g),
        in_specs=[pl.BlockSpec((1, 1, 2, ncp, width), lambda b, gg: (b, gg, 0, 0, 0)),
                  pl.BlockSpec(pe.shape, lambda b, gg: (0, 0, 0, 0)),
                  pl.BlockSpec(w1.shape, lambda b, gg: (0, 0, 0, 0)),
                  pl.BlockSpec(w2.shape, lambda b, gg: (0, 0, 0))],
        out_specs=pl.BlockSpec((1, 1, ncp, 2 * DK_C), lambda b, gg: (b, gg, 0, 0)),
        out_shape=jax.ShapeDtypeStruct((bsz, g, ncp, 2 * DK_C), F32),
        compiler_params=_cparams(2), name="nsa_compress",
    )(u, pe, w1, w2)


def _cmp_attn_body(q_ref, kv_ref, bias_ref, ov_ref, gate_ref, o_ref, sel_ref, *, T, ncp, n_sel, n_top, scale):
    qi = pl.program_id(2)
    kc = kv_ref[0, 0, :, :DK_C]
    vc = kv_ref[0, 0, :, DK_C:]
    t = qi * T + lax.broadcasted_iota(jnp.int32, (T, ncp), 0)
    c = lax.broadcasted_iota(jnp.int32, (T, ncp), 1)
    valid = t >= c * CMP_STRIDE + (CMP_LEN - 1)
    validf = valid.astype(F32)
    psum = jnp.zeros((T, ncp), F32)
    outs = []
    for r in range(R_C):
        q = q_ref[0, :, r * DK_C:(r + 1) * DK_C].astype(F32)
        s = lax.dot_general(q, kc, _NT, preferred_element_type=F32, precision=HI) * scale + bias_ref[r]
        s = jnp.where(valid, s, NEG)
        e = jnp.exp(s - jnp.max(s, axis=-1, keepdims=True)) * validf
        p = e / jnp.maximum(jnp.sum(e, axis=-1, keepdims=True), 1e-30)
        outs.append(jnp.dot(p, vc, preferred_element_type=F32, precision=HI))
        psum = psum + p
    o_ref[0] = (jnp.concatenate(outs, axis=-1) * gate_ref[0]).astype(o_ref.dtype)

    imp = jnp.dot(ov_ref[...], psum.T, preferred_element_type=F32, precision=HI)
    tq = qi * T + lax.broadcasted_iota(jnp.int32, (n_sel, T), 1)
    j = lax.broadcasted_iota(jnp.int32, (n_sel, T), 0)
    forced = (j == tq // SEL_BLOCK) | (j == 0)
    work = jnp.where(forced, BIG, jnp.where(j * SEL_BLOCK <= tq, imp, -BIG))
    sel = jnp.zeros((n_sel, T), F32)
    jf = j.astype(F32)
    for _ in range(n_top):
        _, _, pick = _first_max(work, jf, n_sel)
        sel = jnp.where(pick, 1.0, sel)
        work = jnp.where(pick, -jnp.inf, work)
    sel_ref[0, 0] = sel


def _cmp_attn(h, kvc, bias_c, overlap, gates, T=256):
    bsz, seq, _ = h.shape
    ncp = kvc.shape[2]
    n_sel = seq // SEL_BLOCK
    n_top = min(SEL_TOP, n_sel)
    qw = R_C * DK_C
    return pl.pallas_call(
        functools.partial(_cmp_attn_body, T=T, ncp=ncp, n_sel=n_sel, n_top=n_top, scale=DK_C ** -0.5),
        grid=(G_C, bsz, seq // T),
        in_specs=[pl.BlockSpec((1, T, qw), lambda g, b, i: (b, i, g)),
                  pl.BlockSpec((1, 1, ncp, 2 * DK_C), lambda g, b, i: (b, g, 0, 0)),
                  pl.BlockSpec((R_C, T, ncp), lambda g, b, i: (g, i, 0)),
                  pl.BlockSpec(overlap.shape, lambda g, b, i: (0, 0)),
                  pl.BlockSpec((1, T, qw), lambda g, b, i: (b, i, g))],
        out_specs=[pl.BlockSpec((1, T, qw), lambda g, b, i: (b, i, g)),
                   pl.BlockSpec((1, 1, n_sel, T), lambda g, b, i: (b, g, 0, i))],
        out_shape=[jax.ShapeDtypeStruct((bsz, seq, H_C * DK_C), BF16),
                   jax.ShapeDtypeStruct((bsz, G_C, n_sel, seq), F32)],
        compiler_params=_cparams(3), name="nsa_cmp_attn",
    )(h, kvc, bias_c, overlap, gates)


def _first_max(work, idx, n):
    mx = jnp.max(work, axis=0, keepdims=True)
    first = jnp.min(jnp.where(work == mx, idx, float(n)), axis=0, keepdims=True)
    return mx, first, idx == first


def _router_body(x_ref, wt_ref, b_ref, tri_ref, e_ref, g_ref, r_ref, cnt_ref, carry_scr, *, tm):
    i = pl.program_id(0)

    @pl.when(i == 0)
    def _():
        carry_scr[...] = jnp.zeros(carry_scr.shape, F32)

    st = lax.dot_general(wt_ref[...], x_ref[...], _NT, preferred_element_type=F32, precision=HI)
    scores = jax.nn.sigmoid(st)
    sel = scores + b_ref[...]
    per = N_EXPERTS // N_EXPERT_GROUPS
    fiota = lambda rows: lax.broadcasted_iota(jnp.int32, (rows, tm), 0).astype(F32)
    i_per, i_grp, i_exp = fiota(per), fiota(N_EXPERT_GROUPS), fiota(N_EXPERTS)
    grp_scores = []
    for g in range(N_EXPERT_GROUPS):
        blk = sel[g * per:(g + 1) * per]
        m1, _, pick = _first_max(blk, i_per, per)
        grp_scores.append(m1 + jnp.max(jnp.where(pick, -jnp.inf, blk), axis=0, keepdims=True))
    work = jnp.concatenate(grp_scores, axis=0)
    gmask = jnp.zeros((N_EXPERT_GROUPS, tm), F32)
    for _ in range(TOPK_GROUPS):
        _, _, pick = _first_max(work, i_grp, N_EXPERT_GROUPS)
        gmask = jnp.where(pick, 1.0, gmask)
        work = jnp.where(pick, -jnp.inf, work)
    work = jnp.concatenate([jnp.where(gmask[g:g + 1] > 0.5, sel[g * per:(g + 1) * per], NEG)
                            for g in range(N_EXPERT_GROUPS)], axis=0)
    picks, firsts, vals = [], [], []
    for _ in range(TOP_K):
        _, first, pick = _first_max(work, i_exp, N_EXPERTS)
        picks.append(pick)
        firsts.append(first)
        vals.append(jnp.sum(jnp.where(pick, scores, 0.0), axis=0, keepdims=True))
        work = jnp.where(pick, -jnp.inf, work)
    val = jnp.concatenate(vals, axis=0)
    g_ref[...] = val / jnp.sum(val, axis=0, keepdims=True) * ROUTED_SCALE
    e_ref[...] = jnp.concatenate(firsts, axis=0).astype(jnp.int32)
    onehot = picks[0].astype(F32)
    for pick in picks[1:]:
        onehot = onehot + pick.astype(F32)
    before = jnp.dot(onehot.astype(BF16), tri_ref[...], preferred_element_type=F32) + carry_scr[...]
    r_ref[...] = jnp.concatenate([jnp.sum(jnp.where(pick, before, 0.0), axis=0, keepdims=True)
                                  for pick in picks], axis=0).astype(jnp.int32)
    carry = carry_scr[...] + jnp.sum(onehot, axis=1, keepdims=True)
    carry_scr[...] = carry
    cnt_ref[...] = jnp.broadcast_to(carry, cnt_ref.shape)


def _router(xf, router_w, router_b, tm=256):
    n, d = xf.shape
    tri = (jnp.arange(tm)[:, None] < jnp.arange(tm)[None, :]).astype(BF16)
    col = lambda i: (0, i)
    fixed = lambda i: (0, 0)
    return pl.pallas_call(
        functools.partial(_router_body, tm=tm), grid=(n // tm,),
        in_specs=[pl.BlockSpec((tm, d), lambda i: (i, 0)), pl.BlockSpec((N_EXPERTS, d), fixed),
                  pl.BlockSpec((N_EXPERTS, 1), fixed), pl.BlockSpec((tm, tm), fixed)],
        out_specs=[pl.BlockSpec((TOP_K, tm), col), pl.BlockSpec((TOP_K, tm), col),
                   pl.BlockSpec((TOP_K, tm), col), pl.BlockSpec((N_EXPERTS, LANE), fixed)],
        out_shape=[jax.ShapeDtypeStruct((TOP_K, n), jnp.int32), jax.ShapeDtypeStruct((TOP_K, n), F32),
                   jax.ShapeDtypeStruct((TOP_K, n), jnp.int32), jax.ShapeDtypeStruct((N_EXPERTS, LANE), F32)],
        scratch_shapes=[pltpu.VMEM((N_EXPERTS, 1), F32)],
        compiler_params=_cparams(1), name="router",
    )(xf, router_w.T, router_b.astype(F32)[:, None], tri)


def _pack_rows(x):
    w = x.shape[-1] // 2
    lo = lax.bitcast_convert_type(x[:, :w].astype(BF16).astype(F32), jnp.uint32)
    hi = lax.bitcast_convert_type(x[:, w:].astype(BF16).astype(F32), jnp.uint32)
    return (lo >> 16) | (hi & jnp.uint32(0xFFFF0000))


def _unpack_rows(p):
    lo = lax.bitcast_convert_type(p << 16, F32)
    hi = lax.bitcast_convert_type(p & jnp.uint32(0xFFFF0000), F32)
    return lo, hi


def _dispatch_body(ps_ref, fill_ref, e_ref, r_ref, x_ref, z_ref, xs_hbm, xp_scr, sem, *, tm, n_blocks):
    i = pl.program_id(0)
    xp_scr[...] = _pack_rows(x_ref[...])

    def row_copy(src, dst):
        return pltpu.make_async_copy(xp_scr.at[pl.ds(src, 1)], xs_hbm.at[pl.ds(dst, 1)], sem)

    def zero_copy(dst):
        return pltpu.make_async_copy(z_ref.at[pl.ds(0, 1)], xs_hbm.at[pl.ds(dst, 1)], sem)

    def zero_block(blk):
        return pltpu.make_async_copy(z_ref, xs_hbm.at[pl.ds(blk * MOE_BM, MOE_BM)], sem)

    def issue(t, c):
        for k in range(TOP_K):
            row_copy(t, ps_ref[e_ref[k, t]] + r_ref[k, t]).start(priority=k % 2)
        return c

    def drain(t, c):
        for k in range(TOP_K):
            row_copy(0, 0).wait()
        return c

    lax.fori_loop(0, tm, issue, 0)
    lax.fori_loop(0, tm, drain, 0)

    @pl.when(i == 0)
    def _():
        def fill(e, c):
            lo, hi = fill_ref[2 * e], fill_ref[2 * e + 1]

            def start(r, cc):
                zero_copy(r).start()
                return cc

            def wait(r, cc):
                zero_copy(0).wait()
                return cc

            lax.fori_loop(lo, hi, start, 0)
            lax.fori_loop(lo, hi, wait, 0)
            return c

        lax.fori_loop(0, N_EXPERTS, fill, 0)

        def tail_start(blk, c):
            zero_block(blk).start()
            return c

        def tail_wait(blk, c):
            zero_block(0).wait()
            return c

        lax.fori_loop(fill_ref[2 * N_EXPERTS], n_blocks, tail_start, 0)
        lax.fori_loop(fill_ref[2 * N_EXPERTS], n_blocks, tail_wait, 0)


def _dispatch(xb, e_idx, rank, pad_start, fill, p, tm=128):
    n, d = xb.shape
    smem_col = pl.BlockSpec((TOP_K, tm), lambda i, ps, fl: (0, i), memory_space=pltpu.SMEM)
    grid_spec = pltpu.PrefetchScalarGridSpec(
        num_scalar_prefetch=2, grid=(n // tm,),
        in_specs=[smem_col, smem_col, pl.BlockSpec((tm, d), lambda i, ps, fl: (i, 0)),
                  pl.BlockSpec((MOE_BM, d // 2), lambda i, ps, fl: (0, 0))],
        out_specs=pl.BlockSpec(memory_space=pl.ANY),
        scratch_shapes=[pltpu.VMEM((tm, d // 2), jnp.uint32), pltpu.SemaphoreType.DMA(())])
    return pl.pallas_call(
        functools.partial(_dispatch_body, tm=tm, n_blocks=p // MOE_BM), grid_spec=grid_spec,
        out_shape=jax.ShapeDtypeStruct((p, d // 2), jnp.uint32),
        compiler_params=_cparams(1), name="dispatch",
    )(pad_start, fill, e_idx, rank, xb, jnp.zeros((MOE_BM, d // 2), jnp.uint32))


def _moe_ffn_body(be_ref, nb_ref, x_ref, wg_ref, wu_ref, wd_ref, o_ref, wg_scr, wu_scr, wd_scr):
    i = pl.program_id(0)
    half = x_ref.shape[1]

    @pl.when((i == 0) | (be_ref[i] != be_ref[jnp.maximum(i - 1, 0)]))
    def _():
        wg_scr[...] = wg_ref[0].astype(BF16)
        wu_scr[...] = wu_ref[0].astype(BF16)
        wd_scr[...] = wd_ref[0].astype(BF16)

    @pl.when(i < nb_ref[0])
    def _():
        lo, hi = _unpack_rows(x_ref[...])
        lo, hi = lo.astype(BF16), hi.astype(BF16)
        hg = (jnp.dot(lo, wg_scr[:half], preferred_element_type=F32)
              + jnp.dot(hi, wg_scr[half:], preferred_element_type=F32))
        hu = (jnp.dot(lo, wu_scr[:half], preferred_element_type=F32)
              + jnp.dot(hi, wu_scr[half:], preferred_element_type=F32))
        hb = (hg * jax.nn.sigmoid(hg) * hu).astype(BF16)
        o_ref[...] = _pack_rows(jnp.dot(hb, wd_scr[...], preferred_element_type=F32))

    @pl.when(i >= nb_ref[0])
    def _():
        o_ref[...] = jnp.zeros(o_ref.shape, o_ref.dtype)


def _moe_ffn(xs, blk_e, n_used, wg, wu, wd):
    p, half = xs.shape
    d = 2 * half
    n_blocks = p // MOE_BM
    grid_spec = pltpu.PrefetchScalarGridSpec(
        num_scalar_prefetch=2, grid=(n_blocks,),
        in_specs=[pl.BlockSpec((MOE_BM, half), lambda i, be, nb: (i, 0)),
                  pl.BlockSpec((1, d, D_EXPERT), lambda i, be, nb: (be[i], 0, 0)),
                  pl.BlockSpec((1, d, D_EXPERT), lambda i, be, nb: (be[i], 0, 0)),
                  pl.BlockSpec((1, D_EXPERT, d), lambda i, be, nb: (be[i], 0, 0))],
        out_specs=pl.BlockSpec((MOE_BM, half), lambda i, be, nb: (i, 0)),
        scratch_shapes=[pltpu.VMEM((d, D_EXPERT), BF16), pltpu.VMEM((d, D_EXPERT), BF16),
                        pltpu.VMEM((D_EXPERT, d), BF16)])
    return pl.pallas_call(
        _moe_ffn_body, grid_spec=grid_spec, out_shape=jax.ShapeDtypeStruct((p, half), jnp.uint32),
        compiler_params=_cparams(1), name="expert_ffn",
    )(blk_e, n_used, xs, wg, wu, wd)


def _combine_body(ps_ref, e_ref, r_ref, en_ref, rn_ref, gate_ref, xb_ref, xf_ref, y_hbm, sg_ref, su_ref, sd_ref,
                  g_ref, b_ref, of_ref, ob_ref, buf, sem, *, tm):
    i = pl.program_id(0)
    slot = i % 2

    def row_copy(s, k, t, src):
        return pltpu.make_async_copy(y_hbm.at[pl.ds(src, 1)], buf.at[s, k, pl.ds(t, 1)], sem.at[s])

    def issue_tile(er, rr, s):
        def issue(t, c):
            for k in range(TOP_K):
                row_copy(s, k, t, ps_ref[er[k, t]] + rr[k, t]).start(priority=k % 2)
            return c

        lax.fori_loop(0, tm, issue, 0)

    @pl.when(i == 0)
    def _():
        issue_tile(e_ref, r_ref, 0)

    @pl.when(i + 1 < pl.num_programs(0))
    def _():
        issue_tile(en_ref, rn_ref, 1 - slot)

    x = xb_ref[...]
    hg = jnp.dot(x, sg_ref[...], preferred_element_type=F32)
    hu = jnp.dot(x, su_ref[...], preferred_element_type=F32)
    hb = (hg * jax.nn.sigmoid(hg) * hu).astype(BF16)
    y = jnp.dot(hb, sd_ref[...], preferred_element_type=F32)

    def drain(t, c):
        for k in range(TOP_K):
            row_copy(slot, k, t, 0).wait()
        return c

    lax.fori_loop(0, tm, drain, 0)
    gate = gate_ref[...]
    half = buf.shape[-1]
    y_lo, y_hi = y[:, :half], y[:, half:]
    for k in range(TOP_K):
        lo, hi = _unpack_rows(buf[slot, k])
        y_lo = y_lo + gate[:, k:k + 1] * lo
        y_hi = y_hi + gate[:, k:k + 1] * hi
    out = _layer_norm(ALPHA * xf_ref[...] + jnp.concatenate([y_lo, y_hi], axis=-1), g_ref[...], b_ref[...])
    of_ref[...] = out
    ob_ref[...] = out.astype(BF16)


def _combine(pad_start, e_idx, rank, gate, xb, xf, y, sg, su, sd, g, b, tm=128):
    n, d = xf.shape
    n_tiles = n // tm
    row = lambda i, ps: (i, 0)
    fixed = lambda i, ps: (0, 0)
    smem_col = pl.BlockSpec((TOP_K, tm), lambda i, ps: (0, i), memory_space=pltpu.SMEM)
    smem_next = pl.BlockSpec((TOP_K, tm), lambda i, ps: (0, jnp.minimum(i + 1, n_tiles - 1)),
                             memory_space=pltpu.SMEM)
    grid_spec = pltpu.PrefetchScalarGridSpec(
        num_scalar_prefetch=1, grid=(n_tiles,),
        in_specs=[smem_col, smem_col, smem_next, smem_next, pl.BlockSpec((tm, TOP_K), row),
                  pl.BlockSpec((tm, d), row), pl.BlockSpec((tm, d), row), pl.BlockSpec(memory_space=pl.ANY),
                  pl.BlockSpec(sg.shape, fixed), pl.BlockSpec(su.shape, fixed), pl.BlockSpec(sd.shape, fixed),
                  pl.BlockSpec((1, d), fixed), pl.BlockSpec((1, d), fixed)],
        out_specs=[pl.BlockSpec((tm, d), row), pl.BlockSpec((tm, d), row)],
        scratch_shapes=[pltpu.VMEM((2, TOP_K, tm, d // 2), jnp.uint32), pltpu.SemaphoreType.DMA((2,))])
    return pl.pallas_call(
        functools.partial(_combine_body, tm=tm), grid_spec=grid_spec,
        out_shape=[jax.ShapeDtypeStruct((n, d), F32), jax.ShapeDtypeStruct((n, d), BF16)],
        compiler_params=_cparams(1), name="combine",
    )(pad_start, e_idx, rank, e_idx, rank, gate, xb, xf, y, sg, su, sd, g, b)


def _moe(xf, xb, router_w, router_b, w_gate, w_up, w_down, sh_gate, sh_up, sh_down, ln_g, ln_b):
    n, d = xf.shape
    e_idx, gate, rank, cnt = _router(xf, router_w, router_b)
    counts = cnt[:, 0].astype(jnp.int32)
    padded = (counts + MOE_BM - 1) // MOE_BM * MOE_BM
    pad_end = jnp.cumsum(padded)
    pad_start = (pad_end - padded).astype(jnp.int32)
    n_blocks = (n * TOP_K + N_EXPERTS * (MOE_BM - 1) + MOE_BM - 1) // MOE_BM
    blk_e = jnp.minimum(jnp.sum(pad_end[None, :] <= (jnp.arange(n_blocks) * MOE_BM)[:, None], axis=1),
                        N_EXPERTS - 1).astype(jnp.int32)
    n_used = (pad_end[-1] // MOE_BM).astype(jnp.int32).reshape(1)
    fill = jnp.concatenate([jnp.stack([pad_start + counts, pad_end], axis=1).reshape(-1), n_used]).astype(jnp.int32)
    xs = _dispatch(xb, e_idx, rank, pad_start, fill, n_blocks * MOE_BM)
    y = _moe_ffn(xs, blk_e, n_used, w_gate, w_up, w_down)
    return _combine(pad_start, e_idx, rank, gate.T, xb, xf, y, sh_gate, sh_up, sh_down, ln_g[None], ln_b[None])


def _rope_tables(seq):
    half = ROPE_DIM // 2
    freqs = ROPE_THETA ** (-jnp.arange(half, dtype=F32) / half)
    ang = jnp.arange(seq).astype(F32)[:, None] * freqs
    cos = jnp.concatenate([jnp.cos(ang)] * 2, -1)
    sin = jnp.concatenate([jnp.sin(ang)] * 2, -1)
    z = lambda w: jnp.zeros((seq, w), F32)
    pad = LANE - NOPE - ROPE_DIM
    cos_q = jnp.concatenate([jnp.ones((seq, NOPE), F32), cos, z(pad)], -1)
    sin_q = jnp.concatenate([z(NOPE), sin, z(pad)], -1)
    cos_k = jnp.concatenate([z(NOPE), cos, z(pad)], -1)
    return cos_q, sin_q, cos_k, sin_q


def _rot_cols(w):
    half = w.shape[-1] // 2
    return jnp.concatenate([-w[..., half:], w[..., :half]], -1)


def _mixer_ab(xb, xf, bsz, seq, w_in, q_norm, w_uq, kv_norm, w_ukv, w_out, ln_g, ln_b, rope_tabs, dil_bias):
    n = bsz * seq
    d = w_in.shape[0]
    c0 = Q_LORA + KV_LORA
    w_kr = w_in[:, c0:c0 + ROPE_DIM]
    zc = lambda w: jnp.zeros((d, w), F32)
    pad = LANE - NOPE - ROPE_DIM
    w1 = jnp.concatenate([w_in[:, :c0], zc(NOPE), w_kr, zc(pad), zc(NOPE), _rot_cols(w_kr), zc(pad)], 1)
    h1 = _mm(xb, w1.astype(BF16), F32, tn=LANE)
    h2 = _mm(xb, w_in[:, c0 + ROPE_DIM:].astype(BF16), BF16, tn=256)

    wq = w_uq.reshape(Q_LORA, H_A, NOPE + ROPE_DIM)
    zq = jnp.zeros((Q_LORA, H_A, pad), F32)
    wq_main = jnp.concatenate([wq, zq], -1).reshape(Q_LORA, H_A * LANE)
    wq_rot = jnp.concatenate([jnp.zeros((Q_LORA, H_A, NOPE), F32), _rot_cols(wq[..., NOPE:]), zq], -1)
    wq_rot = wq_rot.reshape(Q_LORA, H_A * LANE)
    wkv = w_ukv.reshape(KV_LORA, H_A, NOPE + MLA_V)
    wk = jnp.concatenate([wkv[..., :NOPE], jnp.zeros((KV_LORA, H_A, LANE - NOPE), F32)], -1)
    wk = wk.reshape(KV_LORA, H_A * LANE)
    wv = wkv[..., NOPE:].reshape(KV_LORA, H_A * MLA_V)
    q_a, k_a, v_a = _mla_up(h1, q_norm[None], wq_main.astype(BF16), wq_rot.astype(BF16), kv_norm[None],
                            wk.astype(BF16), wv.astype(BF16), *rope_tabs, seq)
    o_a = _flash(q_a.reshape(bsz, seq, -1), k_a.reshape(bsz, seq, -1), v_a.reshape(bsz, seq, -1), name="mla_attn",
                 n_outer=H_A // 2, T=256, dq=LANE, dv=MLA_V,
                 q_col=lambda g: g, k_col=lambda g: g, v_col=lambda g: g,
                 q_offs=(0, LANE), k_offs=(0, LANE), v_offs=(0, MLA_V),
                 scale=(NOPE + ROPE_DIM) ** -0.5, out_cols=H_A * MLA_V, out_col=lambda g: g)

    gw = H_B_GROUP * HD_B
    hq = h2.reshape(bsz, seq, 3, len(DIL_PAIRS), gw)
    outs, lses = [], []
    for gi, (window, dil) in enumerate(DIL_PAIRS):
        L = seq // dil
        t = hq[:, :, :, gi].reshape(bsz, L, dil, 3 * gw).transpose(0, 2, 1, 3).reshape(bsz, seq, 3 * gw)
        T = min(256, L)
        o, lse = _flash(t, t, t, name="dilated_attn", n_outer=2, T=T, dq=HD_B, dv=HD_B,
                        q_col=lambda g: g, k_col=lambda g: 2 + g, v_col=lambda g: 4 + g,
                        q_offs=(0, HD_B), k_offs=(0, HD_B), v_offs=(0, HD_B), scale=HD_B ** -0.5,
                        out_cols=gw, out_col=lambda g: g, bias=dil_bias[gi], bias_mode="pair",
                        bias_idx=lambda g: g, want_lse=True, seg_tiles=L // T, nback=1)
        outs.append(o.reshape(bsz, dil, L, gw).transpose(0, 2, 1, 3).reshape(bsz, seq, H_B_GROUP, HD_B).astype(F32))
        lses.append(lse.reshape(bsz, H_B_GROUP, dil, L).transpose(0, 3, 2, 1).reshape(bsz, seq, H_B_GROUP))
    w = jax.nn.softmax(jnp.stack(lses), axis=0)
    o_b = jnp.sum(w[..., None] * jnp.stack(outs), axis=0).astype(BF16).reshape(n, gw)
    na = H_A * MLA_V
    return _out_ln([o_a.reshape(n, na)], o_b, w_out[:na].astype(BF16), w_out[na:].astype(BF16),
                   xf, ln_g[None], ln_b[None])


def _mixer_cd(xb, xf, bsz, seq, w_in, pos_k, k_w1, k_w2, pos_v, v_w1, v_w2, lq1, lk1, lq2, lk2, d_norm,
              w_out, ln_g, ln_b, lam_init, tabs):
    n = bsz * seq
    qc_w = H_C * DK_C
    kv_w = G_C * DK_C
    off = qc_w
    kvs = []
    for _ in range(3):
        wk_ = w_in[:, off:off + kv_w].reshape(-1, G_C, DK_C)
        wv_ = w_in[:, off + kv_w:off + 2 * kv_w].reshape(-1, G_C, DK_C)
        kvs.append(jnp.concatenate([wk_, wv_], -1).reshape(-1, 2 * kv_w))
        off += 2 * kv_w
    g_off = off
    d_off = off + 3 * H_C
    w_main = jnp.concatenate([w_in[:, :qc_w]] + kvs + [w_in[:, d_off:]], 1)
    h = _mm(xb, w_main.astype(BF16), BF16, tn=256).reshape(bsz, seq, -1)
    w_g = jnp.repeat(w_in[:, g_off:d_off], DK_C, axis=1)
    gates = _mm(xb, w_g.astype(BF16), F32, tn=256, act="sigmoid").reshape(bsz, seq, -1)

    ncp = seq // CMP_STRIDE
    half = CMP_STRIDE * DK_C
    kv_cmp = h[:, :, qc_w:qc_w + 2 * kv_w].reshape(bsz, ncp, CMP_STRIDE, G_C, 2, DK_C)
    u = kv_cmp.transpose(0, 3, 4, 1, 2, 5).reshape(bsz, G_C, 2, ncp, half)
    pe = jnp.stack([pos_k.reshape(2, 1, half), pos_v.reshape(2, 1, half)])
    w1 = jnp.stack([k_w1.reshape(2, half, CMP_HID), v_w1.reshape(2, half, CMP_HID)])
    w2 = jnp.stack([k_w2, v_w2])
    kvc = _compress(u, pe, w1, w2)
    o_cmp, sel = _cmp_attn(h, kvc, tabs["bias_c"], tabs["overlap"], gates)

    cb = qc_w // LANE
    scale = DK_C ** -0.5
    n_pairs = H_C // 2
    nsa = dict(n_outer=n_pairs, dq=DK_C, dv=DK_C, q_col=lambda g: g, q_offs=(0, DK_C), k_offs=(0, 0),
               v_offs=(DK_C, DK_C), scale=scale, out_cols=qc_w, out_col=lambda g: g, bias_mode="pair",
               bias_idx=lambda g: g, gate=gates, vmem=VMEM_LIMIT, k_w=LANE, v_w=LANE)
    o_sel = _flash(h, h, h, name="nsa_sel_attn", T=256, k_col=lambda g: cb + 2 + g // 2, v_col=lambda g: cb + 2 + g // 2,
                   bias=tabs["bias_sel"], sel=sel, sel_idx=lambda g: g // 2,
                   gate_col=lambda g: n_pairs + g, **nsa)
    o_win = _flash(h, h, h, name="nsa_win_attn", T=WIN, k_col=lambda g: cb + 4 + g // 2, v_col=lambda g: cb + 4 + g // 2,
                   bias=tabs["bias_win"], gate_col=lambda g: 2 * n_pairs + g, nback=1, **nsa)

    lam = (jnp.exp(jnp.sum(lq1.astype(F32) * lk1.astype(F32)))
           - jnp.exp(jnp.sum(lq2.astype(F32) * lk2.astype(F32))) + lam_init).reshape(1, 1)
    db = cb + 6
    o_d = _flash(h, h, h, name="diff_attn", n_outer=H_D, T=256, dq=DD, dv=2 * DD,
                 q_col=lambda g: db + g, k_col=lambda g: db + H_D + g, v_col=lambda g: db + 2 * H_D + g,
                 q_offs=(0, DD), k_offs=(0, DD), v_offs=(0, 0), scale=DD ** -0.5,
                 out_cols=H_D * 2 * DD, out_col=lambda g: g, bias=tabs["bias_d"], bias_mode="shared",
                 bias_idx=lambda g: g, lam=lam, dnorm=d_norm[:, None], lam_init=lam_init, epilogue="diff",
                 vmem=VMEM_LIMIT)
    r2 = lambda a: a.reshape(n, -1)
    return _out_ln([r2(o_cmp), r2(o_sel), r2(o_win)], r2(o_d), w_out[:qc_w].astype(BF16),
                   w_out[qc_w:].astype(BF16), xf, ln_g[None], ln_b[None])


def _nsa_tables(rel_bias, seq):
    tab_c = rel_bias[:, H_B:H_B + H_C]
    tab_d = rel_bias[:, H_B + H_C:H_B + H_C + H_D]
    ncp = seq // CMP_STRIDE
    n_sel = seq // SEL_BLOCK
    pos = jnp.arange(seq)
    x = jnp.arange(2 * ncp)
    c_minus_a = jnp.where(x < ncp, x, x - 2 * ncp)
    dist = -CMP_STRIDE * c_minus_a[None, :] + jnp.arange(CMP_STRIDE)[:, None] - (CMP_LEN - 1)
    w = jnp.moveaxis(tab_c[_rel_bucket(dist)].astype(F32), -1, 0)
    bias_c = _toeplitz(w, ncp).transpose(0, 2, 1, 3).reshape(H_C, seq, ncp)
    c0 = jnp.arange(ncp) * CMP_STRIDE
    s0 = jnp.arange(n_sel) * SEL_BLOCK
    overlap = jnp.maximum(jnp.minimum(c0[:, None] + CMP_LEN, s0[None, :] + SEL_BLOCK)
                          - jnp.maximum(c0[:, None], s0[None, :]), 0).astype(F32) / CMP_LEN
    return {
        "bias_c": bias_c, "overlap": overlap.T,
        "bias_sel": _toeplitz_bias(tab_c, 256, seq // 256, 1, seq),
        "bias_win": _toeplitz_bias(tab_c, WIN, 2, 1, WIN - 1),
        "bias_d": _toeplitz_bias(tab_d, 256, seq // 256, 1, seq),
    }


def kernel(x, rel_bias, ab_w_in, mla_q_norm, mla_w_uq, mla_kv_norm, mla_w_ukv, ab_w_out, cd_w_in, nsa_cmp_pos_k, nsa_cmp_k_w1, nsa_cmp_k_w2, nsa_cmp_pos_v, nsa_cmp_v_w1, nsa_cmp_v_w2, diff_lambda_q1, diff_lambda_k1, diff_lambda_q2, diff_lambda_k2, diff_norm, cd_w_out, ln1_g, ln1_b, ln2_g, ln2_b, router_w, router_b, exp_w_gate, exp_w_up, exp_w_down, sh_w_gate, sh_w_up, sh_w_down):
    bsz, seq, d = x.shape
    n = bsz * seq
    depth = ln1_g.shape[0]
    rope_tabs = _rope_tables(seq)
    dil_bias = [_toeplitz_bias(rel_bias[:, gi * H_B_GROUP:(gi + 1) * H_B_GROUP], min(256, seq // dil), 2, dil,
                               window // dil) for gi, (window, dil) in enumerate(DIL_PAIRS)]
    nsa_tabs = _nsa_tables(rel_bias, seq)
    xf = x.reshape(n, d)
    xb = xf.astype(BF16)
    for l in range(depth):
        i = l // 2
        if l % 2 == 0:
            xf, xb = _mixer_ab(xb, xf, bsz, seq, ab_w_in[i], mla_q_norm[i], mla_w_uq[i], mla_kv_norm[i],
                               mla_w_ukv[i], ab_w_out[i], ln1_g[l], ln1_b[l], rope_tabs, dil_bias)
        else:
            lam_init = 0.8 - 0.6 * math.exp(-0.3 * l)
            xf, xb = _mixer_cd(xb, xf, bsz, seq, cd_w_in[i], nsa_cmp_pos_k[i], nsa_cmp_k_w1[i],
                               nsa_cmp_k_w2[i], nsa_cmp_pos_v[i], nsa_cmp_v_w1[i], nsa_cmp_v_w2[i],
                               diff_lambda_q1[i], diff_lambda_k1[i], diff_lambda_q2[i], diff_lambda_k2[i],
                               diff_norm[i], cd_w_out[i], ln1_g[l], ln1_b[l], lam_init, nsa_tabs)
        xf, xb = _moe(xf, xb, router_w[l], router_b[l], exp_w_gate[l], exp_w_up[l],
                      exp_w_down[l], sh_w_gate[l].astype(BF16), sh_w_up[l].astype(BF16),
                      sh_w_down[l].astype(BF16), ln2_g[l], ln2_b[l])
    return xf.reshape(bsz, seq, d)
```

```python
import functools
import math

import jax
import jax.numpy as jnp
from jax import lax
from jax.experimental import pallas as pl
from jax.experimental.pallas import tpu as pltpu

F32 = jnp.float32
BF16 = jnp.bfloat16
HI = lax.Precision.HIGHEST

DEPTH = 4
NEG = -1e30
BIG = 1e9
LN_EPS = 1e-5
RMS_EPS = 1e-6
ALPHA = (2 * DEPTH) ** 0.25

N_BUCKETS = 32
REL_MAX_DIST = 2048

H_A = 12
NOPE = 64
ROPE_DIM = 32
MLA_V = 64
Q_LORA = 256
KV_LORA = 128
ROPE_THETA = 10000.0

DIL_PAIRS = ((128, 1), (512, 4), (2048, 16))
H_B_GROUP = 4
H_B = 12
HD_B = 64

H_C = 8
G_C = 2
R_C = 4
DK_C = 64
CMP_LEN = 32
CMP_STRIDE = 16
CMP_HID = 64
SEL_BLOCK = 64
SEL_TOP = 16
WIN = 512

H_D = 4
DD = 64

N_EXPERTS = 64
TOP_K = 8
N_EXPERT_GROUPS = 8
TOPK_GROUPS = 4
D_EXPERT = 256
ROUTED_SCALE = 2.5

LANE = 128
MOE_BM = 256
ATT_TQ = 512
ATT_TK = 256
VMEM_LIMIT = 56 * 1024 * 1024

_NT = (((1,), (1,)), ((), ()))


def _cparams(n_axes, vmem=None):
    return pltpu.CompilerParams(dimension_semantics=("arbitrary",) * n_axes, vmem_limit_bytes=vmem)


def _mm_body(x_ref, w_ref, o_ref, *, act, precision):
    y = jnp.dot(x_ref[...], w_ref[...], preferred_element_type=F32, precision=precision)
    if act == "sigmoid":
        y = jax.nn.sigmoid(y)
    o_ref[...] = y.astype(o_ref.dtype)


def _mm(x, w, out_dtype, tn, tm=512, act=None, precision=None):
    m, k = x.shape
    nc = w.shape[1]
    return pl.pallas_call(
        functools.partial(_mm_body, act=act, precision=precision),
        grid=(m // tm, nc // tn),
        in_specs=[pl.BlockSpec((tm, k), lambda i, j: (i, 0)),
                  pl.BlockSpec((k, tn), lambda i, j: (0, j))],
        out_specs=pl.BlockSpec((tm, tn), lambda i, j: (i, j)),
        out_shape=jax.ShapeDtypeStruct((m, nc), out_dtype),
        compiler_params=_cparams(2), name="proj",
    )(x, w)


def _layer_norm(z, g, b):
    mu = jnp.mean(z, axis=-1, keepdims=True)
    zc = z - mu
    var = jnp.mean(zc * zc, axis=-1, keepdims=True)
    return zc * lax.rsqrt(var + LN_EPS) * g + b


def _out_ln_body(*refs, n_sum):
    a0 = refs[0][...].astype(F32)
    for r in refs[1:n_sum]:
        a0 = a0 + r[...].astype(F32)
    a1_ref, w0_ref, w1_ref, r_ref, g_ref, b_ref, of_ref, ob_ref = refs[n_sum:]
    y = jnp.dot(a0.astype(BF16), w0_ref[...], preferred_element_type=F32)
    y = y + jnp.dot(a1_ref[...], w1_ref[...], preferred_element_type=F32)
    out = _layer_norm(ALPHA * r_ref[...] + y, g_ref[...], b_ref[...])
    of_ref[...] = out
    ob_ref[...] = out.astype(BF16)


def _out_ln(a0s, a1, w0, w1, resid, g, b, tm=256):
    n, d = resid.shape
    k0, k1 = w0.shape[0], w1.shape[0]
    row = lambda i: (i, 0)
    fixed = lambda i: (0, 0)
    return pl.pallas_call(
        functools.partial(_out_ln_body, n_sum=len(a0s)),
        grid=(n // tm,),
        in_specs=[pl.BlockSpec((tm, k0), row)] * len(a0s) + [
            pl.BlockSpec((tm, k1), row), pl.BlockSpec((k0, d), fixed), pl.BlockSpec((k1, d), fixed),
            pl.BlockSpec((tm, d), row), pl.BlockSpec((1, d), fixed), pl.BlockSpec((1, d), fixed)],
        out_specs=[pl.BlockSpec((tm, d), row), pl.BlockSpec((tm, d), row)],
        out_shape=[jax.ShapeDtypeStruct((n, d), F32), jax.ShapeDtypeStruct((n, d), BF16)],
        compiler_params=_cparams(1), name="out_proj_ln",
    )(*a0s, a1, w0, w1, resid, g, b)


def _rms(x, g):
    return x * lax.rsqrt(jnp.mean(x * x, axis=-1, keepdims=True) + RMS_EPS) * g


def _mla_q_body(c_ref, g_ref, w_ref, wr_ref, cos_ref, sin_ref, o_ref):
    cn = _rms(c_ref[...], g_ref[...]).astype(BF16)
    a = jnp.dot(cn, w_ref[...], preferred_element_type=F32)
    r = jnp.dot(cn, wr_ref[...], preferred_element_type=F32)
    cos, sin = cos_ref[...], sin_ref[...]
    for h in range(H_A):
        sl = slice(h * LANE, (h + 1) * LANE)
        o_ref[:, sl] = (a[:, sl] * cos + r[:, sl] * sin).astype(o_ref.dtype)


def _mla_kv_body(c_ref, kr_ref, krr_ref, g_ref, wk_ref, wv_ref, cos_ref, sin_ref, k_ref, v_ref):
    cn = _rms(c_ref[...], g_ref[...]).astype(BF16)
    kn = jnp.dot(cn, wk_ref[...], preferred_element_type=F32)
    rope = kr_ref[...] * cos_ref[...] + krr_ref[...] * sin_ref[...]
    for h in range(H_A):
        sl = slice(h * LANE, (h + 1) * LANE)
        k_ref[:, sl] = (kn[:, sl] + rope).astype(k_ref.dtype)
    v_ref[...] = jnp.dot(cn, wv_ref[...], preferred_element_type=F32).astype(v_ref.dtype)


def _mla_up(h1, q_norm, wq, wq_rot, kv_norm, wk, wv, cos_q, sin_q, cos_k, sin_k, seq, tm=512):
    n = h1.shape[0]
    spt = seq // tm
    row = lambda c: (lambda i: (i, c))
    pos = lambda i: (i % spt, 0)
    fixed = lambda i: (0, 0)
    q_a = pl.pallas_call(
        _mla_q_body, grid=(n // tm,),
        in_specs=[pl.BlockSpec((tm, Q_LORA), row(0)), pl.BlockSpec((1, Q_LORA), fixed),
                  pl.BlockSpec(wq.shape, fixed), pl.BlockSpec(wq_rot.shape, fixed),
                  pl.BlockSpec((tm, LANE), pos), pl.BlockSpec((tm, LANE), pos)],
        out_specs=pl.BlockSpec((tm, H_A * LANE), row(0)),
        out_shape=jax.ShapeDtypeStruct((n, H_A * LANE), BF16),
        compiler_params=_cparams(1), name="mla_q_up",
    )(h1, q_norm, wq, wq_rot, cos_q, sin_q)
    k_a, v_a = pl.pallas_call(
        _mla_kv_body, grid=(n // tm,),
        in_specs=[pl.BlockSpec((tm, LANE), row(2)), pl.BlockSpec((tm, LANE), row(3)),
                  pl.BlockSpec((tm, LANE), row(4)), pl.BlockSpec((1, KV_LORA), fixed),
                  pl.BlockSpec(wk.shape, fixed), pl.BlockSpec(wv.shape, fixed),
                  pl.BlockSpec((tm, LANE), pos), pl.BlockSpec((tm, LANE), pos)],
        out_specs=[pl.BlockSpec((tm, H_A * LANE), row(0)), pl.BlockSpec((tm, H_A * MLA_V), row(0))],
        out_shape=[jax.ShapeDtypeStruct((n, H_A * LANE), BF16), jax.ShapeDtypeStruct((n, H_A * MLA_V), BF16)],
        compiler_params=_cparams(1), name="mla_kv_up",
    )(h1, h1, h1, kv_norm, wk, wv, cos_k, sin_k)
    return q_a, k_a, v_a


def _flash_body(*refs, T, TK, dq, dv, q_offs, k_offs, v_offs, scale, bias_mode, has_sel, has_gate,
                epilogue, want_lse, seg_tiles, nback, lam_init):
    R = T // TK
    it = iter(refs)
    q_ref, k_ref, v_ref = next(it), next(it), next(it)
    bias_ref = next(it) if bias_mode else None
    sel_ref = next(it) if has_sel else None
    gate_ref = next(it) if has_gate else None
    lam_ref, dn_ref = (next(it), next(it)) if epilogue == "diff" else (None, None)
    o_ref = next(it)
    lse_ref = next(it) if want_lse else None
    vt_scr = next(it)

    qi = pl.program_id(2)
    seq = v_ref.shape[1]

    @pl.when(qi == 0)
    def _():
        for c in range(seq // TK):
            vt_scr[:, c * TK:(c + 1) * TK] = v_ref[0, c * TK:(c + 1) * TK, :].astype(F32).T.astype(BF16)

    qfull = q_ref[0].astype(F32)
    fold_scale = math.frexp(scale)[0] == 0.5
    qts = [(qfull[:, off:off + dq] * (scale if fold_scale else 1.0)).T.astype(BF16) for off in q_offs]

    def qk(kc):
        kfull = k_ref[0, pl.ds(pl.multiple_of(kc * TK, TK), TK), :]
        return tuple(jnp.dot(kfull[:, k_offs[u]:k_offs[u] + dq], qts[u], preferred_element_type=F32)
                     for u in range(2))

    def update(kc, state, scores, diag):
        start = pl.multiple_of(kc * TK, TK)
        new_state = []
        sel_add = None
        if has_sel:
            per = TK // SEL_BLOCK
            rows = [sel_ref[0, 0, pl.ds(kc * per + a, 1), :] for a in range(per)]
            sel_add = jnp.concatenate([jnp.broadcast_to((r - 1.0) * (-NEG), (SEL_BLOCK, T)) for r in rows], axis=0)
        for u in range(2):
            vt = vt_scr[v_offs[u]:v_offs[u] + dv, pl.ds(start, TK)]
            s = scores[u]
            if not fold_scale:
                s = s * scale
            if bias_mode:
                s = s + bias_ref[u if bias_mode == "pair" else 0, R * qi - kc + (R - 1)]
            elif diag is not None:
                key = lax.broadcasted_iota(jnp.int32, (TK, T), 0) + diag * TK
                qry = lax.broadcasted_iota(jnp.int32, (TK, T), 1)
                s = jnp.where(key <= qry, s, NEG)
            if has_sel:
                s = s + sel_add
            m_prev, l_prev, acc_prev = state[u]
            m_new = jnp.maximum(m_prev, jnp.max(s, axis=0, keepdims=True))
            alpha = jnp.exp(m_prev - m_new)
            p = jnp.exp(s - m_new)
            l_new = alpha * l_prev + jnp.sum(p, axis=0, keepdims=True)
            acc_new = alpha * acc_prev + jnp.dot(vt, p.astype(BF16), preferred_element_type=F32)
            new_state.append((m_new, l_new, acc_new))
        return tuple(new_state)

    init = tuple((jnp.full((1, T), NEG, F32), jnp.zeros((1, T), F32), jnp.zeros((dv, T), F32)) for _ in range(2))
    if nback is None:
        lo = 0
    elif seg_tiles is not None:
        lo = jnp.where(qi % seg_tiles == 0, qi, qi - nback)
    else:
        lo = jnp.maximum(qi - nback, 0)

    def step(kc, carry):
        state, scores = carry
        nxt = qk(kc + 1)
        return update(kc, state, scores, None), nxt

    state, scores = lax.fori_loop(lo, R * qi, step, (init, qk(lo)))
    for a in range(R):
        nxt = qk(R * qi + a + 1) if a + 1 < R else None
        state = update(R * qi + a, state, scores, a)
        scores = nxt

    outs = [acc / l for _, l, acc in state]
    if epilogue == "diff":
        a = outs[0] - lam_ref[0, 0] * outs[1]
        rinv = lax.rsqrt(jnp.mean(a * a, axis=0, keepdims=True) + RMS_EPS)
        o = (a * rinv * dn_ref[...] * (1.0 - lam_init)).T
    else:
        o = jnp.concatenate(outs, axis=0).T
        if has_gate:
            o = o * gate_ref[0]
    o_ref[0] = o.astype(o_ref.dtype)
    if want_lse:
        lse_ref[0, 0] = jnp.concatenate([m + jnp.log(l) for m, l, _ in state], axis=0)


def _flash(q, k, v, *, n_outer, T, dq, dv, q_col, k_col, v_col, q_offs, k_offs, v_offs, scale,
           out_cols, out_col, bias=None, bias_mode=None, bias_idx=None, sel=None, sel_idx=None,
           gate=None, gate_col=None, lam=None, dnorm=None, lam_init=0.0, epilogue="plain", want_lse=False,
           seg_tiles=None, nback=None, vmem=None, k_w=None, v_w=None, name="flash", TK=None):
    bsz, seq, _ = q.shape
    TK = TK or T
    assert T % TK == 0 and (nback is None or T == TK)
    nq = seq // T
    qw = max(o + dq for o in q_offs)
    kw = k_w or max(o + dq for o in k_offs)
    vw = v_w or max(o + dv for o in v_offs)
    ow = dv if epilogue == "diff" else 2 * dv
    in_specs = [pl.BlockSpec((1, T, qw), lambda g, b, i: (b, i, q_col(g))),
                pl.BlockSpec((1, seq, kw), lambda g, b, i: (b, 0, k_col(g))),
                pl.BlockSpec((1, seq, vw), lambda g, b, i: (b, 0, v_col(g)))]
    args = [q, k, v]
    if bias_mode:
        nb = 2 if bias_mode == "pair" else 1
        in_specs.append(pl.BlockSpec((nb,) + bias.shape[1:], lambda g, b, i: (bias_idx(g), 0, 0, 0)))
        args.append(bias)
    if sel is not None:
        in_specs.append(pl.BlockSpec((1, 1, sel.shape[2], T), lambda g, b, i: (b, sel_idx(g), 0, i)))
        args.append(sel)
    if gate is not None:
        in_specs.append(pl.BlockSpec((1, T, ow), lambda g, b, i: (b, i, gate_col(g))))
        args.append(gate)
    if epilogue == "diff":
        in_specs.append(pl.BlockSpec(memory_space=pltpu.SMEM))
        in_specs.append(pl.BlockSpec((dv, 1), lambda g, b, i: (0, 0)))
        args += [lam, dnorm]
    out_specs = [pl.BlockSpec((1, T, ow), lambda g, b, i: (b, i, out_col(g)))]
    out_shape = [jax.ShapeDtypeStruct((bsz, seq, out_cols), BF16)]
    if want_lse:
        out_specs.append(pl.BlockSpec((1, 1, 2, T), lambda g, b, i: (b, g, 0, i)))
        out_shape.append(jax.ShapeDtypeStruct((bsz, n_outer, 2, seq), F32))
    body = functools.partial(
        _flash_body, T=T, TK=TK, dq=dq, dv=dv, q_offs=q_offs, k_offs=k_offs, v_offs=v_offs, scale=scale,
        bias_mode=bias_mode, has_sel=sel is not None, has_gate=gate is not None, epilogue=epilogue,
        want_lse=want_lse, seg_tiles=seg_tiles, nback=nback, lam_init=lam_init)
    res = pl.pallas_call(
        body, grid=(n_outer, bsz, nq), in_specs=in_specs, out_specs=out_specs, out_shape=out_shape,
        scratch_shapes=[pltpu.VMEM((vw, seq), BF16)],
        compiler_params=_cparams(3, vmem), name=name,
    )(*args)
    return res if want_lse else res[0]


def _rel_bucket(dist):
    n = jnp.maximum(dist, 0)
    exact = N_BUCKETS // 2
    log_ratio = jnp.log(jnp.maximum(n, 1).astype(F32) / exact) / math.log(REL_MAX_DIST / exact)
    large = exact + (log_ratio * (N_BUCKETS - exact)).astype(jnp.int32)
    return jnp.where(n < exact, n, jnp.minimum(large, N_BUCKETS - 1))


def _toeplitz_bias(tab, T, n_d, dist_scale, max_dist, TK=None):
    TK = TK or T
    R = T // TK
    wlen = T + TK
    x = jnp.arange(wlen)
    dist = (jnp.arange(n_d)[:, None] - (R - 1)) * TK + jnp.where(x < T, x, x - wlen)[None, :]
    w = tab[_rel_bucket(dist * dist_scale)].astype(F32)
    w = jnp.where(((dist >= 0) & (dist <= max_dist))[..., None], w, NEG)
    return _toeplitz(jnp.moveaxis(w, -1, 0), TK, T)


def _toeplitz(w, rows, cols):
    wlen = w.shape[-1]
    tiled = jnp.tile(w, (1,) * (w.ndim - 1) + (rows,))[..., :rows * (wlen - 1)]
    return tiled.reshape(w.shape[:-1] + (rows, wlen - 1))[..., :cols]


def _compress_body(u_ref, pe_ref, w1_ref, w2_ref, o_ref, *, ncp):
    outs = []
    for a in range(2):
        u = u_ref[0, 0, a].astype(F32)
        p1 = jnp.dot(u + pe_ref[a, 0], w1_ref[a, 0], preferred_element_type=F32, precision=HI)
        p2 = jnp.dot(u + pe_ref[a, 1], w1_ref[a, 1], preferred_element_type=F32, precision=HI)
        hid = jax.nn.gelu(p1 + pltpu.roll(p2, ncp - 1, 0))
        outs.append(jnp.dot(hid, w2_ref[a], preferred_element_type=F32, precision=HI))
    o_ref[0, 0] = jnp.concatenate(outs, axis=-1)


def _compress(u, pe, w1, w2):
    bsz, g, _, ncp, width = u.shape
    return pl.pallas_call(
        functools.partial(_compress_body, ncp=ncp), grid=(bsz, g),
        in_specs=[pl.BlockSpec((1, 1, 2, ncp, width), lambda b, gg: (b, gg, 0, 0, 0)),
                  pl.BlockSpec(pe.shape, lambda b, gg: (0, 0, 0, 0)),
                  pl.BlockSpec(w1.shape, lambda b, gg: (0, 0, 0, 0)),
                  pl.BlockSpec(w2.shape, lambda b, gg: (0, 0, 0))],
        out_specs=pl.BlockSpec((1, 1, ncp, 2 * DK_C), lambda b, gg: (b, gg, 0, 0)),
        out_shape=jax.ShapeDtypeStruct((bsz, g, ncp, 2 * DK_C), F32),
        compiler_params=_cparams(2), name="nsa_compress",
    )(u, pe, w1, w2)


def _cmp_attn_body(q_ref, kv_ref, bias_ref, ov_ref, gate_ref, o_ref, sel_ref, *, T, ncp, n_sel, n_top, scale):
    qi = pl.program_id(2)
    kc = kv_ref[0, 0, :, :DK_C]
    vc = kv_ref[0, 0, :, DK_C:]
    t = qi * T + lax.broadcasted_iota(jnp.int32, (T, ncp), 0)
    c = lax.broadcasted_iota(jnp.int32, (T, ncp), 1)
    valid = t >= c * CMP_STRIDE + (CMP_LEN - 1)
    validf = valid.astype(F32)
    psum = jnp.zeros((T, ncp), F32)
    outs = []
    for r in range(R_C):
        q = q_ref[0, :, r * DK_C:(r + 1) * DK_C].astype(F32)
        s = lax.dot_general(q, kc, _NT, preferred_element_type=F32, precision=HI) * scale + bias_ref[r]
        s = jnp.where(valid, s, NEG)
        e = jnp.exp(s - jnp.max(s, axis=-1, keepdims=True)) * validf
        p = e / jnp.maximum(jnp.sum(e, axis=-1, keepdims=True), 1e-30)
        outs.append(jnp.dot(p, vc, preferred_element_type=F32, precision=HI))
        psum = psum + p
    o_ref[0] = (jnp.concatenate(outs, axis=-1) * gate_ref[0]).astype(o_ref.dtype)

    imp = jnp.dot(ov_ref[...], psum.T, preferred_element_type=F32, precision=HI)
    tq = qi * T + lax.broadcasted_iota(jnp.int32, (n_sel, T), 1)
    j = lax.broadcasted_iota(jnp.int32, (n_sel, T), 0)
    forced = (j == tq // SEL_BLOCK) | (j == 0)
    work = jnp.where(forced, BIG, jnp.where(j * SEL_BLOCK <= tq, imp, -BIG))
    sel = jnp.zeros((n_sel, T), F32)
    jf = j.astype(F32)
    for _ in range(n_top):
        _, _, pick = _first_max(work, jf, n_sel)
        sel = jnp.where(pick, 1.0, sel)
        work = jnp.where(pick, -jnp.inf, work)
    sel_ref[0, 0] = sel


def _cmp_attn(h, kvc, bias_c, overlap, gates, T=256):
    bsz, seq, _ = h.shape
    ncp = kvc.shape[2]
    n_sel = seq // SEL_BLOCK
    n_top = min(SEL_TOP, n_sel)
    qw = R_C * DK_C
    return pl.pallas_call(
        functools.partial(_cmp_attn_body, T=T, ncp=ncp, n_sel=n_sel, n_top=n_top, scale=DK_C ** -0.5),
        grid=(G_C, bsz, seq // T),
        in_specs=[pl.BlockSpec((1, T, qw), lambda g, b, i: (b, i, g)),
                  pl.BlockSpec((1, 1, ncp, 2 * DK_C), lambda g, b, i: (b, g, 0, 0)),
                  pl.BlockSpec((R_C, T, ncp), lambda g, b, i: (g, i, 0)),
                  pl.BlockSpec(overlap.shape, lambda g, b, i: (0, 0)),
                  pl.BlockSpec((1, T, qw), lambda g, b, i: (b, i, g))],
        out_specs=[pl.BlockSpec((1, T, qw), lambda g, b, i: (b, i, g)),
                   pl.BlockSpec((1, 1, n_sel, T), lambda g, b, i: (b, g, 0, i))],
        out_shape=[jax.ShapeDtypeStruct((bsz, seq, H_C * DK_C), BF16),
                   jax.ShapeDtypeStruct((bsz, G_C, n_sel, seq), F32)],
        compiler_params=_cparams(3), name="nsa_cmp_attn",
    )(h, kvc, bias_c, overlap, gates)


def _first_max(work, idx, n):
    mx = jnp.max(work, axis=0, keepdims=True)
    first = jnp.min(jnp.where(work == mx, idx, float(n)), axis=0, keepdims=True)
    return mx, first, idx == first


def _router_body(x_ref, wt_ref, b_ref, tri_ref, e_ref, g_ref, r_ref, cnt_ref, carry_scr, *, tm):
    i = pl.program_id(0)

    @pl.when(i == 0)
    def _():
        carry_scr[...] = jnp.zeros(carry_scr.shape, F32)

    st = lax.dot_general(wt_ref[...], x_ref[...], _NT, preferred_element_type=F32, precision=HI)
    scores = jax.nn.sigmoid(st)
    sel = scores + b_ref[...]
    per = N_EXPERTS // N_EXPERT_GROUPS
    fiota = lambda rows: lax.broadcasted_iota(jnp.int32, (rows, tm), 0).astype(F32)
    i_per, i_grp, i_exp = fiota(per), fiota(N_EXPERT_GROUPS), fiota(N_EXPERTS)
    grp_scores = []
    for g in range(N_EXPERT_GROUPS):
        blk = sel[g * per:(g + 1) * per]
        m1, _, pick = _first_max(blk, i_per, per)
        grp_scores.append(m1 + jnp.max(jnp.where(pick, -jnp.inf, blk), axis=0, keepdims=True))
    work = jnp.concatenate(grp_scores, axis=0)
    gmask = jnp.zeros((N_EXPERT_GROUPS, tm), F32)
    for _ in range(TOPK_GROUPS):
        _, _, pick = _first_max(work, i_grp, N_EXPERT_GROUPS)
        gmask = jnp.where(pick, 1.0, gmask)
        work = jnp.where(pick, -jnp.inf, work)
    work = jnp.concatenate([jnp.where(gmask[g:g + 1] > 0.5, sel[g * per:(g + 1) * per], NEG)
                            for g in range(N_EXPERT_GROUPS)], axis=0)
    picks, firsts, vals = [], [], []
    for _ in range(TOP_K):
        _, first, pick = _first_max(work, i_exp, N_EXPERTS)
        picks.append(pick)
        firsts.append(first)
        vals.append(jnp.sum(jnp.where(pick, scores, 0.0), axis=0, keepdims=True))
        work = jnp.where(pick, -jnp.inf, work)
    val = jnp.concatenate(vals, axis=0)
    g_ref[...] = val / jnp.sum(val, axis=0, keepdims=True) * ROUTED_SCALE
    e_ref[...] = jnp.concatenate(firsts, axis=0).astype(jnp.int32)
    onehot = picks[0].astype(F32)
    for pick in picks[1:]:
        onehot = onehot + pick.astype(F32)
    before = jnp.dot(onehot.astype(BF16), tri_ref[...], preferred_element_type=F32) + carry_scr[...]
    r_ref[...] = jnp.concatenate([jnp.sum(jnp.where(pick, before, 0.0), axis=0, keepdims=True)
                                  for pick in picks], axis=0).astype(jnp.int32)
    carry = carry_scr[...] + jnp.sum(onehot, axis=1, keepdims=True)
    carry_scr[...] = carry
    cnt_ref[...] = jnp.broadcast_to(carry, cnt_ref.shape)


def _router(xf, router_w, router_b, tm=256):
    n, d = xf.shape
    tri = (jnp.arange(tm)[:, None] < jnp.arange(tm)[None, :]).astype(BF16)
    col = lambda i: (0, i)
    fixed = lambda i: (0, 0)
    return pl.pallas_call(
        functools.partial(_router_body, tm=tm), grid=(n // tm,),
        in_specs=[pl.BlockSpec((tm, d), lambda i: (i, 0)), pl.BlockSpec((N_EXPERTS, d), fixed),
                  pl.BlockSpec((N_EXPERTS, 1), fixed), pl.BlockSpec((tm, tm), fixed)],
        out_specs=[pl.BlockSpec((TOP_K, tm), col), pl.BlockSpec((TOP_K, tm), col),
                   pl.BlockSpec((TOP_K, tm), col), pl.BlockSpec((N_EXPERTS, LANE), fixed)],
        out_shape=[jax.ShapeDtypeStruct((TOP_K, n), jnp.int32), jax.ShapeDtypeStruct((TOP_K, n), F32),
                   jax.ShapeDtypeStruct((TOP_K, n), jnp.int32), jax.ShapeDtypeStruct((N_EXPERTS, LANE), F32)],
        scratch_shapes=[pltpu.VMEM((N_EXPERTS, 1), F32)],
        compiler_params=_cparams(1), name="router",
    )(xf, router_w.T, router_b.astype(F32)[:, None], tri)


def _pack_rows(x):
    w = x.shape[-1] // 2
    lo = lax.bitcast_convert_type(x[:, :w].astype(BF16).astype(F32), jnp.uint32)
    hi = lax.bitcast_convert_type(x[:, w:].astype(BF16).astype(F32), jnp.uint32)
    return (lo >> 16) | (hi & jnp.uint32(0xFFFF0000))


def _unpack_rows(p):
    lo = lax.bitcast_convert_type(p << 16, F32)
    hi = lax.bitcast_convert_type(p & jnp.uint32(0xFFFF0000), F32)
    return lo, hi


def _dispatch_body(fill_ref, pos_ref, x_ref, z_ref, xs_hbm, xp_scr, sem, *, tm, n_blocks):
    i = pl.program_id(0)
    xp_scr[...] = _pack_rows(x_ref[...])

    def row_copy(src, dst):
        return pltpu.make_async_copy(xp_scr.at[pl.ds(src, 1)], xs_hbm.at[pl.ds(dst, 1)], sem)

    def zero_copy(dst):
        return pltpu.make_async_copy(z_ref.at[pl.ds(0, 1)], xs_hbm.at[pl.ds(dst, 1)], sem)

    def zero_block(blk):
        return pltpu.make_async_copy(z_ref, xs_hbm.at[pl.ds(blk * MOE_BM, MOE_BM)], sem)

    def issue(t, c):
        for k in range(TOP_K):
            row_copy(t, pos_ref[k, t]).start(priority=k % 2)
        return c

    def drain(t, c):
        for k in range(TOP_K):
            row_copy(0, 0).wait()
        return c

    lax.fori_loop(0, tm, issue, 0)
    lax.fori_loop(0, tm, drain, 0)

    @pl.when(i == 0)
    def _():
        def fill(e, c):
            lo, hi = fill_ref[2 * e], fill_ref[2 * e + 1]

            def start(r, cc):
                zero_copy(r).start()
                return cc

            def wait(r, cc):
                zero_copy(0).wait()
                return cc

            lax.fori_loop(lo, hi, start, 0)
            lax.fori_loop(lo, hi, wait, 0)
            return c

        lax.fori_loop(0, N_EXPERTS, fill, 0)

        def tail_start(blk, c):
            zero_block(blk).start()
            return c

        def tail_wait(blk, c):
            zero_block(0).wait()
            return c

        lax.fori_loop(fill_ref[2 * N_EXPERTS], n_blocks, tail_start, 0)
        lax.fori_loop(fill_ref[2 * N_EXPERTS], n_blocks, tail_wait, 0)


def _dispatch(xb, pos, fill, p, tm=128):
    n, d = xb.shape
    grid_spec = pltpu.PrefetchScalarGridSpec(
        num_scalar_prefetch=1, grid=(n // tm,),
        in_specs=[pl.BlockSpec((TOP_K, tm), lambda i, fl: (0, i), memory_space=pltpu.SMEM),
                  pl.BlockSpec((tm, d), lambda i, fl: (i, 0)),
                  pl.BlockSpec((MOE_BM, d // 2), lambda i, fl: (0, 0))],
        out_specs=pl.BlockSpec(memory_space=pl.ANY),
        scratch_shapes=[pltpu.VMEM((tm, d // 2), jnp.uint32), pltpu.SemaphoreType.DMA(())])
    return pl.pallas_call(
        functools.partial(_dispatch_body, tm=tm, n_blocks=p // MOE_BM), grid_spec=grid_spec,
        out_shape=jax.ShapeDtypeStruct((p, d // 2), jnp.uint32),
        compiler_params=_cparams(1), name="dispatch",
    )(fill, pos, xb, jnp.zeros((MOE_BM, d // 2), jnp.uint32))


def _moe_ffn_body(be_ref, nb_ref, x_ref, wg_ref, wu_ref, wd_ref, o_ref, wg_scr, wu_scr, wd_scr):
    i = pl.program_id(0)
    half = x_ref.shape[1]

    @pl.when((i == 0) | (be_ref[i] != be_ref[jnp.maximum(i - 1, 0)]))
    def _():
        wg_scr[...] = wg_ref[0].astype(BF16)
        wu_scr[...] = wu_ref[0].astype(BF16)
        wd_scr[...] = wd_ref[0].astype(BF16)

    @pl.when(i < nb_ref[0])
    def _():
        lo, hi = _unpack_rows(x_ref[...])
        lo, hi = lo.astype(BF16), hi.astype(BF16)
        hg = (jnp.dot(lo, wg_scr[:half], preferred_element_type=F32)
              + jnp.dot(hi, wg_scr[half:], preferred_element_type=F32))
        hu = (jnp.dot(lo, wu_scr[:half], preferred_element_type=F32)
              + jnp.dot(hi, wu_scr[half:], preferred_element_type=F32))
        hb = (hg * jax.nn.sigmoid(hg) * hu).astype(BF16)
        o_ref[...] = _pack_rows(jnp.dot(hb, wd_scr[...], preferred_element_type=F32))

    @pl.when(i >= nb_ref[0])
    def _():
        o_ref[...] = jnp.zeros(o_ref.shape, o_ref.dtype)


def _moe_ffn(xs, blk_e, n_used, wg, wu, wd):
    p, half = xs.shape
    d = 2 * half
    n_blocks = p // MOE_BM
    grid_spec = pltpu.PrefetchScalarGridSpec(
        num_scalar_prefetch=2, grid=(n_blocks,),
        in_specs=[pl.BlockSpec((MOE_BM, half), lambda i, be, nb: (i, 0)),
                  pl.BlockSpec((1, d, D_EXPERT), lambda i, be, nb: (be[i], 0, 0)),
                  pl.BlockSpec((1, d, D_EXPERT), lambda i, be, nb: (be[i], 0, 0)),
                  pl.BlockSpec((1, D_EXPERT, d), lambda i, be, nb: (be[i], 0, 0))],
        out_specs=pl.BlockSpec((MOE_BM, half), lambda i, be, nb: (i, 0)),
        scratch_shapes=[pltpu.VMEM((d, D_EXPERT), BF16), pltpu.VMEM((d, D_EXPERT), BF16),
                        pltpu.VMEM((D_EXPERT, d), BF16)])
    return pl.pallas_call(
        _moe_ffn_body, grid_spec=grid_spec, out_shape=jax.ShapeDtypeStruct((p, half), jnp.uint32),
        compiler_params=_cparams(1), name="expert_ffn",
    )(blk_e, n_used, xs, wg, wu, wd)


def _combine_body(pos_ref, posn_ref, gate_ref, xb_ref, xf_ref, y_hbm, sg_ref, su_ref, sd_ref,
                  g_ref, b_ref, of_ref, ob_ref, buf, sem, *, tm):
    i = pl.program_id(0)
    slot = i % 2

    def row_copy(s, k, t, src):
        return pltpu.make_async_copy(y_hbm.at[pl.ds(src, 1)], buf.at[s, k, pl.ds(t, 1)], sem.at[s])

    def issue_tile(pr, s):
        def issue(t, c):
            for k in range(TOP_K):
                row_copy(s, k, t, pr[k, t]).start(priority=k % 2)
            return c

        lax.fori_loop(0, tm, issue, 0)

    @pl.when(i == 0)
    def _():
        issue_tile(pos_ref, 0)

    @pl.when(i + 1 < pl.num_programs(0))
    def _():
        issue_tile(posn_ref, 1 - slot)

    x = xb_ref[...]
    hg = jnp.dot(x, sg_ref[...], preferred_element_type=F32)
    hu = jnp.dot(x, su_ref[...], preferred_element_type=F32)
    hb = (hg * jax.nn.sigmoid(hg) * hu).astype(BF16)
    y = jnp.dot(hb, sd_ref[...], preferred_element_type=F32)

    def drain(t, c):
        for k in range(TOP_K):
            row_copy(slot, k, t, 0).wait()
        return c

    lax.fori_loop(0, tm, drain, 0)
    gate = gate_ref[...]
    half = buf.shape[-1]
    y_lo, y_hi = y[:, :half], y[:, half:]
    for k in range(TOP_K):
        lo, hi = _unpack_rows(buf[slot, k])
        y_lo = y_lo + gate[:, k:k + 1] * lo
        y_hi = y_hi + gate[:, k:k + 1] * hi
    out = _layer_norm(ALPHA * xf_ref[...] + jnp.concatenate([y_lo, y_hi], axis=-1), g_ref[...], b_ref[...])
    of_ref[...] = out
    ob_ref[...] = out.astype(BF16)


def _combine(pos, gate, xb, xf, y, sg, su, sd, g, b, tm=128):
    n, d = xf.shape
    n_tiles = n // tm
    row = lambda i: (i, 0)
    fixed = lambda i: (0, 0)
    smem_col = pl.BlockSpec((TOP_K, tm), lambda i: (0, i), memory_space=pltpu.SMEM)
    smem_next = pl.BlockSpec((TOP_K, tm), lambda i: (0, jnp.minimum(i + 1, n_tiles - 1)),
                             memory_space=pltpu.SMEM)
    grid_spec = pltpu.PrefetchScalarGridSpec(
        num_scalar_prefetch=0, grid=(n_tiles,),
        in_specs=[smem_col, smem_next, pl.BlockSpec((tm, TOP_K), row),
                  pl.BlockSpec((tm, d), row), pl.BlockSpec((tm, d), row), pl.BlockSpec(memory_space=pl.ANY),
                  pl.BlockSpec(sg.shape, fixed), pl.BlockSpec(su.shape, fixed), pl.BlockSpec(sd.shape, fixed),
                  pl.BlockSpec((1, d), fixed), pl.BlockSpec((1, d), fixed)],
        out_specs=[pl.BlockSpec((tm, d), row), pl.BlockSpec((tm, d), row)],
        scratch_shapes=[pltpu.VMEM((2, TOP_K, tm, d // 2), jnp.uint32), pltpu.SemaphoreType.DMA((2,))])
    return pl.pallas_call(
        functools.partial(_combine_body, tm=tm), grid_spec=grid_spec,
        out_shape=[jax.ShapeDtypeStruct((n, d), F32), jax.ShapeDtypeStruct((n, d), BF16)],
        compiler_params=_cparams(1), name="combine",
    )(pos, pos, gate, xb, xf, y, sg, su, sd, g, b)


def _moe(xf, xb, router_w, router_b, w_gate, w_up, w_down, sh_gate, sh_up, sh_down, ln_g, ln_b):
    n, d = xf.shape
    e_idx, gate, rank, cnt = _router(xf, router_w, router_b)
    counts = cnt[:, 0].astype(jnp.int32)
    padded = (counts + MOE_BM - 1) // MOE_BM * MOE_BM
    pad_end = jnp.cumsum(padded)
    pad_start = (pad_end - padded).astype(jnp.int32)
    n_blocks = (n * TOP_K + N_EXPERTS * (MOE_BM - 1) + MOE_BM - 1) // MOE_BM
    blk_e = jnp.minimum(jnp.sum(pad_end[None, :] <= (jnp.arange(n_blocks) * MOE_BM)[:, None], axis=1),
                        N_EXPERTS - 1).astype(jnp.int32)
    n_used = (pad_end[-1] // MOE_BM).astype(jnp.int32).reshape(1)
    fill = jnp.concatenate([jnp.stack([pad_start + counts, pad_end], axis=1).reshape(-1), n_used]).astype(jnp.int32)
    experts = jnp.arange(N_EXPERTS, dtype=jnp.int32)[:, None, None]
    pos = rank + jnp.sum(jnp.where(e_idx[None] == experts, pad_start[:, None, None], 0), axis=0)
    xs = _dispatch(xb, pos, fill, n_blocks * MOE_BM)
    y = _moe_ffn(xs, blk_e, n_used, w_gate, w_up, w_down)
    return _combine(pos, gate.T, xb, xf, y, sh_gate, sh_up, sh_down, ln_g[None], ln_b[None])


def _rope_tables(seq):
    half = ROPE_DIM // 2
    freqs = ROPE_THETA ** (-jnp.arange(half, dtype=F32) / half)
    ang = jnp.arange(seq).astype(F32)[:, None] * freqs
    cos = jnp.concatenate([jnp.cos(ang)] * 2, -1)
    sin = jnp.concatenate([jnp.sin(ang)] * 2, -1)
    z = lambda w: jnp.zeros((seq, w), F32)
    pad = LANE - NOPE - ROPE_DIM
    cos_q = jnp.concatenate([jnp.ones((seq, NOPE), F32), cos, z(pad)], -1)
    sin_q = jnp.concatenate([z(NOPE), sin, z(pad)], -1)
    cos_k = jnp.concatenate([z(NOPE), cos, z(pad)], -1)
    return cos_q, sin_q, cos_k, sin_q


def _rot_cols(w):
    half = w.shape[-1] // 2
    return jnp.concatenate([-w[..., half:], w[..., :half]], -1)


def _mixer_ab(xb, xf, bsz, seq, w_in, q_norm, w_uq, kv_norm, w_ukv, w_out, ln_g, ln_b, rope_tabs, dil_bias):
    n = bsz * seq
    d = w_in.shape[0]
    c0 = Q_LORA + KV_LORA
    w_kr = w_in[:, c0:c0 + ROPE_DIM]
    zc = lambda w: jnp.zeros((d, w), F32)
    pad = LANE - NOPE - ROPE_DIM
    w1 = jnp.concatenate([w_in[:, :c0], zc(NOPE), w_kr, zc(pad), zc(NOPE), _rot_cols(w_kr), zc(pad)], 1)
    h1 = _mm(xb, w1.astype(BF16), F32, tn=LANE)
    h2 = _mm(xb, w_in[:, c0 + ROPE_DIM:].astype(BF16), BF16, tn=256)

    wq = w_uq.reshape(Q_LORA, H_A, NOPE + ROPE_DIM)
    zq = jnp.zeros((Q_LORA, H_A, pad), F32)
    wq_main = jnp.concatenate([wq, zq], -1).reshape(Q_LORA, H_A * LANE)
    wq_rot = jnp.concatenate([jnp.zeros((Q_LORA, H_A, NOPE), F32), _rot_cols(wq[..., NOPE:]), zq], -1)
    wq_rot = wq_rot.reshape(Q_LORA, H_A * LANE)
    wkv = w_ukv.reshape(KV_LORA, H_A, NOPE + MLA_V)
    wk = jnp.concatenate([wkv[..., :NOPE], jnp.zeros((KV_LORA, H_A, LANE - NOPE), F32)], -1)
    wk = wk.reshape(KV_LORA, H_A * LANE)
    wv = wkv[..., NOPE:].reshape(KV_LORA, H_A * MLA_V)
    q_a, k_a, v_a = _mla_up(h1, q_norm[None], wq_main.astype(BF16), wq_rot.astype(BF16), kv_norm[None],
                            wk.astype(BF16), wv.astype(BF16), *rope_tabs, seq)
    o_a = _flash(q_a.reshape(bsz, seq, -1), k_a.reshape(bsz, seq, -1), v_a.reshape(bsz, seq, -1), name="mla_attn",
                 n_outer=H_A // 2, T=min(ATT_TQ, seq), TK=ATT_TK, dq=LANE, dv=MLA_V,
                 q_col=lambda g: g, k_col=lambda g: g, v_col=lambda g: g,
                 q_offs=(0, LANE), k_offs=(0, LANE), v_offs=(0, MLA_V),
                 scale=(NOPE + ROPE_DIM) ** -0.5, out_cols=H_A * MLA_V, out_col=lambda g: g)

    gw = H_B_GROUP * HD_B
    hq = h2.reshape(bsz, seq, 3, len(DIL_PAIRS), gw)
    outs, lses = [], []
    for gi, (window, dil) in enumerate(DIL_PAIRS):
        L = seq // dil
        t = hq[:, :, :, gi].reshape(bsz, L, dil, 3 * gw).transpose(0, 2, 1, 3).reshape(bsz, seq, 3 * gw)
        T = min(256, L)
        o, lse = _flash(t, t, t, name="dilated_attn", n_outer=2, T=T, dq=HD_B, dv=HD_B,
                        q_col=lambda g: g, k_col=lambda g: 2 + g, v_col=lambda g: 4 + g,
                        q_offs=(0, HD_B), k_offs=(0, HD_B), v_offs=(0, HD_B), scale=HD_B ** -0.5,
                        out_cols=gw, out_col=lambda g: g, bias=dil_bias[gi], bias_mode="pair",
                        bias_idx=lambda g: g, want_lse=True, seg_tiles=L // T, nback=1)
        outs.append(o.reshape(bsz, dil, L, gw).transpose(0, 2, 1, 3).reshape(bsz, seq, H_B_GROUP, HD_B).astype(F32))
        lses.append(lse.reshape(bsz, H_B_GROUP, dil, L).transpose(0, 3, 2, 1).reshape(bsz, seq, H_B_GROUP))
    w = jax.nn.softmax(jnp.stack(lses), axis=0)
    o_b = jnp.sum(w[..., None] * jnp.stack(outs), axis=0).astype(BF16).reshape(n, gw)
    na = H_A * MLA_V
    return _out_ln([o_a.reshape(n, na)], o_b, w_out[:na].astype(BF16), w_out[na:].astype(BF16),
                   xf, ln_g[None], ln_b[None])


def _mixer_cd(xb, xf, bsz, seq, w_in, pos_k, k_w1, k_w2, pos_v, v_w1, v_w2, lq1, lk1, lq2, lk2, d_norm,
              w_out, ln_g, ln_b, lam_init, tabs):
    n = bsz * seq
    qc_w = H_C * DK_C
    kv_w = G_C * DK_C
    off = qc_w
    kvs = []
    for _ in range(3):
        wk_ = w_in[:, off:off + kv_w].reshape(-1, G_C, DK_C)
        wv_ = w_in[:, off + kv_w:off + 2 * kv_w].reshape(-1, G_C, DK_C)
        kvs.append(jnp.concatenate([wk_, wv_], -1).reshape(-1, 2 * kv_w))
        off += 2 * kv_w
    g_off = off
    d_off = off + 3 * H_C
    w_main = jnp.concatenate([w_in[:, :qc_w]] + kvs + [w_in[:, d_off:]], 1)
    h = _mm(xb, w_main.astype(BF16), BF16, tn=256).reshape(bsz, seq, -1)
    w_g = jnp.repeat(w_in[:, g_off:d_off], DK_C, axis=1)
    gates = _mm(xb, w_g.astype(BF16), F32, tn=256, act="sigmoid").reshape(bsz, seq, -1)

    ncp = seq // CMP_STRIDE
    half = CMP_STRIDE * DK_C
    kv_cmp = h[:, :, qc_w:qc_w + 2 * kv_w].reshape(bsz, ncp, CMP_STRIDE, G_C, 2, DK_C)
    u = kv_cmp.transpose(0, 3, 4, 1, 2, 5).reshape(bsz, G_C, 2, ncp, half)
    pe = jnp.stack([pos_k.reshape(2, 1, half), pos_v.reshape(2, 1, half)])
    w1 = jnp.stack([k_w1.reshape(2, half, CMP_HID), v_w1.reshape(2, half, CMP_HID)])
    w2 = jnp.stack([k_w2, v_w2])
    kvc = _compress(u, pe, w1, w2)
    o_cmp, sel = _cmp_attn(h, kvc, tabs["bias_c"], tabs["overlap"], gates)

    cb = qc_w // LANE
    scale = DK_C ** -0.5
    n_pairs = H_C // 2
    nsa = dict(n_outer=n_pairs, dq=DK_C, dv=DK_C, q_col=lambda g: g, q_offs=(0, DK_C), k_offs=(0, 0),
               v_offs=(DK_C, DK_C), scale=scale, out_cols=qc_w, out_col=lambda g: g, bias_mode="pair",
               bias_idx=lambda g: g, gate=gates, vmem=VMEM_LIMIT, k_w=LANE, v_w=LANE)
    o_sel = _flash(h, h, h, name="nsa_sel_attn", T=ATT_TQ, TK=ATT_TK, k_col=lambda g: cb + 2 + g // 2, v_col=lambda g: cb + 2 + g // 2,
                   bias=tabs["bias_sel"], sel=sel, sel_idx=lambda g: g // 2,
                   gate_col=lambda g: n_pairs + g, **nsa)
    o_win = _flash(h, h, h, name="nsa_win_attn", T=WIN, k_col=lambda g: cb + 4 + g // 2, v_col=lambda g: cb + 4 + g // 2,
                   bias=tabs["bias_win"], gate_col=lambda g: 2 * n_pairs + g, nback=1, **nsa)

    lam = (jnp.exp(jnp.sum(lq1.astype(F32) * lk1.astype(F32)))
           - jnp.exp(jnp.sum(lq2.astype(F32) * lk2.astype(F32))) + lam_init).reshape(1, 1)
    db = cb + 6
    o_d = _flash(h, h, h, name="diff_attn", n_outer=H_D, T=ATT_TQ, TK=ATT_TK, dq=DD, dv=2 * DD,
                 q_col=lambda g: db + g, k_col=lambda g: db + H_D + g, v_col=lambda g: db + 2 * H_D + g,
                 q_offs=(0, DD), k_offs=(0, DD), v_offs=(0, 0), scale=DD ** -0.5,
                 out_cols=H_D * 2 * DD, out_col=lambda g: g, bias=tabs["bias_d"], bias_mode="shared",
                 bias_idx=lambda g: g, lam=lam, dnorm=d_norm[:, None], lam_init=lam_init, epilogue="diff",
                 vmem=VMEM_LIMIT)
    r2 = lambda a: a.reshape(n, -1)
    return _out_ln([r2(o_cmp), r2(o_sel), r2(o_win)], r2(o_d), w_out[:qc_w].astype(BF16),
                   w_out[qc_w:].astype(BF16), xf, ln_g[None], ln_b[None])


def _nsa_tables(rel_bias, seq):
    tab_c = rel_bias[:, H_B:H_B + H_C]
    tab_d = rel_bias[:, H_B + H_C:H_B + H_C + H_D]
    ncp = seq // CMP_STRIDE
    n_sel = seq // SEL_BLOCK
    pos = jnp.arange(seq)
    x = jnp.arange(2 * ncp)
    c_minus_a = jnp.where(x < ncp, x, x - 2 * ncp)
    dist = -CMP_STRIDE * c_minus_a[None, :] + jnp.arange(CMP_STRIDE)[:, None] - (CMP_LEN - 1)
    w = jnp.moveaxis(tab_c[_rel_bucket(dist)].astype(F32), -1, 0)
    bias_c = _toeplitz(w, ncp, ncp).transpose(0, 2, 1, 3).reshape(H_C, seq, ncp)
    c0 = jnp.arange(ncp) * CMP_STRIDE
    s0 = jnp.arange(n_sel) * SEL_BLOCK
    overlap = jnp.maximum(jnp.minimum(c0[:, None] + CMP_LEN, s0[None, :] + SEL_BLOCK)
                          - jnp.maximum(c0[:, None], s0[None, :]), 0).astype(F32) / CMP_LEN
    return {
        "bias_c": bias_c, "overlap": overlap.T,
        "bias_sel": _toeplitz_bias(tab_c, ATT_TQ, seq // ATT_TK, 1, seq, ATT_TK),
        "bias_win": _toeplitz_bias(tab_c, WIN, 2, 1, WIN - 1),
        "bias_d": _toeplitz_bias(tab_d, ATT_TQ, seq // ATT_TK, 1, seq, ATT_TK),
    }


def kernel(x, rel_bias, ab_w_in, mla_q_norm, mla_w_uq, mla_kv_norm, mla_w_ukv, ab_w_out, cd_w_in, nsa_cmp_pos_k, nsa_cmp_k_w1, nsa_cmp_k_w2, nsa_cmp_pos_v, nsa_cmp_v_w1, nsa_cmp_v_w2, diff_lambda_q1, diff_lambda_k1, diff_lambda_q2, diff_lambda_k2, diff_norm, cd_w_out, ln1_g, ln1_b, ln2_g, ln2_b, router_w, router_b, exp_w_gate, exp_w_up, exp_w_down, sh_w_gate, sh_w_up, sh_w_down):
    bsz, seq, d = x.shape
    n = bsz * seq
    depth = ln1_g.shape[0]
    rope_tabs = _rope_tables(seq)
    dil_bias = [_toeplitz_bias(rel_bias[:, gi * H_B_GROUP:(gi + 1) * H_B_GROUP], min(256, seq // dil), 2, dil,
                               window // dil) for gi, (window, dil) in enumerate(DIL_PAIRS)]
    nsa_tabs = _nsa_tables(rel_bias, seq)
    xf = x.reshape(n, d)
    xb = xf.astype(BF16)
    for l in range(depth):
        i = l // 2
        if l % 2 == 0:
            xf, xb = _mixer_ab(xb, xf, bsz, seq, ab_w_in[i], mla_q_norm[i], mla_w_uq[i], mla_kv_norm[i],
                               mla_w_ukv[i], ab_w_out[i], ln1_g[l], ln1_b[l], rope_tabs, dil_bias)
        else:
            lam_init = 0.8 - 0.6 * math.exp(-0.3 * l)
            xf, xb = _mixer_cd(xb, xf, bsz, seq, cd_w_in[i], nsa_cmp_pos_k[i], nsa_cmp_k_w1[i],
                               nsa_cmp_k_w2[i], nsa_cmp_pos_v[i], nsa_cmp_v_w1[i], nsa_cmp_v_w2[i],
                               diff_lambda_q1[i], diff_lambda_k1[i], diff_lambda_q2[i], diff_lambda_k2[i],
                               diff_norm[i], cd_w_out[i], ln1_g[l], ln1_b[l], lam_init, nsa_tabs)
        xf, xb = _moe(xf, xb, router_w[l], router_b[l], exp_w_gate[l], exp_w_up[l],
                      exp_w_down[l], sh_w_gate[l].astype(BF16), sh_w_up[l].astype(BF16),
                      sh_w_down[l].astype(BF16), ln2_g[l], ln2_b[l])
    return xf.reshape(bsz, seq, d)
```

```python
import functools
import math

import jax
import jax.numpy as jnp
from jax import lax
from jax.experimental import pallas as pl
from jax.experimental.pallas import tpu as pltpu

F32 = jnp.float32
BF16 = jnp.bfloat16
HI = lax.Precision.HIGHEST

DEPTH = 4
NEG = -1e30
BIG = 1e9
LN_EPS = 1e-5
RMS_EPS = 1e-6
ALPHA = (2 * DEPTH) ** 0.25

N_BUCKETS = 32
REL_MAX_DIST = 2048

H_A = 12
NOPE = 64
ROPE_DIM = 32
MLA_V = 64
Q_LORA = 256
KV_LORA = 128
ROPE_THETA = 10000.0

DIL_PAIRS = ((128, 1), (512, 4), (2048, 16))
H_B_GROUP = 4
H_B = 12
HD_B = 64

H_C = 8
G_C = 2
R_C = 4
DK_C = 64
CMP_LEN = 32
CMP_STRIDE = 16
CMP_HID = 64
SEL_BLOCK = 64
SEL_TOP = 16
WIN = 512

H_D = 4
DD = 64

N_EXPERTS = 64
TOP_K = 8
N_EXPERT_GROUPS = 8
TOPK_GROUPS = 4
D_EXPERT = 256
ROUTED_SCALE = 2.5

LANE = 128
MOE_BM = 256
ATT_TQ = 512
ATT_TK = 256
VMEM_LIMIT = 56 * 1024 * 1024

_NT = (((1,), (1,)), ((), ()))


def _cparams(n_axes, vmem=None):
    return pltpu.CompilerParams(dimension_semantics=("arbitrary",) * n_axes, vmem_limit_bytes=vmem)


def _mm_body(x_ref, w_ref, o_ref, *, act, precision):
    y = jnp.dot(x_ref[...], w_ref[...], preferred_element_type=F32, precision=precision)
    if act == "sigmoid":
        y = jax.nn.sigmoid(y)
    o_ref[...] = y.astype(o_ref.dtype)


def _mm(x, w, out_dtype, tn, tm=1024, act=None, precision=None):
    m, k = x.shape
    nc = w.shape[1]
    tm = min(tm, m)
    return pl.pallas_call(
        functools.partial(_mm_body, act=act, precision=precision),
        grid=(m // tm, nc // tn),
        in_specs=[pl.BlockSpec((tm, k), lambda i, j: (i, 0)),
                  pl.BlockSpec((k, tn), lambda i, j: (0, j))],
        out_specs=pl.BlockSpec((tm, tn), lambda i, j: (i, j)),
        out_shape=jax.ShapeDtypeStruct((m, nc), out_dtype),
        compiler_params=_cparams(2, VMEM_LIMIT), name="proj",
    )(x, w)


def _layer_norm(z, g, b):
    mu = jnp.mean(z, axis=-1, keepdims=True)
    zc = z - mu
    var = jnp.mean(zc * zc, axis=-1, keepdims=True)
    return zc * lax.rsqrt(var + LN_EPS) * g + b


def _out_ln_body(*refs, n_sum):
    a0 = refs[0][...].astype(F32)
    for r in refs[1:n_sum]:
        a0 = a0 + r[...].astype(F32)
    a1_ref, w0_ref, w1_ref, r_ref, g_ref, b_ref, of_ref, ob_ref = refs[n_sum:]
    y = jnp.dot(a0.astype(BF16), w0_ref[...], preferred_element_type=F32)
    y = y + jnp.dot(a1_ref[...], w1_ref[...], preferred_element_type=F32)
    out = _layer_norm(ALPHA * r_ref[...] + y, g_ref[...], b_ref[...])
    of_ref[...] = out
    ob_ref[...] = out.astype(BF16)


def _out_ln(a0s, a1, w0, w1, resid, g, b, tm=256):
    n, d = resid.shape
    k0, k1 = w0.shape[0], w1.shape[0]
    row = lambda i: (i, 0)
    fixed = lambda i: (0, 0)
    return pl.pallas_call(
        functools.partial(_out_ln_body, n_sum=len(a0s)),
        grid=(n // tm,),
        in_specs=[pl.BlockSpec((tm, k0), row)] * len(a0s) + [
            pl.BlockSpec((tm, k1), row), pl.BlockSpec((k0, d), fixed), pl.BlockSpec((k1, d), fixed),
            pl.BlockSpec((tm, d), row), pl.BlockSpec((1, d), fixed), pl.BlockSpec((1, d), fixed)],
        out_specs=[pl.BlockSpec((tm, d), row), pl.BlockSpec((tm, d), row)],
        out_shape=[jax.ShapeDtypeStruct((n, d), F32), jax.ShapeDtypeStruct((n, d), BF16)],
        compiler_params=_cparams(1), name="out_proj_ln",
    )(*a0s, a1, w0, w1, resid, g, b)


def _rms(x, g):
    return x * lax.rsqrt(jnp.mean(x * x, axis=-1, keepdims=True) + RMS_EPS) * g


def _mla_q_body(c_ref, g_ref, w_ref, wr_ref, cos_ref, sin_ref, o_ref):
    cn = _rms(c_ref[...], g_ref[...]).astype(BF16)
    a = jnp.dot(cn, w_ref[...], preferred_element_type=F32)
    r = jnp.dot(cn, wr_ref[...], preferred_element_type=F32)
    cos, sin = cos_ref[...], sin_ref[...]
    for h in range(H_A):
        sl = slice(h * LANE, (h + 1) * LANE)
        o_ref[:, sl] = (a[:, sl] * cos + r[:, sl] * sin).astype(o_ref.dtype)


def _mla_kv_body(c_ref, kr_ref, krr_ref, g_ref, wk_ref, wv_ref, cos_ref, sin_ref, k_ref, v_ref):
    cn = _rms(c_ref[...], g_ref[...]).astype(BF16)
    kn = jnp.dot(cn, wk_ref[...], preferred_element_type=F32)
    rope = kr_ref[...] * cos_ref[...] + krr_ref[...] * sin_ref[...]
    for h in range(H_A):
        sl = slice(h * LANE, (h + 1) * LANE)
        k_ref[:, sl] = (kn[:, sl] + rope).astype(k_ref.dtype)
    v_ref[...] = jnp.dot(cn, wv_ref[...], preferred_element_type=F32).astype(v_ref.dtype)


def _mla_up(h1, q_norm, wq, wq_rot, kv_norm, wk, wv, cos_q, sin_q, cos_k, sin_k, seq, tm=512):
    n = h1.shape[0]
    spt = seq // tm
    row = lambda c: (lambda i: (i, c))
    pos = lambda i: (i % spt, 0)
    fixed = lambda i: (0, 0)
    q_a = pl.pallas_call(
        _mla_q_body, grid=(n // tm,),
        in_specs=[pl.BlockSpec((tm, Q_LORA), row(0)), pl.BlockSpec((1, Q_LORA), fixed),
                  pl.BlockSpec(wq.shape, fixed), pl.BlockSpec(wq_rot.shape, fixed),
                  pl.BlockSpec((tm, LANE), pos), pl.BlockSpec((tm, LANE), pos)],
        out_specs=pl.BlockSpec((tm, H_A * LANE), row(0)),
        out_shape=jax.ShapeDtypeStruct((n, H_A * LANE), BF16),
        compiler_params=_cparams(1), name="mla_q_up",
    )(h1, q_norm, wq, wq_rot, cos_q, sin_q)
    k_a, v_a = pl.pallas_call(
        _mla_kv_body, grid=(n // tm,),
        in_specs=[pl.BlockSpec((tm, LANE), row(2)), pl.BlockSpec((tm, LANE), row(3)),
                  pl.BlockSpec((tm, LANE), row(4)), pl.BlockSpec((1, KV_LORA), fixed),
                  pl.BlockSpec(wk.shape, fixed), pl.BlockSpec(wv.shape, fixed),
                  pl.BlockSpec((tm, LANE), pos), pl.BlockSpec((tm, LANE), pos)],
        out_specs=[pl.BlockSpec((tm, H_A * LANE), row(0)), pl.BlockSpec((tm, H_A * MLA_V), row(0))],
        out_shape=[jax.ShapeDtypeStruct((n, H_A * LANE), BF16), jax.ShapeDtypeStruct((n, H_A * MLA_V), BF16)],
        compiler_params=_cparams(1), name="mla_kv_up",
    )(h1, h1, h1, kv_norm, wk, wv, cos_k, sin_k)
    return q_a, k_a, v_a


def _flash_body(*refs, T, TK, dq, dv, q_offs, k_offs, v_offs, scale, bias_mode, has_sel, has_gate,
                epilogue, want_lse, seg_tiles, nback, lam_init):
    R = T // TK
    it = iter(refs)
    q_ref, k_ref, v_ref = next(it), next(it), next(it)
    bias_ref = next(it) if bias_mode else None
    sel_ref = next(it) if has_sel else None
    gate_ref = next(it) if has_gate else None
    lam_ref, dn_ref = (next(it), next(it)) if epilogue == "diff" else (None, None)
    o_ref = next(it)
    lse_ref = next(it) if want_lse else None
    vt_scr = next(it)

    qi = pl.program_id(2)
    seq = v_ref.shape[1]

    @pl.when(qi == 0)
    def _():
        for c in range(seq // TK):
            vt_scr[:, c * TK:(c + 1) * TK] = v_ref[0, c * TK:(c + 1) * TK, :].astype(F32).T.astype(BF16)

    qfull = q_ref[0].astype(F32)
    fold_scale = math.frexp(scale)[0] == 0.5
    qts = [(qfull[:, off:off + dq] * (scale if fold_scale else 1.0)).T.astype(BF16) for off in q_offs]

    def qk(kc):
        kfull = k_ref[0, pl.ds(pl.multiple_of(kc * TK, TK), TK), :]
        return tuple(jnp.dot(kfull[:, k_offs[u]:k_offs[u] + dq], qts[u], preferred_element_type=F32)
                     for u in range(2))

    def update(kc, state, scores, diag):
        start = pl.multiple_of(kc * TK, TK)
        new_state = []
        sel_add = None
        if has_sel:
            per = TK // SEL_BLOCK
            rows = [sel_ref[0, 0, pl.ds(kc * per + a, 1), :] for a in range(per)]
            sel_add = jnp.concatenate([jnp.broadcast_to((r - 1.0) * (-NEG), (SEL_BLOCK, T)) for r in rows], axis=0)
        for u in range(2):
            vt = vt_scr[v_offs[u]:v_offs[u] + dv, pl.ds(start, TK)]
            s = scores[u]
            if not fold_scale:
                s = s * scale
            if bias_mode:
                s = s + bias_ref[u if bias_mode == "pair" else 0, R * qi - kc + (R - 1)]
            elif diag is not None:
                key = lax.broadcasted_iota(jnp.int32, (TK, T), 0) + diag * TK
                qry = lax.broadcasted_iota(jnp.int32, (TK, T), 1)
                s = jnp.where(key <= qry, s, NEG)
            if has_sel:
                s = s + sel_add
            m_prev, l_prev, acc_prev = state[u]
            m_new = jnp.maximum(m_prev, jnp.max(s, axis=0, keepdims=True))
            alpha = jnp.exp(m_prev - m_new)
            p = jnp.exp(s - m_new)
            l_new = alpha * l_prev + jnp.sum(p, axis=0, keepdims=True)
            acc_new = alpha * acc_prev + jnp.dot(vt, p.astype(BF16), preferred_element_type=F32)
            new_state.append((m_new, l_new, acc_new))
        return tuple(new_state)

    init = tuple((jnp.full((1, T), NEG, F32), jnp.zeros((1, T), F32), jnp.zeros((dv, T), F32)) for _ in range(2))
    if nback is None:
        lo = 0
    elif seg_tiles is not None:
        lo = jnp.where(qi % seg_tiles == 0, qi, qi - nback)
    else:
        lo = jnp.maximum(qi - nback, 0)

    def step(kc, carry):
        state, scores = carry
        nxt = qk(kc + 1)
        return update(kc, state, scores, None), nxt

    state, scores = lax.fori_loop(lo, R * qi, step, (init, qk(lo)))
    for a in range(R):
        nxt = qk(R * qi + a + 1) if a + 1 < R else None
        state = update(R * qi + a, state, scores, a)
        scores = nxt

    outs = [acc / l for _, l, acc in state]
    if epilogue == "diff":
        a = outs[0] - lam_ref[0, 0] * outs[1]
        rinv = lax.rsqrt(jnp.mean(a * a, axis=0, keepdims=True) + RMS_EPS)
        o = (a * rinv * dn_ref[...] * (1.0 - lam_init)).T
    else:
        o = jnp.concatenate(outs, axis=0).T
        if has_gate:
            o = o * gate_ref[0]
    o_ref[0] = o.astype(o_ref.dtype)
    if want_lse:
        lse_ref[0, 0] = jnp.concatenate([m + jnp.log(l) for m, l, _ in state], axis=0)


def _flash(q, k, v, *, n_outer, T, dq, dv, q_col, k_col, v_col, q_offs, k_offs, v_offs, scale,
           out_cols, out_col, bias=None, bias_mode=None, bias_idx=None, sel=None, sel_idx=None,
           gate=None, gate_col=None, lam=None, dnorm=None, lam_init=0.0, epilogue="plain", want_lse=False,
           seg_tiles=None, nback=None, vmem=None, k_w=None, v_w=None, name="flash", TK=None,
           rep=1, rep_in=0, rep_out=0):
    bsz, seq, _ = q.shape
    TK = TK or T
    assert T % TK == 0 and (nback is None or T == TK)
    nq = seq // T
    qw = max(o + dq for o in q_offs)
    kw = k_w or max(o + dq for o in k_offs)
    vw = v_w or max(o + dv for o in v_offs)
    ow = dv if epilogue == "diff" else 2 * dv
    in_specs = [pl.BlockSpec((1, T, qw), lambda g, b, i: (b // rep, i, q_col(g) + (b % rep) * rep_in)),
                pl.BlockSpec((1, seq, kw), lambda g, b, i: (b // rep, 0, k_col(g) + (b % rep) * rep_in)),
                pl.BlockSpec((1, seq, vw), lambda g, b, i: (b // rep, 0, v_col(g) + (b % rep) * rep_in))]
    args = [q, k, v]
    if bias_mode:
        nb = 2 if bias_mode == "pair" else 1
        in_specs.append(pl.BlockSpec((nb,) + bias.shape[1:], lambda g, b, i: (bias_idx(g), 0, 0, 0)))
        args.append(bias)
    if sel is not None:
        in_specs.append(pl.BlockSpec((1, 1, sel.shape[2], T), lambda g, b, i: (b, sel_idx(g), 0, i)))
        args.append(sel)
    if gate is not None:
        in_specs.append(pl.BlockSpec((1, T, ow), lambda g, b, i: (b, i, gate_col(g))))
        args.append(gate)
    if epilogue == "diff":
        in_specs.append(pl.BlockSpec(memory_space=pltpu.SMEM))
        in_specs.append(pl.BlockSpec((dv, 1), lambda g, b, i: (0, 0)))
        args += [lam, dnorm]
    out_specs = [pl.BlockSpec((1, T, ow), lambda g, b, i: (b // rep, i, out_col(g) + (b % rep) * rep_out))]
    out_shape = [jax.ShapeDtypeStruct((bsz, seq, out_cols), BF16)]
    if want_lse:
        out_specs.append(pl.BlockSpec((1, 1, 2, T), lambda g, b, i: (b, g, 0, i)))
        out_shape.append(jax.ShapeDtypeStruct((bsz * rep, n_outer, 2, seq), F32))
    body = functools.partial(
        _flash_body, T=T, TK=TK, dq=dq, dv=dv, q_offs=q_offs, k_offs=k_offs, v_offs=v_offs, scale=scale,
        bias_mode=bias_mode, has_sel=sel is not None, has_gate=gate is not None, epilogue=epilogue,
        want_lse=want_lse, seg_tiles=seg_tiles, nback=nback, lam_init=lam_init)
    res = pl.pallas_call(
        body, grid=(n_outer, bsz * rep, nq), in_specs=in_specs, out_specs=out_specs, out_shape=out_shape,
        scratch_shapes=[pltpu.VMEM((vw, seq), BF16)],
        compiler_params=_cparams(3, vmem), name=name,
    )(*args)
    return res if want_lse else res[0]


def _rel_bucket(dist):
    n = jnp.maximum(dist, 0)
    exact = N_BUCKETS // 2
    log_ratio = jnp.log(jnp.maximum(n, 1).astype(F32) / exact) / math.log(REL_MAX_DIST / exact)
    large = exact + (log_ratio * (N_BUCKETS - exact)).astype(jnp.int32)
    return jnp.where(n < exact, n, jnp.minimum(large, N_BUCKETS - 1))


def _toeplitz_bias(tab, T, n_d, dist_scale, max_dist, TK=None):
    TK = TK or T
    R = T // TK
    wlen = T + TK
    x = jnp.arange(wlen)
    dist = (jnp.arange(n_d)[:, None] - (R - 1)) * TK + jnp.where(x < T, x, x - wlen)[None, :]
    w = tab[_rel_bucket(dist * dist_scale)].astype(F32)
    w = jnp.where(((dist >= 0) & (dist <= max_dist))[..., None], w, NEG)
    return _toeplitz(jnp.moveaxis(w, -1, 0), TK, T)


def _toeplitz(w, rows, cols):
    wlen = w.shape[-1]
    tiled = jnp.tile(w, (1,) * (w.ndim - 1) + (rows,))[..., :rows * (wlen - 1)]
    return tiled.reshape(w.shape[:-1] + (rows, wlen - 1))[..., :cols]


def _compress_body(u_ref, pe_ref, w1_ref, w2_ref, o_ref, *, ncp):
    outs = []
    for a in range(2):
        u = u_ref[0, 0, a].astype(F32)
        p1 = jnp.dot(u + pe_ref[a, 0], w1_ref[a, 0], preferred_element_type=F32, precision=HI)
        p2 = jnp.dot(u + pe_ref[a, 1], w1_ref[a, 1], preferred_element_type=F32, precision=HI)
        hid = jax.nn.gelu(p1 + pltpu.roll(p2, ncp - 1, 0))
        outs.append(jnp.dot(hid, w2_ref[a], preferred_element_type=F32, precision=HI))
    o_ref[0, 0] = jnp.concatenate(outs, axis=-1)


def _compress(u, pe, w1, w2):
    bsz, g, _, ncp, width = u.shape
    return pl.pallas_call(
        functools.partial(_compress_body, ncp=ncp), grid=(bsz, g),
        in_specs=[pl.BlockSpec((1, 1, 2, ncp, width), lambda b, gg: (b, gg, 0, 0, 0)),
                  pl.BlockSpec(pe.shape, lambda b, gg: (0, 0, 0, 0)),
                  pl.BlockSpec(w1.shape, lambda b, gg: (0, 0, 0, 0)),
                  pl.BlockSpec(w2.shape, lambda b, gg: (0, 0, 0))],
        out_specs=pl.BlockSpec((1, 1, ncp, 2 * DK_C), lambda b, gg: (b, gg, 0, 0)),
        out_shape=jax.ShapeDtypeStruct((bsz, g, ncp, 2 * DK_C), F32),
        compiler_params=_cparams(2), name="nsa_compress",
    )(u, pe, w1, w2)


def _cmp_attn_body(q_ref, kv_ref, bias_ref, ov_ref, gate_ref, o_ref, sel_ref, *, T, ncp, n_sel, n_top, scale):
    qi = pl.program_id(2)
    kc = kv_ref[0, 0, :, :DK_C]
    vc = kv_ref[0, 0, :, DK_C:]
    t = qi * T + lax.broadcasted_iota(jnp.int32, (T, ncp), 0)
    c = lax.broadcasted_iota(jnp.int32, (T, ncp), 1)
    valid = t >= c * CMP_STRIDE + (CMP_LEN - 1)
    validf = valid.astype(F32)
    psum = jnp.zeros((T, ncp), F32)
    outs = []
    for r in range(R_C):
        q = q_ref[0, :, r * DK_C:(r + 1) * DK_C].astype(F32)
        s = lax.dot_general(q, kc, _NT, preferred_element_type=F32, precision=HI) * scale + bias_ref[r]
        s = jnp.where(valid, s, NEG)
        e = jnp.exp(s - jnp.max(s, axis=-1, keepdims=True)) * validf
        p = e / jnp.maximum(jnp.sum(e, axis=-1, keepdims=True), 1e-30)
        outs.append(jnp.dot(p, vc, preferred_element_type=F32, precision=HI))
        psum = psum + p
    o_ref[0] = (jnp.concatenate(outs, axis=-1) * gate_ref[0]).astype(o_ref.dtype)

    imp = jnp.dot(ov_ref[...], psum.T, preferred_element_type=F32, precision=HI)
    tq = qi * T + lax.broadcasted_iota(jnp.int32, (n_sel, T), 1)
    j = lax.broadcasted_iota(jnp.int32, (n_sel, T), 0)
    forced = (j == tq // SEL_BLOCK) | (j == 0)
    work = jnp.where(forced, BIG, jnp.where(j * SEL_BLOCK <= tq, imp, -BIG))
    sel = jnp.zeros((n_sel, T), F32)
    jf = j.astype(F32)
    for _ in range(n_top):
        _, _, pick = _first_max(work, jf, n_sel)
        sel = jnp.where(pick, 1.0, sel)
        work = jnp.where(pick, -jnp.inf, work)
    sel_ref[0, 0] = sel


def _cmp_attn(h, kvc, bias_c, overlap, gates, T=256):
    bsz, seq, _ = h.shape
    ncp = kvc.shape[2]
    n_sel = seq // SEL_BLOCK
    n_top = min(SEL_TOP, n_sel)
    qw = R_C * DK_C
    return pl.pallas_call(
        functools.partial(_cmp_attn_body, T=T, ncp=ncp, n_sel=n_sel, n_top=n_top, scale=DK_C ** -0.5),
        grid=(G_C, bsz, seq // T),
        in_specs=[pl.BlockSpec((1, T, qw), lambda g, b, i: (b, i, g)),
                  pl.BlockSpec((1, 1, ncp, 2 * DK_C), lambda g, b, i: (b, g, 0, 0)),
                  pl.BlockSpec((R_C, T, ncp), lambda g, b, i: (g, i, 0)),
                  pl.BlockSpec(overlap.shape, lambda g, b, i: (0, 0)),
                  pl.BlockSpec((1, T, qw), lambda g, b, i: (b, i, g))],
        out_specs=[pl.BlockSpec((1, T, qw), lambda g, b, i: (b, i, g)),
                   pl.BlockSpec((1, 1, n_sel, T), lambda g, b, i: (b, g, 0, i))],
        out_shape=[jax.ShapeDtypeStruct((bsz, seq, H_C * DK_C), BF16),
                   jax.ShapeDtypeStruct((bsz, G_C, n_sel, seq), F32)],
        compiler_params=_cparams(3), name="nsa_cmp_attn",
    )(h, kvc, bias_c, overlap, gates)


def _first_max(work, idx, n):
    mx = jnp.max(work, axis=0, keepdims=True)
    first = jnp.min(jnp.where(work == mx, idx, float(n)), axis=0, keepdims=True)
    return mx, first, idx == first


def _router_body(x_ref, wt_ref, b_ref, tri_ref, e_ref, g_ref, r_ref, cnt_ref, carry_scr, *, tm):
    i = pl.program_id(0)

    @pl.when(i == 0)
    def _():
        carry_scr[...] = jnp.zeros(carry_scr.shape, F32)

    st = lax.dot_general(wt_ref[...], x_ref[...], _NT, preferred_element_type=F32, precision=HI)
    scores = jax.nn.sigmoid(st)
    sel = scores + b_ref[...]
    per = N_EXPERTS // N_EXPERT_GROUPS
    fiota = lambda rows: lax.broadcasted_iota(jnp.int32, (rows, tm), 0).astype(F32)
    i_per, i_grp, i_exp = fiota(per), fiota(N_EXPERT_GROUPS), fiota(N_EXPERTS)
    grp_scores = []
    for g in range(N_EXPERT_GROUPS):
        blk = sel[g * per:(g + 1) * per]
        m1, _, pick = _first_max(blk, i_per, per)
        grp_scores.append(m1 + jnp.max(jnp.where(pick, -jnp.inf, blk), axis=0, keepdims=True))
    work = jnp.concatenate(grp_scores, axis=0)
    gmask = jnp.zeros((N_EXPERT_GROUPS, tm), F32)
    for _ in range(TOPK_GROUPS):
        _, _, pick = _first_max(work, i_grp, N_EXPERT_GROUPS)
        gmask = jnp.where(pick, 1.0, gmask)
        work = jnp.where(pick, -jnp.inf, work)
    work = jnp.concatenate([jnp.where(gmask[g:g + 1] > 0.5, sel[g * per:(g + 1) * per], NEG)
                            for g in range(N_EXPERT_GROUPS)], axis=0)
    picks, firsts, vals = [], [], []
    for _ in range(TOP_K):
        _, first, pick = _first_max(work, i_exp, N_EXPERTS)
        picks.append(pick)
        firsts.append(first)
        vals.append(jnp.sum(jnp.where(pick, scores, 0.0), axis=0, keepdims=True))
        work = jnp.where(pick, -jnp.inf, work)
    val = jnp.concatenate(vals, axis=0)
    g_ref[...] = val / jnp.sum(val, axis=0, keepdims=True) * ROUTED_SCALE
    e_ref[...] = jnp.concatenate(firsts, axis=0).astype(jnp.int32)
    onehot = picks[0].astype(F32)
    for pick in picks[1:]:
        onehot = onehot + pick.astype(F32)
    before = jnp.dot(onehot.astype(BF16), tri_ref[...], preferred_element_type=F32) + carry_scr[...]
    r_ref[...] = jnp.concatenate([jnp.sum(jnp.where(pick, before, 0.0), axis=0, keepdims=True)
                                  for pick in picks], axis=0).astype(jnp.int32)
    carry = carry_scr[...] + jnp.sum(onehot, axis=1, keepdims=True)
    carry_scr[...] = carry
    cnt_ref[...] = jnp.broadcast_to(carry, cnt_ref.shape)


def _router(xf, router_w, router_b, tm=256):
    n, d = xf.shape
    tri = (jnp.arange(tm)[:, None] < jnp.arange(tm)[None, :]).astype(BF16)
    col = lambda i: (0, i)
    fixed = lambda i: (0, 0)
    return pl.pallas_call(
        functools.partial(_router_body, tm=tm), grid=(n // tm,),
        in_specs=[pl.BlockSpec((tm, d), lambda i: (i, 0)), pl.BlockSpec((N_EXPERTS, d), fixed),
                  pl.BlockSpec((N_EXPERTS, 1), fixed), pl.BlockSpec((tm, tm), fixed)],
        out_specs=[pl.BlockSpec((TOP_K, tm), col), pl.BlockSpec((TOP_K, tm), col),
                   pl.BlockSpec((TOP_K, tm), col), pl.BlockSpec((N_EXPERTS, LANE), fixed)],
        out_shape=[jax.ShapeDtypeStruct((TOP_K, n), jnp.int32), jax.ShapeDtypeStruct((TOP_K, n), F32),
                   jax.ShapeDtypeStruct((TOP_K, n), jnp.int32), jax.ShapeDtypeStruct((N_EXPERTS, LANE), F32)],
        scratch_shapes=[pltpu.VMEM((N_EXPERTS, 1), F32)],
        compiler_params=_cparams(1), name="router",
    )(xf, router_w.T, router_b.astype(F32)[:, None], tri)


def _pack_rows(x):
    w = x.shape[-1] // 2
    lo = lax.bitcast_convert_type(x[:, :w].astype(BF16).astype(F32), jnp.uint32)
    hi = lax.bitcast_convert_type(x[:, w:].astype(BF16).astype(F32), jnp.uint32)
    return (lo >> 16) | (hi & jnp.uint32(0xFFFF0000))


def _unpack_rows(p):
    lo = lax.bitcast_convert_type(p << 16, F32)
    hi = lax.bitcast_convert_type(p & jnp.uint32(0xFFFF0000), F32)
    return lo, hi


def _dispatch_body(fill_ref, pos_ref, x_ref, z_ref, xs_hbm, xp_scr, sem, *, tm, n_blocks):
    i = pl.program_id(0)
    slot = i % 2
    xp_scr[slot] = _pack_rows(x_ref[...])
    zsem = sem.at[2]

    def row_copy(s, src, dst):
        return pltpu.make_async_copy(xp_scr.at[s, pl.ds(src, 1)], xs_hbm.at[pl.ds(dst, 1)], sem.at[s])

    def zero_copy(dst):
        return pltpu.make_async_copy(z_ref.at[pl.ds(0, 1)], xs_hbm.at[pl.ds(dst, 1)], zsem)

    def zero_block(blk):
        return pltpu.make_async_copy(z_ref, xs_hbm.at[pl.ds(blk * MOE_BM, MOE_BM)], zsem)

    def issue(t, c):
        for k in range(TOP_K):
            row_copy(slot, t, pos_ref[k, t]).start(priority=k % 2)
        return c

    def drain(s):
        def body(t, c):
            for k in range(TOP_K):
                row_copy(s, 0, 0).wait()
            return c

        lax.fori_loop(0, tm, body, 0)

    lax.fori_loop(0, tm, issue, 0)

    @pl.when(i > 0)
    def _():
        drain(1 - slot)

    @pl.when(i == pl.num_programs(0) - 1)
    def _():
        drain(slot)

    @pl.when(i == 0)
    def _():
        def fill(e, c):
            lo, hi = fill_ref[2 * e], fill_ref[2 * e + 1]

            def start(r, cc):
                zero_copy(r).start()
                return cc

            def wait(r, cc):
                zero_copy(0).wait()
                return cc

            lax.fori_loop(lo, hi, start, 0)
            lax.fori_loop(lo, hi, wait, 0)
            return c

        lax.fori_loop(0, N_EXPERTS, fill, 0)

        def tail_start(blk, c):
            zero_block(blk).start()
            return c

        def tail_wait(blk, c):
            zero_block(0).wait()
            return c

        lax.fori_loop(fill_ref[2 * N_EXPERTS], n_blocks, tail_start, 0)
        lax.fori_loop(fill_ref[2 * N_EXPERTS], n_blocks, tail_wait, 0)


def _dispatch(xb, pos, fill, p, tm=128):
    n, d = xb.shape
    grid_spec = pltpu.PrefetchScalarGridSpec(
        num_scalar_prefetch=1, grid=(n // tm,),
        in_specs=[pl.BlockSpec((TOP_K, tm), lambda i, fl: (0, i), memory_space=pltpu.SMEM),
                  pl.BlockSpec((tm, d), lambda i, fl: (i, 0)),
                  pl.BlockSpec((MOE_BM, d // 2), lambda i, fl: (0, 0))],
        out_specs=pl.BlockSpec(memory_space=pl.ANY),
        scratch_shapes=[pltpu.VMEM((2, tm, d // 2), jnp.uint32), pltpu.SemaphoreType.DMA((3,))])
    return pl.pallas_call(
        functools.partial(_dispatch_body, tm=tm, n_blocks=p // MOE_BM), grid_spec=grid_spec,
        out_shape=jax.ShapeDtypeStruct((p, d // 2), jnp.uint32),
        compiler_params=_cparams(1), name="dispatch",
    )(fill, pos, xb, jnp.zeros((MOE_BM, d // 2), jnp.uint32))


def _moe_ffn_body(be_ref, nb_ref, x_ref, wg_ref, wu_ref, wd_ref, o_ref, wg_scr, wu_scr, wd_scr):
    i = pl.program_id(0)
    half = x_ref.shape[1]

    @pl.when((i == 0) | (be_ref[i] != be_ref[jnp.maximum(i - 1, 0)]))
    def _():
        wg_scr[...] = wg_ref[0].astype(BF16)
        wu_scr[...] = wu_ref[0].astype(BF16)
        wd_scr[...] = wd_ref[0].astype(BF16)

    @pl.when(i < nb_ref[0])
    def _():
        lo, hi = _unpack_rows(x_ref[...])
        lo, hi = lo.astype(BF16), hi.astype(BF16)
        hg = (jnp.dot(lo, wg_scr[:half], preferred_element_type=F32)
              + jnp.dot(hi, wg_scr[half:], preferred_element_type=F32))
        hu = (jnp.dot(lo, wu_scr[:half], preferred_element_type=F32)
              + jnp.dot(hi, wu_scr[half:], preferred_element_type=F32))
        hb = (hg * jax.nn.sigmoid(hg) * hu).astype(BF16)
        o_ref[...] = _pack_rows(jnp.dot(hb, wd_scr[...], preferred_element_type=F32))

    @pl.when(i >= nb_ref[0])
    def _():
        o_ref[...] = jnp.zeros(o_ref.shape, o_ref.dtype)


def _moe_ffn(xs, blk_e, n_used, wg, wu, wd):
    p, half = xs.shape
    d = 2 * half
    n_blocks = p // MOE_BM
    grid_spec = pltpu.PrefetchScalarGridSpec(
        num_scalar_prefetch=2, grid=(n_blocks,),
        in_specs=[pl.BlockSpec((MOE_BM, half), lambda i, be, nb: (i, 0)),
                  pl.BlockSpec((1, d, D_EXPERT), lambda i, be, nb: (be[i], 0, 0)),
                  pl.BlockSpec((1, d, D_EXPERT), lambda i, be, nb: (be[i], 0, 0)),
                  pl.BlockSpec((1, D_EXPERT, d), lambda i, be, nb: (be[i], 0, 0))],
        out_specs=pl.BlockSpec((MOE_BM, half), lambda i, be, nb: (i, 0)),
        scratch_shapes=[pltpu.VMEM((d, D_EXPERT), BF16), pltpu.VMEM((d, D_EXPERT), BF16),
                        pltpu.VMEM((D_EXPERT, d), BF16)])
    return pl.pallas_call(
        _moe_ffn_body, grid_spec=grid_spec, out_shape=jax.ShapeDtypeStruct((p, half), jnp.uint32),
        compiler_params=_cparams(1), name="expert_ffn",
    )(blk_e, n_used, xs, wg, wu, wd)


def _combine_body(pos_ref, posn_ref, gate_ref, xb_ref, xf_ref, y_hbm, sg_ref, su_ref, sd_ref,
                  g_ref, b_ref, of_ref, ob_ref, buf, sem, *, tm):
    i = pl.program_id(0)
    slot = i % 2

    def row_copy(s, k, t, src):
        return pltpu.make_async_copy(y_hbm.at[pl.ds(src, 1)], buf.at[s, k, pl.ds(t, 1)], sem.at[s])

    def issue_tile(pr, s):
        def issue(t, c):
            for k in range(TOP_K):
                row_copy(s, k, t, pr[k, t]).start(priority=k % 2)
            return c

        lax.fori_loop(0, tm, issue, 0)

    @pl.when(i == 0)
    def _():
        issue_tile(pos_ref, 0)

    @pl.when(i + 1 < pl.num_programs(0))
    def _():
        issue_tile(posn_ref, 1 - slot)

    x = xb_ref[...]
    hg = jnp.dot(x, sg_ref[...], preferred_element_type=F32)
    hu = jnp.dot(x, su_ref[...], preferred_element_type=F32)
    hb = (hg * jax.nn.sigmoid(hg) * hu).astype(BF16)
    y = jnp.dot(hb, sd_ref[...], preferred_element_type=F32)

    def drain(t, c):
        for k in range(TOP_K):
            row_copy(slot, k, t, 0).wait()
        return c

    lax.fori_loop(0, tm, drain, 0)
    gate = gate_ref[...]
    half = buf.shape[-1]
    y_lo, y_hi = y[:, :half], y[:, half:]
    for k in range(TOP_K):
        lo, hi = _unpack_rows(buf[slot, k])
        y_lo = y_lo + gate[:, k:k + 1] * lo
        y_hi = y_hi + gate[:, k:k + 1] * hi
    out = _layer_norm(ALPHA * xf_ref[...] + jnp.concatenate([y_lo, y_hi], axis=-1), g_ref[...], b_ref[...])
    of_ref[...] = out
    ob_ref[...] = out.astype(BF16)


def _combine(pos, gate, xb, xf, y, sg, su, sd, g, b, tm=128):
    n, d = xf.shape
    n_tiles = n // tm
    row = lambda i: (i, 0)
    fixed = lambda i: (0, 0)
    smem_col = pl.BlockSpec((TOP_K, tm), lambda i: (0, i), memory_space=pltpu.SMEM)
    smem_next = pl.BlockSpec((TOP_K, tm), lambda i: (0, jnp.minimum(i + 1, n_tiles - 1)),
                             memory_space=pltpu.SMEM)
    grid_spec = pltpu.PrefetchScalarGridSpec(
        num_scalar_prefetch=0, grid=(n_tiles,),
        in_specs=[smem_col, smem_next, pl.BlockSpec((tm, TOP_K), row),
                  pl.BlockSpec((tm, d), row), pl.BlockSpec((tm, d), row), pl.BlockSpec(memory_space=pl.ANY),
                  pl.BlockSpec(sg.shape, fixed), pl.BlockSpec(su.shape, fixed), pl.BlockSpec(sd.shape, fixed),
                  pl.BlockSpec((1, d), fixed), pl.BlockSpec((1, d), fixed)],
        out_specs=[pl.BlockSpec((tm, d), row), pl.BlockSpec((tm, d), row)],
        scratch_shapes=[pltpu.VMEM((2, TOP_K, tm, d // 2), jnp.uint32), pltpu.SemaphoreType.DMA((2,))])
    return pl.pallas_call(
        functools.partial(_combine_body, tm=tm), grid_spec=grid_spec,
        out_shape=[jax.ShapeDtypeStruct((n, d), F32), jax.ShapeDtypeStruct((n, d), BF16)],
        compiler_params=_cparams(1), name="combine",
    )(pos, pos, gate, xb, xf, y, sg, su, sd, g, b)


def _moe(xf, xb, router_w, router_b, w_gate, w_up, w_down, sh_gate, sh_up, sh_down, ln_g, ln_b):
    n, d = xf.shape
    e_idx, gate, rank, cnt = _router(xf, router_w, router_b)
    counts = cnt[:, 0].astype(jnp.int32)
    padded = (counts + MOE_BM - 1) // MOE_BM * MOE_BM
    pad_end = jnp.cumsum(padded)
    pad_start = (pad_end - padded).astype(jnp.int32)
    n_blocks = (n * TOP_K + N_EXPERTS * (MOE_BM - 1) + MOE_BM - 1) // MOE_BM
    blk_e = jnp.minimum(jnp.sum(pad_end[None, :] <= (jnp.arange(n_blocks) * MOE_BM)[:, None], axis=1),
                        N_EXPERTS - 1).astype(jnp.int32)
    n_used = (pad_end[-1] // MOE_BM).astype(jnp.int32).reshape(1)
    fill = jnp.concatenate([jnp.stack([pad_start + counts, pad_end], axis=1).reshape(-1), n_used]).astype(jnp.int32)
    experts = jnp.arange(N_EXPERTS, dtype=jnp.int32)[:, None, None]
    pos = rank + jnp.sum(jnp.where(e_idx[None] == experts, pad_start[:, None, None], 0), axis=0)
    xs = _dispatch(xb, pos, fill, n_blocks * MOE_BM)
    y = _moe_ffn(xs, blk_e, n_used, w_gate, w_up, w_down)
    return _combine(pos, gate.T, xb, xf, y, sh_gate, sh_up, sh_down, ln_g[None], ln_b[None])


def _rope_tables(seq):
    half = ROPE_DIM // 2
    freqs = ROPE_THETA ** (-jnp.arange(half, dtype=F32) / half)
    ang = jnp.arange(seq).astype(F32)[:, None] * freqs
    cos = jnp.concatenate([jnp.cos(ang)] * 2, -1)
    sin = jnp.concatenate([jnp.sin(ang)] * 2, -1)
    z = lambda w: jnp.zeros((seq, w), F32)
    pad = LANE - NOPE - ROPE_DIM
    cos_q = jnp.concatenate([jnp.ones((seq, NOPE), F32), cos, z(pad)], -1)
    sin_q = jnp.concatenate([z(NOPE), sin, z(pad)], -1)
    cos_k = jnp.concatenate([z(NOPE), cos, z(pad)], -1)
    return cos_q, sin_q, cos_k, sin_q


def _rot_cols(w):
    half = w.shape[-1] // 2
    return jnp.concatenate([-w[..., half:], w[..., :half]], -1)


def _mixer_ab(xb, xf, bsz, seq, w_in, q_norm, w_uq, kv_norm, w_ukv, w_out, ln_g, ln_b, rope_tabs, dil_bias):
    n = bsz * seq
    d = w_in.shape[0]
    c0 = Q_LORA + KV_LORA
    w_kr = w_in[:, c0:c0 + ROPE_DIM]
    zc = lambda w: jnp.zeros((d, w), F32)
    pad = LANE - NOPE - ROPE_DIM
    w1 = jnp.concatenate([w_in[:, :c0], zc(NOPE), w_kr, zc(pad), zc(NOPE), _rot_cols(w_kr), zc(pad)], 1)
    h1 = _mm(xb, w1.astype(BF16), F32, tn=w1.shape[1])
    h2 = _mm(xb, w_in[:, c0 + ROPE_DIM:].astype(BF16), BF16, tn=3 * HD_B * H_B_GROUP)

    wq = w_uq.reshape(Q_LORA, H_A, NOPE + ROPE_DIM)
    zq = jnp.zeros((Q_LORA, H_A, pad), F32)
    wq_main = jnp.concatenate([wq, zq], -1).reshape(Q_LORA, H_A * LANE)
    wq_rot = jnp.concatenate([jnp.zeros((Q_LORA, H_A, NOPE), F32), _rot_cols(wq[..., NOPE:]), zq], -1)
    wq_rot = wq_rot.reshape(Q_LORA, H_A * LANE)
    wkv = w_ukv.reshape(KV_LORA, H_A, NOPE + MLA_V)
    wk = jnp.concatenate([wkv[..., :NOPE], jnp.zeros((KV_LORA, H_A, LANE - NOPE), F32)], -1)
    wk = wk.reshape(KV_LORA, H_A * LANE)
    wv = wkv[..., NOPE:].reshape(KV_LORA, H_A * MLA_V)
    q_a, k_a, v_a = _mla_up(h1, q_norm[None], wq_main.astype(BF16), wq_rot.astype(BF16), kv_norm[None],
                            wk.astype(BF16), wv.astype(BF16), *rope_tabs, seq)
    o_a = _flash(q_a.reshape(bsz, seq, -1), k_a.reshape(bsz, seq, -1), v_a.reshape(bsz, seq, -1), name="mla_attn",
                 n_outer=H_A // 2, T=min(ATT_TQ, seq), TK=ATT_TK, dq=LANE, dv=MLA_V,
                 q_col=lambda g: g, k_col=lambda g: g, v_col=lambda g: g,
                 q_offs=(0, LANE), k_offs=(0, LANE), v_offs=(0, MLA_V),
                 scale=(NOPE + ROPE_DIM) ** -0.5, out_cols=H_A * MLA_V, out_col=lambda g: g)

    gw = H_B_GROUP * HD_B
    hw = h2.shape[1]
    nqb = H_B * HD_B // LANE
    outs, lses = [], []
    for gi, (window, dil) in enumerate(DIL_PAIRS):
        L = seq // dil
        t = h2.reshape(bsz, L, dil * hw)
        c0 = gi * gw // LANE
        o, lse = _flash(t, t, t, name="dilated_attn", n_outer=2, T=min(256, L), dq=HD_B, dv=HD_B,
                        q_col=lambda g, c0=c0: c0 + g, k_col=lambda g, c0=c0: nqb + c0 + g,
                        v_col=lambda g, c0=c0: 2 * nqb + c0 + g,
                        q_offs=(0, HD_B), k_offs=(0, HD_B), v_offs=(0, HD_B), scale=HD_B ** -0.5,
                        out_cols=dil * gw, out_col=lambda g: g, bias=dil_bias[gi], bias_mode="pair",
                        bias_idx=lambda g: g, want_lse=True, nback=1,
                        rep=dil, rep_in=hw // LANE, rep_out=gw // LANE)
        outs.append(o.reshape(bsz, seq, H_B_GROUP, HD_B).astype(F32))
        lses.append(lse.reshape(bsz, dil, H_B_GROUP, L).transpose(0, 3, 1, 2).reshape(bsz, seq, H_B_GROUP))
    w = jax.nn.softmax(jnp.stack(lses), axis=0)
    o_b = jnp.sum(w[..., None] * jnp.stack(outs), axis=0).astype(BF16).reshape(n, gw)
    na = H_A * MLA_V
    return _out_ln([o_a.reshape(n, na)], o_b, w_out[:na].astype(BF16), w_out[na:].astype(BF16),
                   xf, ln_g[None], ln_b[None])


def _mixer_cd(xb, xf, bsz, seq, w_in, pos_k, k_w1, k_w2, pos_v, v_w1, v_w2, lq1, lk1, lq2, lk2, d_norm,
              w_out, ln_g, ln_b, lam_init, tabs):
    n = bsz * seq
    qc_w = H_C * DK_C
    kv_w = G_C * DK_C
    off = qc_w
    kvs = []
    for _ in range(3):
        wk_ = w_in[:, off:off + kv_w].reshape(-1, G_C, DK_C)
        wv_ = w_in[:, off + kv_w:off + 2 * kv_w].reshape(-1, G_C, DK_C)
        kvs.append(jnp.concatenate([wk_, wv_], -1).reshape(-1, 2 * kv_w))
        off += 2 * kv_w
    g_off = off
    d_off = off + 3 * H_C
    w_main = jnp.concatenate([w_in[:, :qc_w]] + kvs + [w_in[:, d_off:]], 1)
    h = _mm(xb, w_main.astype(BF16), BF16, tn=w_main.shape[1] // 2).reshape(bsz, seq, -1)
    w_g = jnp.repeat(w_in[:, g_off:d_off], DK_C, axis=1)
    gates = _mm(xb, w_g.astype(BF16), F32, tn=w_g.shape[1] // 2, act="sigmoid").reshape(bsz, seq, -1)

    ncp = seq // CMP_STRIDE
    half = CMP_STRIDE * DK_C
    kv_cmp = h[:, :, qc_w:qc_w + 2 * kv_w].reshape(bsz, ncp, CMP_STRIDE, G_C, 2, DK_C)
    u = kv_cmp.transpose(0, 3, 4, 1, 2, 5).reshape(bsz, G_C, 2, ncp, half)
    pe = jnp.stack([pos_k.reshape(2, 1, half), pos_v.reshape(2, 1, half)])
    w1 = jnp.stack([k_w1.reshape(2, half, CMP_HID), v_w1.reshape(2, half, CMP_HID)])
    w2 = jnp.stack([k_w2, v_w2])
    kvc = _compress(u, pe, w1, w2)
    o_cmp, sel = _cmp_attn(h, kvc, tabs["bias_c"], tabs["overlap"], gates)

    cb = qc_w // LANE
    scale = DK_C ** -0.5
    n_pairs = H_C // 2
    nsa = dict(n_outer=n_pairs, dq=DK_C, dv=DK_C, q_col=lambda g: g, q_offs=(0, DK_C), k_offs=(0, 0),
               v_offs=(DK_C, DK_C), scale=scale, out_cols=qc_w, out_col=lambda g: g, bias_mode="pair",
               bias_idx=lambda g: g, gate=gates, vmem=VMEM_LIMIT, k_w=LANE, v_w=LANE)
    o_sel = _flash(h, h, h, name="nsa_sel_attn", T=ATT_TQ, TK=ATT_TK, k_col=lambda g: cb + 2 + g // 2, v_col=lambda g: cb + 2 + g // 2,
                   bias=tabs["bias_sel"], sel=sel, sel_idx=lambda g: g // 2,
                   gate_col=lambda g: n_pairs + g, **nsa)
    o_win = _flash(h, h, h, name="nsa_win_attn", T=WIN, k_col=lambda g: cb + 4 + g // 2, v_col=lambda g: cb + 4 + g // 2,
                   bias=tabs["bias_win"], gate_col=lambda g: 2 * n_pairs + g, nback=1, **nsa)

    lam = (jnp.exp(jnp.sum(lq1.astype(F32) * lk1.astype(F32)))
           - jnp.exp(jnp.sum(lq2.astype(F32) * lk2.astype(F32))) + lam_init).reshape(1, 1)
    db = cb + 6
    o_d = _flash(h, h, h, name="diff_attn", n_outer=H_D, T=ATT_TQ, TK=ATT_TK, dq=DD, dv=2 * DD,
                 q_col=lambda g: db + g, k_col=lambda g: db + H_D + g, v_col=lambda g: db + 2 * H_D + g,
                 q_offs=(0, DD), k_offs=(0, DD), v_offs=(0, 0), scale=DD ** -0.5,
                 out_cols=H_D * 2 * DD, out_col=lambda g: g, bias=tabs["bias_d"], bias_mode="shared",
                 bias_idx=lambda g: g, lam=lam, dnorm=d_norm[:, None], lam_init=lam_init, epilogue="diff",
                 vmem=VMEM_LIMIT)
    r2 = lambda a: a.reshape(n, -1)
    return _out_ln([r2(o_cmp), r2(o_sel), r2(o_win)], r2(o_d), w_out[:qc_w].astype(BF16),
                   w_out[qc_w:].astype(BF16), xf, ln_g[None], ln_b[None])


def _nsa_tables(rel_bias, seq):
    tab_c = rel_bias[:, H_B:H_B + H_C]
    tab_d = rel_bias[:, H_B + H_C:H_B + H_C + H_D]
    ncp = seq // CMP_STRIDE
    n_sel = seq // SEL_BLOCK
    pos = jnp.arange(seq)
    x = jnp.arange(2 * ncp)
    c_minus_a = jnp.where(x < ncp, x, x - 2 * ncp)
    dist = -CMP_STRIDE * c_minus_a[None, :] + jnp.arange(CMP_STRIDE)[:, None] - (CMP_LEN - 1)
    w = jnp.moveaxis(tab_c[_rel_bucket(dist)].astype(F32), -1, 0)
    bias_c = _toeplitz(w, ncp, ncp).transpose(0, 2, 1, 3).reshape(H_C, seq, ncp)
    c0 = jnp.arange(ncp) * CMP_STRIDE
    s0 = jnp.arange(n_sel) * SEL_BLOCK
    overlap = jnp.maximum(jnp.minimum(c0[:, None] + CMP_LEN, s0[None, :] + SEL_BLOCK)
                          - jnp.maximum(c0[:, None], s0[None, :]), 0).astype(F32) / CMP_LEN
    return {
        "bias_c": bias_c, "overlap": overlap.T,
        "bias_sel": _toeplitz_bias(tab_c, ATT_TQ, seq // ATT_TK, 1, seq, ATT_TK),
        "bias_win": _toeplitz_bias(tab_c, WIN, 2, 1, WIN - 1),
        "bias_d": _toeplitz_bias(tab_d, ATT_TQ, seq // ATT_TK, 1, seq, ATT_TK),
    }


def kernel(x, rel_bias, ab_w_in, mla_q_norm, mla_w_uq, mla_kv_norm, mla_w_ukv, ab_w_out, cd_w_in, nsa_cmp_pos_k, nsa_cmp_k_w1, nsa_cmp_k_w2, nsa_cmp_pos_v, nsa_cmp_v_w1, nsa_cmp_v_w2, diff_lambda_q1, diff_lambda_k1, diff_lambda_q2, diff_lambda_k2, diff_norm, cd_w_out, ln1_g, ln1_b, ln2_g, ln2_b, router_w, router_b, exp_w_gate, exp_w_up, exp_w_down, sh_w_gate, sh_w_up, sh_w_down):
    bsz, seq, d = x.shape
    n = bsz * seq
    depth = ln1_g.shape[0]
    rope_tabs = _rope_tables(seq)
    dil_bias = [_toeplitz_bias(rel_bias[:, gi * H_B_GROUP:(gi + 1) * H_B_GROUP], min(256, seq // dil), 2, dil,
                               window // dil) for gi, (window, dil) in enumerate(DIL_PAIRS)]
    nsa_tabs = _nsa_tables(rel_bias, seq)
    xf = x.reshape(n, d)
    xb = xf.astype(BF16)
    for l in range(depth):
        i = l // 2
        if l % 2 == 0:
            xf, xb = _mixer_ab(xb, xf, bsz, seq, ab_w_in[i], mla_q_norm[i], mla_w_uq[i], mla_kv_norm[i],
                               mla_w_ukv[i], ab_w_out[i], ln1_g[l], ln1_b[l], rope_tabs, dil_bias)
        else:
            lam_init = 0.8 - 0.6 * math.exp(-0.3 * l)
            xf, xb = _mixer_cd(xb, xf, bsz, seq, cd_w_in[i], nsa_cmp_pos_k[i], nsa_cmp_k_w1[i],
                               nsa_cmp_k_w2[i], nsa_cmp_pos_v[i], nsa_cmp_v_w1[i], nsa_cmp_v_w2[i],
                               diff_lambda_q1[i], diff_lambda_k1[i], diff_lambda_q2[i], diff_lambda_k2[i],
                               diff_norm[i], cd_w_out[i], ln1_g[l], ln1_b[l], lam_init, nsa_tabs)
        xf, xb = _moe(xf, xb, router_w[l], router_b[l], exp_w_gate[l], exp_w_up[l],
                      exp_w_down[l], sh_w_gate[l].astype(BF16), sh_w_up[l].astype(BF16),
                      sh_w_down[l].astype(BF16), ln2_g[l], ln2_b[l])
    return xf.reshape(bsz, seq, d)
```

```python
import functools
import math

import jax
import jax.numpy as jnp
from jax import lax
from jax.experimental import pallas as pl
from jax.experimental.pallas import tpu as pltpu

F32 = jnp.float32
BF16 = jnp.bfloat16
HI = lax.Precision.HIGHEST

DEPTH = 4
NEG = -1e30
BIG = 1e9
LN_EPS = 1e-5
RMS_EPS = 1e-6
ALPHA = (2 * DEPTH) ** 0.25

N_BUCKETS = 32
REL_MAX_DIST = 2048

H_A = 12
NOPE = 64
ROPE_DIM = 32
MLA_V = 64
Q_LORA = 256
KV_LORA = 128
ROPE_THETA = 10000.0

DIL_PAIRS = ((128, 1), (512, 4), (2048, 16))
H_B_GROUP = 4
H_B = 12
HD_B = 64

H_C = 8
G_C = 2
R_C = 4
DK_C = 64
CMP_LEN = 32
CMP_STRIDE = 16
CMP_HID = 64
SEL_BLOCK = 64
SEL_TOP = 16
WIN = 512

H_D = 4
DD = 64

N_EXPERTS = 64
TOP_K = 8
N_EXPERT_GROUPS = 8
TOPK_GROUPS = 4
D_EXPERT = 256
ROUTED_SCALE = 2.5

LANE = 128
SUBLANES = 8
MOE_BM = 512
ATT_TQ = 512
ATT_TK = 256
VMEM_LIMIT = 56 * 1024 * 1024

_NT = (((1,), (1,)), ((), ()))


def _cparams(n_axes, vmem=None):
    return pltpu.CompilerParams(dimension_semantics=("arbitrary",) * n_axes, vmem_limit_bytes=vmem)


def _mm_body(x_ref, w_ref, o_ref, *, act, precision):
    y = jnp.dot(x_ref[...], w_ref[...], preferred_element_type=F32, precision=precision)
    if act == "sigmoid":
        y = jax.nn.sigmoid(y)
    o_ref[...] = y.astype(o_ref.dtype)


def _mm(x, w, out_dtype, tn, tm=1024, act=None, precision=None):
    m, k = x.shape
    nc = w.shape[1]
    tm = min(tm, m)
    return pl.pallas_call(
        functools.partial(_mm_body, act=act, precision=precision),
        grid=(m // tm, nc // tn),
        in_specs=[pl.BlockSpec((tm, k), lambda i, j: (i, 0)),
                  pl.BlockSpec((k, tn), lambda i, j: (0, j))],
        out_specs=pl.BlockSpec((tm, tn), lambda i, j: (i, j)),
        out_shape=jax.ShapeDtypeStruct((m, nc), out_dtype),
        compiler_params=_cparams(2, VMEM_LIMIT), name="proj",
    )(x, w)


def _layer_norm(z, g, b):
    mu = jnp.mean(z, axis=-1, keepdims=True)
    zc = z - mu
    var = jnp.mean(zc * zc, axis=-1, keepdims=True)
    return zc * lax.rsqrt(var + LN_EPS) * g + b


def _out_ln_body(*refs, n_sum):
    a0 = refs[0][...].astype(F32)
    for r in refs[1:n_sum]:
        a0 = a0 + r[...].astype(F32)
    a1_ref, w0_ref, w1_ref, r_ref, g_ref, b_ref, of_ref, ob_ref = refs[n_sum:]
    y = jnp.dot(a0.astype(BF16), w0_ref[...], preferred_element_type=F32)
    y = y + jnp.dot(a1_ref[...], w1_ref[...], preferred_element_type=F32)
    out = _layer_norm(ALPHA * r_ref[...] + y, g_ref[...], b_ref[...])
    of_ref[...] = out
    ob_ref[...] = out.astype(BF16)


def _out_ln(a0s, a1, w0, w1, resid, g, b, tm=256):
    n, d = resid.shape
    k0, k1 = w0.shape[0], w1.shape[0]
    row = lambda i: (i, 0)
    fixed = lambda i: (0, 0)
    return pl.pallas_call(
        functools.partial(_out_ln_body, n_sum=len(a0s)),
        grid=(n // tm,),
        in_specs=[pl.BlockSpec((tm, k0), row)] * len(a0s) + [
            pl.BlockSpec((tm, k1), row), pl.BlockSpec((k0, d), fixed), pl.BlockSpec((k1, d), fixed),
            pl.BlockSpec((tm, d), row), pl.BlockSpec((1, d), fixed), pl.BlockSpec((1, d), fixed)],
        out_specs=[pl.BlockSpec((tm, d), row), pl.BlockSpec((tm, d), row)],
        out_shape=[jax.ShapeDtypeStruct((n, d), F32), jax.ShapeDtypeStruct((n, d), BF16)],
        compiler_params=_cparams(1), name="out_proj_ln",
    )(*a0s, a1, w0, w1, resid, g, b)


def _rms(x, g):
    return x * lax.rsqrt(jnp.mean(x * x, axis=-1, keepdims=True) + RMS_EPS) * g


def _mla_q_body(c_ref, g_ref, w_ref, wr_ref, cos_ref, sin_ref, o_ref):
    cn = _rms(c_ref[...], g_ref[...]).astype(BF16)
    a = jnp.dot(cn, w_ref[...], preferred_element_type=F32)
    r = jnp.dot(cn, wr_ref[...], preferred_element_type=F32)
    cos, sin = cos_ref[...], sin_ref[...]
    for h in range(H_A):
        sl = slice(h * LANE, (h + 1) * LANE)
        o_ref[:, sl] = (a[:, sl] * cos + r[:, sl] * sin).astype(o_ref.dtype)


def _mla_kv_body(c_ref, kr_ref, krr_ref, g_ref, wk_ref, wv_ref, cos_ref, sin_ref, k_ref, v_ref):
    cn = _rms(c_ref[...], g_ref[...]).astype(BF16)
    kn = jnp.dot(cn, wk_ref[...], preferred_element_type=F32)
    rope = kr_ref[...] * cos_ref[...] + krr_ref[...] * sin_ref[...]
    for h in range(H_A):
        sl = slice(h * LANE, (h + 1) * LANE)
        k_ref[:, sl] = (kn[:, sl] + rope).astype(k_ref.dtype)
    v_ref[...] = jnp.dot(cn, wv_ref[...], preferred_element_type=F32).astype(v_ref.dtype)


def _mla_up(h1, q_norm, wq, wq_rot, kv_norm, wk, wv, cos_q, sin_q, cos_k, sin_k, seq, tm=512):
    n = h1.shape[0]
    spt = seq // tm
    row = lambda c: (lambda i: (i, c))
    pos = lambda i: (i % spt, 0)
    fixed = lambda i: (0, 0)
    q_a = pl.pallas_call(
        _mla_q_body, grid=(n // tm,),
        in_specs=[pl.BlockSpec((tm, Q_LORA), row(0)), pl.BlockSpec((1, Q_LORA), fixed),
                  pl.BlockSpec(wq.shape, fixed), pl.BlockSpec(wq_rot.shape, fixed),
                  pl.BlockSpec((tm, LANE), pos), pl.BlockSpec((tm, LANE), pos)],
        out_specs=pl.BlockSpec((tm, H_A * LANE), row(0)),
        out_shape=jax.ShapeDtypeStruct((n, H_A * LANE), BF16),
        compiler_params=_cparams(1), name="mla_q_up",
    )(h1, q_norm, wq, wq_rot, cos_q, sin_q)
    k_a, v_a = pl.pallas_call(
        _mla_kv_body, grid=(n // tm,),
        in_specs=[pl.BlockSpec((tm, LANE), row(2)), pl.BlockSpec((tm, LANE), row(3)),
                  pl.BlockSpec((tm, LANE), row(4)), pl.BlockSpec((1, KV_LORA), fixed),
                  pl.BlockSpec(wk.shape, fixed), pl.BlockSpec(wv.shape, fixed),
                  pl.BlockSpec((tm, LANE), pos), pl.BlockSpec((tm, LANE), pos)],
        out_specs=[pl.BlockSpec((tm, H_A * LANE), row(0)), pl.BlockSpec((tm, H_A * MLA_V), row(0))],
        out_shape=[jax.ShapeDtypeStruct((n, H_A * LANE), BF16), jax.ShapeDtypeStruct((n, H_A * MLA_V), BF16)],
        compiler_params=_cparams(1), name="mla_kv_up",
    )(h1, h1, h1, kv_norm, wk, wv, cos_k, sin_k)
    return q_a, k_a, v_a


def _flash_body(*refs, T, TK, dq, dv, q_offs, k_offs, v_offs, scale, bias_mode, has_sel, has_gate,
                epilogue, want_lse, seg_tiles, nback, lam_init):
    R = T // TK
    it = iter(refs)
    q_ref, k_ref, v_ref = next(it), next(it), next(it)
    bias_ref = next(it) if bias_mode else None
    sel_ref = next(it) if has_sel else None
    gate_ref = next(it) if has_gate else None
    lam_ref, dn_ref = (next(it), next(it)) if epilogue == "diff" else (None, None)
    o_ref = next(it)
    lse_ref = next(it) if want_lse else None
    vt_scr = next(it)

    qi = pl.program_id(2)
    seq = v_ref.shape[1]

    @pl.when(qi == 0)
    def _():
        for c in range(seq // TK):
            vt_scr[:, c * TK:(c + 1) * TK] = v_ref[0, c * TK:(c + 1) * TK, :].astype(F32).T.astype(BF16)

    qfull = q_ref[0].astype(F32)
    fold_scale = math.frexp(scale)[0] == 0.5
    qts = [(qfull[:, off:off + dq] * (scale if fold_scale else 1.0)).T.astype(BF16) for off in q_offs]

    def qk(kc):
        kfull = k_ref[0, pl.ds(pl.multiple_of(kc * TK, TK), TK), :]
        return tuple(jnp.dot(kfull[:, k_offs[u]:k_offs[u] + dq], qts[u], preferred_element_type=F32)
                     for u in range(2))

    def update(kc, state, scores, diag):
        start = pl.multiple_of(kc * TK, TK)
        new_state = []
        sel_add = None
        if has_sel:
            per = TK // SEL_BLOCK
            rows = [sel_ref[0, 0, pl.ds(kc * per + a, 1), :] for a in range(per)]
            sel_add = jnp.concatenate([jnp.broadcast_to((r - 1.0) * (-NEG), (SEL_BLOCK, T)) for r in rows], axis=0)
        for u in range(2):
            vt = vt_scr[v_offs[u]:v_offs[u] + dv, pl.ds(start, TK)]
            s = scores[u]
            if not fold_scale:
                s = s * scale
            if bias_mode:
                s = s + bias_ref[u if bias_mode == "pair" else 0, R * qi - kc + (R - 1)]
            elif diag is not None:
                key = lax.broadcasted_iota(jnp.int32, (TK, T), 0) + diag * TK
                qry = lax.broadcasted_iota(jnp.int32, (TK, T), 1)
                s = jnp.where(key <= qry, s, NEG)
            if has_sel:
                s = s + sel_add
            m_prev, l_prev, acc_prev = state[u]
            m_new = jnp.maximum(m_prev, jnp.max(s, axis=0, keepdims=True))
            alpha = jnp.exp(m_prev - m_new)
            p = jnp.exp(s - m_new)
            l_new = alpha * l_prev + jnp.sum(p, axis=0, keepdims=True)
            acc_new = alpha * acc_prev + jnp.dot(vt, p.astype(BF16), preferred_element_type=F32)
            new_state.append((m_new, l_new, acc_new))
        return tuple(new_state)

    init = tuple((jnp.full((1, T), NEG, F32), jnp.zeros((1, T), F32), jnp.zeros((dv, T), F32)) for _ in range(2))
    if nback is None:
        lo = 0
    elif seg_tiles is not None:
        lo = jnp.where(qi % seg_tiles == 0, qi, qi - nback)
    else:
        lo = jnp.maximum(qi - nback, 0)

    def step(kc, carry):
        state, scores = carry
        nxt = qk(kc + 1)
        return update(kc, state, scores, None), nxt

    state, scores = lax.fori_loop(lo, R * qi, step, (init, qk(lo)))
    for a in range(R):
        nxt = qk(R * qi + a + 1) if a + 1 < R else None
        state = update(R * qi + a, state, scores, a)
        scores = nxt

    outs = [acc / l for _, l, acc in state]
    if epilogue == "diff":
        a = outs[0] - lam_ref[0, 0] * outs[1]
        rinv = lax.rsqrt(jnp.mean(a * a, axis=0, keepdims=True) + RMS_EPS)
        o = (a * rinv * dn_ref[...] * (1.0 - lam_init)).T
    else:
        o = jnp.concatenate(outs, axis=0).T
        if has_gate:
            o = o * gate_ref[0]
    o_ref[0] = o.astype(o_ref.dtype)
    if want_lse:
        lse_ref[0, 0] = jnp.concatenate([m + jnp.log(l) for m, l, _ in state], axis=0)


def _flash(q, k, v, *, n_outer, T, dq, dv, q_col, k_col, v_col, q_offs, k_offs, v_offs, scale,
           out_cols, out_col, bias=None, bias_mode=None, bias_idx=None, sel=None, sel_idx=None,
           gate=None, gate_col=None, lam=None, dnorm=None, lam_init=0.0, epilogue="plain", want_lse=False,
           seg_tiles=None, nback=None, vmem=None, k_w=None, v_w=None, name="flash", TK=None,
           rep=1, rep_in=0, rep_out=0):
    bsz, seq, _ = q.shape
    TK = TK or T
    assert T % TK == 0 and (nback is None or T == TK)
    nq = seq // T
    qw = max(o + dq for o in q_offs)
    kw = k_w or max(o + dq for o in k_offs)
    vw = v_w or max(o + dv for o in v_offs)
    ow = dv if epilogue == "diff" else 2 * dv
    in_specs = [pl.BlockSpec((1, T, qw), lambda g, b, i: (b // rep, i, q_col(g) + (b % rep) * rep_in)),
                pl.BlockSpec((1, seq, kw), lambda g, b, i: (b // rep, 0, k_col(g) + (b % rep) * rep_in)),
                pl.BlockSpec((1, seq, vw), lambda g, b, i: (b // rep, 0, v_col(g) + (b % rep) * rep_in))]
    args = [q, k, v]
    if bias_mode:
        nb = 2 if bias_mode == "pair" else 1
        in_specs.append(pl.BlockSpec((nb,) + bias.shape[1:], lambda g, b, i: (bias_idx(g), 0, 0, 0)))
        args.append(bias)
    if sel is not None:
        in_specs.append(pl.BlockSpec((1, 1, sel.shape[2], T), lambda g, b, i: (b, sel_idx(g), 0, i)))
        args.append(sel)
    if gate is not None:
        in_specs.append(pl.BlockSpec((1, T, ow), lambda g, b, i: (b, i, gate_col(g))))
        args.append(gate)
    if epilogue == "diff":
        in_specs.append(pl.BlockSpec(memory_space=pltpu.SMEM))
        in_specs.append(pl.BlockSpec((dv, 1), lambda g, b, i: (0, 0)))
        args += [lam, dnorm]
    out_specs = [pl.BlockSpec((1, T, ow), lambda g, b, i: (b // rep, i, out_col(g) + (b % rep) * rep_out))]
    out_shape = [jax.ShapeDtypeStruct((bsz, seq, out_cols), BF16)]
    if want_lse:
        out_specs.append(pl.BlockSpec((1, 1, 2, T), lambda g, b, i: (b, g, 0, i)))
        out_shape.append(jax.ShapeDtypeStruct((bsz * rep, n_outer, 2, seq), F32))
    body = functools.partial(
        _flash_body, T=T, TK=TK, dq=dq, dv=dv, q_offs=q_offs, k_offs=k_offs, v_offs=v_offs, scale=scale,
        bias_mode=bias_mode, has_sel=sel is not None, has_gate=gate is not None, epilogue=epilogue,
        want_lse=want_lse, seg_tiles=seg_tiles, nback=nback, lam_init=lam_init)
    res = pl.pallas_call(
        body, grid=(n_outer, bsz * rep, nq), in_specs=in_specs, out_specs=out_specs, out_shape=out_shape,
        scratch_shapes=[pltpu.VMEM((vw, seq), BF16)],
        compiler_params=_cparams(3, vmem), name=name,
    )(*args)
    return res if want_lse else res[0]


def _rel_bucket(dist):
    n = jnp.maximum(dist, 0)
    exact = N_BUCKETS // 2
    log_ratio = jnp.log(jnp.maximum(n, 1).astype(F32) / exact) / math.log(REL_MAX_DIST / exact)
    large = exact + (log_ratio * (N_BUCKETS - exact)).astype(jnp.int32)
    return jnp.where(n < exact, n, jnp.minimum(large, N_BUCKETS - 1))


def _toeplitz_bias(tab, T, n_d, dist_scale, max_dist, TK=None):
    TK = TK or T
    R = T // TK
    wlen = T + TK
    x = jnp.arange(wlen)
    dist = (jnp.arange(n_d)[:, None] - (R - 1)) * TK + jnp.where(x < T, x, x - wlen)[None, :]
    w = tab[_rel_bucket(dist * dist_scale)].astype(F32)
    w = jnp.where(((dist >= 0) & (dist <= max_dist))[..., None], w, NEG)
    return _toeplitz(jnp.moveaxis(w, -1, 0), TK, T)


def _toeplitz(w, rows, cols):
    wlen = w.shape[-1]
    tiled = jnp.tile(w, (1,) * (w.ndim - 1) + (rows,))[..., :rows * (wlen - 1)]
    return tiled.reshape(w.shape[:-1] + (rows, wlen - 1))[..., :cols]


def _compress_body(u_ref, pe_ref, w1_ref, w2_ref, o_ref, *, ncp):
    outs = []
    for a in range(2):
        u = u_ref[0, 0, a].astype(F32)
        p1 = jnp.dot(u + pe_ref[a, 0], w1_ref[a, 0], preferred_element_type=F32, precision=HI)
        p2 = jnp.dot(u + pe_ref[a, 1], w1_ref[a, 1], preferred_element_type=F32, precision=HI)
        hid = jax.nn.gelu(p1 + pltpu.roll(p2, ncp - 1, 0))
        outs.append(jnp.dot(hid, w2_ref[a], preferred_element_type=F32, precision=HI))
    o_ref[0, 0] = jnp.concatenate(outs, axis=-1)


def _compress(u, pe, w1, w2):
    bsz, g, _, ncp, width = u.shape
    return pl.pallas_call(
        functools.partial(_compress_body, ncp=ncp), grid=(bsz, g),
        in_specs=[pl.BlockSpec((1, 1, 2, ncp, width), lambda b, gg: (b, gg, 0, 0, 0)),
                  pl.BlockSpec(pe.shape, lambda b, gg: (0, 0, 0, 0)),
                  pl.BlockSpec(w1.shape, lambda b, gg: (0, 0, 0, 0)),
                  pl.BlockSpec(w2.shape, lambda b, gg: (0, 0, 0))],
        out_specs=pl.BlockSpec((1, 1, ncp, 2 * DK_C), lambda b, gg: (b, gg, 0, 0)),
        out_shape=jax.ShapeDtypeStruct((bsz, g, ncp, 2 * DK_C), F32),
        compiler_params=_cparams(2), name="nsa_compress",
    )(u, pe, w1, w2)


def _cmp_attn_body(q_ref, kv_ref, bias_ref, ov_ref, gate_ref, o_ref, sel_ref, *, T, ncp, n_sel, n_top, scale):
    qi = pl.program_id(2)
    kc = kv_ref[0, 0, :, :DK_C]
    vc = kv_ref[0, 0, :, DK_C:]
    t = qi * T + lax.broadcasted_iota(jnp.int32, (T, ncp), 0)
    c = lax.broadcasted_iota(jnp.int32, (T, ncp), 1)
    valid = t >= c * CMP_STRIDE + (CMP_LEN - 1)
    validf = valid.astype(F32)
    psum = jnp.zeros((T, ncp), F32)
    outs = []
    for r in range(R_C):
        q = q_ref[0, :, r * DK_C:(r + 1) * DK_C].astype(F32)
        s = lax.dot_general(q, kc, _NT, preferred_element_type=F32, precision=HI) * scale + bias_ref[r]
        s = jnp.where(valid, s, NEG)
        e = jnp.exp(s - jnp.max(s, axis=-1, keepdims=True)) * validf
        p = e / jnp.maximum(jnp.sum(e, axis=-1, keepdims=True), 1e-30)
        outs.append(jnp.dot(p, vc, preferred_element_type=F32, precision=HI))
        psum = psum + p
    o_ref[0] = (jnp.concatenate(outs, axis=-1) * gate_ref[0]).astype(o_ref.dtype)

    imp = jnp.dot(ov_ref[...], psum.T, preferred_element_type=F32, precision=HI)
    tq = qi * T + lax.broadcasted_iota(jnp.int32, (n_sel, T), 1)
    j = lax.broadcasted_iota(jnp.int32, (n_sel, T), 0)
    forced = (j == tq // SEL_BLOCK) | (j == 0)
    work = jnp.where(forced, BIG, jnp.where(j * SEL_BLOCK <= tq, imp, -BIG))
    sel = jnp.zeros((n_sel, T), F32)
    jf = j.astype(F32)
    for _ in range(n_top):
        _, _, pick = _first_max(work, jf, n_sel)
        sel = jnp.where(pick, 1.0, sel)
        work = jnp.where(pick, -jnp.inf, work)
    sel_ref[0, 0] = sel


def _cmp_attn(h, kvc, bias_c, overlap, gates, T=256):
    bsz, seq, _ = h.shape
    ncp = kvc.shape[2]
    n_sel = seq // SEL_BLOCK
    n_top = min(SEL_TOP, n_sel)
    qw = R_C * DK_C
    return pl.pallas_call(
        functools.partial(_cmp_attn_body, T=T, ncp=ncp, n_sel=n_sel, n_top=n_top, scale=DK_C ** -0.5),
        grid=(G_C, bsz, seq // T),
        in_specs=[pl.BlockSpec((1, T, qw), lambda g, b, i: (b, i, g)),
                  pl.BlockSpec((1, 1, ncp, 2 * DK_C), lambda g, b, i: (b, g, 0, 0)),
                  pl.BlockSpec((R_C, T, ncp), lambda g, b, i: (g, i, 0)),
                  pl.BlockSpec(overlap.shape, lambda g, b, i: (0, 0)),
                  pl.BlockSpec((1, T, qw), lambda g, b, i: (b, i, g))],
        out_specs=[pl.BlockSpec((1, T, qw), lambda g, b, i: (b, i, g)),
                   pl.BlockSpec((1, 1, n_sel, T), lambda g, b, i: (b, g, 0, i))],
        out_shape=[jax.ShapeDtypeStruct((bsz, seq, H_C * DK_C), BF16),
                   jax.ShapeDtypeStruct((bsz, G_C, n_sel, seq), F32)],
        compiler_params=_cparams(3), name="nsa_cmp_attn",
    )(h, kvc, bias_c, overlap, gates)


def _first_max(work, idx, n):
    mx = jnp.max(work, axis=0, keepdims=True)
    first = jnp.min(jnp.where(work == mx, idx, float(n)), axis=0, keepdims=True)
    return mx, first, idx == first


def _router_body(x_ref, wt_ref, b_ref, tri_ref, e_ref, g_ref, r_ref, cnt_ref, carry_scr, *, tm):
    i = pl.program_id(0)

    @pl.when(i == 0)
    def _():
        carry_scr[...] = jnp.zeros(carry_scr.shape, F32)

    st = lax.dot_general(wt_ref[...], x_ref[...], _NT, preferred_element_type=F32, precision=HI)
    scores = jax.nn.sigmoid(st)
    sel = scores + b_ref[...]
    per = N_EXPERTS // N_EXPERT_GROUPS
    fiota = lambda rows: lax.broadcasted_iota(jnp.int32, (rows, tm), 0).astype(F32)
    i_per, i_grp, i_exp = fiota(per), fiota(N_EXPERT_GROUPS), fiota(N_EXPERTS)
    grp_scores = []
    for g in range(N_EXPERT_GROUPS):
        blk = sel[g * per:(g + 1) * per]
        m1, _, pick = _first_max(blk, i_per, per)
        grp_scores.append(m1 + jnp.max(jnp.where(pick, -jnp.inf, blk), axis=0, keepdims=True))
    work = jnp.concatenate(grp_scores, axis=0)
    gmask = jnp.zeros((N_EXPERT_GROUPS, tm), F32)
    for _ in range(TOPK_GROUPS):
        _, _, pick = _first_max(work, i_grp, N_EXPERT_GROUPS)
        gmask = jnp.where(pick, 1.0, gmask)
        work = jnp.where(pick, -jnp.inf, work)
    work = jnp.concatenate([jnp.where(gmask[g:g + 1] > 0.5, sel[g * per:(g + 1) * per], NEG)
                            for g in range(N_EXPERT_GROUPS)], axis=0)
    picks, firsts, vals = [], [], []
    for _ in range(TOP_K):
        _, first, pick = _first_max(work, i_exp, N_EXPERTS)
        picks.append(pick)
        firsts.append(first)
        vals.append(jnp.sum(jnp.where(pick, scores, 0.0), axis=0, keepdims=True))
        work = jnp.where(pick, -jnp.inf, work)
    val = jnp.concatenate(vals, axis=0)
    g_ref[...] = val / jnp.sum(val, axis=0, keepdims=True) * ROUTED_SCALE
    e_ref[...] = jnp.concatenate(firsts, axis=0).astype(jnp.int32)
    onehot = picks[0].astype(F32)
    for pick in picks[1:]:
        onehot = onehot + pick.astype(F32)
    before = jnp.dot(onehot.astype(BF16), tri_ref[...], preferred_element_type=F32) + carry_scr[...]
    r_ref[...] = jnp.concatenate([jnp.sum(jnp.where(pick, before, 0.0), axis=0, keepdims=True)
                                  for pick in picks], axis=0).astype(jnp.int32)
    carry = carry_scr[...] + jnp.sum(onehot, axis=1, keepdims=True)
    carry_scr[...] = carry
    cnt_ref[...] = jnp.broadcast_to(carry, cnt_ref.shape)


def _router(xf, router_w, router_b, tm=256):
    n, d = xf.shape
    tri = (jnp.arange(tm)[:, None] < jnp.arange(tm)[None, :]).astype(BF16)
    col = lambda i: (0, i)
    fixed = lambda i: (0, 0)
    return pl.pallas_call(
        functools.partial(_router_body, tm=tm), grid=(n // tm,),
        in_specs=[pl.BlockSpec((tm, d), lambda i: (i, 0)), pl.BlockSpec((N_EXPERTS, d), fixed),
                  pl.BlockSpec((N_EXPERTS, 1), fixed), pl.BlockSpec((tm, tm), fixed)],
        out_specs=[pl.BlockSpec((TOP_K, tm), col), pl.BlockSpec((TOP_K, tm), col),
                   pl.BlockSpec((TOP_K, tm), col), pl.BlockSpec((N_EXPERTS, LANE), fixed)],
        out_shape=[jax.ShapeDtypeStruct((TOP_K, n), jnp.int32), jax.ShapeDtypeStruct((TOP_K, n), F32),
                   jax.ShapeDtypeStruct((TOP_K, n), jnp.int32), jax.ShapeDtypeStruct((N_EXPERTS, LANE), F32)],
        scratch_shapes=[pltpu.VMEM((N_EXPERTS, 1), F32)],
        compiler_params=_cparams(1), name="router",
    )(xf, router_w.T, router_b.astype(F32)[:, None], tri)


def _pack_rows(x):
    w = x.shape[-1] // 2
    lo = lax.bitcast_convert_type(x[:, :w].astype(BF16).astype(F32), jnp.uint32)
    hi = lax.bitcast_convert_type(x[:, w:].astype(BF16).astype(F32), jnp.uint32)
    return (lo >> 16) | (hi & jnp.uint32(0xFFFF0000))


def _unpack_rows(p):
    lo = lax.bitcast_convert_type(p << 16, F32)
    hi = lax.bitcast_convert_type(p & jnp.uint32(0xFFFF0000), F32)
    return lo, hi


def _dispatch_body(fill_ref, pos_ref, x_ref, z_ref, xs_hbm, xp_scr, sem, *, tm, n_blocks):
    i = pl.program_id(0)
    slot = i % 2
    xp_scr[slot] = _pack_rows(x_ref[...]).reshape(tm // SUBLANES, SUBLANES, -1)
    zsem = sem.at[2]

    def row_copy(s, grp, j, dst):
        return pltpu.make_async_copy(xp_scr.at[s, grp, pl.ds(j, 1)], xs_hbm.at[pl.ds(dst, 1)], sem.at[s])

    def zero_copy(dst):
        return pltpu.make_async_copy(z_ref.at[pl.ds(0, 1)], xs_hbm.at[pl.ds(dst, 1)], zsem)

    def zero_block(blk):
        return pltpu.make_async_copy(z_ref, xs_hbm.at[pl.ds(blk * MOE_BM, MOE_BM)], zsem)

    def issue(grp, c):
        for j in range(SUBLANES):
            for k in range(TOP_K):
                row_copy(slot, grp, j, pos_ref[k, grp * SUBLANES + j]).start(priority=k % 2)
        return c

    def drain(s):
        def body(grp, c):
            for _ in range(SUBLANES * TOP_K):
                row_copy(s, 0, 0, 0).wait()
            return c

        lax.fori_loop(0, tm // SUBLANES, body, 0)

    lax.fori_loop(0, tm // SUBLANES, issue, 0)

    @pl.when(i > 0)
    def _():
        drain(1 - slot)

    @pl.when(i == pl.num_programs(0) - 1)
    def _():
        drain(slot)

    @pl.when(i == 0)
    def _():
        def fill(e, c):
            lo, hi = fill_ref[2 * e], fill_ref[2 * e + 1]

            def start(r, cc):
                zero_copy(r).start()
                return cc

            def wait(r, cc):
                zero_copy(0).wait()
                return cc

            lax.fori_loop(lo, hi, start, 0)
            lax.fori_loop(lo, hi, wait, 0)
            return c

        lax.fori_loop(0, N_EXPERTS, fill, 0)

        def tail_start(blk, c):
            zero_block(blk).start()
            return c

        def tail_wait(blk, c):
            zero_block(0).wait()
            return c

        lax.fori_loop(fill_ref[2 * N_EXPERTS], n_blocks, tail_start, 0)
        lax.fori_loop(fill_ref[2 * N_EXPERTS], n_blocks, tail_wait, 0)


def _dispatch(xb, pos, fill, p, tm=128):
    n, d = xb.shape
    grid_spec = pltpu.PrefetchScalarGridSpec(
        num_scalar_prefetch=1, grid=(n // tm,),
        in_specs=[pl.BlockSpec((TOP_K, tm), lambda i, fl: (0, i), memory_space=pltpu.SMEM),
                  pl.BlockSpec((tm, d), lambda i, fl: (i, 0)),
                  pl.BlockSpec((MOE_BM, d // 2), lambda i, fl: (0, 0))],
        out_specs=pl.BlockSpec(memory_space=pl.ANY),
        scratch_shapes=[pltpu.VMEM((2, tm // SUBLANES, SUBLANES, d // 2), jnp.uint32),
                        pltpu.SemaphoreType.DMA((3,))])
    return pl.pallas_call(
        functools.partial(_dispatch_body, tm=tm, n_blocks=p // MOE_BM), grid_spec=grid_spec,
        out_shape=jax.ShapeDtypeStruct((p, d // 2), jnp.uint32),
        compiler_params=_cparams(1), name="dispatch",
    )(fill, pos, xb, jnp.zeros((MOE_BM, d // 2), jnp.uint32))


def _moe_ffn_body(be_ref, nb_ref, x_ref, wg_ref, wu_ref, wd_ref, o_ref, wg_scr, wu_scr, wd_scr):
    i = pl.program_id(0)
    half = x_ref.shape[1]

    @pl.when((i == 0) | (be_ref[i] != be_ref[jnp.maximum(i - 1, 0)]))
    def _():
        wg_scr[...] = wg_ref[0].astype(BF16)
        wu_scr[...] = wu_ref[0].astype(BF16)
        wd_scr[...] = wd_ref[0].astype(BF16)

    @pl.when(i < nb_ref[0])
    def _():
        lo, hi = _unpack_rows(x_ref[...])
        lo, hi = lo.astype(BF16), hi.astype(BF16)
        hg = (jnp.dot(lo, wg_scr[:half], preferred_element_type=F32)
              + jnp.dot(hi, wg_scr[half:], preferred_element_type=F32))
        hu = (jnp.dot(lo, wu_scr[:half], preferred_element_type=F32)
              + jnp.dot(hi, wu_scr[half:], preferred_element_type=F32))
        hb = (hg * jax.nn.sigmoid(hg) * hu).astype(BF16)
        o_ref[...] = _pack_rows(jnp.dot(hb, wd_scr[...], preferred_element_type=F32))

    @pl.when(i >= nb_ref[0])
    def _():
        o_ref[...] = jnp.zeros(o_ref.shape, o_ref.dtype)


def _moe_ffn(xs, blk_e, n_used, wg, wu, wd):
    p, half = xs.shape
    d = 2 * half
    n_blocks = p // MOE_BM
    grid_spec = pltpu.PrefetchScalarGridSpec(
        num_scalar_prefetch=2, grid=(n_blocks,),
        in_specs=[pl.BlockSpec((MOE_BM, half), lambda i, be, nb: (i, 0)),
                  pl.BlockSpec((1, d, D_EXPERT), lambda i, be, nb: (be[i], 0, 0)),
                  pl.BlockSpec((1, d, D_EXPERT), lambda i, be, nb: (be[i], 0, 0)),
                  pl.BlockSpec((1, D_EXPERT, d), lambda i, be, nb: (be[i], 0, 0))],
        out_specs=pl.BlockSpec((MOE_BM, half), lambda i, be, nb: (i, 0)),
        scratch_shapes=[pltpu.VMEM((d, D_EXPERT), BF16), pltpu.VMEM((d, D_EXPERT), BF16),
                        pltpu.VMEM((D_EXPERT, d), BF16)])
    return pl.pallas_call(
        _moe_ffn_body, grid_spec=grid_spec, out_shape=jax.ShapeDtypeStruct((p, half), jnp.uint32),
        compiler_params=_cparams(1), name="expert_ffn",
    )(blk_e, n_used, xs, wg, wu, wd)


def _combine_body(pos_ref, posn_ref, gate_ref, xb_ref, xf_ref, y_hbm, sg_ref, su_ref, sd_ref,
                  g_ref, b_ref, of_ref, ob_ref, buf, sem, *, tm):
    i = pl.program_id(0)
    slot = i % 2

    def row_copy(s, k, grp, j, src):
        return pltpu.make_async_copy(y_hbm.at[pl.ds(src, 1)], buf.at[s, k, grp, pl.ds(j, 1)], sem.at[s])

    def issue_tile(pr, s):
        def issue(grp, c):
            for j in range(SUBLANES):
                for k in range(TOP_K):
                    row_copy(s, k, grp, j, pr[k, grp * SUBLANES + j]).start(priority=k % 2)
            return c

        lax.fori_loop(0, tm // SUBLANES, issue, 0)

    @pl.when(i == 0)
    def _():
        issue_tile(pos_ref, 0)

    @pl.when(i + 1 < pl.num_programs(0))
    def _():
        issue_tile(posn_ref, 1 - slot)

    x = xb_ref[...]
    hg = jnp.dot(x, sg_ref[...], preferred_element_type=F32)
    hu = jnp.dot(x, su_ref[...], preferred_element_type=F32)
    hb = (hg * jax.nn.sigmoid(hg) * hu).astype(BF16)
    y = jnp.dot(hb, sd_ref[...], preferred_element_type=F32)

    def drain(grp, c):
        for _ in range(SUBLANES * TOP_K):
            row_copy(slot, 0, 0, 0, 0).wait()
        return c

    lax.fori_loop(0, tm // SUBLANES, drain, 0)
    gate = gate_ref[...]
    half = buf.shape[-1]
    y_lo, y_hi = y[:, :half], y[:, half:]
    for k in range(TOP_K):
        lo, hi = _unpack_rows(buf[slot, k].reshape(tm, half))
        y_lo = y_lo + gate[:, k:k + 1] * lo
        y_hi = y_hi + gate[:, k:k + 1] * hi
    out = _layer_norm(ALPHA * xf_ref[...] + jnp.concatenate([y_lo, y_hi], axis=-1), g_ref[...], b_ref[...])
    of_ref[...] = out
    ob_ref[...] = out.astype(BF16)


def _combine(pos, gate, xb, xf, y, sg, su, sd, g, b, tm=128):
    n, d = xf.shape
    n_tiles = n // tm
    row = lambda i: (i, 0)
    fixed = lambda i: (0, 0)
    smem_col = pl.BlockSpec((TOP_K, tm), lambda i: (0, i), memory_space=pltpu.SMEM)
    smem_next = pl.BlockSpec((TOP_K, tm), lambda i: (0, jnp.minimum(i + 1, n_tiles - 1)),
                             memory_space=pltpu.SMEM)
    grid_spec = pltpu.PrefetchScalarGridSpec(
        num_scalar_prefetch=0, grid=(n_tiles,),
        in_specs=[smem_col, smem_next, pl.BlockSpec((tm, TOP_K), row),
                  pl.BlockSpec((tm, d), row), pl.BlockSpec((tm, d), row), pl.BlockSpec(memory_space=pl.ANY),
                  pl.BlockSpec(sg.shape, fixed), pl.BlockSpec(su.shape, fixed), pl.BlockSpec(sd.shape, fixed),
                  pl.BlockSpec((1, d), fixed), pl.BlockSpec((1, d), fixed)],
        out_specs=[pl.BlockSpec((tm, d), row), pl.BlockSpec((tm, d), row)],
        scratch_shapes=[pltpu.VMEM((2, TOP_K, tm // SUBLANES, SUBLANES, d // 2), jnp.uint32),
                        pltpu.SemaphoreType.DMA((2,))])
    return pl.pallas_call(
        functools.partial(_combine_body, tm=tm), grid_spec=grid_spec,
        out_shape=[jax.ShapeDtypeStruct((n, d), F32), jax.ShapeDtypeStruct((n, d), BF16)],
        compiler_params=_cparams(1), name="combine",
    )(pos, pos, gate, xb, xf, y, sg, su, sd, g, b)


def _moe(xf, xb, router_w, router_b, w_gate, w_up, w_down, sh_gate, sh_up, sh_down, ln_g, ln_b):
    n, d = xf.shape
    e_idx, gate, rank, cnt = _router(xf, router_w, router_b)
    counts = cnt[:, 0].astype(jnp.int32)
    padded = (counts + MOE_BM - 1) // MOE_BM * MOE_BM
    pad_end = jnp.cumsum(padded)
    pad_start = (pad_end - padded).astype(jnp.int32)
    n_blocks = (n * TOP_K + N_EXPERTS * (MOE_BM - 1) + MOE_BM - 1) // MOE_BM
    blk_e = jnp.minimum(jnp.sum(pad_end[None, :] <= (jnp.arange(n_blocks) * MOE_BM)[:, None], axis=1),
                        N_EXPERTS - 1).astype(jnp.int32)
    n_used = (pad_end[-1] // MOE_BM).astype(jnp.int32).reshape(1)
    fill = jnp.concatenate([jnp.stack([pad_start + counts, pad_end], axis=1).reshape(-1), n_used]).astype(jnp.int32)
    experts = jnp.arange(N_EXPERTS, dtype=jnp.int32)[:, None, None]
    pos = rank + jnp.sum(jnp.where(e_idx[None] == experts, pad_start[:, None, None], 0), axis=0)
    xs = _dispatch(xb, pos, fill, n_blocks * MOE_BM)
    y = _moe_ffn(xs, blk_e, n_used, w_gate, w_up, w_down)
    return _combine(pos, gate.T, xb, xf, y, sh_gate, sh_up, sh_down, ln_g[None], ln_b[None])


def _rope_tables(seq):
    half = ROPE_DIM // 2
    freqs = ROPE_THETA ** (-jnp.arange(half, dtype=F32) / half)
    ang = jnp.arange(seq).astype(F32)[:, None] * freqs
    cos = jnp.concatenate([jnp.cos(ang)] * 2, -1)
    sin = jnp.concatenate([jnp.sin(ang)] * 2, -1)
    z = lambda w: jnp.zeros((seq, w), F32)
    pad = LANE - NOPE - ROPE_DIM
    cos_q = jnp.concatenate([jnp.ones((seq, NOPE), F32), cos, z(pad)], -1)
    sin_q = jnp.concatenate([z(NOPE), sin, z(pad)], -1)
    cos_k = jnp.concatenate([z(NOPE), cos, z(pad)], -1)
    return cos_q, sin_q, cos_k, sin_q


def _rot_cols(w):
    half = w.shape[-1] // 2
    return jnp.concatenate([-w[..., half:], w[..., :half]], -1)


def _mixer_ab(xb, xf, bsz, seq, w_in, q_norm, w_uq, kv_norm, w_ukv, w_out, ln_g, ln_b, rope_tabs, dil_bias):
    n = bsz * seq
    d = w_in.shape[0]
    c0 = Q_LORA + KV_LORA
    w_kr = w_in[:, c0:c0 + ROPE_DIM]
    zc = lambda w: jnp.zeros((d, w), F32)
    pad = LANE - NOPE - ROPE_DIM
    w1 = jnp.concatenate([w_in[:, :c0], zc(NOPE), w_kr, zc(pad), zc(NOPE), _rot_cols(w_kr), zc(pad)], 1)
    h1 = _mm(xb, w1.astype(BF16), F32, tn=w1.shape[1])
    h2 = _mm(xb, w_in[:, c0 + ROPE_DIM:].astype(BF16), BF16, tn=3 * HD_B * H_B_GROUP)

    wq = w_uq.reshape(Q_LORA, H_A, NOPE + ROPE_DIM)
    zq = jnp.zeros((Q_LORA, H_A, pad), F32)
    wq_main = jnp.concatenate([wq, zq], -1).reshape(Q_LORA, H_A * LANE)
    wq_rot = jnp.concatenate([jnp.zeros((Q_LORA, H_A, NOPE), F32), _rot_cols(wq[..., NOPE:]), zq], -1)
    wq_rot = wq_rot.reshape(Q_LORA, H_A * LANE)
    wkv = w_ukv.reshape(KV_LORA, H_A, NOPE + MLA_V)
    wk = jnp.concatenate([wkv[..., :NOPE], jnp.zeros((KV_LORA, H_A, LANE - NOPE), F32)], -1)
    wk = wk.reshape(KV_LORA, H_A * LANE)
    wv = wkv[..., NOPE:].reshape(KV_LORA, H_A * MLA_V)
    q_a, k_a, v_a = _mla_up(h1, q_norm[None], wq_main.astype(BF16), wq_rot.astype(BF16), kv_norm[None],
                            wk.astype(BF16), wv.astype(BF16), *rope_tabs, seq)
    o_a = _flash(q_a.reshape(bsz, seq, -1), k_a.reshape(bsz, seq, -1), v_a.reshape(bsz, seq, -1), name="mla_attn",
                 n_outer=H_A // 2, T=min(ATT_TQ, seq), TK=ATT_TK, dq=LANE, dv=MLA_V,
                 q_col=lambda g: g, k_col=lambda g: g, v_col=lambda g: g,
                 q_offs=(0, LANE), k_offs=(0, LANE), v_offs=(0, MLA_V),
                 scale=(NOPE + ROPE_DIM) ** -0.5, out_cols=H_A * MLA_V, out_col=lambda g: g)

    gw = H_B_GROUP * HD_B
    hw = h2.shape[1]
    nqb = H_B * HD_B // LANE
    outs, lses = [], []
    for gi, (window, dil) in enumerate(DIL_PAIRS):
        L = seq // dil
        t = h2.reshape(bsz, L, dil * hw)
        c0 = gi * gw // LANE
        o, lse = _flash(t, t, t, name="dilated_attn", n_outer=2, T=min(256, L), dq=HD_B, dv=HD_B,
                        q_col=lambda g, c0=c0: c0 + g, k_col=lambda g, c0=c0: nqb + c0 + g,
                        v_col=lambda g, c0=c0: 2 * nqb + c0 + g,
                        q_offs=(0, HD_B), k_offs=(0, HD_B), v_offs=(0, HD_B), scale=HD_B ** -0.5,
                        out_cols=dil * gw, out_col=lambda g: g, bias=dil_bias[gi], bias_mode="pair",
                        bias_idx=lambda g: g, want_lse=True, nback=1,
                        rep=dil, rep_in=hw // LANE, rep_out=gw // LANE)
        outs.append(o.reshape(bsz, seq, H_B_GROUP, HD_B).astype(F32))
        lses.append(lse.reshape(bsz, dil, H_B_GROUP, L).transpose(0, 3, 1, 2).reshape(bsz, seq, H_B_GROUP))
    w = jax.nn.softmax(jnp.stack(lses), axis=0)
    o_b = jnp.sum(w[..., None] * jnp.stack(outs), axis=0).astype(BF16).reshape(n, gw)
    na = H_A * MLA_V
    return _out_ln([o_a.reshape(n, na)], o_b, w_out[:na].astype(BF16), w_out[na:].astype(BF16),
                   xf, ln_g[None], ln_b[None])


def _mixer_cd(xb, xf, bsz, seq, w_in, pos_k, k_w1, k_w2, pos_v, v_w1, v_w2, lq1, lk1, lq2, lk2, d_norm,
              w_out, ln_g, ln_b, lam_init, tabs):
    n = bsz * seq
    qc_w = H_C * DK_C
    kv_w = G_C * DK_C
    off = qc_w
    kvs = []
    for _ in range(3):
        wk_ = w_in[:, off:off + kv_w].reshape(-1, G_C, DK_C)
        wv_ = w_in[:, off + kv_w:off + 2 * kv_w].reshape(-1, G_C, DK_C)
        kvs.append(jnp.concatenate([wk_, wv_], -1).reshape(-1, 2 * kv_w))
        off += 2 * kv_w
    g_off = off
    d_off = off + 3 * H_C
    w_main = jnp.concatenate([w_in[:, :qc_w]] + kvs + [w_in[:, d_off:]], 1)
    h = _mm(xb, w_main.astype(BF16), BF16, tn=w_main.shape[1] // 2).reshape(bsz, seq, -1)
    w_g = jnp.repeat(w_in[:, g_off:d_off], DK_C, axis=1)
    gates = _mm(xb, w_g.astype(BF16), F32, tn=w_g.shape[1] // 2, act="sigmoid").reshape(bsz, seq, -1)

    ncp = seq // CMP_STRIDE
    half = CMP_STRIDE * DK_C
    kv_cmp = h[:, :, qc_w:qc_w + 2 * kv_w].reshape(bsz, ncp, CMP_STRIDE, G_C, 2, DK_C)
    u = kv_cmp.transpose(0, 3, 4, 1, 2, 5).reshape(bsz, G_C, 2, ncp, half)
    pe = jnp.stack([pos_k.reshape(2, 1, half), pos_v.reshape(2, 1, half)])
    w1 = jnp.stack([k_w1.reshape(2, half, CMP_HID), v_w1.reshape(2, half, CMP_HID)])
    w2 = jnp.stack([k_w2, v_w2])
    kvc = _compress(u, pe, w1, w2)
    o_cmp, sel = _cmp_attn(h, kvc, tabs["bias_c"], tabs["overlap"], gates)

    cb = qc_w // LANE
    scale = DK_C ** -0.5
    n_pairs = H_C // 2
    nsa = dict(n_outer=n_pairs, dq=DK_C, dv=DK_C, q_col=lambda g: g, q_offs=(0, DK_C), k_offs=(0, 0),
               v_offs=(DK_C, DK_C), scale=scale, out_cols=qc_w, out_col=lambda g: g, bias_mode="pair",
               bias_idx=lambda g: g, gate=gates, vmem=VMEM_LIMIT, k_w=LANE, v_w=LANE)
    o_sel = _flash(h, h, h, name="nsa_sel_attn", T=ATT_TQ, TK=ATT_TK, k_col=lambda g: cb + 2 + g // 2, v_col=lambda g: cb + 2 + g // 2,
                   bias=tabs["bias_sel"], sel=sel, sel_idx=lambda g: g // 2,
                   gate_col=lambda g: n_pairs + g, **nsa)
    o_win = _flash(h, h, h, name="nsa_win_attn", T=WIN, k_col=lambda g: cb + 4 + g // 2, v_col=lambda g: cb + 4 + g // 2,
                   bias=tabs["bias_win"], gate_col=lambda g: 2 * n_pairs + g, nback=1, **nsa)

    lam = (jnp.exp(jnp.sum(lq1.astype(F32) * lk1.astype(F32)))
           - jnp.exp(jnp.sum(lq2.astype(F32) * lk2.astype(F32))) + lam_init).reshape(1, 1)
    db = cb + 6
    o_d = _flash(h, h, h, name="diff_attn", n_outer=H_D, T=ATT_TQ, TK=ATT_TK, dq=DD, dv=2 * DD,
                 q_col=lambda g: db + g, k_col=lambda g: db + H_D + g, v_col=lambda g: db + 2 * H_D + g,
                 q_offs=(0, DD), k_offs=(0, DD), v_offs=(0, 0), scale=DD ** -0.5,
                 out_cols=H_D * 2 * DD, out_col=lambda g: g, bias=tabs["bias_d"], bias_mode="shared",
                 bias_idx=lambda g: g, lam=lam, dnorm=d_norm[:, None], lam_init=lam_init, epilogue="diff",
                 vmem=VMEM_LIMIT)
    r2 = lambda a: a.reshape(n, -1)
    return _out_ln([r2(o_cmp), r2(o_sel), r2(o_win)], r2(o_d), w_out[:qc_w].astype(BF16),
                   w_out[qc_w:].astype(BF16), xf, ln_g[None], ln_b[None])


def _nsa_tables(rel_bias, seq):
    tab_c = rel_bias[:, H_B:H_B + H_C]
    tab_d = rel_bias[:, H_B + H_C:H_B + H_C + H_D]
    ncp = seq // CMP_STRIDE
    n_sel = seq // SEL_BLOCK
    pos = jnp.arange(seq)
    x = jnp.arange(2 * ncp)
    c_minus_a = jnp.where(x < ncp, x, x - 2 * ncp)
    dist = -CMP_STRIDE * c_minus_a[None, :] + jnp.arange(CMP_STRIDE)[:, None] - (CMP_LEN - 1)
    w = jnp.moveaxis(tab_c[_rel_bucket(dist)].astype(F32), -1, 0)
    bias_c = _toeplitz(w, ncp, ncp).transpose(0, 2, 1, 3).reshape(H_C, seq, ncp)
    c0 = jnp.arange(ncp) * CMP_STRIDE
    s0 = jnp.arange(n_sel) * SEL_BLOCK
    overlap = jnp.maximum(jnp.minimum(c0[:, None] + CMP_LEN, s0[None, :] + SEL_BLOCK)
                          - jnp.maximum(c0[:, None], s0[None, :]), 0).astype(F32) / CMP_LEN
    return {
        "bias_c": bias_c, "overlap": overlap.T,
        "bias_sel": _toeplitz_bias(tab_c, ATT_TQ, seq // ATT_TK, 1, seq, ATT_TK),
        "bias_win": _toeplitz_bias(tab_c, WIN, 2, 1, WIN - 1),
        "bias_d": _toeplitz_bias(tab_d, ATT_TQ, seq // ATT_TK, 1, seq, ATT_TK),
    }


def kernel(x, rel_bias, ab_w_in, mla_q_norm, mla_w_uq, mla_kv_norm, mla_w_ukv, ab_w_out, cd_w_in, nsa_cmp_pos_k, nsa_cmp_k_w1, nsa_cmp_k_w2, nsa_cmp_pos_v, nsa_cmp_v_w1, nsa_cmp_v_w2, diff_lambda_q1, diff_lambda_k1, diff_lambda_q2, diff_lambda_k2, diff_norm, cd_w_out, ln1_g, ln1_b, ln2_g, ln2_b, router_w, router_b, exp_w_gate, exp_w_up, exp_w_down, sh_w_gate, sh_w_up, sh_w_down):
    bsz, seq, d = x.shape
    n = bsz * seq
    depth = ln1_g.shape[0]
    rope_tabs = _rope_tables(seq)
    dil_bias = [_toeplitz_bias(rel_bias[:, gi * H_B_GROUP:(gi + 1) * H_B_GROUP], min(256, seq // dil), 2, dil,
                               window // dil) for gi, (window, dil) in enumerate(DIL_PAIRS)]
    nsa_tabs = _nsa_tables(rel_bias, seq)
    xf = x.reshape(n, d)
    xb = xf.astype(BF16)
    for l in range(depth):
        i = l // 2
        if l % 2 == 0:
            xf, xb = _mixer_ab(xb, xf, bsz, seq, ab_w_in[i], mla_q_norm[i], mla_w_uq[i], mla_kv_norm[i],
                               mla_w_ukv[i], ab_w_out[i], ln1_g[l], ln1_b[l], rope_tabs, dil_bias)
        else:
            lam_init = 0.8 - 0.6 * math.exp(-0.3 * l)
            xf, xb = _mixer_cd(xb, xf, bsz, seq, cd_w_in[i], nsa_cmp_pos_k[i], nsa_cmp_k_w1[i],
                               nsa_cmp_k_w2[i], nsa_cmp_pos_v[i], nsa_cmp_v_w1[i], nsa_cmp_v_w2[i],
                               diff_lambda_q1[i], diff_lambda_k1[i], diff_lambda_q2[i], diff_lambda_k2[i],
                               diff_norm[i], cd_w_out[i], ln1_g[l], ln1_b[l], lam_init, nsa_tabs)
        xf, xb = _moe(xf, xb, router_w[l], router_b[l], exp_w_gate[l], exp_w_up[l],
                      exp_w_down[l], sh_w_gate[l].astype(BF16), sh_w_up[l].astype(BF16),
                      sh_w_down[l].astype(BF16), ln2_g[l], ln2_b[l])
    return xf.reshape(bsz, seq, d)
```

```python
import functools
import math

import jax
import jax.numpy as jnp
from jax import lax
from jax.experimental import pallas as pl
from jax.experimental.pallas import tpu as pltpu

F32 = jnp.float32
BF16 = jnp.bfloat16
HI = lax.Precision.HIGHEST

DEPTH = 4
NEG = -1e30
BIG = 1e9
LN_EPS = 1e-5
RMS_EPS = 1e-6
ALPHA = (2 * DEPTH) ** 0.25

N_BUCKETS = 32
REL_MAX_DIST = 2048

H_A = 12
NOPE = 64
ROPE_DIM = 32
MLA_V = 64
Q_LORA = 256
KV_LORA = 128
ROPE_THETA = 10000.0

DIL_PAIRS = ((128, 1), (512, 4), (2048, 16))
H_B_GROUP = 4
H_B = 12
HD_B = 64

H_C = 8
G_C = 2
R_C = 4
DK_C = 64
CMP_LEN = 32
CMP_STRIDE = 16
CMP_HID = 64
SEL_BLOCK = 64
SEL_TOP = 16
WIN = 512

H_D = 4
DD = 64

N_EXPERTS = 64
TOP_K = 8
N_EXPERT_GROUPS = 8
TOPK_GROUPS = 4
D_EXPERT = 256
ROUTED_SCALE = 2.5

LANE = 128
SUBLANES = 8
MOE_BM = 512
ATT_TQ = 512
ATT_TK = 256
VMEM_LIMIT = 56 * 1024 * 1024

_NT = (((1,), (1,)), ((), ()))


def _cparams(n_axes, vmem=None):
    return pltpu.CompilerParams(dimension_semantics=("arbitrary",) * n_axes, vmem_limit_bytes=vmem)


def _mm_body(x_ref, w_ref, o_ref, *, act, precision):
    y = jnp.dot(x_ref[...], w_ref[...], preferred_element_type=F32, precision=precision)
    if act == "sigmoid":
        y = jax.nn.sigmoid(y)
    o_ref[...] = y.astype(o_ref.dtype)


def _mm(x, w, out_dtype, tn, tm=1024, act=None, precision=None):
    m, k = x.shape
    nc = w.shape[1]
    tm = min(tm, m)
    return pl.pallas_call(
        functools.partial(_mm_body, act=act, precision=precision),
        grid=(m // tm, nc // tn),
        in_specs=[pl.BlockSpec((tm, k), lambda i, j: (i, 0)),
                  pl.BlockSpec((k, tn), lambda i, j: (0, j))],
        out_specs=pl.BlockSpec((tm, tn), lambda i, j: (i, j)),
        out_shape=jax.ShapeDtypeStruct((m, nc), out_dtype),
        compiler_params=_cparams(2, VMEM_LIMIT), name="proj",
    )(x, w)


def _proj_dilated_body(x_ref, w_ref, *rest, tm, gw3):
    o_refs, scr = rest[:-1], rest[-1]
    for gi, (_, dil) in enumerate(DIL_PAIRS):
        y = jnp.dot(x_ref[...], w_ref[:, gi * gw3:(gi + 1) * gw3], preferred_element_type=F32)
        rows = tm // dil
        for c in range(gw3 // LANE):
            scr[c] = y[:, c * LANE:(c + 1) * LANE]
        for r in range(dil):
            for c in range(gw3 // LANE):
                col = r * gw3 + c * LANE
                o_refs[gi][0, :, col:col + LANE] = scr[c, pl.ds(r, rows, stride=dil), :].astype(BF16)


def _proj_dilated(xb, w, bsz, seq, tm=512):
    n, k = xb.shape
    gw3 = w.shape[1] // len(DIL_PAIRS)
    spt = seq // tm
    return pl.pallas_call(
        functools.partial(_proj_dilated_body, tm=tm, gw3=gw3), grid=(n // tm,),
        in_specs=[pl.BlockSpec((tm, k), lambda i: (i, 0)), pl.BlockSpec(w.shape, lambda i: (0, 0))],
        out_specs=[pl.BlockSpec((1, tm // dil, dil * gw3), lambda i: (i // spt, i % spt, 0))
                   for _, dil in DIL_PAIRS],
        out_shape=[jax.ShapeDtypeStruct((bsz, seq // dil, dil * gw3), BF16) for _, dil in DIL_PAIRS],
        scratch_shapes=[pltpu.VMEM((gw3 // LANE, tm, LANE), F32)],
        compiler_params=_cparams(1, VMEM_LIMIT), name="proj_dilated",
    )(xb, w)


def _layer_norm(z, g, b):
    mu = jnp.mean(z, axis=-1, keepdims=True)
    zc = z - mu
    var = jnp.mean(zc * zc, axis=-1, keepdims=True)
    return zc * lax.rsqrt(var + LN_EPS) * g + b


def _out_ln_body(*refs, n_sum):
    a0 = refs[0][...].astype(F32)
    for r in refs[1:n_sum]:
        a0 = a0 + r[...].astype(F32)
    a1_ref, w0_ref, w1_ref, r_ref, g_ref, b_ref, of_ref, ob_ref = refs[n_sum:]
    y = jnp.dot(a0.astype(BF16), w0_ref[...], preferred_element_type=F32)
    y = y + jnp.dot(a1_ref[...], w1_ref[...], preferred_element_type=F32)
    out = _layer_norm(ALPHA * r_ref[...] + y, g_ref[...], b_ref[...])
    of_ref[...] = out
    ob_ref[...] = out.astype(BF16)


def _out_ln(a0s, a1, w0, w1, resid, g, b, tm=256):
    n, d = resid.shape
    k0, k1 = w0.shape[0], w1.shape[0]
    row = lambda i: (i, 0)
    fixed = lambda i: (0, 0)
    return pl.pallas_call(
        functools.partial(_out_ln_body, n_sum=len(a0s)),
        grid=(n // tm,),
        in_specs=[pl.BlockSpec((tm, k0), row)] * len(a0s) + [
            pl.BlockSpec((tm, k1), row), pl.BlockSpec((k0, d), fixed), pl.BlockSpec((k1, d), fixed),
            pl.BlockSpec((tm, d), row), pl.BlockSpec((1, d), fixed), pl.BlockSpec((1, d), fixed)],
        out_specs=[pl.BlockSpec((tm, d), row), pl.BlockSpec((tm, d), row)],
        out_shape=[jax.ShapeDtypeStruct((n, d), F32), jax.ShapeDtypeStruct((n, d), BF16)],
        compiler_params=_cparams(1), name="out_proj_ln",
    )(*a0s, a1, w0, w1, resid, g, b)


def _rms(x, g):
    return x * lax.rsqrt(jnp.mean(x * x, axis=-1, keepdims=True) + RMS_EPS) * g


def _mla_q_body(c_ref, g_ref, w_ref, wr_ref, cos_ref, sin_ref, o_ref):
    cn = _rms(c_ref[...], g_ref[...]).astype(BF16)
    a = jnp.dot(cn, w_ref[...], preferred_element_type=F32)
    r = jnp.dot(cn, wr_ref[...], preferred_element_type=F32)
    cos, sin = cos_ref[...], sin_ref[...]
    for h in range(H_A):
        sl = slice(h * LANE, (h + 1) * LANE)
        o_ref[:, sl] = (a[:, sl] * cos + r[:, sl] * sin).astype(o_ref.dtype)


def _mla_kv_body(c_ref, kr_ref, krr_ref, g_ref, wk_ref, wv_ref, cos_ref, sin_ref, k_ref, v_ref):
    cn = _rms(c_ref[...], g_ref[...]).astype(BF16)
    kn = jnp.dot(cn, wk_ref[...], preferred_element_type=F32)
    rope = kr_ref[...] * cos_ref[...] + krr_ref[...] * sin_ref[...]
    for h in range(H_A):
        sl = slice(h * LANE, (h + 1) * LANE)
        k_ref[:, sl] = (kn[:, sl] + rope).astype(k_ref.dtype)
    v_ref[...] = jnp.dot(cn, wv_ref[...], preferred_element_type=F32).astype(v_ref.dtype)


def _mla_up(h1, q_norm, wq, wq_rot, kv_norm, wk, wv, cos_q, sin_q, cos_k, sin_k, seq, tm=512):
    n = h1.shape[0]
    spt = seq // tm
    row = lambda c: (lambda i: (i, c))
    pos = lambda i: (i % spt, 0)
    fixed = lambda i: (0, 0)
    q_a = pl.pallas_call(
        _mla_q_body, grid=(n // tm,),
        in_specs=[pl.BlockSpec((tm, Q_LORA), row(0)), pl.BlockSpec((1, Q_LORA), fixed),
                  pl.BlockSpec(wq.shape, fixed), pl.BlockSpec(wq_rot.shape, fixed),
                  pl.BlockSpec((tm, LANE), pos), pl.BlockSpec((tm, LANE), pos)],
        out_specs=pl.BlockSpec((tm, H_A * LANE), row(0)),
        out_shape=jax.ShapeDtypeStruct((n, H_A * LANE), BF16),
        compiler_params=_cparams(1), name="mla_q_up",
    )(h1, q_norm, wq, wq_rot, cos_q, sin_q)
    k_a, v_a = pl.pallas_call(
        _mla_kv_body, grid=(n // tm,),
        in_specs=[pl.BlockSpec((tm, LANE), row(2)), pl.BlockSpec((tm, LANE), row(3)),
                  pl.BlockSpec((tm, LANE), row(4)), pl.BlockSpec((1, KV_LORA), fixed),
                  pl.BlockSpec(wk.shape, fixed), pl.BlockSpec(wv.shape, fixed),
                  pl.BlockSpec((tm, LANE), pos), pl.BlockSpec((tm, LANE), pos)],
        out_specs=[pl.BlockSpec((tm, H_A * LANE), row(0)), pl.BlockSpec((tm, H_A * MLA_V), row(0))],
        out_shape=[jax.ShapeDtypeStruct((n, H_A * LANE), BF16), jax.ShapeDtypeStruct((n, H_A * MLA_V), BF16)],
        compiler_params=_cparams(1), name="mla_kv_up",
    )(h1, h1, h1, kv_norm, wk, wv, cos_k, sin_k)
    return q_a, k_a, v_a


def _flash_body(*refs, T, TK, dq, dv, q_offs, k_offs, v_offs, scale, bias_mode, has_sel, has_gate,
                epilogue, want_lse, seg_tiles, nback, lam_init):
    R = T // TK
    it = iter(refs)
    q_ref, k_ref, v_ref = next(it), next(it), next(it)
    bias_ref = next(it) if bias_mode else None
    sel_ref = next(it) if has_sel else None
    gate_ref = next(it) if has_gate else None
    lam_ref, dn_ref = (next(it), next(it)) if epilogue == "diff" else (None, None)
    o_ref = next(it)
    lse_ref = next(it) if want_lse else None
    vt_scr = next(it)

    qi = pl.program_id(2)
    seq = v_ref.shape[1]

    @pl.when(qi == 0)
    def _():
        for c in range(seq // TK):
            vt_scr[:, c * TK:(c + 1) * TK] = v_ref[0, c * TK:(c + 1) * TK, :].astype(F32).T.astype(BF16)

    qfull = q_ref[0].astype(F32)
    fold_scale = math.frexp(scale)[0] == 0.5
    qts = [(qfull[:, off:off + dq] * (scale if fold_scale else 1.0)).T.astype(BF16) for off in q_offs]

    def qk(kc):
        kfull = k_ref[0, pl.ds(pl.multiple_of(kc * TK, TK), TK), :]
        return tuple(jnp.dot(kfull[:, k_offs[u]:k_offs[u] + dq], qts[u], preferred_element_type=F32)
                     for u in range(2))

    def update(kc, state, scores, diag):
        start = pl.multiple_of(kc * TK, TK)
        new_state = []
        sel_add = None
        if has_sel:
            per = TK // SEL_BLOCK
            rows = [sel_ref[0, 0, pl.ds(kc * per + a, 1), :] for a in range(per)]
            sel_add = jnp.concatenate([jnp.broadcast_to((r - 1.0) * (-NEG), (SEL_BLOCK, T)) for r in rows], axis=0)
        for u in range(2):
            vt = vt_scr[v_offs[u]:v_offs[u] + dv, pl.ds(start, TK)]
            s = scores[u]
            if not fold_scale:
                s = s * scale
            if bias_mode:
                s = s + bias_ref[u if bias_mode == "pair" else 0, R * qi - kc + (R - 1)]
            elif diag is not None:
                key = lax.broadcasted_iota(jnp.int32, (TK, T), 0) + diag * TK
                qry = lax.broadcasted_iota(jnp.int32, (TK, T), 1)
                s = jnp.where(key <= qry, s, NEG)
            if has_sel:
                s = s + sel_add
            m_prev, l_prev, acc_prev = state[u]
            m_new = jnp.maximum(m_prev, jnp.max(s, axis=0, keepdims=True))
            alpha = jnp.exp(m_prev - m_new)
            p = jnp.exp(s - m_new)
            l_new = alpha * l_prev + jnp.sum(p, axis=0, keepdims=True)
            acc_new = alpha * acc_prev + jnp.dot(vt, p.astype(BF16), preferred_element_type=F32)
            new_state.append((m_new, l_new, acc_new))
        return tuple(new_state)

    init = tuple((jnp.full((1, T), NEG, F32), jnp.zeros((1, T), F32), jnp.zeros((dv, T), F32)) for _ in range(2))
    if nback is None:
        lo = 0
    elif seg_tiles is not None:
        lo = jnp.where(qi % seg_tiles == 0, qi, qi - nback)
    else:
        lo = jnp.maximum(qi - nback, 0)

    def step(kc, carry):
        state, scores = carry
        nxt = qk(kc + 1)
        return update(kc, state, scores, None), nxt

    state, scores = lax.fori_loop(lo, R * qi, step, (init, qk(lo)))
    for a in range(R):
        nxt = qk(R * qi + a + 1) if a + 1 < R else None
        state = update(R * qi + a, state, scores, a)
        scores = nxt

    outs = [acc / l for _, l, acc in state]
    if epilogue == "diff":
        a = outs[0] - lam_ref[0, 0] * outs[1]
        rinv = lax.rsqrt(jnp.mean(a * a, axis=0, keepdims=True) + RMS_EPS)
        o = (a * rinv * dn_ref[...] * (1.0 - lam_init)).T
    else:
        o = jnp.concatenate(outs, axis=0).T
        if has_gate:
            o = o * gate_ref[0]
    o_ref[0] = o.astype(o_ref.dtype)
    if want_lse:
        lse_ref[0, 0] = jnp.concatenate([m + jnp.log(l) for m, l, _ in state], axis=0)


def _flash(q, k, v, *, n_outer, T, dq, dv, q_col, k_col, v_col, q_offs, k_offs, v_offs, scale,
           out_cols, out_col, bias=None, bias_mode=None, bias_idx=None, sel=None, sel_idx=None,
           gate=None, gate_col=None, lam=None, dnorm=None, lam_init=0.0, epilogue="plain", want_lse=False,
           seg_tiles=None, nback=None, vmem=None, k_w=None, v_w=None, name="flash", TK=None,
           rep=1, rep_in=0, rep_out=0):
    bsz, seq, _ = q.shape
    TK = TK or T
    assert T % TK == 0 and (nback is None or T == TK)
    nq = seq // T
    qw = max(o + dq for o in q_offs)
    kw = k_w or max(o + dq for o in k_offs)
    vw = v_w or max(o + dv for o in v_offs)
    ow = dv if epilogue == "diff" else 2 * dv
    in_specs = [pl.BlockSpec((1, T, qw), lambda g, b, i: (b // rep, i, q_col(g) + (b % rep) * rep_in)),
                pl.BlockSpec((1, seq, kw), lambda g, b, i: (b // rep, 0, k_col(g) + (b % rep) * rep_in)),
                pl.BlockSpec((1, seq, vw), lambda g, b, i: (b // rep, 0, v_col(g) + (b % rep) * rep_in))]
    args = [q, k, v]
    if bias_mode:
        nb = 2 if bias_mode == "pair" else 1
        in_specs.append(pl.BlockSpec((nb,) + bias.shape[1:], lambda g, b, i: (bias_idx(g), 0, 0, 0)))
        args.append(bias)
    if sel is not None:
        in_specs.append(pl.BlockSpec((1, 1, sel.shape[2], T), lambda g, b, i: (b, sel_idx(g), 0, i)))
        args.append(sel)
    if gate is not None:
        in_specs.append(pl.BlockSpec((1, T, ow), lambda g, b, i: (b, i, gate_col(g))))
        args.append(gate)
    if epilogue == "diff":
        in_specs.append(pl.BlockSpec(memory_space=pltpu.SMEM))
        in_specs.append(pl.BlockSpec((dv, 1), lambda g, b, i: (0, 0)))
        args += [lam, dnorm]
    out_specs = [pl.BlockSpec((1, T, ow), lambda g, b, i: (b // rep, i, out_col(g) + (b % rep) * rep_out))]
    out_shape = [jax.ShapeDtypeStruct((bsz, seq, out_cols), BF16)]
    if want_lse:
        out_specs.append(pl.BlockSpec((1, 1, 2, T), lambda g, b, i: (b, g, 0, i)))
        out_shape.append(jax.ShapeDtypeStruct((bsz * rep, n_outer, 2, seq), F32))
    body = functools.partial(
        _flash_body, T=T, TK=TK, dq=dq, dv=dv, q_offs=q_offs, k_offs=k_offs, v_offs=v_offs, scale=scale,
        bias_mode=bias_mode, has_sel=sel is not None, has_gate=gate is not None, epilogue=epilogue,
        want_lse=want_lse, seg_tiles=seg_tiles, nback=nback, lam_init=lam_init)
    res = pl.pallas_call(
        body, grid=(n_outer, bsz * rep, nq), in_specs=in_specs, out_specs=out_specs, out_shape=out_shape,
        scratch_shapes=[pltpu.VMEM((vw, seq), BF16)],
        compiler_params=_cparams(3, vmem), name=name,
    )(*args)
    return res if want_lse else res[0]


def _rel_bucket(dist):
    n = jnp.maximum(dist, 0)
    exact = N_BUCKETS // 2
    log_ratio = jnp.log(jnp.maximum(n, 1).astype(F32) / exact) / math.log(REL_MAX_DIST / exact)
    large = exact + (log_ratio * (N_BUCKETS - exact)).astype(jnp.int32)
    return jnp.where(n < exact, n, jnp.minimum(large, N_BUCKETS - 1))


def _toeplitz_bias(tab, T, n_d, dist_scale, max_dist, TK=None):
    TK = TK or T
    R = T // TK
    wlen = T + TK
    x = jnp.arange(wlen)
    dist = (jnp.arange(n_d)[:, None] - (R - 1)) * TK + jnp.where(x < T, x, x - wlen)[None, :]
    w = tab[_rel_bucket(dist * dist_scale)].astype(F32)
    w = jnp.where(((dist >= 0) & (dist <= max_dist))[..., None], w, NEG)
    return _toeplitz(jnp.moveaxis(w, -1, 0), TK, T)


def _toeplitz(w, rows, cols):
    wlen = w.shape[-1]
    tiled = jnp.tile(w, (1,) * (w.ndim - 1) + (rows,))[..., :rows * (wlen - 1)]
    return tiled.reshape(w.shape[:-1] + (rows, wlen - 1))[..., :cols]


def _compress_body(u_ref, pe_ref, w1_ref, w2_ref, o_ref, *, ncp):
    outs = []
    for a in range(2):
        u = u_ref[0, 0, a].astype(F32)
        p1 = jnp.dot(u + pe_ref[a, 0], w1_ref[a, 0], preferred_element_type=F32, precision=HI)
        p2 = jnp.dot(u + pe_ref[a, 1], w1_ref[a, 1], preferred_element_type=F32, precision=HI)
        hid = jax.nn.gelu(p1 + pltpu.roll(p2, ncp - 1, 0))
        outs.append(jnp.dot(hid, w2_ref[a], preferred_element_type=F32, precision=HI))
    o_ref[0, 0] = jnp.concatenate(outs, axis=-1)


def _compress(u, pe, w1, w2):
    bsz, g, _, ncp, width = u.shape
    return pl.pallas_call(
        functools.partial(_compress_body, ncp=ncp), grid=(bsz, g),
        in_specs=[pl.BlockSpec((1, 1, 2, ncp, width), lambda b, gg: (b, gg, 0, 0, 0)),
                  pl.BlockSpec(pe.shape, lambda b, gg: (0, 0, 0, 0)),
                  pl.BlockSpec(w1.shape, lambda b, gg: (0, 0, 0, 0)),
                  pl.BlockSpec(w2.shape, lambda b, gg: (0, 0, 0))],
        out_specs=pl.BlockSpec((1, 1, ncp, 2 * DK_C), lambda b, gg: (b, gg, 0, 0)),
        out_shape=jax.ShapeDtypeStruct((bsz, g, ncp, 2 * DK_C), F32),
        compiler_params=_cparams(2), name="nsa_compress",
    )(u, pe, w1, w2)


def _cmp_attn_body(q_ref, kv_ref, bias_ref, ov_ref, gate_ref, o_ref, sel_ref, *, T, ncp, n_sel, n_top, scale):
    qi = pl.program_id(2)
    kc = kv_ref[0, 0, :, :DK_C]
    vc = kv_ref[0, 0, :, DK_C:]
    t = qi * T + lax.broadcasted_iota(jnp.int32, (T, ncp), 0)
    c = lax.broadcasted_iota(jnp.int32, (T, ncp), 1)
    valid = t >= c * CMP_STRIDE + (CMP_LEN - 1)
    validf = valid.astype(F32)
    psum = jnp.zeros((T, ncp), F32)
    outs = []
    for r in range(R_C):
        q = q_ref[0, :, r * DK_C:(r + 1) * DK_C].astype(F32)
        s = lax.dot_general(q, kc, _NT, preferred_element_type=F32, precision=HI) * scale + bias_ref[r]
        s = jnp.where(valid, s, NEG)
        e = jnp.exp(s - jnp.max(s, axis=-1, keepdims=True)) * validf
        p = e / jnp.maximum(jnp.sum(e, axis=-1, keepdims=True), 1e-30)
        outs.append(jnp.dot(p, vc, preferred_element_type=F32, precision=HI))
        psum = psum + p
    o_ref[0] = (jnp.concatenate(outs, axis=-1) * gate_ref[0]).astype(o_ref.dtype)

    imp = jnp.dot(ov_ref[...], psum.T, preferred_element_type=F32, precision=HI)
    tq = qi * T + lax.broadcasted_iota(jnp.int32, (n_sel, T), 1)
    j = lax.broadcasted_iota(jnp.int32, (n_sel, T), 0)
    forced = (j == tq // SEL_BLOCK) | (j == 0)
    work = jnp.where(forced, BIG, jnp.where(j * SEL_BLOCK <= tq, imp, -BIG))
    sel = jnp.zeros((n_sel, T), F32)
    jf = j.astype(F32)
    for _ in range(n_top):
        _, _, pick = _first_max(work, jf, n_sel)
        sel = jnp.where(pick, 1.0, sel)
        work = jnp.where(pick, -jnp.inf, work)
    sel_ref[0, 0] = sel


def _cmp_attn(h, kvc, bias_c, overlap, gates, T=256):
    bsz, seq, _ = h.shape
    ncp = kvc.shape[2]
    n_sel = seq // SEL_BLOCK
    n_top = min(SEL_TOP, n_sel)
    qw = R_C * DK_C
    return pl.pallas_call(
        functools.partial(_cmp_attn_body, T=T, ncp=ncp, n_sel=n_sel, n_top=n_top, scale=DK_C ** -0.5),
        grid=(G_C, bsz, seq // T),
        in_specs=[pl.BlockSpec((1, T, qw), lambda g, b, i: (b, i, g)),
                  pl.BlockSpec((1, 1, ncp, 2 * DK_C), lambda g, b, i: (b, g, 0, 0)),
                  pl.BlockSpec((R_C, T, ncp), lambda g, b, i: (g, i, 0)),
                  pl.BlockSpec(overlap.shape, lambda g, b, i: (0, 0)),
                  pl.BlockSpec((1, T, qw), lambda g, b, i: (b, i, g))],
        out_specs=[pl.BlockSpec((1, T, qw), lambda g, b, i: (b, i, g)),
                   pl.BlockSpec((1, 1, n_sel, T), lambda g, b, i: (b, g, 0, i))],
        out_shape=[jax.ShapeDtypeStruct((bsz, seq, H_C * DK_C), BF16),
                   jax.ShapeDtypeStruct((bsz, G_C, n_sel, seq), F32)],
        compiler_params=_cparams(3), name="nsa_cmp_attn",
    )(h, kvc, bias_c, overlap, gates)


def _first_max(work, idx, n):
    mx = jnp.max(work, axis=0, keepdims=True)
    first = jnp.min(jnp.where(work == mx, idx, float(n)), axis=0, keepdims=True)
    return mx, first, idx == first


def _router_body(x_ref, wt_ref, b_ref, tri_ref, e_ref, g_ref, r_ref, cnt_ref, carry_scr, *, tm):
    i = pl.program_id(0)

    @pl.when(i == 0)
    def _():
        carry_scr[...] = jnp.zeros(carry_scr.shape, F32)

    st = lax.dot_general(wt_ref[...], x_ref[...], _NT, preferred_element_type=F32, precision=HI)
    scores = jax.nn.sigmoid(st)
    sel = scores + b_ref[...]
    per = N_EXPERTS // N_EXPERT_GROUPS
    fiota = lambda rows: lax.broadcasted_iota(jnp.int32, (rows, tm), 0).astype(F32)
    i_per, i_grp, i_exp = fiota(per), fiota(N_EXPERT_GROUPS), fiota(N_EXPERTS)
    grp_scores = []
    for g in range(N_EXPERT_GROUPS):
        blk = sel[g * per:(g + 1) * per]
        m1, _, pick = _first_max(blk, i_per, per)
        grp_scores.append(m1 + jnp.max(jnp.where(pick, -jnp.inf, blk), axis=0, keepdims=True))
    work = jnp.concatenate(grp_scores, axis=0)
    gmask = jnp.zeros((N_EXPERT_GROUPS, tm), F32)
    for _ in range(TOPK_GROUPS):
        _, _, pick = _first_max(work, i_grp, N_EXPERT_GROUPS)
        gmask = jnp.where(pick, 1.0, gmask)
        work = jnp.where(pick, -jnp.inf, work)
    work = jnp.concatenate([jnp.where(gmask[g:g + 1] > 0.5, sel[g * per:(g + 1) * per], NEG)
                            for g in range(N_EXPERT_GROUPS)], axis=0)
    picks, firsts, vals = [], [], []
    for _ in range(TOP_K):
        _, first, pick = _first_max(work, i_exp, N_EXPERTS)
        picks.append(pick)
        firsts.append(first)
        vals.append(jnp.sum(jnp.where(pick, scores, 0.0), axis=0, keepdims=True))
        work = jnp.where(pick, -jnp.inf, work)
    val = jnp.concatenate(vals, axis=0)
    g_ref[...] = val / jnp.sum(val, axis=0, keepdims=True) * ROUTED_SCALE
    e_ref[...] = jnp.concatenate(firsts, axis=0).astype(jnp.int32)
    onehot = picks[0].astype(F32)
    for pick in picks[1:]:
        onehot = onehot + pick.astype(F32)
    before = jnp.dot(onehot.astype(BF16), tri_ref[...], preferred_element_type=F32) + carry_scr[...]
    r_ref[...] = jnp.concatenate([jnp.sum(jnp.where(pick, before, 0.0), axis=0, keepdims=True)
                                  for pick in picks], axis=0).astype(jnp.int32)
    carry = carry_scr[...] + jnp.sum(onehot, axis=1, keepdims=True)
    carry_scr[...] = carry
    cnt_ref[...] = jnp.broadcast_to(carry, cnt_ref.shape)


def _router(xf, router_w, router_b, tm=256):
    n, d = xf.shape
    tri = (jnp.arange(tm)[:, None] < jnp.arange(tm)[None, :]).astype(BF16)
    col = lambda i: (0, i)
    fixed = lambda i: (0, 0)
    return pl.pallas_call(
        functools.partial(_router_body, tm=tm), grid=(n // tm,),
        in_specs=[pl.BlockSpec((tm, d), lambda i: (i, 0)), pl.BlockSpec((N_EXPERTS, d), fixed),
                  pl.BlockSpec((N_EXPERTS, 1), fixed), pl.BlockSpec((tm, tm), fixed)],
        out_specs=[pl.BlockSpec((TOP_K, tm), col), pl.BlockSpec((TOP_K, tm), col),
                   pl.BlockSpec((TOP_K, tm), col), pl.BlockSpec((N_EXPERTS, LANE), fixed)],
        out_shape=[jax.ShapeDtypeStruct((TOP_K, n), jnp.int32), jax.ShapeDtypeStruct((TOP_K, n), F32),
                   jax.ShapeDtypeStruct((TOP_K, n), jnp.int32), jax.ShapeDtypeStruct((N_EXPERTS, LANE), F32)],
        scratch_shapes=[pltpu.VMEM((N_EXPERTS, 1), F32)],
        compiler_params=_cparams(1), name="router",
    )(xf, router_w.T, router_b.astype(F32)[:, None], tri)


def _pack_rows(x):
    w = x.shape[-1] // 2
    lo = lax.bitcast_convert_type(x[:, :w].astype(BF16).astype(F32), jnp.uint32)
    hi = lax.bitcast_convert_type(x[:, w:].astype(BF16).astype(F32), jnp.uint32)
    return (lo >> 16) | (hi & jnp.uint32(0xFFFF0000))


def _unpack_rows(p):
    lo = lax.bitcast_convert_type(p << 16, F32)
    hi = lax.bitcast_convert_type(p & jnp.uint32(0xFFFF0000), F32)
    return lo, hi


def _dispatch_body(fill_ref, pos_ref, x_ref, z_ref, xs_hbm, xp_scr, sem, *, tm, n_blocks):
    i = pl.program_id(0)
    slot = i % 2
    xp_scr[slot] = _pack_rows(x_ref[...]).reshape(tm // SUBLANES, SUBLANES, -1)
    zsem = sem.at[2]

    def row_copy(s, grp, j, dst):
        return pltpu.make_async_copy(xp_scr.at[s, grp, pl.ds(j, 1)], xs_hbm.at[pl.ds(dst, 1)], sem.at[s])

    def zero_copy(dst):
        return pltpu.make_async_copy(z_ref.at[pl.ds(0, 1)], xs_hbm.at[pl.ds(dst, 1)], zsem)

    def zero_block(blk):
        return pltpu.make_async_copy(z_ref, xs_hbm.at[pl.ds(blk * MOE_BM, MOE_BM)], zsem)

    def issue(grp, c):
        for j in range(SUBLANES):
            for k in range(TOP_K):
                row_copy(slot, grp, j, pos_ref[k, grp * SUBLANES + j]).start(priority=k % 2)
        return c

    def drain(s):
        def body(grp, c):
            for _ in range(SUBLANES * TOP_K):
                row_copy(s, 0, 0, 0).wait()
            return c

        lax.fori_loop(0, tm // SUBLANES, body, 0)

    lax.fori_loop(0, tm // SUBLANES, issue, 0)

    @pl.when(i > 0)
    def _():
        drain(1 - slot)

    @pl.when(i == pl.num_programs(0) - 1)
    def _():
        drain(slot)

    @pl.when(i == 0)
    def _():
        def fill(e, c):
            lo, hi = fill_ref[2 * e], fill_ref[2 * e + 1]

            def start(r, cc):
                zero_copy(r).start()
                return cc

            def wait(r, cc):
                zero_copy(0).wait()
                return cc

            lax.fori_loop(lo, hi, start, 0)
            lax.fori_loop(lo, hi, wait, 0)
            return c

        lax.fori_loop(0, N_EXPERTS, fill, 0)

        def tail_start(blk, c):
            zero_block(blk).start()
            return c

        def tail_wait(blk, c):
            zero_block(0).wait()
            return c

        lax.fori_loop(fill_ref[2 * N_EXPERTS], n_blocks, tail_start, 0)
        lax.fori_loop(fill_ref[2 * N_EXPERTS], n_blocks, tail_wait, 0)


def _dispatch(xb, pos, fill, p, tm=128):
    n, d = xb.shape
    grid_spec = pltpu.PrefetchScalarGridSpec(
        num_scalar_prefetch=1, grid=(n // tm,),
        in_specs=[pl.BlockSpec((TOP_K, tm), lambda i, fl: (0, i), memory_space=pltpu.SMEM),
                  pl.BlockSpec((tm, d), lambda i, fl: (i, 0)),
                  pl.BlockSpec((MOE_BM, d // 2), lambda i, fl: (0, 0))],
        out_specs=pl.BlockSpec(memory_space=pl.ANY),
        scratch_shapes=[pltpu.VMEM((2, tm // SUBLANES, SUBLANES, d // 2), jnp.uint32),
                        pltpu.SemaphoreType.DMA((3,))])
    return pl.pallas_call(
        functools.partial(_dispatch_body, tm=tm, n_blocks=p // MOE_BM), grid_spec=grid_spec,
        out_shape=jax.ShapeDtypeStruct((p, d // 2), jnp.uint32),
        compiler_params=_cparams(1), name="dispatch",
    )(fill, pos, xb, jnp.zeros((MOE_BM, d // 2), jnp.uint32))


def _moe_ffn_body(be_ref, nb_ref, x_ref, wg_ref, wu_ref, wd_ref, o_ref, wg_scr, wu_scr, wd_scr):
    i = pl.program_id(0)
    half = x_ref.shape[1]

    @pl.when((i == 0) | (be_ref[i] != be_ref[jnp.maximum(i - 1, 0)]))
    def _():
        wg_scr[...] = wg_ref[0, 0].astype(BF16)
        wu_scr[...] = wu_ref[0, 0].astype(BF16)
        wd_scr[...] = wd_ref[0, 0].astype(BF16)

    @pl.when(i < nb_ref[0])
    def _():
        lo, hi = _unpack_rows(x_ref[...])
        lo, hi = lo.astype(BF16), hi.astype(BF16)
        hg = (jnp.dot(lo, wg_scr[:half], preferred_element_type=F32)
              + jnp.dot(hi, wg_scr[half:], preferred_element_type=F32))
        hu = (jnp.dot(lo, wu_scr[:half], preferred_element_type=F32)
              + jnp.dot(hi, wu_scr[half:], preferred_element_type=F32))
        hb = (hg * jax.nn.sigmoid(hg) * hu).astype(BF16)
        o_ref[...] = _pack_rows(jnp.dot(hb, wd_scr[...], preferred_element_type=F32))

    @pl.when(i >= nb_ref[0])
    def _():
        o_ref[...] = jnp.zeros(o_ref.shape, o_ref.dtype)


def _moe_ffn(xs, blk_e, n_used, wg, wu, wd, layer):
    p, half = xs.shape
    d = 2 * half
    n_blocks = p // MOE_BM
    grid_spec = pltpu.PrefetchScalarGridSpec(
        num_scalar_prefetch=2, grid=(n_blocks,),
        in_specs=[pl.BlockSpec((MOE_BM, half), lambda i, be, nb: (i, 0)),
                  pl.BlockSpec((1, 1, d, D_EXPERT), lambda i, be, nb: (layer, be[i], 0, 0)),
                  pl.BlockSpec((1, 1, d, D_EXPERT), lambda i, be, nb: (layer, be[i], 0, 0)),
                  pl.BlockSpec((1, 1, D_EXPERT, d), lambda i, be, nb: (layer, be[i], 0, 0))],
        out_specs=pl.BlockSpec((MOE_BM, half), lambda i, be, nb: (i, 0)),
        scratch_shapes=[pltpu.VMEM((d, D_EXPERT), BF16), pltpu.VMEM((d, D_EXPERT), BF16),
                        pltpu.VMEM((D_EXPERT, d), BF16)])
    return pl.pallas_call(
        _moe_ffn_body, grid_spec=grid_spec, out_shape=jax.ShapeDtypeStruct((p, half), jnp.uint32),
        compiler_params=_cparams(1), name="expert_ffn",
    )(blk_e, n_used, xs, wg, wu, wd)


def _combine_body(pos_ref, posn_ref, gate_ref, xb_ref, xf_ref, y_hbm, sg_ref, su_ref, sd_ref,
                  g_ref, b_ref, of_ref, ob_ref, buf, sem, *, tm):
    i = pl.program_id(0)
    slot = i % 2

    def row_copy(s, k, grp, j, src):
        return pltpu.make_async_copy(y_hbm.at[pl.ds(src, 1)], buf.at[s, k, grp, pl.ds(j, 1)], sem.at[s])

    def issue_tile(pr, s):
        def issue(grp, c):
            for j in range(SUBLANES):
                for k in range(TOP_K):
                    row_copy(s, k, grp, j, pr[k, grp * SUBLANES + j]).start(priority=k % 2)
            return c

        lax.fori_loop(0, tm // SUBLANES, issue, 0)

    @pl.when(i == 0)
    def _():
        issue_tile(pos_ref, 0)

    @pl.when(i + 1 < pl.num_programs(0))
    def _():
        issue_tile(posn_ref, 1 - slot)

    x = xb_ref[...]
    hg = jnp.dot(x, sg_ref[...], preferred_element_type=F32)
    hu = jnp.dot(x, su_ref[...], preferred_element_type=F32)
    hb = (hg * jax.nn.sigmoid(hg) * hu).astype(BF16)
    y = jnp.dot(hb, sd_ref[...], preferred_element_type=F32)

    def drain(grp, c):
        for _ in range(SUBLANES * TOP_K):
            row_copy(slot, 0, 0, 0, 0).wait()
        return c

    lax.fori_loop(0, tm // SUBLANES, drain, 0)
    gate = gate_ref[...]
    half = buf.shape[-1]
    y_lo, y_hi = y[:, :half], y[:, half:]
    for k in range(TOP_K):
        lo, hi = _unpack_rows(buf[slot, k].reshape(tm, half))
        y_lo = y_lo + gate[:, k:k + 1] * lo
        y_hi = y_hi + gate[:, k:k + 1] * hi
    out = _layer_norm(ALPHA * xf_ref[...] + jnp.concatenate([y_lo, y_hi], axis=-1), g_ref[...], b_ref[...])
    of_ref[...] = out
    ob_ref[...] = out.astype(BF16)


def _combine(pos, gate, xb, xf, y, sg, su, sd, g, b, tm=128):
    n, d = xf.shape
    n_tiles = n // tm
    row = lambda i: (i, 0)
    fixed = lambda i: (0, 0)
    smem_col = pl.BlockSpec((TOP_K, tm), lambda i: (0, i), memory_space=pltpu.SMEM)
    smem_next = pl.BlockSpec((TOP_K, tm), lambda i: (0, jnp.minimum(i + 1, n_tiles - 1)),
                             memory_space=pltpu.SMEM)
    grid_spec = pltpu.PrefetchScalarGridSpec(
        num_scalar_prefetch=0, grid=(n_tiles,),
        in_specs=[smem_col, smem_next, pl.BlockSpec((tm, TOP_K), row),
                  pl.BlockSpec((tm, d), row), pl.BlockSpec((tm, d), row), pl.BlockSpec(memory_space=pl.ANY),
                  pl.BlockSpec(sg.shape, fixed), pl.BlockSpec(su.shape, fixed), pl.BlockSpec(sd.shape, fixed),
                  pl.BlockSpec((1, d), fixed), pl.BlockSpec((1, d), fixed)],
        out_specs=[pl.BlockSpec((tm, d), row), pl.BlockSpec((tm, d), row)],
        scratch_shapes=[pltpu.VMEM((2, TOP_K, tm // SUBLANES, SUBLANES, d // 2), jnp.uint32),
                        pltpu.SemaphoreType.DMA((2,))])
    return pl.pallas_call(
        functools.partial(_combine_body, tm=tm), grid_spec=grid_spec,
        out_shape=[jax.ShapeDtypeStruct((n, d), F32), jax.ShapeDtypeStruct((n, d), BF16)],
        compiler_params=_cparams(1), name="combine",
    )(pos, pos, gate, xb, xf, y, sg, su, sd, g, b)


def _moe(xf, xb, router_w, router_b, w_gate, w_up, w_down, sh_gate, sh_up, sh_down, ln_g, ln_b, layer=0):
    n, d = xf.shape
    e_idx, gate, rank, cnt = _router(xf, router_w, router_b)
    counts = cnt[:, 0].astype(jnp.int32)
    padded = (counts + MOE_BM - 1) // MOE_BM * MOE_BM
    pad_end = jnp.cumsum(padded)
    pad_start = (pad_end - padded).astype(jnp.int32)
    n_blocks = (n * TOP_K + N_EXPERTS * (MOE_BM - 1) + MOE_BM - 1) // MOE_BM
    blk_e = jnp.minimum(jnp.sum(pad_end[None, :] <= (jnp.arange(n_blocks) * MOE_BM)[:, None], axis=1),
                        N_EXPERTS - 1).astype(jnp.int32)
    n_used = (pad_end[-1] // MOE_BM).astype(jnp.int32).reshape(1)
    fill = jnp.concatenate([jnp.stack([pad_start + counts, pad_end], axis=1).reshape(-1), n_used]).astype(jnp.int32)
    experts = jnp.arange(N_EXPERTS, dtype=jnp.int32)[:, None, None]
    pos = rank + jnp.sum(jnp.where(e_idx[None] == experts, pad_start[:, None, None], 0), axis=0)
    xs = _dispatch(xb, pos, fill, n_blocks * MOE_BM)
    y = _moe_ffn(xs, blk_e, n_used, w_gate, w_up, w_down, layer)
    return _combine(pos, gate.T, xb, xf, y, sh_gate, sh_up, sh_down, ln_g[None], ln_b[None])


def _rope_tables(seq):
    half = ROPE_DIM // 2
    freqs = ROPE_THETA ** (-jnp.arange(half, dtype=F32) / half)
    ang = jnp.arange(seq).astype(F32)[:, None] * freqs
    cos = jnp.concatenate([jnp.cos(ang)] * 2, -1)
    sin = jnp.concatenate([jnp.sin(ang)] * 2, -1)
    z = lambda w: jnp.zeros((seq, w), F32)
    pad = LANE - NOPE - ROPE_DIM
    cos_q = jnp.concatenate([jnp.ones((seq, NOPE), F32), cos, z(pad)], -1)
    sin_q = jnp.concatenate([z(NOPE), sin, z(pad)], -1)
    cos_k = jnp.concatenate([z(NOPE), cos, z(pad)], -1)
    return cos_q, sin_q, cos_k, sin_q


def _rot_cols(w):
    half = w.shape[-1] // 2
    return jnp.concatenate([-w[..., half:], w[..., :half]], -1)


def _mixer_ab(xb, xf, bsz, seq, w_in, q_norm, w_uq, kv_norm, w_ukv, w_out, ln_g, ln_b, rope_tabs, dil_bias):
    n = bsz * seq
    d = w_in.shape[0]
    c0 = Q_LORA + KV_LORA
    w_kr = w_in[:, c0:c0 + ROPE_DIM]
    zc = lambda w: jnp.zeros((d, w), F32)
    pad = LANE - NOPE - ROPE_DIM
    w1 = jnp.concatenate([w_in[:, :c0], zc(NOPE), w_kr, zc(pad), zc(NOPE), _rot_cols(w_kr), zc(pad)], 1)
    h1 = _mm(xb, w1.astype(BF16), F32, tn=w1.shape[1])
    gw = H_B_GROUP * HD_B
    w_b = w_in[:, c0 + ROPE_DIM:].reshape(d, 3, len(DIL_PAIRS), gw).transpose(0, 2, 1, 3).reshape(d, -1)
    h2 = _proj_dilated(xb, w_b.astype(BF16), bsz, seq)

    wq = w_uq.reshape(Q_LORA, H_A, NOPE + ROPE_DIM)
    zq = jnp.zeros((Q_LORA, H_A, pad), F32)
    wq_main = jnp.concatenate([wq, zq], -1).reshape(Q_LORA, H_A * LANE)
    wq_rot = jnp.concatenate([jnp.zeros((Q_LORA, H_A, NOPE), F32), _rot_cols(wq[..., NOPE:]), zq], -1)
    wq_rot = wq_rot.reshape(Q_LORA, H_A * LANE)
    wkv = w_ukv.reshape(KV_LORA, H_A, NOPE + MLA_V)
    wk = jnp.concatenate([wkv[..., :NOPE], jnp.zeros((KV_LORA, H_A, LANE - NOPE), F32)], -1)
    wk = wk.reshape(KV_LORA, H_A * LANE)
    wv = wkv[..., NOPE:].reshape(KV_LORA, H_A * MLA_V)
    q_a, k_a, v_a = _mla_up(h1, q_norm[None], wq_main.astype(BF16), wq_rot.astype(BF16), kv_norm[None],
                            wk.astype(BF16), wv.astype(BF16), *rope_tabs, seq)
    o_a = _flash(q_a.reshape(bsz, seq, -1), k_a.reshape(bsz, seq, -1), v_a.reshape(bsz, seq, -1), name="mla_attn",
                 n_outer=H_A // 2, T=min(ATT_TQ, seq), TK=ATT_TK, dq=LANE, dv=MLA_V,
                 q_col=lambda g: g, k_col=lambda g: g, v_col=lambda g: g,
                 q_offs=(0, LANE), k_offs=(0, LANE), v_offs=(0, MLA_V),
                 scale=(NOPE + ROPE_DIM) ** -0.5, out_cols=H_A * MLA_V, out_col=lambda g: g)

    gb = gw // LANE
    outs, lses = [], []
    for gi, (window, dil) in enumerate(DIL_PAIRS):
        L = seq // dil
        t = h2[gi]
        o, lse = _flash(t, t, t, name="dilated_attn", n_outer=2, T=min(256, L), dq=HD_B, dv=HD_B,
                        q_col=lambda g: g, k_col=lambda g: gb + g, v_col=lambda g: 2 * gb + g,
                        q_offs=(0, HD_B), k_offs=(0, HD_B), v_offs=(0, HD_B), scale=HD_B ** -0.5,
                        out_cols=dil * gw, out_col=lambda g: g, bias=dil_bias[gi], bias_mode="pair",
                        bias_idx=lambda g: g, want_lse=True, nback=1,
                        rep=dil, rep_in=3 * gb, rep_out=gb)
        outs.append(o.reshape(bsz, seq, H_B_GROUP, HD_B).astype(F32))
        lses.append(lse.reshape(bsz, dil, H_B_GROUP, L).transpose(0, 3, 1, 2).reshape(bsz, seq, H_B_GROUP))
    w = jax.nn.softmax(jnp.stack(lses), axis=0)
    o_b = jnp.sum(w[..., None] * jnp.stack(outs), axis=0).astype(BF16).reshape(n, gw)
    na = H_A * MLA_V
    return _out_ln([o_a.reshape(n, na)], o_b, w_out[:na].astype(BF16), w_out[na:].astype(BF16),
                   xf, ln_g[None], ln_b[None])


def _mixer_cd(xb, xf, bsz, seq, w_in, pos_k, k_w1, k_w2, pos_v, v_w1, v_w2, lq1, lk1, lq2, lk2, d_norm,
              w_out, ln_g, ln_b, lam_init, tabs):
    n = bsz * seq
    qc_w = H_C * DK_C
    kv_w = G_C * DK_C
    off = qc_w
    kvs = []
    for _ in range(3):
        wk_ = w_in[:, off:off + kv_w].reshape(-1, G_C, DK_C)
        wv_ = w_in[:, off + kv_w:off + 2 * kv_w].reshape(-1, G_C, DK_C)
        kvs.append(jnp.concatenate([wk_, wv_], -1).reshape(-1, 2 * kv_w))
        off += 2 * kv_w
    g_off = off
    d_off = off + 3 * H_C
    w_main = jnp.concatenate([w_in[:, :qc_w]] + kvs + [w_in[:, d_off:]], 1)
    h = _mm(xb, w_main.astype(BF16), BF16, tn=w_main.shape[1] // 2).reshape(bsz, seq, -1)
    w_g = jnp.repeat(w_in[:, g_off:d_off], DK_C, axis=1)
    gates = _mm(xb, w_g.astype(BF16), F32, tn=w_g.shape[1] // 2, act="sigmoid").reshape(bsz, seq, -1)

    ncp = seq // CMP_STRIDE
    half = CMP_STRIDE * DK_C
    kv_cmp = h[:, :, qc_w:qc_w + 2 * kv_w].reshape(bsz, ncp, CMP_STRIDE, G_C, 2, DK_C)
    u = kv_cmp.transpose(0, 3, 4, 1, 2, 5).reshape(bsz, G_C, 2, ncp, half)
    pe = jnp.stack([pos_k.reshape(2, 1, half), pos_v.reshape(2, 1, half)])
    w1 = jnp.stack([k_w1.reshape(2, half, CMP_HID), v_w1.reshape(2, half, CMP_HID)])
    w2 = jnp.stack([k_w2, v_w2])
    kvc = _compress(u, pe, w1, w2)
    o_cmp, sel = _cmp_attn(h, kvc, tabs["bias_c"], tabs["overlap"], gates)

    cb = qc_w // LANE
    scale = DK_C ** -0.5
    n_pairs = H_C // 2
    nsa = dict(n_outer=n_pairs, dq=DK_C, dv=DK_C, q_col=lambda g: g, q_offs=(0, DK_C), k_offs=(0, 0),
               v_offs=(DK_C, DK_C), scale=scale, out_cols=qc_w, out_col=lambda g: g, bias_mode="pair",
               bias_idx=lambda g: g, gate=gates, vmem=VMEM_LIMIT, k_w=LANE, v_w=LANE)
    o_sel = _flash(h, h, h, name="nsa_sel_attn", T=ATT_TQ, TK=ATT_TK, k_col=lambda g: cb + 2 + g // 2, v_col=lambda g: cb + 2 + g // 2,
                   bias=tabs["bias_sel"], sel=sel, sel_idx=lambda g: g // 2,
                   gate_col=lambda g: n_pairs + g, **nsa)
    o_win = _flash(h, h, h, name="nsa_win_attn", T=WIN, k_col=lambda g: cb + 4 + g // 2, v_col=lambda g: cb + 4 + g // 2,
                   bias=tabs["bias_win"], gate_col=lambda g: 2 * n_pairs + g, nback=1, **nsa)

    lam = (jnp.exp(jnp.sum(lq1.astype(F32) * lk1.astype(F32)))
           - jnp.exp(jnp.sum(lq2.astype(F32) * lk2.astype(F32))) + lam_init).reshape(1, 1)
    db = cb + 6
    o_d = _flash(h, h, h, name="diff_attn", n_outer=H_D, T=ATT_TQ, TK=ATT_TK, dq=DD, dv=2 * DD,
                 q_col=lambda g: db + g, k_col=lambda g: db + H_D + g, v_col=lambda g: db + 2 * H_D + g,
                 q_offs=(0, DD), k_offs=(0, DD), v_offs=(0, 0), scale=DD ** -0.5,
                 out_cols=H_D * 2 * DD, out_col=lambda g: g, bias=tabs["bias_d"], bias_mode="shared",
                 bias_idx=lambda g: g, lam=lam, dnorm=d_norm[:, None], lam_init=lam_init, epilogue="diff",
                 vmem=VMEM_LIMIT)
    r2 = lambda a: a.reshape(n, -1)
    return _out_ln([r2(o_cmp), r2(o_sel), r2(o_win)], r2(o_d), w_out[:qc_w].astype(BF16),
                   w_out[qc_w:].astype(BF16), xf, ln_g[None], ln_b[None])


def _nsa_tables(rel_bias, seq):
    tab_c = rel_bias[:, H_B:H_B + H_C]
    tab_d = rel_bias[:, H_B + H_C:H_B + H_C + H_D]
    ncp = seq // CMP_STRIDE
    n_sel = seq // SEL_BLOCK
    pos = jnp.arange(seq)
    x = jnp.arange(2 * ncp)
    c_minus_a = jnp.where(x < ncp, x, x - 2 * ncp)
    dist = -CMP_STRIDE * c_minus_a[None, :] + jnp.arange(CMP_STRIDE)[:, None] - (CMP_LEN - 1)
    w = jnp.moveaxis(tab_c[_rel_bucket(dist)].astype(F32), -1, 0)
    bias_c = _toeplitz(w, ncp, ncp).transpose(0, 2, 1, 3).reshape(H_C, seq, ncp)
    c0 = jnp.arange(ncp) * CMP_STRIDE
    s0 = jnp.arange(n_sel) * SEL_BLOCK
    overlap = jnp.maximum(jnp.minimum(c0[:, None] + CMP_LEN, s0[None, :] + SEL_BLOCK)
                          - jnp.maximum(c0[:, None], s0[None, :]), 0).astype(F32) / CMP_LEN
    return {
        "bias_c": bias_c, "overlap": overlap.T,
        "bias_sel": _toeplitz_bias(tab_c, ATT_TQ, seq // ATT_TK, 1, seq, ATT_TK),
        "bias_win": _toeplitz_bias(tab_c, WIN, 2, 1, WIN - 1),
        "bias_d": _toeplitz_bias(tab_d, ATT_TQ, seq // ATT_TK, 1, seq, ATT_TK),
    }


def kernel(x, rel_bias, ab_w_in, mla_q_norm, mla_w_uq, mla_kv_norm, mla_w_ukv, ab_w_out, cd_w_in, nsa_cmp_pos_k, nsa_cmp_k_w1, nsa_cmp_k_w2, nsa_cmp_pos_v, nsa_cmp_v_w1, nsa_cmp_v_w2, diff_lambda_q1, diff_lambda_k1, diff_lambda_q2, diff_lambda_k2, diff_norm, cd_w_out, ln1_g, ln1_b, ln2_g, ln2_b, router_w, router_b, exp_w_gate, exp_w_up, exp_w_down, sh_w_gate, sh_w_up, sh_w_down):
    bsz, seq, d = x.shape
    n = bsz * seq
    depth = ln1_g.shape[0]
    rope_tabs = _rope_tables(seq)
    dil_bias = [_toeplitz_bias(rel_bias[:, gi * H_B_GROUP:(gi + 1) * H_B_GROUP], min(256, seq // dil), 2, dil,
                               window // dil) for gi, (window, dil) in enumerate(DIL_PAIRS)]
    nsa_tabs = _nsa_tables(rel_bias, seq)
    xf = x.reshape(n, d)
    xb = xf.astype(BF16)
    for l in range(depth):
        i = l // 2
        if l % 2 == 0:
            xf, xb = _mixer_ab(xb, xf, bsz, seq, ab_w_in[i], mla_q_norm[i], mla_w_uq[i], mla_kv_norm[i],
                               mla_w_ukv[i], ab_w_out[i], ln1_g[l], ln1_b[l], rope_tabs, dil_bias)
        else:
            lam_init = 0.8 - 0.6 * math.exp(-0.3 * l)
            xf, xb = _mixer_cd(xb, xf, bsz, seq, cd_w_in[i], nsa_cmp_pos_k[i], nsa_cmp_k_w1[i],
                               nsa_cmp_k_w2[i], nsa_cmp_pos_v[i], nsa_cmp_v_w1[i], nsa_cmp_v_w2[i],
                               diff_lambda_q1[i], diff_lambda_k1[i], diff_lambda_q2[i], diff_lambda_k2[i],
                               diff_norm[i], cd_w_out[i], ln1_g[l], ln1_b[l], lam_init, nsa_tabs)
        xf, xb = _moe(xf, xb, router_w[l], router_b[l], exp_w_gate, exp_w_up, exp_w_down,
                      sh_w_gate[l].astype(BF16), sh_w_up[l].astype(BF16), sh_w_down[l].astype(BF16),
                      ln2_g[l], ln2_b[l], layer=l)
    return xf.reshape(bsz, seq, d)
```

```python
import functools
import math

import jax
import jax.numpy as jnp
from jax import lax
from jax.experimental import pallas as pl
from jax.experimental.pallas import tpu as pltpu

F32 = jnp.float32
BF16 = jnp.bfloat16
HI = lax.Precision.HIGHEST

DEPTH = 4
NEG = -1e30
BIG = 1e9
LN_EPS = 1e-5
RMS_EPS = 1e-6
ALPHA = (2 * DEPTH) ** 0.25

N_BUCKETS = 32
REL_MAX_DIST = 2048

H_A = 12
NOPE = 64
ROPE_DIM = 32
MLA_V = 64
Q_LORA = 256
KV_LORA = 128
ROPE_THETA = 10000.0

DIL_PAIRS = ((128, 1), (512, 4), (2048, 16))
H_B_GROUP = 4
H_B = 12
HD_B = 64

H_C = 8
G_C = 2
R_C = 4
DK_C = 64
CMP_LEN = 32
CMP_STRIDE = 16
CMP_HID = 64
SEL_BLOCK = 64
SEL_TOP = 16
WIN = 512

H_D = 4
DD = 64

N_EXPERTS = 64
TOP_K = 8
N_EXPERT_GROUPS = 8
TOPK_GROUPS = 4
D_EXPERT = 256
ROUTED_SCALE = 2.5

LANE = 128
SUBLANES = 8
MOE_BM = 512
ATT_TQ = 512
ATT_TK = 256
VMEM_LIMIT = 56 * 1024 * 1024

_NT = (((1,), (1,)), ((), ()))


def _cparams(n_axes, vmem=None):
    return pltpu.CompilerParams(dimension_semantics=("arbitrary",) * n_axes, vmem_limit_bytes=vmem)


def _mm_body(x_ref, w_ref, o_ref, *, act, precision):
    y = jnp.dot(x_ref[...], w_ref[...], preferred_element_type=F32, precision=precision)
    if act == "sigmoid":
        y = jax.nn.sigmoid(y)
    o_ref[...] = y.astype(o_ref.dtype)


def _mm(x, w, out_dtype, tn, tm=1024, act=None, precision=None):
    m, k = x.shape
    nc = w.shape[1]
    tm = min(tm, m)
    return pl.pallas_call(
        functools.partial(_mm_body, act=act, precision=precision),
        grid=(m // tm, nc // tn),
        in_specs=[pl.BlockSpec((tm, k), lambda i, j: (i, 0)),
                  pl.BlockSpec((k, tn), lambda i, j: (0, j))],
        out_specs=pl.BlockSpec((tm, tn), lambda i, j: (i, j)),
        out_shape=jax.ShapeDtypeStruct((m, nc), out_dtype),
        compiler_params=_cparams(2, VMEM_LIMIT), name="proj",
    )(x, w)


def _proj_dilated_body(x_ref, w_ref, *rest, tm, gw3):
    o_refs, scr = rest[:-1], rest[-1]
    for gi, (_, dil) in enumerate(DIL_PAIRS):
        y = jnp.dot(x_ref[...], w_ref[:, gi * gw3:(gi + 1) * gw3], preferred_element_type=F32)
        rows = tm // dil
        for c in range(gw3 // LANE):
            scr[c] = y[:, c * LANE:(c + 1) * LANE]
        for r in range(dil):
            for c in range(gw3 // LANE):
                col = r * gw3 + c * LANE
                o_refs[gi][0, :, col:col + LANE] = scr[c, pl.ds(r, rows, stride=dil), :].astype(BF16)


def _proj_dilated(xb, w, bsz, seq, tm=512):
    n, k = xb.shape
    gw3 = w.shape[1] // len(DIL_PAIRS)
    spt = seq // tm
    return pl.pallas_call(
        functools.partial(_proj_dilated_body, tm=tm, gw3=gw3), grid=(n // tm,),
        in_specs=[pl.BlockSpec((tm, k), lambda i: (i, 0)), pl.BlockSpec(w.shape, lambda i: (0, 0))],
        out_specs=[pl.BlockSpec((1, tm // dil, dil * gw3), lambda i: (i // spt, i % spt, 0))
                   for _, dil in DIL_PAIRS],
        out_shape=[jax.ShapeDtypeStruct((bsz, seq // dil, dil * gw3), BF16) for _, dil in DIL_PAIRS],
        scratch_shapes=[pltpu.VMEM((gw3 // LANE, tm, LANE), F32)],
        compiler_params=_cparams(1, VMEM_LIMIT), name="proj_dilated",
    )(xb, w)


def _layer_norm(z, g, b):
    mu = jnp.mean(z, axis=-1, keepdims=True)
    zc = z - mu
    var = jnp.mean(zc * zc, axis=-1, keepdims=True)
    return zc * lax.rsqrt(var + LN_EPS) * g + b


def _out_ln_body(*refs, n_sum):
    a0 = refs[0][...].astype(F32)
    for r in refs[1:n_sum]:
        a0 = a0 + r[...].astype(F32)
    a1_ref, w0_ref, w1_ref, r_ref, g_ref, b_ref, of_ref, ob_ref = refs[n_sum:]
    y = jnp.dot(a0.astype(BF16), w0_ref[...], preferred_element_type=F32)
    y = y + jnp.dot(a1_ref[...], w1_ref[...], preferred_element_type=F32)
    out = _layer_norm(ALPHA * r_ref[...] + y, g_ref[...], b_ref[...])
    of_ref[...] = out
    ob_ref[...] = out.astype(BF16)


def _out_ln(a0s, a1, w0, w1, resid, g, b, tm=256):
    n, d = resid.shape
    k0, k1 = w0.shape[0], w1.shape[0]
    row = lambda i: (i, 0)
    fixed = lambda i: (0, 0)
    return pl.pallas_call(
        functools.partial(_out_ln_body, n_sum=len(a0s)),
        grid=(n // tm,),
        in_specs=[pl.BlockSpec((tm, k0), row)] * len(a0s) + [
            pl.BlockSpec((tm, k1), row), pl.BlockSpec((k0, d), fixed), pl.BlockSpec((k1, d), fixed),
            pl.BlockSpec((tm, d), row), pl.BlockSpec((1, d), fixed), pl.BlockSpec((1, d), fixed)],
        out_specs=[pl.BlockSpec((tm, d), row), pl.BlockSpec((tm, d), row)],
        out_shape=[jax.ShapeDtypeStruct((n, d), F32), jax.ShapeDtypeStruct((n, d), BF16)],
        compiler_params=_cparams(1), name="out_proj_ln",
    )(*a0s, a1, w0, w1, resid, g, b)


def _rms(x, g):
    return x * lax.rsqrt(jnp.mean(x * x, axis=-1, keepdims=True) + RMS_EPS) * g


def _mla_q_body(c_ref, g_ref, w_ref, wr_ref, cos_ref, sin_ref, o_ref):
    cn = _rms(c_ref[...], g_ref[...]).astype(BF16)
    a = jnp.dot(cn, w_ref[...], preferred_element_type=F32)
    r = jnp.dot(cn, wr_ref[...], preferred_element_type=F32)
    cos, sin = cos_ref[...], sin_ref[...]
    for h in range(H_A):
        sl = slice(h * LANE, (h + 1) * LANE)
        o_ref[:, sl] = (a[:, sl] * cos + r[:, sl] * sin).astype(o_ref.dtype)


def _mla_kv_body(c_ref, kr_ref, krr_ref, g_ref, wk_ref, wv_ref, cos_ref, sin_ref, k_ref, v_ref):
    cn = _rms(c_ref[...], g_ref[...]).astype(BF16)
    kn = jnp.dot(cn, wk_ref[...], preferred_element_type=F32)
    rope = kr_ref[...] * cos_ref[...] + krr_ref[...] * sin_ref[...]
    for h in range(H_A):
        sl = slice(h * LANE, (h + 1) * LANE)
        k_ref[:, sl] = (kn[:, sl] + rope).astype(k_ref.dtype)
    v_ref[...] = jnp.dot(cn, wv_ref[...], preferred_element_type=F32).astype(v_ref.dtype)


def _mla_up(h1, q_norm, wq, wq_rot, kv_norm, wk, wv, cos_q, sin_q, cos_k, sin_k, seq, tm=512):
    n = h1.shape[0]
    spt = seq // tm
    row = lambda c: (lambda i: (i, c))
    pos = lambda i: (i % spt, 0)
    fixed = lambda i: (0, 0)
    q_a = pl.pallas_call(
        _mla_q_body, grid=(n // tm,),
        in_specs=[pl.BlockSpec((tm, Q_LORA), row(0)), pl.BlockSpec((1, Q_LORA), fixed),
                  pl.BlockSpec(wq.shape, fixed), pl.BlockSpec(wq_rot.shape, fixed),
                  pl.BlockSpec((tm, LANE), pos), pl.BlockSpec((tm, LANE), pos)],
        out_specs=pl.BlockSpec((tm, H_A * LANE), row(0)),
        out_shape=jax.ShapeDtypeStruct((n, H_A * LANE), BF16),
        compiler_params=_cparams(1), name="mla_q_up",
    )(h1, q_norm, wq, wq_rot, cos_q, sin_q)
    k_a, v_a = pl.pallas_call(
        _mla_kv_body, grid=(n // tm,),
        in_specs=[pl.BlockSpec((tm, LANE), row(2)), pl.BlockSpec((tm, LANE), row(3)),
                  pl.BlockSpec((tm, LANE), row(4)), pl.BlockSpec((1, KV_LORA), fixed),
                  pl.BlockSpec(wk.shape, fixed), pl.BlockSpec(wv.shape, fixed),
                  pl.BlockSpec((tm, LANE), pos), pl.BlockSpec((tm, LANE), pos)],
        out_specs=[pl.BlockSpec((tm, H_A * LANE), row(0)), pl.BlockSpec((tm, H_A * MLA_V), row(0))],
        out_shape=[jax.ShapeDtypeStruct((n, H_A * LANE), BF16), jax.ShapeDtypeStruct((n, H_A * MLA_V), BF16)],
        compiler_params=_cparams(1), name="mla_kv_up",
    )(h1, h1, h1, kv_norm, wk, wv, cos_k, sin_k)
    return q_a, k_a, v_a


def _flash_body(*refs, T, TK, dq, dv, q_offs, k_offs, v_offs, scale, bias_mode, has_sel, has_gate,
                epilogue, want_lse, seg_tiles, nback, lam_init):
    R = T // TK
    it = iter(refs)
    q_ref, k_ref, v_ref = next(it), next(it), next(it)
    bias_ref = next(it) if bias_mode else None
    sel_ref = next(it) if has_sel else None
    gate_ref = next(it) if has_gate else None
    lam_ref, dn_ref = (next(it), next(it)) if epilogue == "diff" else (None, None)
    o_ref = next(it)
    lse_ref = next(it) if want_lse else None
    vt_scr = next(it)

    qi = pl.program_id(2)
    seq = v_ref.shape[1]

    @pl.when(qi == 0)
    def _():
        for c in range(seq // TK):
            vt_scr[:, c * TK:(c + 1) * TK] = v_ref[0, c * TK:(c + 1) * TK, :].astype(F32).T.astype(BF16)

    qfull = q_ref[0].astype(F32)
    fold_scale = math.frexp(scale)[0] == 0.5
    qts = [(qfull[:, off:off + dq] * (scale if fold_scale else 1.0)).T.astype(BF16) for off in q_offs]

    def qk(kc):
        kfull = k_ref[0, pl.ds(pl.multiple_of(kc * TK, TK), TK), :]
        return tuple(jnp.dot(kfull[:, k_offs[u]:k_offs[u] + dq], qts[u], preferred_element_type=F32)
                     for u in range(2))

    def update(kc, state, scores, diag):
        start = pl.multiple_of(kc * TK, TK)
        new_state = []
        sel_add = None
        if has_sel:
            per = TK // SEL_BLOCK
            rows = [sel_ref[0, 0, pl.ds(kc * per + a, 1), :] for a in range(per)]
            sel_add = jnp.concatenate([jnp.broadcast_to((r - 1.0) * (-NEG), (SEL_BLOCK, T)) for r in rows], axis=0)
        for u in range(2):
            vt = vt_scr[v_offs[u]:v_offs[u] + dv, pl.ds(start, TK)]
            s = scores[u]
            if not fold_scale:
                s = s * scale
            if bias_mode:
                s = s + bias_ref[u if bias_mode == "pair" else 0, R * qi - kc + (R - 1)]
            elif diag is not None:
                key = lax.broadcasted_iota(jnp.int32, (TK, T), 0) + diag * TK
                qry = lax.broadcasted_iota(jnp.int32, (TK, T), 1)
                s = jnp.where(key <= qry, s, NEG)
            if has_sel:
                s = s + sel_add
            m_prev, l_prev, acc_prev = state[u]
            m_new = jnp.maximum(m_prev, jnp.max(s, axis=0, keepdims=True))
            alpha = jnp.exp(m_prev - m_new)
            p = jnp.exp(s - m_new)
            l_new = alpha * l_prev + jnp.sum(p, axis=0, keepdims=True)
            acc_new = alpha * acc_prev + jnp.dot(vt, p.astype(BF16), preferred_element_type=F32)
            new_state.append((m_new, l_new, acc_new))
        return tuple(new_state)

    init = tuple((jnp.full((1, T), NEG, F32), jnp.zeros((1, T), F32), jnp.zeros((dv, T), F32)) for _ in range(2))
    lo = 0 if nback is None else jnp.maximum(R * qi - nback, 0)

    def step(kc, carry):
        state, scores = carry
        nxt = qk(kc + 1)
        return update(kc, state, scores, None), nxt

    state, scores = lax.fori_loop(lo, R * qi, step, (init, qk(lo)))
    for a in range(R):
        nxt = qk(R * qi + a + 1) if a + 1 < R else None
        state = update(R * qi + a, state, scores, a)
        scores = nxt

    outs = [acc / l for _, l, acc in state]
    if epilogue == "diff":
        a = outs[0] - lam_ref[0, 0] * outs[1]
        rinv = lax.rsqrt(jnp.mean(a * a, axis=0, keepdims=True) + RMS_EPS)
        o = (a * rinv * dn_ref[...] * (1.0 - lam_init)).T
    else:
        o = jnp.concatenate(outs, axis=0).T
        if has_gate:
            o = o * gate_ref[0]
    o_ref[0] = o.astype(o_ref.dtype)
    if want_lse:
        lse_ref[0, 0] = jnp.concatenate([m + jnp.log(l) for m, l, _ in state], axis=0)


def _flash(q, k, v, *, n_outer, T, dq, dv, q_col, k_col, v_col, q_offs, k_offs, v_offs, scale,
           out_cols, out_col, bias=None, bias_mode=None, bias_idx=None, sel=None, sel_idx=None,
           gate=None, gate_col=None, lam=None, dnorm=None, lam_init=0.0, epilogue="plain", want_lse=False,
           seg_tiles=None, nback=None, vmem=None, k_w=None, v_w=None, name="flash", TK=None,
           rep=1, rep_in=0, rep_out=0):
    bsz, seq, _ = q.shape
    TK = TK or T
    assert T % TK == 0
    nq = seq // T
    qw = max(o + dq for o in q_offs)
    kw = k_w or max(o + dq for o in k_offs)
    vw = v_w or max(o + dv for o in v_offs)
    ow = dv if epilogue == "diff" else 2 * dv
    in_specs = [pl.BlockSpec((1, T, qw), lambda g, b, i: (b // rep, i, q_col(g) + (b % rep) * rep_in)),
                pl.BlockSpec((1, seq, kw), lambda g, b, i: (b // rep, 0, k_col(g) + (b % rep) * rep_in)),
                pl.BlockSpec((1, seq, vw), lambda g, b, i: (b // rep, 0, v_col(g) + (b % rep) * rep_in))]
    args = [q, k, v]
    if bias_mode:
        nb = 2 if bias_mode == "pair" else 1
        in_specs.append(pl.BlockSpec((nb,) + bias.shape[1:], lambda g, b, i: (bias_idx(g), 0, 0, 0)))
        args.append(bias)
    if sel is not None:
        in_specs.append(pl.BlockSpec((1, 1, sel.shape[2], T), lambda g, b, i: (b, sel_idx(g), 0, i)))
        args.append(sel)
    if gate is not None:
        in_specs.append(pl.BlockSpec((1, T, ow), lambda g, b, i: (b, i, gate_col(g))))
        args.append(gate)
    if epilogue == "diff":
        in_specs.append(pl.BlockSpec(memory_space=pltpu.SMEM))
        in_specs.append(pl.BlockSpec((dv, 1), lambda g, b, i: (0, 0)))
        args += [lam, dnorm]
    out_specs = [pl.BlockSpec((1, T, ow), lambda g, b, i: (b // rep, i, out_col(g) + (b % rep) * rep_out))]
    out_shape = [jax.ShapeDtypeStruct((bsz, seq, out_cols), BF16)]
    if want_lse:
        out_specs.append(pl.BlockSpec((1, 1, 2, T), lambda g, b, i: (b, g, 0, i)))
        out_shape.append(jax.ShapeDtypeStruct((bsz * rep, n_outer, 2, seq), F32))
    body = functools.partial(
        _flash_body, T=T, TK=TK, dq=dq, dv=dv, q_offs=q_offs, k_offs=k_offs, v_offs=v_offs, scale=scale,
        bias_mode=bias_mode, has_sel=sel is not None, has_gate=gate is not None, epilogue=epilogue,
        want_lse=want_lse, seg_tiles=seg_tiles, nback=nback, lam_init=lam_init)
    res = pl.pallas_call(
        body, grid=(n_outer, bsz * rep, nq), in_specs=in_specs, out_specs=out_specs, out_shape=out_shape,
        scratch_shapes=[pltpu.VMEM((vw, seq), BF16)],
        compiler_params=_cparams(3, vmem), name=name,
    )(*args)
    return res if want_lse else res[0]


def _rel_bucket(dist):
    n = jnp.maximum(dist, 0)
    exact = N_BUCKETS // 2
    log_ratio = jnp.log(jnp.maximum(n, 1).astype(F32) / exact) / math.log(REL_MAX_DIST / exact)
    large = exact + (log_ratio * (N_BUCKETS - exact)).astype(jnp.int32)
    return jnp.where(n < exact, n, jnp.minimum(large, N_BUCKETS - 1))


def _toeplitz_bias(tab, T, n_d, dist_scale, max_dist, TK=None):
    TK = TK or T
    R = T // TK
    wlen = T + TK
    x = jnp.arange(wlen)
    dist = (jnp.arange(n_d)[:, None] - (R - 1)) * TK + jnp.where(x < T, x, x - wlen)[None, :]
    w = tab[_rel_bucket(dist * dist_scale)].astype(F32)
    w = jnp.where(((dist >= 0) & (dist <= max_dist))[..., None], w, NEG)
    return _toeplitz(jnp.moveaxis(w, -1, 0), TK, T)


def _toeplitz_body(w_ref, o_ref, *, rows, cols):
    x = jnp.broadcast_to(w_ref[0], (rows, w_ref.shape[-1]))
    o_ref[0] = pltpu.roll(x, 0, 1, stride=1, stride_axis=0)[:, :cols]


def _toeplitz(w, rows, cols):
    wlen = w.shape[-1]
    w2 = w.reshape(-1, 1, wlen)
    out = pl.pallas_call(
        functools.partial(_toeplitz_body, rows=rows, cols=cols), grid=(w2.shape[0],),
        in_specs=[pl.BlockSpec((1, 1, wlen), lambda i: (i, 0, 0))],
        out_specs=pl.BlockSpec((1, rows, cols), lambda i: (i, 0, 0)),
        out_shape=jax.ShapeDtypeStruct((w2.shape[0], rows, cols), w.dtype),
        compiler_params=_cparams(1), name="toeplitz",
    )(w2)
    return out.reshape(w.shape[:-1] + (rows, cols))


def _compress_body(u_ref, pe_ref, w1_ref, w2_ref, o_ref, *, ncp):
    outs = []
    for a in range(2):
        u = u_ref[0, 0, a].astype(F32)
        p1 = jnp.dot(u + pe_ref[a, 0], w1_ref[a, 0], preferred_element_type=F32, precision=HI)
        p2 = jnp.dot(u + pe_ref[a, 1], w1_ref[a, 1], preferred_element_type=F32, precision=HI)
        hid = jax.nn.gelu(p1 + pltpu.roll(p2, ncp - 1, 0))
        outs.append(jnp.dot(hid, w2_ref[a], preferred_element_type=F32, precision=HI))
    o_ref[0, 0] = jnp.concatenate(outs, axis=-1)


def _compress(u, pe, w1, w2):
    bsz, g, _, ncp, width = u.shape
    return pl.pallas_call(
        functools.partial(_compress_body, ncp=ncp), grid=(bsz, g),
        in_specs=[pl.BlockSpec((1, 1, 2, ncp, width), lambda b, gg: (b, gg, 0, 0, 0)),
                  pl.BlockSpec(pe.shape, lambda b, gg: (0, 0, 0, 0)),
                  pl.BlockSpec(w1.shape, lambda b, gg: (0, 0, 0, 0)),
                  pl.BlockSpec(w2.shape, lambda b, gg: (0, 0, 0))],
        out_specs=pl.BlockSpec((1, 1, ncp, 2 * DK_C), lambda b, gg: (b, gg, 0, 0)),
        out_shape=jax.ShapeDtypeStruct((bsz, g, ncp, 2 * DK_C), F32),
        compiler_params=_cparams(2), name="nsa_compress",
    )(u, pe, w1, w2)


def _cmp_attn_body(q_ref, kv_ref, bias_ref, ov_ref, gate_ref, o_ref, sel_ref, *, T, ncp, n_sel, n_top, scale):
    qi = pl.program_id(2)
    kc = kv_ref[0, 0, :, :DK_C]
    vc_b = kv_ref[0, 0, :, DK_C:].astype(BF16)
    t = qi * T + lax.broadcasted_iota(jnp.int32, (T, ncp), 0)
    c = lax.broadcasted_iota(jnp.int32, (T, ncp), 1)
    valid = t >= c * CMP_STRIDE + (CMP_LEN - 1)
    validf = valid.astype(F32)
    psum = jnp.zeros((T, ncp), F32)
    outs = []
    for r in range(R_C):
        q = q_ref[0, :, r * DK_C:(r + 1) * DK_C].astype(F32)
        s = lax.dot_general(q, kc, _NT, preferred_element_type=F32, precision=HI) * scale + bias_ref[r]
        s = jnp.where(valid, s, NEG)
        e = jnp.exp(s - jnp.max(s, axis=-1, keepdims=True)) * validf
        p = e / jnp.maximum(jnp.sum(e, axis=-1, keepdims=True), 1e-30)
        outs.append(jnp.dot(p.astype(BF16), vc_b, preferred_element_type=F32))
        psum = psum + p
    o_ref[0] = (jnp.concatenate(outs, axis=-1) * gate_ref[0]).astype(o_ref.dtype)

    imp = jnp.dot(ov_ref[...], psum.T, preferred_element_type=F32, precision=HI)
    tq = qi * T + lax.broadcasted_iota(jnp.int32, (n_sel, T), 1)
    j = lax.broadcasted_iota(jnp.int32, (n_sel, T), 0)
    forced = (j == tq // SEL_BLOCK) | (j == 0)
    work = jnp.where(forced, BIG, jnp.where(j * SEL_BLOCK <= tq, imp, -BIG))
    sel = jnp.zeros((n_sel, T), F32)
    jf = j.astype(F32)
    for _ in range(n_top):
        _, _, pick = _first_max(work, jf, n_sel)
        sel = jnp.where(pick, 1.0, sel)
        work = jnp.where(pick, -jnp.inf, work)
    sel_ref[0, 0] = sel


def _cmp_attn(h, kvc, bias_c, overlap, gates, T=256):
    bsz, seq, _ = h.shape
    ncp = kvc.shape[2]
    n_sel = seq // SEL_BLOCK
    n_top = min(SEL_TOP, n_sel)
    qw = R_C * DK_C
    return pl.pallas_call(
        functools.partial(_cmp_attn_body, T=T, ncp=ncp, n_sel=n_sel, n_top=n_top, scale=DK_C ** -0.5),
        grid=(G_C, bsz, seq // T),
        in_specs=[pl.BlockSpec((1, T, qw), lambda g, b, i: (b, i, g)),
                  pl.BlockSpec((1, 1, ncp, 2 * DK_C), lambda g, b, i: (b, g, 0, 0)),
                  pl.BlockSpec((R_C, T, ncp), lambda g, b, i: (g, i, 0)),
                  pl.BlockSpec(overlap.shape, lambda g, b, i: (0, 0)),
                  pl.BlockSpec((1, T, qw), lambda g, b, i: (b, i, g))],
        out_specs=[pl.BlockSpec((1, T, qw), lambda g, b, i: (b, i, g)),
                   pl.BlockSpec((1, 1, n_sel, T), lambda g, b, i: (b, g, 0, i))],
        out_shape=[jax.ShapeDtypeStruct((bsz, seq, H_C * DK_C), BF16),
                   jax.ShapeDtypeStruct((bsz, G_C, n_sel, seq), F32)],
        compiler_params=_cparams(3), name="nsa_cmp_attn",
    )(h, kvc, bias_c, overlap, gates)


def _first_max(work, idx, n):
    mx = jnp.max(work, axis=0, keepdims=True)
    first = jnp.min(jnp.where(work == mx, idx, float(n)), axis=0, keepdims=True)
    return mx, first, idx == first


def _router_body(x_ref, wt_ref, b_ref, tri_ref, e_ref, g_ref, r_ref, cnt_ref, carry_scr, *, tm):
    i = pl.program_id(0)

    @pl.when(i == 0)
    def _():
        carry_scr[...] = jnp.zeros(carry_scr.shape, F32)

    st = lax.dot_general(wt_ref[...], x_ref[...], _NT, preferred_element_type=F32, precision=HI)
    scores = jax.nn.sigmoid(st)
    sel = scores + b_ref[...]
    per = N_EXPERTS // N_EXPERT_GROUPS
    fiota = lambda rows: lax.broadcasted_iota(jnp.int32, (rows, tm), 0).astype(F32)
    i_per, i_grp, i_exp = fiota(per), fiota(N_EXPERT_GROUPS), fiota(N_EXPERTS)
    grp_scores = []
    for g in range(N_EXPERT_GROUPS):
        blk = sel[g * per:(g + 1) * per]
        m1, _, pick = _first_max(blk, i_per, per)
        grp_scores.append(m1 + jnp.max(jnp.where(pick, -jnp.inf, blk), axis=0, keepdims=True))
    work = jnp.concatenate(grp_scores, axis=0)
    gmask = jnp.zeros((N_EXPERT_GROUPS, tm), F32)
    for _ in range(TOPK_GROUPS):
        _, _, pick = _first_max(work, i_grp, N_EXPERT_GROUPS)
        gmask = jnp.where(pick, 1.0, gmask)
        work = jnp.where(pick, -jnp.inf, work)
    work = jnp.concatenate([jnp.where(gmask[g:g + 1] > 0.5, sel[g * per:(g + 1) * per], NEG)
                            for g in range(N_EXPERT_GROUPS)], axis=0)
    picks, firsts, vals = [], [], []
    for _ in range(TOP_K):
        _, first, pick = _first_max(work, i_exp, N_EXPERTS)
        picks.append(pick)
        firsts.append(first)
        vals.append(jnp.sum(jnp.where(pick, scores, 0.0), axis=0, keepdims=True))
        work = jnp.where(pick, -jnp.inf, work)
    val = jnp.concatenate(vals, axis=0)
    g_ref[...] = val / jnp.sum(val, axis=0, keepdims=True) * ROUTED_SCALE
    e_ref[...] = jnp.concatenate(firsts, axis=0).astype(jnp.int32)
    onehot = picks[0].astype(F32)
    for pick in picks[1:]:
        onehot = onehot + pick.astype(F32)
    before = jnp.dot(onehot.astype(BF16), tri_ref[...], preferred_element_type=F32) + carry_scr[...]
    r_ref[...] = jnp.concatenate([jnp.sum(jnp.where(pick, before, 0.0), axis=0, keepdims=True)
                                  for pick in picks], axis=0).astype(jnp.int32)
    carry = carry_scr[...] + jnp.sum(onehot, axis=1, keepdims=True)
    carry_scr[...] = carry
    cnt_ref[...] = jnp.broadcast_to(carry, cnt_ref.shape)


def _router(xf, router_w, router_b, tm=256):
    n, d = xf.shape
    tri = (jnp.arange(tm)[:, None] < jnp.arange(tm)[None, :]).astype(BF16)
    col = lambda i: (0, i)
    fixed = lambda i: (0, 0)
    return pl.pallas_call(
        functools.partial(_router_body, tm=tm), grid=(n // tm,),
        in_specs=[pl.BlockSpec((tm, d), lambda i: (i, 0)), pl.BlockSpec((N_EXPERTS, d), fixed),
                  pl.BlockSpec((N_EXPERTS, 1), fixed), pl.BlockSpec((tm, tm), fixed)],
        out_specs=[pl.BlockSpec((TOP_K, tm), col), pl.BlockSpec((TOP_K, tm), col),
                   pl.BlockSpec((TOP_K, tm), col), pl.BlockSpec((N_EXPERTS, LANE), fixed)],
        out_shape=[jax.ShapeDtypeStruct((TOP_K, n), jnp.int32), jax.ShapeDtypeStruct((TOP_K, n), F32),
                   jax.ShapeDtypeStruct((TOP_K, n), jnp.int32), jax.ShapeDtypeStruct((N_EXPERTS, LANE), F32)],
        scratch_shapes=[pltpu.VMEM((N_EXPERTS, 1), F32)],
        compiler_params=_cparams(1), name="router",
    )(xf, router_w.T, router_b.astype(F32)[:, None], tri)


def _pack_rows(x):
    w = x.shape[-1] // 2
    lo = lax.bitcast_convert_type(x[:, :w].astype(BF16).astype(F32), jnp.uint32)
    hi = lax.bitcast_convert_type(x[:, w:].astype(BF16).astype(F32), jnp.uint32)
    return (lo >> 16) | (hi & jnp.uint32(0xFFFF0000))


def _unpack_rows(p):
    lo = lax.bitcast_convert_type(p << 16, F32)
    hi = lax.bitcast_convert_type(p & jnp.uint32(0xFFFF0000), F32)
    return lo, hi


def _dispatch_body(fill_ref, pos_ref, x_ref, z_ref, xs_hbm, xp_scr, sem, *, tm, n_blocks):
    i = pl.program_id(0)
    slot = i % 2
    xp_scr[slot] = _pack_rows(x_ref[...]).reshape(tm // SUBLANES, SUBLANES, -1)
    zsem = sem.at[2]

    def row_copy(s, grp, j, dst):
        return pltpu.make_async_copy(xp_scr.at[s, grp, pl.ds(j, 1)], xs_hbm.at[pl.ds(dst, 1)], sem.at[s])

    def zero_copy(dst):
        return pltpu.make_async_copy(z_ref.at[pl.ds(0, 1)], xs_hbm.at[pl.ds(dst, 1)], zsem)

    def zero_block(blk):
        return pltpu.make_async_copy(z_ref, xs_hbm.at[pl.ds(blk * MOE_BM, MOE_BM)], zsem)

    def issue(grp, c):
        for j in range(SUBLANES):
            for k in range(TOP_K):
                row_copy(slot, grp, j, pos_ref[k, grp * SUBLANES + j]).start(priority=k % 2)
        return c

    def drain(s):
        def body(grp, c):
            for _ in range(SUBLANES * TOP_K):
                row_copy(s, 0, 0, 0).wait()
            return c

        lax.fori_loop(0, tm // SUBLANES, body, 0)

    lax.fori_loop(0, tm // SUBLANES, issue, 0)

    @pl.when(i > 0)
    def _():
        drain(1 - slot)

    @pl.when(i == pl.num_programs(0) - 1)
    def _():
        drain(slot)

    @pl.when(i == 0)
    def _():
        def fill(e, c):
            lo, hi = fill_ref[2 * e], fill_ref[2 * e + 1]

            def start(r, cc):
                zero_copy(r).start()
                return cc

            def wait(r, cc):
                zero_copy(0).wait()
                return cc

            lax.fori_loop(lo, hi, start, 0)
            lax.fori_loop(lo, hi, wait, 0)
            return c

        lax.fori_loop(0, N_EXPERTS, fill, 0)

        def tail_start(blk, c):
            zero_block(blk).start()
            return c

        def tail_wait(blk, c):
            zero_block(0).wait()
            return c

        lax.fori_loop(fill_ref[2 * N_EXPERTS], n_blocks, tail_start, 0)
        lax.fori_loop(fill_ref[2 * N_EXPERTS], n_blocks, tail_wait, 0)


def _dispatch(xb, pos, fill, p, tm=128):
    n, d = xb.shape
    grid_spec = pltpu.PrefetchScalarGridSpec(
        num_scalar_prefetch=1, grid=(n // tm,),
        in_specs=[pl.BlockSpec((TOP_K, tm), lambda i, fl: (0, i), memory_space=pltpu.SMEM),
                  pl.BlockSpec((tm, d), lambda i, fl: (i, 0)),
                  pl.BlockSpec((MOE_BM, d // 2), lambda i, fl: (0, 0))],
        out_specs=pl.BlockSpec(memory_space=pl.ANY),
        scratch_shapes=[pltpu.VMEM((2, tm // SUBLANES, SUBLANES, d // 2), jnp.uint32),
                        pltpu.SemaphoreType.DMA((3,))])
    return pl.pallas_call(
        functools.partial(_dispatch_body, tm=tm, n_blocks=p // MOE_BM), grid_spec=grid_spec,
        out_shape=jax.ShapeDtypeStruct((p, d // 2), jnp.uint32),
        compiler_params=_cparams(1), name="dispatch",
    )(fill, pos, xb, jnp.zeros((MOE_BM, d // 2), jnp.uint32))


def _moe_ffn_body(be_ref, nb_ref, x_ref, wg_ref, wu_ref, wd_ref, o_ref, wg_scr, wu_scr, wd_scr):
    i = pl.program_id(0)
    half = x_ref.shape[1]

    @pl.when((i == 0) | (be_ref[i] != be_ref[jnp.maximum(i - 1, 0)]))
    def _():
        wg_scr[...] = wg_ref[0, 0].astype(BF16)
        wu_scr[...] = wu_ref[0, 0].astype(BF16)
        wd_scr[...] = wd_ref[0, 0].astype(BF16)

    @pl.when(i < nb_ref[0])
    def _():
        lo, hi = _unpack_rows(x_ref[...])
        lo, hi = lo.astype(BF16), hi.astype(BF16)
        hg = (jnp.dot(lo, wg_scr[:half], preferred_element_type=F32)
              + jnp.dot(hi, wg_scr[half:], preferred_element_type=F32))
        hu = (jnp.dot(lo, wu_scr[:half], preferred_element_type=F32)
              + jnp.dot(hi, wu_scr[half:], preferred_element_type=F32))
        hb = (hg * jax.nn.sigmoid(hg) * hu).astype(BF16)
        o_ref[...] = _pack_rows(jnp.dot(hb, wd_scr[...], preferred_element_type=F32))

    @pl.when(i >= nb_ref[0])
    def _():
        o_ref[...] = jnp.zeros(o_ref.shape, o_ref.dtype)


def _moe_ffn(xs, blk_e, n_used, wg, wu, wd, layer):
    p, half = xs.shape
    d = 2 * half
    n_blocks = p // MOE_BM
    grid_spec = pltpu.PrefetchScalarGridSpec(
        num_scalar_prefetch=2, grid=(n_blocks,),
        in_specs=[pl.BlockSpec((MOE_BM, half), lambda i, be, nb: (i, 0)),
                  pl.BlockSpec((1, 1, d, D_EXPERT), lambda i, be, nb: (layer, be[i], 0, 0)),
                  pl.BlockSpec((1, 1, d, D_EXPERT), lambda i, be, nb: (layer, be[i], 0, 0)),
                  pl.BlockSpec((1, 1, D_EXPERT, d), lambda i, be, nb: (layer, be[i], 0, 0))],
        out_specs=pl.BlockSpec((MOE_BM, half), lambda i, be, nb: (i, 0)),
        scratch_shapes=[pltpu.VMEM((d, D_EXPERT), BF16), pltpu.VMEM((d, D_EXPERT), BF16),
                        pltpu.VMEM((D_EXPERT, d), BF16)])
    return pl.pallas_call(
        _moe_ffn_body, grid_spec=grid_spec, out_shape=jax.ShapeDtypeStruct((p, half), jnp.uint32),
        compiler_params=_cparams(1), name="expert_ffn",
    )(blk_e, n_used, xs, wg, wu, wd)


def _combine_body(pos_ref, posn_ref, gate_ref, xb_ref, xf_ref, y_hbm, sg_ref, su_ref, sd_ref,
                  g_ref, b_ref, of_ref, ob_ref, buf, sem, *, tm):
    i = pl.program_id(0)
    slot = i % 2

    def row_copy(s, k, grp, j, src):
        return pltpu.make_async_copy(y_hbm.at[pl.ds(src, 1)], buf.at[s, k, grp, pl.ds(j, 1)], sem.at[s])

    def issue_tile(pr, s):
        def issue(grp, c):
            for j in range(SUBLANES):
                for k in range(TOP_K):
                    row_copy(s, k, grp, j, pr[k, grp * SUBLANES + j]).start(priority=k % 2)
            return c

        lax.fori_loop(0, tm // SUBLANES, issue, 0)

    @pl.when(i == 0)
    def _():
        issue_tile(pos_ref, 0)

    @pl.when(i + 1 < pl.num_programs(0))
    def _():
        issue_tile(posn_ref, 1 - slot)

    x = xb_ref[...]
    hg = jnp.dot(x, sg_ref[...], preferred_element_type=F32)
    hu = jnp.dot(x, su_ref[...], preferred_element_type=F32)
    hb = (hg * jax.nn.sigmoid(hg) * hu).astype(BF16)
    y = jnp.dot(hb, sd_ref[...], preferred_element_type=F32)

    def drain(grp, c):
        for _ in range(SUBLANES * TOP_K):
            row_copy(slot, 0, 0, 0, 0).wait()
        return c

    lax.fori_loop(0, tm // SUBLANES, drain, 0)
    gate = gate_ref[...]
    half = buf.shape[-1]
    y_lo, y_hi = y[:, :half], y[:, half:]
    for k in range(TOP_K):
        lo, hi = _unpack_rows(buf[slot, k].reshape(tm, half))
        y_lo = y_lo + gate[:, k:k + 1] * lo
        y_hi = y_hi + gate[:, k:k + 1] * hi
    out = _layer_norm(ALPHA * xf_ref[...] + jnp.concatenate([y_lo, y_hi], axis=-1), g_ref[...], b_ref[...])
    of_ref[...] = out
    ob_ref[...] = out.astype(BF16)


def _combine(pos, gate, xb, xf, y, sg, su, sd, g, b, tm=128):
    n, d = xf.shape
    n_tiles = n // tm
    row = lambda i: (i, 0)
    fixed = lambda i: (0, 0)
    smem_col = pl.BlockSpec((TOP_K, tm), lambda i: (0, i), memory_space=pltpu.SMEM)
    smem_next = pl.BlockSpec((TOP_K, tm), lambda i: (0, jnp.minimum(i + 1, n_tiles - 1)),
                             memory_space=pltpu.SMEM)
    grid_spec = pltpu.PrefetchScalarGridSpec(
        num_scalar_prefetch=0, grid=(n_tiles,),
        in_specs=[smem_col, smem_next, pl.BlockSpec((tm, TOP_K), row),
                  pl.BlockSpec((tm, d), row), pl.BlockSpec((tm, d), row), pl.BlockSpec(memory_space=pl.ANY),
                  pl.BlockSpec(sg.shape, fixed), pl.BlockSpec(su.shape, fixed), pl.BlockSpec(sd.shape, fixed),
                  pl.BlockSpec((1, d), fixed), pl.BlockSpec((1, d), fixed)],
        out_specs=[pl.BlockSpec((tm, d), row), pl.BlockSpec((tm, d), row)],
        scratch_shapes=[pltpu.VMEM((2, TOP_K, tm // SUBLANES, SUBLANES, d // 2), jnp.uint32),
                        pltpu.SemaphoreType.DMA((2,))])
    return pl.pallas_call(
        functools.partial(_combine_body, tm=tm), grid_spec=grid_spec,
        out_shape=[jax.ShapeDtypeStruct((n, d), F32), jax.ShapeDtypeStruct((n, d), BF16)],
        compiler_params=_cparams(1), name="combine",
    )(pos, pos, gate, xb, xf, y, sg, su, sd, g, b)


def _moe(xf, xb, router_w, router_b, w_gate, w_up, w_down, sh_gate, sh_up, sh_down, ln_g, ln_b, layer=0):
    n, d = xf.shape
    e_idx, gate, rank, cnt = _router(xf, router_w, router_b)
    counts = cnt[:, 0].astype(jnp.int32)
    padded = (counts + MOE_BM - 1) // MOE_BM * MOE_BM
    pad_end = jnp.cumsum(padded)
    pad_start = (pad_end - padded).astype(jnp.int32)
    n_blocks = (n * TOP_K + N_EXPERTS * (MOE_BM - 1) + MOE_BM - 1) // MOE_BM
    blk_e = jnp.minimum(jnp.sum(pad_end[None, :] <= (jnp.arange(n_blocks) * MOE_BM)[:, None], axis=1),
                        N_EXPERTS - 1).astype(jnp.int32)
    n_used = (pad_end[-1] // MOE_BM).astype(jnp.int32).reshape(1)
    fill = jnp.concatenate([jnp.stack([pad_start + counts, pad_end], axis=1).reshape(-1), n_used]).astype(jnp.int32)
    experts = jnp.arange(N_EXPERTS, dtype=jnp.int32)[:, None, None]
    pos = rank + jnp.sum(jnp.where(e_idx[None] == experts, pad_start[:, None, None], 0), axis=0)
    xs = _dispatch(xb, pos, fill, n_blocks * MOE_BM)
    y = _moe_ffn(xs, blk_e, n_used, w_gate, w_up, w_down, layer)
    return _combine(pos, gate.T, xb, xf, y, sh_gate, sh_up, sh_down, ln_g[None], ln_b[None])


def _rope_tables(seq):
    half = ROPE_DIM // 2
    freqs = ROPE_THETA ** (-jnp.arange(half, dtype=F32) / half)
    ang = jnp.arange(seq).astype(F32)[:, None] * freqs
    cos = jnp.concatenate([jnp.cos(ang)] * 2, -1)
    sin = jnp.concatenate([jnp.sin(ang)] * 2, -1)
    z = lambda w: jnp.zeros((seq, w), F32)
    pad = LANE - NOPE - ROPE_DIM
    cos_q = jnp.concatenate([jnp.ones((seq, NOPE), F32), cos, z(pad)], -1)
    sin_q = jnp.concatenate([z(NOPE), sin, z(pad)], -1)
    cos_k = jnp.concatenate([z(NOPE), cos, z(pad)], -1)
    return cos_q, sin_q, cos_k, sin_q


def _rot_cols(w):
    half = w.shape[-1] // 2
    return jnp.concatenate([-w[..., half:], w[..., :half]], -1)


def _mixer_ab(xb, xf, bsz, seq, w_in, q_norm, w_uq, kv_norm, w_ukv, w_out, ln_g, ln_b, rope_tabs, dil_bias):
    n = bsz * seq
    d = w_in.shape[0]
    c0 = Q_LORA + KV_LORA
    w_kr = w_in[:, c0:c0 + ROPE_DIM]
    zc = lambda w: jnp.zeros((d, w), F32)
    pad = LANE - NOPE - ROPE_DIM
    w1 = jnp.concatenate([w_in[:, :c0], zc(NOPE), w_kr, zc(pad), zc(NOPE), _rot_cols(w_kr), zc(pad)], 1)
    h1 = _mm(xb, w1.astype(BF16), F32, tn=w1.shape[1])
    gw = H_B_GROUP * HD_B
    w_b = w_in[:, c0 + ROPE_DIM:].reshape(d, 3, len(DIL_PAIRS), gw).transpose(0, 2, 1, 3).reshape(d, -1)
    h2 = _proj_dilated(xb, w_b.astype(BF16), bsz, seq)

    wq = w_uq.reshape(Q_LORA, H_A, NOPE + ROPE_DIM)
    zq = jnp.zeros((Q_LORA, H_A, pad), F32)
    wq_main = jnp.concatenate([wq, zq], -1).reshape(Q_LORA, H_A * LANE)
    wq_rot = jnp.concatenate([jnp.zeros((Q_LORA, H_A, NOPE), F32), _rot_cols(wq[..., NOPE:]), zq], -1)
    wq_rot = wq_rot.reshape(Q_LORA, H_A * LANE)
    wkv = w_ukv.reshape(KV_LORA, H_A, NOPE + MLA_V)
    wk = jnp.concatenate([wkv[..., :NOPE], jnp.zeros((KV_LORA, H_A, LANE - NOPE), F32)], -1)
    wk = wk.reshape(KV_LORA, H_A * LANE)
    wv = wkv[..., NOPE:].reshape(KV_LORA, H_A * MLA_V)
    q_a, k_a, v_a = _mla_up(h1, q_norm[None], wq_main.astype(BF16), wq_rot.astype(BF16), kv_norm[None],
                            wk.astype(BF16), wv.astype(BF16), *rope_tabs, seq)
    o_a = _flash(q_a.reshape(bsz, seq, -1), k_a.reshape(bsz, seq, -1), v_a.reshape(bsz, seq, -1), name="mla_attn",
                 n_outer=H_A // 2, T=min(ATT_TQ, seq), TK=ATT_TK, dq=LANE, dv=MLA_V,
                 q_col=lambda g: g, k_col=lambda g: g, v_col=lambda g: g,
                 q_offs=(0, LANE), k_offs=(0, LANE), v_offs=(0, MLA_V),
                 scale=(NOPE + ROPE_DIM) ** -0.5, out_cols=H_A * MLA_V, out_col=lambda g: g)

    gb = gw // LANE
    outs, lses = [], []
    for gi, (window, dil) in enumerate(DIL_PAIRS):
        L = seq // dil
        t = h2[gi]
        o, lse = _flash(t, t, t, name="dilated_attn", n_outer=2, T=min(256, L), dq=HD_B, dv=HD_B,
                        q_col=lambda g: g, k_col=lambda g: gb + g, v_col=lambda g: 2 * gb + g,
                        q_offs=(0, HD_B), k_offs=(0, HD_B), v_offs=(0, HD_B), scale=HD_B ** -0.5,
                        out_cols=dil * gw, out_col=lambda g: g, bias=dil_bias[gi], bias_mode="pair",
                        bias_idx=lambda g: g, want_lse=True, nback=1,
                        rep=dil, rep_in=3 * gb, rep_out=gb)
        outs.append(o.reshape(bsz, seq, H_B_GROUP, HD_B).astype(F32))
        lses.append(lse.reshape(bsz, dil, H_B_GROUP, L).transpose(0, 3, 1, 2).reshape(bsz, seq, H_B_GROUP))
    w = jax.nn.softmax(jnp.stack(lses), axis=0)
    o_b = jnp.sum(w[..., None] * jnp.stack(outs), axis=0).astype(BF16).reshape(n, gw)
    na = H_A * MLA_V
    return _out_ln([o_a.reshape(n, na)], o_b, w_out[:na].astype(BF16), w_out[na:].astype(BF16),
                   xf, ln_g[None], ln_b[None])


def _mixer_cd(xb, xf, bsz, seq, w_in, pos_k, k_w1, k_w2, pos_v, v_w1, v_w2, lq1, lk1, lq2, lk2, d_norm,
              w_out, ln_g, ln_b, lam_init, tabs):
    n = bsz * seq
    qc_w = H_C * DK_C
    kv_w = G_C * DK_C
    off = qc_w
    kvs = []
    for _ in range(3):
        wk_ = w_in[:, off:off + kv_w].reshape(-1, G_C, DK_C)
        wv_ = w_in[:, off + kv_w:off + 2 * kv_w].reshape(-1, G_C, DK_C)
        kvs.append(jnp.concatenate([wk_, wv_], -1).reshape(-1, 2 * kv_w))
        off += 2 * kv_w
    g_off = off
    d_off = off + 3 * H_C
    w_main = jnp.concatenate([w_in[:, :qc_w]] + kvs + [w_in[:, d_off:]], 1)
    h = _mm(xb, w_main.astype(BF16), BF16, tn=w_main.shape[1] // 2).reshape(bsz, seq, -1)
    w_g = jnp.repeat(w_in[:, g_off:d_off], DK_C, axis=1)
    gates = _mm(xb, w_g.astype(BF16), F32, tn=w_g.shape[1] // 2, act="sigmoid").reshape(bsz, seq, -1)

    ncp = seq // CMP_STRIDE
    half = CMP_STRIDE * DK_C
    kv_cmp = h[:, :, qc_w:qc_w + 2 * kv_w].reshape(bsz, ncp, CMP_STRIDE, G_C, 2, DK_C)
    u = kv_cmp.transpose(0, 3, 4, 1, 2, 5).reshape(bsz, G_C, 2, ncp, half)
    pe = jnp.stack([pos_k.reshape(2, 1, half), pos_v.reshape(2, 1, half)])
    w1 = jnp.stack([k_w1.reshape(2, half, CMP_HID), v_w1.reshape(2, half, CMP_HID)])
    w2 = jnp.stack([k_w2, v_w2])
    kvc = _compress(u, pe, w1, w2)
    o_cmp, sel = _cmp_attn(h, kvc, tabs["bias_c"], tabs["overlap"], gates)

    cb = qc_w // LANE
    scale = DK_C ** -0.5
    n_pairs = H_C // 2
    nsa = dict(n_outer=n_pairs, dq=DK_C, dv=DK_C, q_col=lambda g: g, q_offs=(0, DK_C), k_offs=(0, 0),
               v_offs=(DK_C, DK_C), scale=scale, out_cols=qc_w, out_col=lambda g: g, bias_mode="pair",
               bias_idx=lambda g: g, gate=gates, vmem=VMEM_LIMIT, k_w=LANE, v_w=LANE)
    o_sel = _flash(h, h, h, name="nsa_sel_attn", T=ATT_TQ, TK=ATT_TK, k_col=lambda g: cb + 2 + g // 2, v_col=lambda g: cb + 2 + g // 2,
                   bias=tabs["bias_sel"], sel=sel, sel_idx=lambda g: g // 2,
                   gate_col=lambda g: n_pairs + g, **nsa)
    o_win = _flash(h, h, h, name="nsa_win_attn", T=ATT_TQ, TK=ATT_TK, k_col=lambda g: cb + 4 + g // 2,
                   v_col=lambda g: cb + 4 + g // 2, bias=tabs["bias_win"], gate_col=lambda g: 2 * n_pairs + g,
                   nback=-(-(WIN - 1) // ATT_TK), **nsa)

    lam = (jnp.exp(jnp.sum(lq1.astype(F32) * lk1.astype(F32)))
           - jnp.exp(jnp.sum(lq2.astype(F32) * lk2.astype(F32))) + lam_init).reshape(1, 1)
    db = cb + 6
    o_d = _flash(h, h, h, name="diff_attn", n_outer=H_D, T=ATT_TQ, TK=ATT_TK, dq=DD, dv=2 * DD,
                 q_col=lambda g: db + g, k_col=lambda g: db + H_D + g, v_col=lambda g: db + 2 * H_D + g,
                 q_offs=(0, DD), k_offs=(0, DD), v_offs=(0, 0), scale=DD ** -0.5,
                 out_cols=H_D * 2 * DD, out_col=lambda g: g, bias=tabs["bias_d"], bias_mode="shared",
                 bias_idx=lambda g: g, lam=lam, dnorm=d_norm[:, None], lam_init=lam_init, epilogue="diff",
                 vmem=VMEM_LIMIT)
    r2 = lambda a: a.reshape(n, -1)
    return _out_ln([r2(o_cmp), r2(o_sel), r2(o_win)], r2(o_d), w_out[:qc_w].astype(BF16),
                   w_out[qc_w:].astype(BF16), xf, ln_g[None], ln_b[None])


def _nsa_tables(rel_bias, seq):
    tab_c = rel_bias[:, H_B:H_B + H_C]
    tab_d = rel_bias[:, H_B + H_C:H_B + H_C + H_D]
    ncp = seq // CMP_STRIDE
    n_sel = seq // SEL_BLOCK
    pos = jnp.arange(seq)
    x = jnp.arange(2 * ncp)
    c_minus_a = jnp.where(x < ncp, x, x - 2 * ncp)
    dist = -CMP_STRIDE * c_minus_a[None, :] + jnp.arange(CMP_STRIDE)[:, None] - (CMP_LEN - 1)
    w = jnp.moveaxis(tab_c[_rel_bucket(dist)].astype(F32), -1, 0)
    bias_c = _toeplitz(w, ncp, ncp).transpose(0, 2, 1, 3).reshape(H_C, seq, ncp)
    c0 = jnp.arange(ncp) * CMP_STRIDE
    s0 = jnp.arange(n_sel) * SEL_BLOCK
    overlap = jnp.maximum(jnp.minimum(c0[:, None] + CMP_LEN, s0[None, :] + SEL_BLOCK)
                          - jnp.maximum(c0[:, None], s0[None, :]), 0).astype(F32) / CMP_LEN
    return {
        "bias_c": bias_c, "overlap": overlap.T,
        "bias_sel": _toeplitz_bias(tab_c, ATT_TQ, seq // ATT_TK, 1, seq, ATT_TK),
        "bias_win": _toeplitz_bias(tab_c, ATT_TQ, ATT_TQ // ATT_TK + -(-(WIN - 1) // ATT_TK), 1, WIN - 1, ATT_TK),
        "bias_d": _toeplitz_bias(tab_d, ATT_TQ, seq // ATT_TK, 1, seq, ATT_TK),
    }


def kernel(x, rel_bias, ab_w_in, mla_q_norm, mla_w_uq, mla_kv_norm, mla_w_ukv, ab_w_out, cd_w_in, nsa_cmp_pos_k, nsa_cmp_k_w1, nsa_cmp_k_w2, nsa_cmp_pos_v, nsa_cmp_v_w1, nsa_cmp_v_w2, diff_lambda_q1, diff_lambda_k1, diff_lambda_q2, diff_lambda_k2, diff_norm, cd_w_out, ln1_g, ln1_b, ln2_g, ln2_b, router_w, router_b, exp_w_gate, exp_w_up, exp_w_down, sh_w_gate, sh_w_up, sh_w_down):
    bsz, seq, d = x.shape
    n = bsz * seq
    depth = ln1_g.shape[0]
    rope_tabs = _rope_tables(seq)
    dil_bias = [_toeplitz_bias(rel_bias[:, gi * H_B_GROUP:(gi + 1) * H_B_GROUP], min(256, seq // dil), 2, dil,
                               window // dil) for gi, (window, dil) in enumerate(DIL_PAIRS)]
    nsa_tabs = _nsa_tables(rel_bias, seq)
    xf = x.reshape(n, d)
    xb = xf.astype(BF16)
    for l in range(depth):
        i = l // 2
        if l % 2 == 0:
            xf, xb = _mixer_ab(xb, xf, bsz, seq, ab_w_in[i], mla_q_norm[i], mla_w_uq[i], mla_kv_norm[i],
                               mla_w_ukv[i], ab_w_out[i], ln1_g[l], ln1_b[l], rope_tabs, dil_bias)
        else:
            lam_init = 0.8 - 0.6 * math.exp(-0.3 * l)
            xf, xb = _mixer_cd(xb, xf, bsz, seq, cd_w_in[i], nsa_cmp_pos_k[i], nsa_cmp_k_w1[i],
                               nsa_cmp_k_w2[i], nsa_cmp_pos_v[i], nsa_cmp_v_w1[i], nsa_cmp_v_w2[i],
                               diff_lambda_q1[i], diff_lambda_k1[i], diff_lambda_q2[i], diff_lambda_k2[i],
                               diff_norm[i], cd_w_out[i], ln1_g[l], ln1_b[l], lam_init, nsa_tabs)
        xf, xb = _moe(xf, xb, router_w[l], router_b[l], exp_w_gate, exp_w_up, exp_w_down,
                      sh_w_gate[l].astype(BF16), sh_w_up[l].astype(BF16), sh_w_down[l].astype(BF16),
                      ln2_g[l], ln2_b[l], layer=l)
    return xf.reshape(bsz, seq, d)
```

```python
import functools
import math

import jax
import jax.numpy as jnp
from jax import lax
from jax.experimental import pallas as pl
from jax.experimental.pallas import tpu as pltpu

F32 = jnp.float32
BF16 = jnp.bfloat16
HI = lax.Precision.HIGHEST

DEPTH = 4
NEG = -1e30
BIG = 1e9
LN_EPS = 1e-5
RMS_EPS = 1e-6
ALPHA = (2 * DEPTH) ** 0.25

N_BUCKETS = 32
REL_MAX_DIST = 2048

H_A = 12
NOPE = 64
ROPE_DIM = 32
MLA_V = 64
Q_LORA = 256
KV_LORA = 128
ROPE_THETA = 10000.0

DIL_PAIRS = ((128, 1), (512, 4), (2048, 16))
H_B_GROUP = 4
H_B = 12
HD_B = 64

H_C = 8
G_C = 2
R_C = 4
DK_C = 64
CMP_LEN = 32
CMP_STRIDE = 16
CMP_HID = 64
SEL_BLOCK = 64
SEL_TOP = 16
WIN = 512

H_D = 4
DD = 64

N_EXPERTS = 64
TOP_K = 8
N_EXPERT_GROUPS = 8
TOPK_GROUPS = 4
D_EXPERT = 256
ROUTED_SCALE = 2.5

LANE = 128
SUBLANES = 8
MOE_BM = 512
ATT_TQ = 512
ATT_TK = 256
VMEM_LIMIT = 56 * 1024 * 1024

_NT = (((1,), (1,)), ((), ()))


def _cparams(n_axes, vmem=None):
    return pltpu.CompilerParams(dimension_semantics=("arbitrary",) * n_axes, vmem_limit_bytes=vmem)


def _mm_body(x_ref, w_ref, o_ref, *, act, precision):
    y = jnp.dot(x_ref[...], w_ref[...], preferred_element_type=F32, precision=precision)
    if act == "sigmoid":
        y = jax.nn.sigmoid(y)
    o_ref[...] = y.astype(o_ref.dtype)


def _mm(x, w, out_dtype, tn, tm=1024, act=None, precision=None):
    m, k = x.shape
    nc = w.shape[1]
    tm = min(tm, m)
    return pl.pallas_call(
        functools.partial(_mm_body, act=act, precision=precision),
        grid=(m // tm, nc // tn),
        in_specs=[pl.BlockSpec((tm, k), lambda i, j: (i, 0)),
                  pl.BlockSpec((k, tn), lambda i, j: (0, j))],
        out_specs=pl.BlockSpec((tm, tn), lambda i, j: (i, j)),
        out_shape=jax.ShapeDtypeStruct((m, nc), out_dtype),
        compiler_params=_cparams(2, VMEM_LIMIT), name="proj",
    )(x, w)


def _proj_dilated_body(x_ref, w_ref, *rest, tm, gw3):
    o_refs, scr = rest[:-1], rest[-1]
    for gi, (_, dil) in enumerate(DIL_PAIRS):
        y = jnp.dot(x_ref[...], w_ref[:, gi * gw3:(gi + 1) * gw3], preferred_element_type=F32)
        rows = tm // dil
        for c in range(gw3 // LANE):
            scr[c] = y[:, c * LANE:(c + 1) * LANE]
        for r in range(dil):
            for c in range(gw3 // LANE):
                col = r * gw3 + c * LANE
                o_refs[gi][0, :, col:col + LANE] = scr[c, pl.ds(r, rows, stride=dil), :].astype(BF16)


def _proj_dilated(xb, w, bsz, seq, tm=512):
    n, k = xb.shape
    gw3 = w.shape[1] // len(DIL_PAIRS)
    spt = seq // tm
    return pl.pallas_call(
        functools.partial(_proj_dilated_body, tm=tm, gw3=gw3), grid=(n // tm,),
        in_specs=[pl.BlockSpec((tm, k), lambda i: (i, 0)), pl.BlockSpec(w.shape, lambda i: (0, 0))],
        out_specs=[pl.BlockSpec((1, tm // dil, dil * gw3), lambda i: (i // spt, i % spt, 0))
                   for _, dil in DIL_PAIRS],
        out_shape=[jax.ShapeDtypeStruct((bsz, seq // dil, dil * gw3), BF16) for _, dil in DIL_PAIRS],
        scratch_shapes=[pltpu.VMEM((gw3 // LANE, tm, LANE), F32)],
        compiler_params=_cparams(1, VMEM_LIMIT), name="proj_dilated",
    )(xb, w)


def _layer_norm(z, g, b):
    mu = jnp.mean(z, axis=-1, keepdims=True)
    zc = z - mu
    var = jnp.mean(zc * zc, axis=-1, keepdims=True)
    return zc * lax.rsqrt(var + LN_EPS) * g + b


def _out_ln_body(*refs, n_sum):
    a0 = refs[0][...].astype(F32)
    for r in refs[1:n_sum]:
        a0 = a0 + r[...].astype(F32)
    a1_ref, w0_ref, w1_ref, r_ref, g_ref, b_ref, of_ref, ob_ref = refs[n_sum:]
    y = jnp.dot(a0.astype(BF16), w0_ref[...], preferred_element_type=F32)
    y = y + jnp.dot(a1_ref[...], w1_ref[...], preferred_element_type=F32)
    out = _layer_norm(ALPHA * r_ref[...] + y, g_ref[...], b_ref[...])
    of_ref[...] = out
    ob_ref[...] = out.astype(BF16)


def _out_ln(a0s, a1, w0, w1, resid, g, b, tm=256):
    n, d = resid.shape
    k0, k1 = w0.shape[0], w1.shape[0]
    row = lambda i: (i, 0)
    fixed = lambda i: (0, 0)
    return pl.pallas_call(
        functools.partial(_out_ln_body, n_sum=len(a0s)),
        grid=(n // tm,),
        in_specs=[pl.BlockSpec((tm, k0), row)] * len(a0s) + [
            pl.BlockSpec((tm, k1), row), pl.BlockSpec((k0, d), fixed), pl.BlockSpec((k1, d), fixed),
            pl.BlockSpec((tm, d), row), pl.BlockSpec((1, d), fixed), pl.BlockSpec((1, d), fixed)],
        out_specs=[pl.BlockSpec((tm, d), row), pl.BlockSpec((tm, d), row)],
        out_shape=[jax.ShapeDtypeStruct((n, d), F32), jax.ShapeDtypeStruct((n, d), BF16)],
        compiler_params=_cparams(1), name="out_proj_ln",
    )(*a0s, a1, w0, w1, resid, g, b)


def _rms(x, g):
    return x * lax.rsqrt(jnp.mean(x * x, axis=-1, keepdims=True) + RMS_EPS) * g


def _mla_q_body(c_ref, g_ref, w_ref, wr_ref, cos_ref, sin_ref, o_ref):
    cn = _rms(c_ref[...], g_ref[...]).astype(BF16)
    a = jnp.dot(cn, w_ref[...], preferred_element_type=F32)
    r = jnp.dot(cn, wr_ref[...], preferred_element_type=F32)
    cos, sin = cos_ref[...], sin_ref[...]
    for h in range(H_A):
        sl = slice(h * LANE, (h + 1) * LANE)
        o_ref[:, sl] = (a[:, sl] * cos + r[:, sl] * sin).astype(o_ref.dtype)


def _mla_kv_body(c_ref, kr_ref, krr_ref, g_ref, wk_ref, wv_ref, cos_ref, sin_ref, k_ref, v_ref):
    cn = _rms(c_ref[...], g_ref[...]).astype(BF16)
    kn = jnp.dot(cn, wk_ref[...], preferred_element_type=F32)
    rope = kr_ref[...] * cos_ref[...] + krr_ref[...] * sin_ref[...]
    for h in range(H_A):
        sl = slice(h * LANE, (h + 1) * LANE)
        k_ref[:, sl] = (kn[:, sl] + rope).astype(k_ref.dtype)
    v_ref[...] = jnp.dot(cn, wv_ref[...], preferred_element_type=F32).astype(v_ref.dtype)


def _mla_up(h1, q_norm, wq, wq_rot, kv_norm, wk, wv, cos_q, sin_q, cos_k, sin_k, seq, tm=512):
    n = h1.shape[0]
    spt = seq // tm
    row = lambda c: (lambda i: (i, c))
    pos = lambda i: (i % spt, 0)
    fixed = lambda i: (0, 0)
    q_a = pl.pallas_call(
        _mla_q_body, grid=(n // tm,),
        in_specs=[pl.BlockSpec((tm, Q_LORA), row(0)), pl.BlockSpec((1, Q_LORA), fixed),
                  pl.BlockSpec(wq.shape, fixed), pl.BlockSpec(wq_rot.shape, fixed),
                  pl.BlockSpec((tm, LANE), pos), pl.BlockSpec((tm, LANE), pos)],
        out_specs=pl.BlockSpec((tm, H_A * LANE), row(0)),
        out_shape=jax.ShapeDtypeStruct((n, H_A * LANE), BF16),
        compiler_params=_cparams(1), name="mla_q_up",
    )(h1, q_norm, wq, wq_rot, cos_q, sin_q)
    k_a, v_a = pl.pallas_call(
        _mla_kv_body, grid=(n // tm,),
        in_specs=[pl.BlockSpec((tm, LANE), row(2)), pl.BlockSpec((tm, LANE), row(3)),
                  pl.BlockSpec((tm, LANE), row(4)), pl.BlockSpec((1, KV_LORA), fixed),
                  pl.BlockSpec(wk.shape, fixed), pl.BlockSpec(wv.shape, fixed),
                  pl.BlockSpec((tm, LANE), pos), pl.BlockSpec((tm, LANE), pos)],
        out_specs=[pl.BlockSpec((tm, H_A * LANE), row(0)), pl.BlockSpec((tm, H_A * MLA_V), row(0))],
        out_shape=[jax.ShapeDtypeStruct((n, H_A * LANE), BF16), jax.ShapeDtypeStruct((n, H_A * MLA_V), BF16)],
        compiler_params=_cparams(1), name="mla_kv_up",
    )(h1, h1, h1, kv_norm, wk, wv, cos_k, sin_k)
    return q_a, k_a, v_a


def _flash_body(*refs, T, TK, dq, dv, q_offs, k_offs, v_offs, scale, bias_mode, has_sel, has_gate,
                epilogue, want_lse, seg_tiles, nback, lam_init):
    R = T // TK
    it = iter(refs)
    q_ref, k_ref, v_ref = next(it), next(it), next(it)
    bias_ref = next(it) if bias_mode else None
    sel_ref = next(it) if has_sel else None
    gate_ref = next(it) if has_gate else None
    lam_ref, dn_ref = (next(it), next(it)) if epilogue == "diff" else (None, None)
    o_ref = next(it)
    lse_ref = next(it) if want_lse else None
    vt_scr = next(it)

    qi = pl.program_id(2)
    seq = v_ref.shape[1]

    @pl.when(qi == 0)
    def _():
        for c in range(seq // TK):
            vt_scr[:, c * TK:(c + 1) * TK] = v_ref[0, c * TK:(c + 1) * TK, :].astype(F32).T.astype(BF16)

    qfull = q_ref[0].astype(F32)
    fold_scale = math.frexp(scale)[0] == 0.5
    qts = [(qfull[:, off:off + dq] * (scale if fold_scale else 1.0)).T.astype(BF16) for off in q_offs]

    def qk(kc):
        kfull = k_ref[0, pl.ds(pl.multiple_of(kc * TK, TK), TK), :]
        return tuple(jnp.dot(kfull[:, k_offs[u]:k_offs[u] + dq], qts[u], preferred_element_type=F32)
                     for u in range(2))

    def update(kc, state, scores, diag):
        start = pl.multiple_of(kc * TK, TK)
        new_state = []
        sel_add = None
        if has_sel:
            per = TK // SEL_BLOCK
            rows = [sel_ref[0, 0, pl.ds(kc * per + a, 1), :] for a in range(per)]
            sel_add = jnp.concatenate([jnp.broadcast_to((r - 1.0) * (-NEG), (SEL_BLOCK, T)) for r in rows], axis=0)
        for u in range(2):
            vt = vt_scr[v_offs[u]:v_offs[u] + dv, pl.ds(start, TK)]
            s = scores[u]
            if not fold_scale:
                s = s * scale
            if bias_mode:
                s = s + bias_ref[u if bias_mode == "pair" else 0, R * qi - kc + (R - 1)]
            elif diag is not None:
                key = lax.broadcasted_iota(jnp.int32, (TK, T), 0) + diag * TK
                qry = lax.broadcasted_iota(jnp.int32, (TK, T), 1)
                s = jnp.where(key <= qry, s, NEG)
            if has_sel:
                s = s + sel_add
            m_prev, l_prev, acc_prev = state[u]
            m_new = jnp.maximum(m_prev, jnp.max(s, axis=0, keepdims=True))
            alpha = jnp.exp(m_prev - m_new)
            p = jnp.exp(s - m_new)
            l_new = alpha * l_prev + jnp.sum(p, axis=0, keepdims=True)
            acc_new = alpha * acc_prev + jnp.dot(vt, p.astype(BF16), preferred_element_type=F32)
            new_state.append((m_new, l_new, acc_new))
        return tuple(new_state)

    init = tuple((jnp.full((1, T), NEG, F32), jnp.zeros((1, T), F32), jnp.zeros((dv, T), F32)) for _ in range(2))
    lo = 0 if nback is None else jnp.maximum(R * qi - nback, 0)

    def step(kc, carry):
        state, scores = carry
        nxt = qk(kc + 1)
        return update(kc, state, scores, None), nxt

    state, scores = lax.fori_loop(lo, R * qi, step, (init, qk(lo)))
    for a in range(R):
        nxt = qk(R * qi + a + 1) if a + 1 < R else None
        state = update(R * qi + a, state, scores, a)
        scores = nxt

    outs = [acc / l for _, l, acc in state]
    if epilogue == "diff":
        a = outs[0] - lam_ref[0, 0] * outs[1]
        rinv = lax.rsqrt(jnp.mean(a * a, axis=0, keepdims=True) + RMS_EPS)
        o = (a * rinv * dn_ref[...] * (1.0 - lam_init)).T
    else:
        o = jnp.concatenate(outs, axis=0).T
        if has_gate:
            o = o * gate_ref[0]
    o_ref[0] = o.astype(o_ref.dtype)
    if want_lse:
        lse_ref[0, 0] = jnp.concatenate([m + jnp.log(l) for m, l, _ in state], axis=0)


def _flash(q, k, v, *, n_outer, T, dq, dv, q_col, k_col, v_col, q_offs, k_offs, v_offs, scale,
           out_cols, out_col, bias=None, bias_mode=None, bias_idx=None, sel=None, sel_idx=None,
           gate=None, gate_col=None, lam=None, dnorm=None, lam_init=0.0, epilogue="plain", want_lse=False,
           seg_tiles=None, nback=None, vmem=None, k_w=None, v_w=None, name="flash", TK=None,
           rep=1, rep_in=0, rep_out=0):
    bsz, seq, _ = q.shape
    TK = TK or T
    assert T % TK == 0
    nq = seq // T
    qw = max(o + dq for o in q_offs)
    kw = k_w or max(o + dq for o in k_offs)
    vw = v_w or max(o + dv for o in v_offs)
    ow = dv if epilogue == "diff" else 2 * dv
    in_specs = [pl.BlockSpec((1, T, qw), lambda g, b, i: (b // rep, i, q_col(g) + (b % rep) * rep_in)),
                pl.BlockSpec((1, seq, kw), lambda g, b, i: (b // rep, 0, k_col(g) + (b % rep) * rep_in)),
                pl.BlockSpec((1, seq, vw), lambda g, b, i: (b // rep, 0, v_col(g) + (b % rep) * rep_in))]
    args = [q, k, v]
    if bias_mode:
        nb = 2 if bias_mode == "pair" else 1
        in_specs.append(pl.BlockSpec((nb,) + bias.shape[1:], lambda g, b, i: (bias_idx(g), 0, 0, 0)))
        args.append(bias)
    if sel is not None:
        in_specs.append(pl.BlockSpec((1, 1, sel.shape[2], T), lambda g, b, i: (b, sel_idx(g), 0, i)))
        args.append(sel)
    if gate is not None:
        in_specs.append(pl.BlockSpec((1, T, ow), lambda g, b, i: (b, i, gate_col(g))))
        args.append(gate)
    if epilogue == "diff":
        in_specs.append(pl.BlockSpec(memory_space=pltpu.SMEM))
        in_specs.append(pl.BlockSpec((dv, 1), lambda g, b, i: (0, 0)))
        args += [lam, dnorm]
    out_specs = [pl.BlockSpec((1, T, ow), lambda g, b, i: (b // rep, i, out_col(g) + (b % rep) * rep_out))]
    out_shape = [jax.ShapeDtypeStruct((bsz, seq, out_cols), BF16)]
    if want_lse:
        out_specs.append(pl.BlockSpec((1, 1, 2, T), lambda g, b, i: (b, g, 0, i)))
        out_shape.append(jax.ShapeDtypeStruct((bsz * rep, n_outer, 2, seq), F32))
    body = functools.partial(
        _flash_body, T=T, TK=TK, dq=dq, dv=dv, q_offs=q_offs, k_offs=k_offs, v_offs=v_offs, scale=scale,
        bias_mode=bias_mode, has_sel=sel is not None, has_gate=gate is not None, epilogue=epilogue,
        want_lse=want_lse, seg_tiles=seg_tiles, nback=nback, lam_init=lam_init)
    res = pl.pallas_call(
        body, grid=(n_outer, bsz * rep, nq), in_specs=in_specs, out_specs=out_specs, out_shape=out_shape,
        scratch_shapes=[pltpu.VMEM((vw, seq), BF16)],
        compiler_params=_cparams(3, vmem), name=name,
    )(*args)
    return res if want_lse else res[0]


def _rel_bucket(dist):
    n = jnp.maximum(dist, 0)
    exact = N_BUCKETS // 2
    log_ratio = jnp.log(jnp.maximum(n, 1).astype(F32) / exact) / math.log(REL_MAX_DIST / exact)
    large = exact + (log_ratio * (N_BUCKETS - exact)).astype(jnp.int32)
    return jnp.where(n < exact, n, jnp.minimum(large, N_BUCKETS - 1))


def _toeplitz_bias(tab, T, n_d, dist_scale, max_dist, TK=None):
    TK = TK or T
    R = T // TK
    wlen = T + TK
    x = jnp.arange(wlen)
    dist = (jnp.arange(n_d)[:, None] - (R - 1)) * TK + jnp.where(x < T, x, x - wlen)[None, :]
    w = tab[_rel_bucket(dist * dist_scale)].astype(F32)
    w = jnp.where(((dist >= 0) & (dist <= max_dist))[..., None], w, NEG)
    return _toeplitz(jnp.moveaxis(w, -1, 0), TK, T)


def _toeplitz_body(w_ref, o_ref, *, rows, cols):
    x = jnp.broadcast_to(w_ref[0], (rows, w_ref.shape[-1]))
    o_ref[0] = pltpu.roll(x, 0, 1, stride=1, stride_axis=0)[:, :cols]


def _toeplitz(w, rows, cols):
    wlen = w.shape[-1]
    w2 = w.reshape(-1, 1, wlen)
    out = pl.pallas_call(
        functools.partial(_toeplitz_body, rows=rows, cols=cols), grid=(w2.shape[0],),
        in_specs=[pl.BlockSpec((1, 1, wlen), lambda i: (i, 0, 0))],
        out_specs=pl.BlockSpec((1, rows, cols), lambda i: (i, 0, 0)),
        out_shape=jax.ShapeDtypeStruct((w2.shape[0], rows, cols), w.dtype),
        compiler_params=_cparams(1), name="toeplitz",
    )(w2)
    return out.reshape(w.shape[:-1] + (rows, cols))


def _compress_body(u_ref, pe_ref, w1_ref, w2_ref, o_ref, *, ncp):
    outs = []
    for a in range(2):
        u = u_ref[0, 0, a].astype(F32)
        p1 = jnp.dot(u + pe_ref[a, 0], w1_ref[a, 0], preferred_element_type=F32, precision=HI)
        p2 = jnp.dot(u + pe_ref[a, 1], w1_ref[a, 1], preferred_element_type=F32, precision=HI)
        hid = jax.nn.gelu(p1 + pltpu.roll(p2, ncp - 1, 0))
        outs.append(jnp.dot(hid, w2_ref[a], preferred_element_type=F32, precision=HI))
    o_ref[0, 0] = jnp.concatenate(outs, axis=-1)


def _compress(u, pe, w1, w2):
    bsz, g, _, ncp, width = u.shape
    return pl.pallas_call(
        functools.partial(_compress_body, ncp=ncp), grid=(bsz, g),
        in_specs=[pl.BlockSpec((1, 1, 2, ncp, width), lambda b, gg: (b, gg, 0, 0, 0)),
                  pl.BlockSpec(pe.shape, lambda b, gg: (0, 0, 0, 0)),
                  pl.BlockSpec(w1.shape, lambda b, gg: (0, 0, 0, 0)),
                  pl.BlockSpec(w2.shape, lambda b, gg: (0, 0, 0))],
        out_specs=pl.BlockSpec((1, 1, ncp, 2 * DK_C), lambda b, gg: (b, gg, 0, 0)),
        out_shape=jax.ShapeDtypeStruct((bsz, g, ncp, 2 * DK_C), F32),
        compiler_params=_cparams(2), name="nsa_compress",
    )(u, pe, w1, w2)


def _cmp_attn_body(q_ref, kv_ref, bias_ref, ov_ref, gate_ref, o_ref, sel_ref, *, T, ncp, n_sel, n_top, scale):
    qi = pl.program_id(2)
    kc = kv_ref[0, 0, :, :DK_C]
    vc = kv_ref[0, 0, :, DK_C:]
    t = qi * T + lax.broadcasted_iota(jnp.int32, (T, ncp), 0)
    c = lax.broadcasted_iota(jnp.int32, (T, ncp), 1)
    valid = t >= c * CMP_STRIDE + (CMP_LEN - 1)
    validf = valid.astype(F32)
    psum = jnp.zeros((T, ncp), F32)
    outs = []
    for r in range(R_C):
        q = q_ref[0, :, r * DK_C:(r + 1) * DK_C].astype(F32)
        s = lax.dot_general(q, kc, _NT, preferred_element_type=F32, precision=HI) * scale + bias_ref[r]
        s = jnp.where(valid, s, NEG)
        e = jnp.exp(s - jnp.max(s, axis=-1, keepdims=True)) * validf
        p = e / jnp.maximum(jnp.sum(e, axis=-1, keepdims=True), 1e-30)
        outs.append(jnp.dot(p, vc, preferred_element_type=F32, precision=HI))
        psum = psum + p
    o_ref[0] = (jnp.concatenate(outs, axis=-1) * gate_ref[0]).astype(o_ref.dtype)

    imp = jnp.dot(ov_ref[...], psum.T, preferred_element_type=F32, precision=HI)
    tq = qi * T + lax.broadcasted_iota(jnp.int32, (n_sel, T), 1)
    j = lax.broadcasted_iota(jnp.int32, (n_sel, T), 0)
    forced = (j == tq // SEL_BLOCK) | (j == 0)
    work = jnp.where(forced, BIG, jnp.where(j * SEL_BLOCK <= tq, imp, -BIG))
    sel = jnp.zeros((n_sel, T), F32)
    jf = j.astype(F32)
    for _ in range(n_top):
        _, _, pick = _first_max(work, jf, n_sel)
        sel = jnp.where(pick, 1.0, sel)
        work = jnp.where(pick, -jnp.inf, work)
    sel_ref[0, 0] = sel


def _cmp_attn(h, kvc, bias_c, overlap, gates, T=256):
    bsz, seq, _ = h.shape
    ncp = kvc.shape[2]
    n_sel = seq // SEL_BLOCK
    n_top = min(SEL_TOP, n_sel)
    qw = R_C * DK_C
    return pl.pallas_call(
        functools.partial(_cmp_attn_body, T=T, ncp=ncp, n_sel=n_sel, n_top=n_top, scale=DK_C ** -0.5),
        grid=(G_C, bsz, seq // T),
        in_specs=[pl.BlockSpec((1, T, qw), lambda g, b, i: (b, i, g)),
                  pl.BlockSpec((1, 1, ncp, 2 * DK_C), lambda g, b, i: (b, g, 0, 0)),
                  pl.BlockSpec((R_C, T, ncp), lambda g, b, i: (g, i, 0)),
                  pl.BlockSpec(overlap.shape, lambda g, b, i: (0, 0)),
                  pl.BlockSpec((1, T, qw), lambda g, b, i: (b, i, g))],
        out_specs=[pl.BlockSpec((1, T, qw), lambda g, b, i: (b, i, g)),
                   pl.BlockSpec((1, 1, n_sel, T), lambda g, b, i: (b, g, 0, i))],
        out_shape=[jax.ShapeDtypeStruct((bsz, seq, H_C * DK_C), BF16),
                   jax.ShapeDtypeStruct((bsz, G_C, n_sel, seq), F32)],
        compiler_params=_cparams(3), name="nsa_cmp_attn",
    )(h, kvc, bias_c, overlap, gates)


def _first_max(work, idx, n):
    mx = jnp.max(work, axis=0, keepdims=True)
    first = jnp.min(jnp.where(work == mx, idx, float(n)), axis=0, keepdims=True)
    return mx, first, idx == first


def _router_body(x_ref, wt_ref, b_ref, tri_ref, e_ref, g_ref, r_ref, cnt_ref, carry_scr, *, tm):
    i = pl.program_id(0)

    @pl.when(i == 0)
    def _():
        carry_scr[...] = jnp.zeros(carry_scr.shape, F32)

    st = lax.dot_general(wt_ref[...], x_ref[...], _NT, preferred_element_type=F32, precision=HI)
    scores = jax.nn.sigmoid(st)
    sel = scores + b_ref[...]
    per = N_EXPERTS // N_EXPERT_GROUPS
    fiota = lambda rows: lax.broadcasted_iota(jnp.int32, (rows, tm), 0).astype(F32)
    i_per, i_grp, i_exp = fiota(per), fiota(N_EXPERT_GROUPS), fiota(N_EXPERTS)
    grp_scores = []
    for g in range(N_EXPERT_GROUPS):
        blk = sel[g * per:(g + 1) * per]
        m1, _, pick = _first_max(blk, i_per, per)
        grp_scores.append(m1 + jnp.max(jnp.where(pick, -jnp.inf, blk), axis=0, keepdims=True))
    work = jnp.concatenate(grp_scores, axis=0)
    gmask = jnp.zeros((N_EXPERT_GROUPS, tm), F32)
    for _ in range(TOPK_GROUPS):
        _, _, pick = _first_max(work, i_grp, N_EXPERT_GROUPS)
        gmask = jnp.where(pick, 1.0, gmask)
        work = jnp.where(pick, -jnp.inf, work)
    work = jnp.concatenate([jnp.where(gmask[g:g + 1] > 0.5, sel[g * per:(g + 1) * per], NEG)
                            for g in range(N_EXPERT_GROUPS)], axis=0)
    picks, firsts, vals = [], [], []
    for _ in range(TOP_K):
        _, first, pick = _first_max(work, i_exp, N_EXPERTS)
        picks.append(pick)
        firsts.append(first)
        vals.append(jnp.sum(jnp.where(pick, scores, 0.0), axis=0, keepdims=True))
        work = jnp.where(pick, -jnp.inf, work)
    val = jnp.concatenate(vals, axis=0)
    g_ref[...] = val / jnp.sum(val, axis=0, keepdims=True) * ROUTED_SCALE
    e_ref[...] = jnp.concatenate(firsts, axis=0).astype(jnp.int32)
    onehot = picks[0].astype(F32)
    for pick in picks[1:]:
        onehot = onehot + pick.astype(F32)
    before = jnp.dot(onehot.astype(BF16), tri_ref[...], preferred_element_type=F32) + carry_scr[...]
    r_ref[...] = jnp.concatenate([jnp.sum(jnp.where(pick, before, 0.0), axis=0, keepdims=True)
                                  for pick in picks], axis=0).astype(jnp.int32)
    carry = carry_scr[...] + jnp.sum(onehot, axis=1, keepdims=True)
    carry_scr[...] = carry
    cnt_ref[...] = jnp.broadcast_to(carry, cnt_ref.shape)


def _router(xf, router_w, router_b, tm=256):
    n, d = xf.shape
    tri = (jnp.arange(tm)[:, None] < jnp.arange(tm)[None, :]).astype(BF16)
    col = lambda i: (0, i)
    fixed = lambda i: (0, 0)
    return pl.pallas_call(
        functools.partial(_router_body, tm=tm), grid=(n // tm,),
        in_specs=[pl.BlockSpec((tm, d), lambda i: (i, 0)), pl.BlockSpec((N_EXPERTS, d), fixed),
                  pl.BlockSpec((N_EXPERTS, 1), fixed), pl.BlockSpec((tm, tm), fixed)],
        out_specs=[pl.BlockSpec((TOP_K, tm), col), pl.BlockSpec((TOP_K, tm), col),
                   pl.BlockSpec((TOP_K, tm), col), pl.BlockSpec((N_EXPERTS, LANE), fixed)],
        out_shape=[jax.ShapeDtypeStruct((TOP_K, n), jnp.int32), jax.ShapeDtypeStruct((TOP_K, n), F32),
                   jax.ShapeDtypeStruct((TOP_K, n), jnp.int32), jax.ShapeDtypeStruct((N_EXPERTS, LANE), F32)],
        scratch_shapes=[pltpu.VMEM((N_EXPERTS, 1), F32)],
        compiler_params=_cparams(1), name="router",
    )(xf, router_w.T, router_b.astype(F32)[:, None], tri)


def _pack_rows(x):
    w = x.shape[-1] // 2
    lo = lax.bitcast_convert_type(x[:, :w].astype(BF16).astype(F32), jnp.uint32)
    hi = lax.bitcast_convert_type(x[:, w:].astype(BF16).astype(F32), jnp.uint32)
    return (lo >> 16) | (hi & jnp.uint32(0xFFFF0000))


def _unpack_rows(p):
    lo = lax.bitcast_convert_type(p << 16, F32)
    hi = lax.bitcast_convert_type(p & jnp.uint32(0xFFFF0000), F32)
    return lo, hi


def _dispatch_body(fill_ref, pos_ref, x_ref, z_ref, xs_hbm, xp_scr, sem, *, tm, n_blocks):
    i = pl.program_id(0)
    slot = i % 2
    xp_scr[slot] = _pack_rows(x_ref[...]).reshape(tm // SUBLANES, SUBLANES, -1)
    zsem = sem.at[2]

    def row_copy(s, grp, j, dst):
        return pltpu.make_async_copy(xp_scr.at[s, grp, pl.ds(j, 1)], xs_hbm.at[pl.ds(dst, 1)], sem.at[s])

    def zero_rows(dst, size):
        return pltpu.make_async_copy(z_ref.at[pl.ds(0, size)], xs_hbm.at[pl.ds(dst, size)], zsem)

    def zero_block(blk):
        return pltpu.make_async_copy(z_ref, xs_hbm.at[pl.ds(blk * MOE_BM, MOE_BM)], zsem)

    def issue(grp, c):
        for j in range(SUBLANES):
            for k in range(TOP_K):
                row_copy(slot, grp, j, pos_ref[k, grp * SUBLANES + j]).start(priority=k % 2)
        return c

    def drain(s):
        def body(grp, c):
            for _ in range(SUBLANES * TOP_K):
                row_copy(s, 0, 0, 0).wait()
            return c

        lax.fori_loop(0, tm // SUBLANES, body, 0)

    lax.fori_loop(0, tm // SUBLANES, issue, 0)

    @pl.when(i > 0)
    def _():
        drain(1 - slot)

    @pl.when(i == pl.num_programs(0) - 1)
    def _():
        drain(slot)

    @pl.when(i == 0)
    def _():
        def fill(e, c):
            lo, cnt = fill_ref[2 * e], fill_ref[2 * e + 1] - fill_ref[2 * e]

            def one_row(r, cc):
                cp = zero_rows(r, 1)
                cp.start()
                cp.wait()
                return cc

            head = jnp.minimum((-lo) & (SUBLANES - 1), cnt)
            lax.fori_loop(lo, lo + head, one_row, 0)
            base, rem = lo + head, cnt - head
            for bit in reversed(range(SUBLANES.bit_length() - 1, MOE_BM.bit_length() - 1)):
                size = 1 << bit

                @pl.when((rem & size) != 0)
                def _():
                    cp = zero_rows(pl.multiple_of(base + ((rem >> (bit + 1)) << (bit + 1)), SUBLANES), size)
                    cp.start()
                    cp.wait()

            lax.fori_loop(base + (rem & -SUBLANES), base + rem, one_row, 0)
            return c

        lax.fori_loop(0, N_EXPERTS, fill, 0)

        def tail_start(blk, c):
            zero_block(blk).start()
            return c

        def tail_wait(blk, c):
            zero_block(0).wait()
            return c

        lax.fori_loop(fill_ref[2 * N_EXPERTS], n_blocks, tail_start, 0)
        lax.fori_loop(fill_ref[2 * N_EXPERTS], n_blocks, tail_wait, 0)


def _dispatch(xb, pos, fill, p, tm=128):
    n, d = xb.shape
    grid_spec = pltpu.PrefetchScalarGridSpec(
        num_scalar_prefetch=1, grid=(n // tm,),
        in_specs=[pl.BlockSpec((TOP_K, tm), lambda i, fl: (0, i), memory_space=pltpu.SMEM),
                  pl.BlockSpec((tm, d), lambda i, fl: (i, 0)),
                  pl.BlockSpec((MOE_BM, d // 2), lambda i, fl: (0, 0))],
        out_specs=pl.BlockSpec(memory_space=pl.ANY),
        scratch_shapes=[pltpu.VMEM((2, tm // SUBLANES, SUBLANES, d // 2), jnp.uint32),
                        pltpu.SemaphoreType.DMA((3,))])
    return pl.pallas_call(
        functools.partial(_dispatch_body, tm=tm, n_blocks=p // MOE_BM), grid_spec=grid_spec,
        out_shape=jax.ShapeDtypeStruct((p, d // 2), jnp.uint32),
        compiler_params=_cparams(1), name="dispatch",
    )(fill, pos, xb, jnp.zeros((MOE_BM, d // 2), jnp.uint32))


def _moe_ffn_body(be_ref, nb_ref, x_ref, wg_ref, wu_ref, wd_ref, o_ref, wg_scr, wu_scr, wd_scr):
    i = pl.program_id(0)
    half = x_ref.shape[1]

    @pl.when((i == 0) | (be_ref[i] != be_ref[jnp.maximum(i - 1, 0)]))
    def _():
        wg_scr[...] = wg_ref[0, 0].astype(BF16)
        wu_scr[...] = wu_ref[0, 0].astype(BF16)
        wd_scr[...] = wd_ref[0, 0].astype(BF16)

    @pl.when(i < nb_ref[0])
    def _():
        lo, hi = _unpack_rows(x_ref[...])
        lo, hi = lo.astype(BF16), hi.astype(BF16)
        hg = (jnp.dot(lo, wg_scr[:half], preferred_element_type=F32)
              + jnp.dot(hi, wg_scr[half:], preferred_element_type=F32))
        hu = (jnp.dot(lo, wu_scr[:half], preferred_element_type=F32)
              + jnp.dot(hi, wu_scr[half:], preferred_element_type=F32))
        hb = (hg * jax.nn.sigmoid(hg) * hu).astype(BF16)
        o_ref[...] = _pack_rows(jnp.dot(hb, wd_scr[...], preferred_element_type=F32))

    @pl.when(i >= nb_ref[0])
    def _():
        o_ref[...] = jnp.zeros(o_ref.shape, o_ref.dtype)


def _moe_ffn(xs, blk_e, n_used, wg, wu, wd, layer):
    p, half = xs.shape
    d = 2 * half
    n_blocks = p // MOE_BM
    grid_spec = pltpu.PrefetchScalarGridSpec(
        num_scalar_prefetch=2, grid=(n_blocks,),
        in_specs=[pl.BlockSpec((MOE_BM, half), lambda i, be, nb: (i, 0)),
                  pl.BlockSpec((1, 1, d, D_EXPERT), lambda i, be, nb: (layer, be[i], 0, 0)),
                  pl.BlockSpec((1, 1, d, D_EXPERT), lambda i, be, nb: (layer, be[i], 0, 0)),
                  pl.BlockSpec((1, 1, D_EXPERT, d), lambda i, be, nb: (layer, be[i], 0, 0))],
        out_specs=pl.BlockSpec((MOE_BM, half), lambda i, be, nb: (i, 0)),
        scratch_shapes=[pltpu.VMEM((d, D_EXPERT), BF16), pltpu.VMEM((d, D_EXPERT), BF16),
                        pltpu.VMEM((D_EXPERT, d), BF16)])
    return pl.pallas_call(
        _moe_ffn_body, grid_spec=grid_spec, out_shape=jax.ShapeDtypeStruct((p, half), jnp.uint32),
        compiler_params=_cparams(1), name="expert_ffn",
    )(blk_e, n_used, xs, wg, wu, wd)


def _combine_body(pos_ref, posn_ref, gate_ref, xb_ref, xf_ref, y_hbm, sg_ref, su_ref, sd_ref,
                  g_ref, b_ref, of_ref, ob_ref, buf, sem, *, tm):
    i = pl.program_id(0)
    slot = i % 2

    def row_copy(s, k, grp, j, src):
        return pltpu.make_async_copy(y_hbm.at[pl.ds(src, 1)], buf.at[s, k, grp, pl.ds(j, 1)], sem.at[s])

    def issue_tile(pr, s):
        def issue(grp, c):
            for j in range(SUBLANES):
                for k in range(TOP_K):
                    row_copy(s, k, grp, j, pr[k, grp * SUBLANES + j]).start(priority=k % 2)
            return c

        lax.fori_loop(0, tm // SUBLANES, issue, 0)

    @pl.when(i == 0)
    def _():
        issue_tile(pos_ref, 0)

    @pl.when(i + 1 < pl.num_programs(0))
    def _():
        issue_tile(posn_ref, 1 - slot)

    x = xb_ref[...]
    hg = jnp.dot(x, sg_ref[...], preferred_element_type=F32)
    hu = jnp.dot(x, su_ref[...], preferred_element_type=F32)
    hb = (hg * jax.nn.sigmoid(hg) * hu).astype(BF16)
    y = jnp.dot(hb, sd_ref[...], preferred_element_type=F32)

    def drain(grp, c):
        for _ in range(SUBLANES * TOP_K):
            row_copy(slot, 0, 0, 0, 0).wait()
        return c

    lax.fori_loop(0, tm // SUBLANES, drain, 0)
    gate = gate_ref[...]
    half = buf.shape[-1]
    y_lo, y_hi = y[:, :half], y[:, half:]
    for k in range(TOP_K):
        lo, hi = _unpack_rows(buf[slot, k].reshape(tm, half))
        y_lo = y_lo + gate[:, k:k + 1] * lo
        y_hi = y_hi + gate[:, k:k + 1] * hi
    out = _layer_norm(ALPHA * xf_ref[...] + jnp.concatenate([y_lo, y_hi], axis=-1), g_ref[...], b_ref[...])
    of_ref[...] = out
    ob_ref[...] = out.astype(BF16)


def _combine(pos, gate, xb, xf, y, sg, su, sd, g, b, tm=128):
    n, d = xf.shape
    n_tiles = n // tm
    row = lambda i: (i, 0)
    fixed = lambda i: (0, 0)
    smem_col = pl.BlockSpec((TOP_K, tm), lambda i: (0, i), memory_space=pltpu.SMEM)
    smem_next = pl.BlockSpec((TOP_K, tm), lambda i: (0, jnp.minimum(i + 1, n_tiles - 1)),
                             memory_space=pltpu.SMEM)
    grid_spec = pltpu.PrefetchScalarGridSpec(
        num_scalar_prefetch=0, grid=(n_tiles,),
        in_specs=[smem_col, smem_next, pl.BlockSpec((tm, TOP_K), row),
                  pl.BlockSpec((tm, d), row), pl.BlockSpec((tm, d), row), pl.BlockSpec(memory_space=pl.ANY),
                  pl.BlockSpec(sg.shape, fixed), pl.BlockSpec(su.shape, fixed), pl.BlockSpec(sd.shape, fixed),
                  pl.BlockSpec((1, d), fixed), pl.BlockSpec((1, d), fixed)],
        out_specs=[pl.BlockSpec((tm, d), row), pl.BlockSpec((tm, d), row)],
        scratch_shapes=[pltpu.VMEM((2, TOP_K, tm // SUBLANES, SUBLANES, d // 2), jnp.uint32),
                        pltpu.SemaphoreType.DMA((2,))])
    return pl.pallas_call(
        functools.partial(_combine_body, tm=tm), grid_spec=grid_spec,
        out_shape=[jax.ShapeDtypeStruct((n, d), F32), jax.ShapeDtypeStruct((n, d), BF16)],
        compiler_params=_cparams(1), name="combine",
    )(pos, pos, gate, xb, xf, y, sg, su, sd, g, b)


def _moe(xf, xb, router_w, router_b, w_gate, w_up, w_down, sh_gate, sh_up, sh_down, ln_g, ln_b, layer=0):
    n, d = xf.shape
    e_idx, gate, rank, cnt = _router(xf, router_w, router_b)
    counts = cnt[:, 0].astype(jnp.int32)
    padded = (counts + MOE_BM - 1) // MOE_BM * MOE_BM
    pad_end = jnp.cumsum(padded)
    pad_start = (pad_end - padded).astype(jnp.int32)
    n_blocks = (n * TOP_K + N_EXPERTS * (MOE_BM - 1) + MOE_BM - 1) // MOE_BM
    blk_e = jnp.minimum(jnp.sum(pad_end[None, :] <= (jnp.arange(n_blocks) * MOE_BM)[:, None], axis=1),
                        N_EXPERTS - 1).astype(jnp.int32)
    n_used = (pad_end[-1] // MOE_BM).astype(jnp.int32).reshape(1)
    fill = jnp.concatenate([jnp.stack([pad_start + counts, pad_end], axis=1).reshape(-1), n_used]).astype(jnp.int32)
    experts = jnp.arange(N_EXPERTS, dtype=jnp.int32)[:, None, None]
    pos = rank + jnp.sum(jnp.where(e_idx[None] == experts, pad_start[:, None, None], 0), axis=0)
    xs = _dispatch(xb, pos, fill, n_blocks * MOE_BM)
    y = _moe_ffn(xs, blk_e, n_used, w_gate, w_up, w_down, layer)
    return _combine(pos, gate.T, xb, xf, y, sh_gate, sh_up, sh_down, ln_g[None], ln_b[None])


def _rope_tables(seq):
    half = ROPE_DIM // 2
    freqs = ROPE_THETA ** (-jnp.arange(half, dtype=F32) / half)
    ang = jnp.arange(seq).astype(F32)[:, None] * freqs
    cos = jnp.concatenate([jnp.cos(ang)] * 2, -1)
    sin = jnp.concatenate([jnp.sin(ang)] * 2, -1)
    z = lambda w: jnp.zeros((seq, w), F32)
    pad = LANE - NOPE - ROPE_DIM
    cos_q = jnp.concatenate([jnp.ones((seq, NOPE), F32), cos, z(pad)], -1)
    sin_q = jnp.concatenate([z(NOPE), sin, z(pad)], -1)
    cos_k = jnp.concatenate([z(NOPE), cos, z(pad)], -1)
    return cos_q, sin_q, cos_k, sin_q


def _rot_cols(w):
    half = w.shape[-1] // 2
    return jnp.concatenate([-w[..., half:], w[..., :half]], -1)


def _mixer_ab(xb, xf, bsz, seq, w_in, q_norm, w_uq, kv_norm, w_ukv, w_out, ln_g, ln_b, rope_tabs, dil_bias):
    n = bsz * seq
    d = w_in.shape[0]
    c0 = Q_LORA + KV_LORA
    w_kr = w_in[:, c0:c0 + ROPE_DIM]
    zc = lambda w: jnp.zeros((d, w), F32)
    pad = LANE - NOPE - ROPE_DIM
    w1 = jnp.concatenate([w_in[:, :c0], zc(NOPE), w_kr, zc(pad), zc(NOPE), _rot_cols(w_kr), zc(pad)], 1)
    h1 = _mm(xb, w1.astype(BF16), F32, tn=w1.shape[1])
    gw = H_B_GROUP * HD_B
    w_b = w_in[:, c0 + ROPE_DIM:].reshape(d, 3, len(DIL_PAIRS), gw).transpose(0, 2, 1, 3).reshape(d, -1)
    h2 = _proj_dilated(xb, w_b.astype(BF16), bsz, seq)

    wq = w_uq.reshape(Q_LORA, H_A, NOPE + ROPE_DIM)
    zq = jnp.zeros((Q_LORA, H_A, pad), F32)
    wq_main = jnp.concatenate([wq, zq], -1).reshape(Q_LORA, H_A * LANE)
    wq_rot = jnp.concatenate([jnp.zeros((Q_LORA, H_A, NOPE), F32), _rot_cols(wq[..., NOPE:]), zq], -1)
    wq_rot = wq_rot.reshape(Q_LORA, H_A * LANE)
    wkv = w_ukv.reshape(KV_LORA, H_A, NOPE + MLA_V)
    wk = jnp.concatenate([wkv[..., :NOPE], jnp.zeros((KV_LORA, H_A, LANE - NOPE), F32)], -1)
    wk = wk.reshape(KV_LORA, H_A * LANE)
    wv = wkv[..., NOPE:].reshape(KV_LORA, H_A * MLA_V)
    q_a, k_a, v_a = _mla_up(h1, q_norm[None], wq_main.astype(BF16), wq_rot.astype(BF16), kv_norm[None],
                            wk.astype(BF16), wv.astype(BF16), *rope_tabs, seq)
    o_a = _flash(q_a.reshape(bsz, seq, -1), k_a.reshape(bsz, seq, -1), v_a.reshape(bsz, seq, -1), name="mla_attn",
                 n_outer=H_A // 2, T=min(ATT_TQ, seq), TK=ATT_TK, dq=LANE, dv=MLA_V,
                 q_col=lambda g: g, k_col=lambda g: g, v_col=lambda g: g,
                 q_offs=(0, LANE), k_offs=(0, LANE), v_offs=(0, MLA_V),
                 scale=(NOPE + ROPE_DIM) ** -0.5, out_cols=H_A * MLA_V, out_col=lambda g: g)

    gb = gw // LANE
    outs, lses = [], []
    for gi, (window, dil) in enumerate(DIL_PAIRS):
        L = seq // dil
        t = h2[gi]
        o, lse = _flash(t, t, t, name="dilated_attn", n_outer=2, T=min(256, L), dq=HD_B, dv=HD_B,
                        q_col=lambda g: g, k_col=lambda g: gb + g, v_col=lambda g: 2 * gb + g,
                        q_offs=(0, HD_B), k_offs=(0, HD_B), v_offs=(0, HD_B), scale=HD_B ** -0.5,
                        out_cols=dil * gw, out_col=lambda g: g, bias=dil_bias[gi], bias_mode="pair",
                        bias_idx=lambda g: g, want_lse=True, nback=1,
                        rep=dil, rep_in=3 * gb, rep_out=gb)
        outs.append(o.reshape(bsz, seq, H_B_GROUP, HD_B).astype(F32))
        lses.append(lse.reshape(bsz, dil, H_B_GROUP, L).transpose(0, 3, 1, 2).reshape(bsz, seq, H_B_GROUP))
    w = jax.nn.softmax(jnp.stack(lses), axis=0)
    o_b = jnp.sum(w[..., None] * jnp.stack(outs), axis=0).astype(BF16).reshape(n, gw)
    na = H_A * MLA_V
    return _out_ln([o_a.reshape(n, na)], o_b, w_out[:na].astype(BF16), w_out[na:].astype(BF16),
                   xf, ln_g[None], ln_b[None])


def _mixer_cd(xb, xf, bsz, seq, w_in, pos_k, k_w1, k_w2, pos_v, v_w1, v_w2, lq1, lk1, lq2, lk2, d_norm,
              w_out, ln_g, ln_b, lam_init, tabs):
    n = bsz * seq
    qc_w = H_C * DK_C
    kv_w = G_C * DK_C
    off = qc_w
    kvs = []
    for _ in range(3):
        wk_ = w_in[:, off:off + kv_w].reshape(-1, G_C, DK_C)
        wv_ = w_in[:, off + kv_w:off + 2 * kv_w].reshape(-1, G_C, DK_C)
        kvs.append(jnp.concatenate([wk_, wv_], -1).reshape(-1, 2 * kv_w))
        off += 2 * kv_w
    g_off = off
    d_off = off + 3 * H_C
    w_main = jnp.concatenate([w_in[:, :qc_w]] + kvs + [w_in[:, d_off:]], 1)
    h = _mm(xb, w_main.astype(BF16), BF16, tn=w_main.shape[1] // 2).reshape(bsz, seq, -1)
    w_g = jnp.repeat(w_in[:, g_off:d_off], DK_C, axis=1)
    gates = _mm(xb, w_g.astype(BF16), F32, tn=w_g.shape[1] // 2, act="sigmoid").reshape(bsz, seq, -1)

    ncp = seq // CMP_STRIDE
    half = CMP_STRIDE * DK_C
    kv_cmp = h[:, :, qc_w:qc_w + 2 * kv_w].reshape(bsz, ncp, CMP_STRIDE, G_C, 2, DK_C)
    u = kv_cmp.transpose(0, 3, 4, 1, 2, 5).reshape(bsz, G_C, 2, ncp, half)
    pe = jnp.stack([pos_k.reshape(2, 1, half), pos_v.reshape(2, 1, half)])
    w1 = jnp.stack([k_w1.reshape(2, half, CMP_HID), v_w1.reshape(2, half, CMP_HID)])
    w2 = jnp.stack([k_w2, v_w2])
    kvc = _compress(u, pe, w1, w2)
    o_cmp, sel = _cmp_attn(h, kvc, tabs["bias_c"], tabs["overlap"], gates)

    cb = qc_w // LANE
    scale = DK_C ** -0.5
    n_pairs = H_C // 2
    nsa = dict(n_outer=n_pairs, dq=DK_C, dv=DK_C, q_col=lambda g: g, q_offs=(0, DK_C), k_offs=(0, 0),
               v_offs=(DK_C, DK_C), scale=scale, out_cols=qc_w, out_col=lambda g: g, bias_mode="pair",
               bias_idx=lambda g: g, gate=gates, vmem=VMEM_LIMIT, k_w=LANE, v_w=LANE)
    o_sel = _flash(h, h, h, name="nsa_sel_attn", T=ATT_TQ, TK=ATT_TK, k_col=lambda g: cb + 2 + g // 2, v_col=lambda g: cb + 2 + g // 2,
                   bias=tabs["bias_sel"], sel=sel, sel_idx=lambda g: g // 2,
                   gate_col=lambda g: n_pairs + g, **nsa)
    o_win = _flash(h, h, h, name="nsa_win_attn", T=ATT_TQ, TK=ATT_TK, k_col=lambda g: cb + 4 + g // 2,
                   v_col=lambda g: cb + 4 + g // 2, bias=tabs["bias_win"], gate_col=lambda g: 2 * n_pairs + g,
                   nback=-(-(WIN - 1) // ATT_TK), **nsa)

    lam = (jnp.exp(jnp.sum(lq1.astype(F32) * lk1.astype(F32)))
           - jnp.exp(jnp.sum(lq2.astype(F32) * lk2.astype(F32))) + lam_init).reshape(1, 1)
    db = cb + 6
    o_d = _flash(h, h, h, name="diff_attn", n_outer=H_D, T=ATT_TQ, TK=ATT_TK, dq=DD, dv=2 * DD,
                 q_col=lambda g: db + g, k_col=lambda g: db + H_D + g, v_col=lambda g: db + 2 * H_D + g,
                 q_offs=(0, DD), k_offs=(0, DD), v_offs=(0, 0), scale=DD ** -0.5,
                 out_cols=H_D * 2 * DD, out_col=lambda g: g, bias=tabs["bias_d"], bias_mode="shared",
                 bias_idx=lambda g: g, lam=lam, dnorm=d_norm[:, None], lam_init=lam_init, epilogue="diff",
                 vmem=VMEM_LIMIT)
    r2 = lambda a: a.reshape(n, -1)
    return _out_ln([r2(o_cmp), r2(o_sel), r2(o_win)], r2(o_d), w_out[:qc_w].astype(BF16),
                   w_out[qc_w:].astype(BF16), xf, ln_g[None], ln_b[None])


def _nsa_tables(rel_bias, seq):
    tab_c = rel_bias[:, H_B:H_B + H_C]
    tab_d = rel_bias[:, H_B + H_C:H_B + H_C + H_D]
    ncp = seq // CMP_STRIDE
    n_sel = seq // SEL_BLOCK
    pos = jnp.arange(seq)
    x = jnp.arange(2 * ncp)
    c_minus_a = jnp.where(x < ncp, x, x - 2 * ncp)
    dist = -CMP_STRIDE * c_minus_a[None, :] + jnp.arange(CMP_STRIDE)[:, None] - (CMP_LEN - 1)
    w = jnp.moveaxis(tab_c[_rel_bucket(dist)].astype(F32), -1, 0)
    bias_c = _toeplitz(w, ncp, ncp).transpose(0, 2, 1, 3).reshape(H_C, seq, ncp)
    c0 = jnp.arange(ncp) * CMP_STRIDE
    s0 = jnp.arange(n_sel) * SEL_BLOCK
    overlap = jnp.maximum(jnp.minimum(c0[:, None] + CMP_LEN, s0[None, :] + SEL_BLOCK)
                          - jnp.maximum(c0[:, None], s0[None, :]), 0).astype(F32) / CMP_LEN
    return {
        "bias_c": bias_c, "overlap": overlap.T,
        "bias_sel": _toeplitz_bias(tab_c, ATT_TQ, seq // ATT_TK, 1, seq, ATT_TK),
        "bias_win": _toeplitz_bias(tab_c, ATT_TQ, ATT_TQ // ATT_TK + -(-(WIN - 1) // ATT_TK), 1, WIN - 1, ATT_TK),
        "bias_d": _toeplitz_bias(tab_d, ATT_TQ, seq // ATT_TK, 1, seq, ATT_TK),
    }


def kernel(x, rel_bias, ab_w_in, mla_q_norm, mla_w_uq, mla_kv_norm, mla_w_ukv, ab_w_out, cd_w_in, nsa_cmp_pos_k, nsa_cmp_k_w1, nsa_cmp_k_w2, nsa_cmp_pos_v, nsa_cmp_v_w1, nsa_cmp_v_w2, diff_lambda_q1, diff_lambda_k1, diff_lambda_q2, diff_lambda_k2, diff_norm, cd_w_out, ln1_g, ln1_b, ln2_g, ln2_b, router_w, router_b, exp_w_gate, exp_w_up, exp_w_down, sh_w_gate, sh_w_up, sh_w_down):
    bsz, seq, d = x.shape
    n = bsz * seq
    depth = ln1_g.shape[0]
    rope_tabs = _rope_tables(seq)
    dil_bias = [_toeplitz_bias(rel_bias[:, gi * H_B_GROUP:(gi + 1) * H_B_GROUP], min(256, seq // dil), 2, dil,
                               window // dil) for gi, (window, dil) in enumerate(DIL_PAIRS)]
    nsa_tabs = _nsa_tables(rel_bias, seq)
    xf = x.reshape(n, d)
    xb = xf.astype(BF16)
    for l in range(depth):
        i = l // 2
        if l % 2 == 0:
            xf, xb = _mixer_ab(xb, xf, bsz, seq, ab_w_in[i], mla_q_norm[i], mla_w_uq[i], mla_kv_norm[i],
                               mla_w_ukv[i], ab_w_out[i], ln1_g[l], ln1_b[l], rope_tabs, dil_bias)
        else:
            lam_init = 0.8 - 0.6 * math.exp(-0.3 * l)
            xf, xb = _mixer_cd(xb, xf, bsz, seq, cd_w_in[i], nsa_cmp_pos_k[i], nsa_cmp_k_w1[i],
                               nsa_cmp_k_w2[i], nsa_cmp_pos_v[i], nsa_cmp_v_w1[i], nsa_cmp_v_w2[i],
                               diff_lambda_q1[i], diff_lambda_k1[i], diff_lambda_q2[i], diff_lambda_k2[i],
                               diff_norm[i], cd_w_out[i], ln1_g[l], ln1_b[l], lam_init, nsa_tabs)
        xf, xb = _moe(xf, xb, router_w[l], router_b[l], exp_w_gate, exp_w_up, exp_w_down,
                      sh_w_gate[l].astype(BF16), sh_w_up[l].astype(BF16), sh_w_down[l].astype(BF16),
                      ln2_g[l], ln2_b[l], layer=l)
    return xf.reshape(bsz, seq, d)
```

```python
import functools
import math

import jax
import jax.numpy as jnp
from jax import lax
from jax.experimental import pallas as pl
from jax.experimental.pallas import tpu as pltpu

F32 = jnp.float32
BF16 = jnp.bfloat16
HI = lax.Precision.HIGHEST

DEPTH = 4
NEG = -1e30
BIG = 1e9
LN_EPS = 1e-5
RMS_EPS = 1e-6
ALPHA = (2 * DEPTH) ** 0.25

N_BUCKETS = 32
REL_MAX_DIST = 2048

H_A = 12
NOPE = 64
ROPE_DIM = 32
MLA_V = 64
Q_LORA = 256
KV_LORA = 128
ROPE_THETA = 10000.0

DIL_PAIRS = ((128, 1), (512, 4), (2048, 16))
H_B_GROUP = 4
H_B = 12
HD_B = 64

H_C = 8
G_C = 2
R_C = 4
DK_C = 64
CMP_LEN = 32
CMP_STRIDE = 16
CMP_HID = 64
SEL_BLOCK = 64
SEL_TOP = 16
WIN = 512

H_D = 4
DD = 64

N_EXPERTS = 64
TOP_K = 8
N_EXPERT_GROUPS = 8
TOPK_GROUPS = 4
D_EXPERT = 256
ROUTED_SCALE = 2.5

LANE = 128
SUBLANES = 8
MOE_BM = 512
ATT_TQ = 512
ATT_TK = 256
VMEM_LIMIT = 56 * 1024 * 1024

_NT = (((1,), (1,)), ((), ()))


def _cparams(n_axes, vmem=None):
    return pltpu.CompilerParams(dimension_semantics=("arbitrary",) * n_axes, vmem_limit_bytes=vmem)


def _mm_body(x_ref, w_ref, o_ref, *, act, precision):
    y = jnp.dot(x_ref[...], w_ref[...], preferred_element_type=F32, precision=precision)
    if act == "sigmoid":
        y = jax.nn.sigmoid(y)
    o_ref[...] = y.astype(o_ref.dtype)


def _mm(x, w, out_dtype, tn, tm=1024, act=None, precision=None):
    m, k = x.shape
    nc = w.shape[1]
    tm = min(tm, m)
    return pl.pallas_call(
        functools.partial(_mm_body, act=act, precision=precision),
        grid=(m // tm, nc // tn),
        in_specs=[pl.BlockSpec((tm, k), lambda i, j: (i, 0)),
                  pl.BlockSpec((k, tn), lambda i, j: (0, j))],
        out_specs=pl.BlockSpec((tm, tn), lambda i, j: (i, j)),
        out_shape=jax.ShapeDtypeStruct((m, nc), out_dtype),
        compiler_params=_cparams(2, VMEM_LIMIT), name="proj",
    )(x, w)


def _proj_dilated_body(x_ref, w_ref, *rest, tm, gw3):
    o_refs, scr = rest[:-1], rest[-1]
    for gi, (_, dil) in enumerate(DIL_PAIRS):
        y = jnp.dot(x_ref[...], w_ref[:, gi * gw3:(gi + 1) * gw3], preferred_element_type=F32)
        rows = tm // dil
        for c in range(gw3 // LANE):
            scr[c] = y[:, c * LANE:(c + 1) * LANE]
        for r in range(dil):
            for c in range(gw3 // LANE):
                col = r * gw3 + c * LANE
                o_refs[gi][0, :, col:col + LANE] = scr[c, pl.ds(r, rows, stride=dil), :].astype(BF16)


def _proj_dilated(xb, w, bsz, seq, tm=512):
    n, k = xb.shape
    gw3 = w.shape[1] // len(DIL_PAIRS)
    spt = seq // tm
    return pl.pallas_call(
        functools.partial(_proj_dilated_body, tm=tm, gw3=gw3), grid=(n // tm,),
        in_specs=[pl.BlockSpec((tm, k), lambda i: (i, 0)), pl.BlockSpec(w.shape, lambda i: (0, 0))],
        out_specs=[pl.BlockSpec((1, tm // dil, dil * gw3), lambda i: (i // spt, i % spt, 0))
                   for _, dil in DIL_PAIRS],
        out_shape=[jax.ShapeDtypeStruct((bsz, seq // dil, dil * gw3), BF16) for _, dil in DIL_PAIRS],
        scratch_shapes=[pltpu.VMEM((gw3 // LANE, tm, LANE), F32)],
        compiler_params=_cparams(1, VMEM_LIMIT), name="proj_dilated",
    )(xb, w)


def _layer_norm(z, g, b):
    mu = jnp.mean(z, axis=-1, keepdims=True)
    zc = z - mu
    var = jnp.mean(zc * zc, axis=-1, keepdims=True)
    return zc * lax.rsqrt(var + LN_EPS) * g + b


def _out_ln_body(*refs, n_sum):
    a0 = refs[0][...].astype(F32)
    for r in refs[1:n_sum]:
        a0 = a0 + r[...].astype(F32)
    a1_ref, w0_ref, w1_ref, r_ref, g_ref, b_ref, of_ref, ob_ref = refs[n_sum:]
    y = jnp.dot(a0.astype(BF16), w0_ref[...], preferred_element_type=F32)
    y = y + jnp.dot(a1_ref[...], w1_ref[...], preferred_element_type=F32)
    out = _layer_norm(ALPHA * r_ref[...] + y, g_ref[...], b_ref[...])
    of_ref[...] = out
    ob_ref[...] = out.astype(BF16)


def _out_ln(a0s, a1, w0, w1, resid, g, b, tm=512):
    n, d = resid.shape
    k0, k1 = w0.shape[0], w1.shape[0]
    row = lambda i: (i, 0)
    fixed = lambda i: (0, 0)
    return pl.pallas_call(
        functools.partial(_out_ln_body, n_sum=len(a0s)),
        grid=(n // tm,),
        in_specs=[pl.BlockSpec((tm, k0), row)] * len(a0s) + [
            pl.BlockSpec((tm, k1), row), pl.BlockSpec((k0, d), fixed), pl.BlockSpec((k1, d), fixed),
            pl.BlockSpec((tm, d), row), pl.BlockSpec((1, d), fixed), pl.BlockSpec((1, d), fixed)],
        out_specs=[pl.BlockSpec((tm, d), row), pl.BlockSpec((tm, d), row)],
        out_shape=[jax.ShapeDtypeStruct((n, d), F32), jax.ShapeDtypeStruct((n, d), BF16)],
        compiler_params=_cparams(1), name="out_proj_ln",
    )(*a0s, a1, w0, w1, resid, g, b)


def _rms(x, g):
    return x * lax.rsqrt(jnp.mean(x * x, axis=-1, keepdims=True) + RMS_EPS) * g


def _mla_q_body(c_ref, g_ref, w_ref, wr_ref, cos_ref, sin_ref, o_ref):
    cn = _rms(c_ref[...], g_ref[...]).astype(BF16)
    a = jnp.dot(cn, w_ref[...], preferred_element_type=F32)
    r = jnp.dot(cn, wr_ref[...], preferred_element_type=F32)
    cos, sin = cos_ref[...], sin_ref[...]
    for h in range(H_A):
        sl = slice(h * LANE, (h + 1) * LANE)
        o_ref[:, sl] = (a[:, sl] * cos + r[:, sl] * sin).astype(o_ref.dtype)


def _mla_kv_body(c_ref, kr_ref, krr_ref, g_ref, wk_ref, wv_ref, cos_ref, sin_ref, k_ref, v_ref):
    cn = _rms(c_ref[...], g_ref[...]).astype(BF16)
    kn = jnp.dot(cn, wk_ref[...], preferred_element_type=F32)
    rope = kr_ref[...] * cos_ref[...] + krr_ref[...] * sin_ref[...]
    for h in range(H_A):
        sl = slice(h * LANE, (h + 1) * LANE)
        k_ref[:, sl] = (kn[:, sl] + rope).astype(k_ref.dtype)
    v_ref[...] = jnp.dot(cn, wv_ref[...], preferred_element_type=F32).astype(v_ref.dtype)


def _mla_up(h1, q_norm, wq, wq_rot, kv_norm, wk, wv, cos_q, sin_q, cos_k, sin_k, seq, tm=512):
    n = h1.shape[0]
    spt = seq // tm
    row = lambda c: (lambda i: (i, c))
    pos = lambda i: (i % spt, 0)
    fixed = lambda i: (0, 0)
    q_a = pl.pallas_call(
        _mla_q_body, grid=(n // tm,),
        in_specs=[pl.BlockSpec((tm, Q_LORA), row(0)), pl.BlockSpec((1, Q_LORA), fixed),
                  pl.BlockSpec(wq.shape, fixed), pl.BlockSpec(wq_rot.shape, fixed),
                  pl.BlockSpec((tm, LANE), pos), pl.BlockSpec((tm, LANE), pos)],
        out_specs=pl.BlockSpec((tm, H_A * LANE), row(0)),
        out_shape=jax.ShapeDtypeStruct((n, H_A * LANE), BF16),
        compiler_params=_cparams(1), name="mla_q_up",
    )(h1, q_norm, wq, wq_rot, cos_q, sin_q)
    k_a, v_a = pl.pallas_call(
        _mla_kv_body, grid=(n // tm,),
        in_specs=[pl.BlockSpec((tm, LANE), row(2)), pl.BlockSpec((tm, LANE), row(3)),
                  pl.BlockSpec((tm, LANE), row(4)), pl.BlockSpec((1, KV_LORA), fixed),
                  pl.BlockSpec(wk.shape, fixed), pl.BlockSpec(wv.shape, fixed),
                  pl.BlockSpec((tm, LANE), pos), pl.BlockSpec((tm, LANE), pos)],
        out_specs=[pl.BlockSpec((tm, H_A * LANE), row(0)), pl.BlockSpec((tm, H_A * MLA_V), row(0))],
        out_shape=[jax.ShapeDtypeStruct((n, H_A * LANE), BF16), jax.ShapeDtypeStruct((n, H_A * MLA_V), BF16)],
        compiler_params=_cparams(1), name="mla_kv_up",
    )(h1, h1, h1, kv_norm, wk, wv, cos_k, sin_k)
    return q_a, k_a, v_a


def _flash_body(*refs, T, TK, dq, dv, q_offs, k_offs, v_offs, scale, bias_mode, has_sel, has_gate,
                epilogue, want_lse, seg_tiles, nback, lam_init):
    R = T // TK
    it = iter(refs)
    q_ref, k_ref, v_ref = next(it), next(it), next(it)
    bias_ref = next(it) if bias_mode else None
    sel_ref = next(it) if has_sel else None
    gate_ref = next(it) if has_gate else None
    lam_ref, dn_ref = (next(it), next(it)) if epilogue == "diff" else (None, None)
    o_ref = next(it)
    lse_ref = next(it) if want_lse else None
    vt_scr = next(it)

    qi = pl.program_id(2)
    seq = v_ref.shape[1]

    @pl.when(qi == 0)
    def _():
        for c in range(seq // TK):
            vt_scr[:, c * TK:(c + 1) * TK] = v_ref[0, c * TK:(c + 1) * TK, :].astype(F32).T.astype(BF16)

    qfull = q_ref[0].astype(F32)
    fold_scale = math.frexp(scale)[0] == 0.5
    qts = [(qfull[:, off:off + dq] * (scale if fold_scale else 1.0)).T.astype(BF16) for off in q_offs]

    def qk(kc):
        kfull = k_ref[0, pl.ds(pl.multiple_of(kc * TK, TK), TK), :]
        return tuple(jnp.dot(kfull[:, k_offs[u]:k_offs[u] + dq], qts[u], preferred_element_type=F32)
                     for u in range(2))

    def update(kc, state, scores, diag):
        start = pl.multiple_of(kc * TK, TK)
        new_state = []
        sel_add = None
        if has_sel:
            per = TK // SEL_BLOCK
            rows = [sel_ref[0, 0, pl.ds(kc * per + a, 1), :] for a in range(per)]
            sel_add = jnp.concatenate([jnp.broadcast_to((r - 1.0) * (-NEG), (SEL_BLOCK, T)) for r in rows], axis=0)
        for u in range(2):
            vt = vt_scr[v_offs[u]:v_offs[u] + dv, pl.ds(start, TK)]
            s = scores[u]
            if not fold_scale:
                s = s * scale
            if bias_mode:
                s = s + bias_ref[u if bias_mode == "pair" else 0, R * qi - kc + (R - 1)]
            elif diag is not None:
                key = lax.broadcasted_iota(jnp.int32, (TK, T), 0) + diag * TK
                qry = lax.broadcasted_iota(jnp.int32, (TK, T), 1)
                s = jnp.where(key <= qry, s, NEG)
            if has_sel:
                s = s + sel_add
            m_prev, l_prev, acc_prev = state[u]
            m_new = jnp.maximum(m_prev, jnp.max(s, axis=0, keepdims=True))
            alpha = jnp.exp(m_prev - m_new)
            p = jnp.exp(s - m_new)
            l_new = alpha * l_prev + jnp.sum(p, axis=0, keepdims=True)
            acc_new = alpha * acc_prev + jnp.dot(vt, p.astype(BF16), preferred_element_type=F32)
            new_state.append((m_new, l_new, acc_new))
        return tuple(new_state)

    init = tuple((jnp.full((1, T), NEG, F32), jnp.zeros((1, T), F32), jnp.zeros((dv, T), F32)) for _ in range(2))
    lo = 0 if nback is None else jnp.maximum(R * qi - nback, 0)

    def step(kc, carry):
        state, scores = carry
        nxt = qk(kc + 1)
        return update(kc, state, scores, None), nxt

    state, scores = lax.fori_loop(lo, R * qi, step, (init, qk(lo)))
    for a in range(R):
        nxt = qk(R * qi + a + 1) if a + 1 < R else None
        state = update(R * qi + a, state, scores, a)
        scores = nxt

    outs = [acc / l for _, l, acc in state]
    if epilogue == "diff":
        a = outs[0] - lam_ref[0, 0] * outs[1]
        rinv = lax.rsqrt(jnp.mean(a * a, axis=0, keepdims=True) + RMS_EPS)
        o = (a * rinv * dn_ref[...] * (1.0 - lam_init)).T
    else:
        o = jnp.concatenate(outs, axis=0).T
        if has_gate:
            o = o * gate_ref[0]
    o_ref[0] = o.astype(o_ref.dtype)
    if want_lse:
        lse_ref[0, 0] = jnp.concatenate([m + jnp.log(l) for m, l, _ in state], axis=0)


def _flash(q, k, v, *, n_outer, T, dq, dv, q_col, k_col, v_col, q_offs, k_offs, v_offs, scale,
           out_cols, out_col, bias=None, bias_mode=None, bias_idx=None, sel=None, sel_idx=None,
           gate=None, gate_col=None, lam=None, dnorm=None, lam_init=0.0, epilogue="plain", want_lse=False,
           seg_tiles=None, nback=None, vmem=None, k_w=None, v_w=None, name="flash", TK=None,
           rep=1, rep_in=0, rep_out=0):
    bsz, seq, _ = q.shape
    TK = TK or T
    assert T % TK == 0
    nq = seq // T
    qw = max(o + dq for o in q_offs)
    kw = k_w or max(o + dq for o in k_offs)
    vw = v_w or max(o + dv for o in v_offs)
    ow = dv if epilogue == "diff" else 2 * dv
    in_specs = [pl.BlockSpec((1, T, qw), lambda g, b, i: (b // rep, i, q_col(g) + (b % rep) * rep_in)),
                pl.BlockSpec((1, seq, kw), lambda g, b, i: (b // rep, 0, k_col(g) + (b % rep) * rep_in)),
                pl.BlockSpec((1, seq, vw), lambda g, b, i: (b // rep, 0, v_col(g) + (b % rep) * rep_in))]
    args = [q, k, v]
    if bias_mode:
        nb = 2 if bias_mode == "pair" else 1
        in_specs.append(pl.BlockSpec((nb,) + bias.shape[1:], lambda g, b, i: (bias_idx(g), 0, 0, 0)))
        args.append(bias)
    if sel is not None:
        in_specs.append(pl.BlockSpec((1, 1, sel.shape[2], T), lambda g, b, i: (b, sel_idx(g), 0, i)))
        args.append(sel)
    if gate is not None:
        in_specs.append(pl.BlockSpec((1, T, ow), lambda g, b, i: (b, i, gate_col(g))))
        args.append(gate)
    if epilogue == "diff":
        in_specs.append(pl.BlockSpec(memory_space=pltpu.SMEM))
        in_specs.append(pl.BlockSpec((dv, 1), lambda g, b, i: (0, 0)))
        args += [lam, dnorm]
    out_specs = [pl.BlockSpec((1, T, ow), lambda g, b, i: (b // rep, i, out_col(g) + (b % rep) * rep_out))]
    out_shape = [jax.ShapeDtypeStruct((bsz, seq, out_cols), BF16)]
    if want_lse:
        out_specs.append(pl.BlockSpec((1, 1, 2, T), lambda g, b, i: (b, g, 0, i)))
        out_shape.append(jax.ShapeDtypeStruct((bsz * rep, n_outer, 2, seq), F32))
    body = functools.partial(
        _flash_body, T=T, TK=TK, dq=dq, dv=dv, q_offs=q_offs, k_offs=k_offs, v_offs=v_offs, scale=scale,
        bias_mode=bias_mode, has_sel=sel is not None, has_gate=gate is not None, epilogue=epilogue,
        want_lse=want_lse, seg_tiles=seg_tiles, nback=nback, lam_init=lam_init)
    res = pl.pallas_call(
        body, grid=(n_outer, bsz * rep, nq), in_specs=in_specs, out_specs=out_specs, out_shape=out_shape,
        scratch_shapes=[pltpu.VMEM((vw, seq), BF16)],
        compiler_params=_cparams(3, vmem), name=name,
    )(*args)
    return res if want_lse else res[0]


def _rel_bucket(dist):
    n = jnp.maximum(dist, 0)
    exact = N_BUCKETS // 2
    log_ratio = jnp.log(jnp.maximum(n, 1).astype(F32) / exact) / math.log(REL_MAX_DIST / exact)
    large = exact + (log_ratio * (N_BUCKETS - exact)).astype(jnp.int32)
    return jnp.where(n < exact, n, jnp.minimum(large, N_BUCKETS - 1))


def _toeplitz_bias(tab, T, n_d, dist_scale, max_dist, TK=None):
    TK = TK or T
    R = T // TK
    wlen = T + TK
    x = jnp.arange(wlen)
    dist = (jnp.arange(n_d)[:, None] - (R - 1)) * TK + jnp.where(x < T, x, x - wlen)[None, :]
    w = tab[_rel_bucket(dist * dist_scale)].astype(F32)
    w = jnp.where(((dist >= 0) & (dist <= max_dist))[..., None], w, NEG)
    return _toeplitz(jnp.moveaxis(w, -1, 0), TK, T)


def _toeplitz_body(w_ref, o_ref, *, rows, cols):
    x = jnp.broadcast_to(w_ref[0], (rows, w_ref.shape[-1]))
    o_ref[0] = pltpu.roll(x, 0, 1, stride=1, stride_axis=0)[:, :cols]


def _toeplitz(w, rows, cols):
    wlen = w.shape[-1]
    w2 = w.reshape(-1, 1, wlen)
    out = pl.pallas_call(
        functools.partial(_toeplitz_body, rows=rows, cols=cols), grid=(w2.shape[0],),
        in_specs=[pl.BlockSpec((1, 1, wlen), lambda i: (i, 0, 0))],
        out_specs=pl.BlockSpec((1, rows, cols), lambda i: (i, 0, 0)),
        out_shape=jax.ShapeDtypeStruct((w2.shape[0], rows, cols), w.dtype),
        compiler_params=_cparams(1), name="toeplitz",
    )(w2)
    return out.reshape(w.shape[:-1] + (rows, cols))


def _compress_body(u_ref, pe_ref, w1_ref, w2_ref, o_ref, *, ncp):
    outs = []
    for a in range(2):
        u = u_ref[0, 0, a].astype(F32)
        p1 = jnp.dot(u + pe_ref[a, 0], w1_ref[a, 0], preferred_element_type=F32, precision=HI)
        p2 = jnp.dot(u + pe_ref[a, 1], w1_ref[a, 1], preferred_element_type=F32, precision=HI)
        hid = jax.nn.gelu(p1 + pltpu.roll(p2, ncp - 1, 0))
        outs.append(jnp.dot(hid, w2_ref[a], preferred_element_type=F32, precision=HI))
    o_ref[0, 0] = jnp.concatenate(outs, axis=-1)


def _compress(u, pe, w1, w2):
    bsz, g, _, ncp, width = u.shape
    return pl.pallas_call(
        functools.partial(_compress_body, ncp=ncp), grid=(bsz, g),
        in_specs=[pl.BlockSpec((1, 1, 2, ncp, width), lambda b, gg: (b, gg, 0, 0, 0)),
                  pl.BlockSpec(pe.shape, lambda b, gg: (0, 0, 0, 0)),
                  pl.BlockSpec(w1.shape, lambda b, gg: (0, 0, 0, 0)),
                  pl.BlockSpec(w2.shape, lambda b, gg: (0, 0, 0))],
        out_specs=pl.BlockSpec((1, 1, ncp, 2 * DK_C), lambda b, gg: (b, gg, 0, 0)),
        out_shape=jax.ShapeDtypeStruct((bsz, g, ncp, 2 * DK_C), F32),
        compiler_params=_cparams(2), name="nsa_compress",
    )(u, pe, w1, w2)


def _cmp_attn_body(q_ref, kv_ref, bias_ref, ov_ref, gate_ref, o_ref, sel_ref, *, T, ncp, n_sel, n_top, scale):
    qi = pl.program_id(2)
    kc = kv_ref[0, 0, :, :DK_C]
    vc = kv_ref[0, 0, :, DK_C:]
    t = qi * T + lax.broadcasted_iota(jnp.int32, (T, ncp), 0)
    c = lax.broadcasted_iota(jnp.int32, (T, ncp), 1)
    valid = t >= c * CMP_STRIDE + (CMP_LEN - 1)
    validf = valid.astype(F32)
    psum = jnp.zeros((T, ncp), F32)
    outs = []
    for r in range(R_C):
        q = q_ref[0, :, r * DK_C:(r + 1) * DK_C].astype(F32)
        s = lax.dot_general(q, kc, _NT, preferred_element_type=F32, precision=HI) * scale + bias_ref[r]
        s = jnp.where(valid, s, NEG)
        e = jnp.exp(s - jnp.max(s, axis=-1, keepdims=True)) * validf
        p = e / jnp.maximum(jnp.sum(e, axis=-1, keepdims=True), 1e-30)
        outs.append(jnp.dot(p, vc, preferred_element_type=F32, precision=HI))
        psum = psum + p
    o_ref[0] = (jnp.concatenate(outs, axis=-1) * gate_ref[0]).astype(o_ref.dtype)

    imp = jnp.dot(ov_ref[...], psum.T, preferred_element_type=F32, precision=HI)
    tq = qi * T + lax.broadcasted_iota(jnp.int32, (n_sel, T), 1)
    j = lax.broadcasted_iota(jnp.int32, (n_sel, T), 0)
    forced = (j == tq // SEL_BLOCK) | (j == 0)
    work = jnp.where(forced, BIG, jnp.where(j * SEL_BLOCK <= tq, imp, -BIG))
    sel = jnp.zeros((n_sel, T), F32)
    jf = j.astype(F32)
    for _ in range(n_top):
        _, _, pick = _first_max(work, jf, n_sel)
        sel = jnp.where(pick, 1.0, sel)
        work = jnp.where(pick, -jnp.inf, work)
    sel_ref[0, 0] = sel


def _cmp_attn(h, kvc, bias_c, overlap, gates, T=256):
    bsz, seq, _ = h.shape
    ncp = kvc.shape[2]
    n_sel = seq // SEL_BLOCK
    n_top = min(SEL_TOP, n_sel)
    qw = R_C * DK_C
    return pl.pallas_call(
        functools.partial(_cmp_attn_body, T=T, ncp=ncp, n_sel=n_sel, n_top=n_top, scale=DK_C ** -0.5),
        grid=(G_C, bsz, seq // T),
        in_specs=[pl.BlockSpec((1, T, qw), lambda g, b, i: (b, i, g)),
                  pl.BlockSpec((1, 1, ncp, 2 * DK_C), lambda g, b, i: (b, g, 0, 0)),
                  pl.BlockSpec((R_C, T, ncp), lambda g, b, i: (g, i, 0)),
                  pl.BlockSpec(overlap.shape, lambda g, b, i: (0, 0)),
                  pl.BlockSpec((1, T, qw), lambda g, b, i: (b, i, g))],
        out_specs=[pl.BlockSpec((1, T, qw), lambda g, b, i: (b, i, g)),
                   pl.BlockSpec((1, 1, n_sel, T), lambda g, b, i: (b, g, 0, i))],
        out_shape=[jax.ShapeDtypeStruct((bsz, seq, H_C * DK_C), BF16),
                   jax.ShapeDtypeStruct((bsz, G_C, n_sel, seq), F32)],
        compiler_params=_cparams(3), name="nsa_cmp_attn",
    )(h, kvc, bias_c, overlap, gates)


def _first_max(work, idx, n):
    mx = jnp.max(work, axis=0, keepdims=True)
    first = jnp.min(jnp.where(work == mx, idx, float(n)), axis=0, keepdims=True)
    return mx, first, idx == first


def _router_body(x_ref, wt_ref, b_ref, tri_ref, e_ref, g_ref, r_ref, cnt_ref, carry_scr, *, tm):
    i = pl.program_id(0)

    @pl.when(i == 0)
    def _():
        carry_scr[...] = jnp.zeros(carry_scr.shape, F32)

    st = lax.dot_general(wt_ref[...], x_ref[...], _NT, preferred_element_type=F32, precision=HI)
    scores = jax.nn.sigmoid(st)
    sel = scores + b_ref[...]
    per = N_EXPERTS // N_EXPERT_GROUPS
    fiota = lambda rows: lax.broadcasted_iota(jnp.int32, (rows, tm), 0).astype(F32)
    i_per, i_grp, i_exp = fiota(per), fiota(N_EXPERT_GROUPS), fiota(N_EXPERTS)
    grp_scores = []
    for g in range(N_EXPERT_GROUPS):
        blk = sel[g * per:(g + 1) * per]
        m1, _, pick = _first_max(blk, i_per, per)
        grp_scores.append(m1 + jnp.max(jnp.where(pick, -jnp.inf, blk), axis=0, keepdims=True))
    work = jnp.concatenate(grp_scores, axis=0)
    gmask = jnp.zeros((N_EXPERT_GROUPS, tm), F32)
    for _ in range(TOPK_GROUPS):
        _, _, pick = _first_max(work, i_grp, N_EXPERT_GROUPS)
        gmask = jnp.where(pick, 1.0, gmask)
        work = jnp.where(pick, -jnp.inf, work)
    work = jnp.concatenate([jnp.where(gmask[g:g + 1] > 0.5, sel[g * per:(g + 1) * per], NEG)
                            for g in range(N_EXPERT_GROUPS)], axis=0)
    picks, firsts, vals = [], [], []
    for _ in range(TOP_K):
        _, first, pick = _first_max(work, i_exp, N_EXPERTS)
        picks.append(pick)
        firsts.append(first)
        vals.append(jnp.sum(jnp.where(pick, scores, 0.0), axis=0, keepdims=True))
        work = jnp.where(pick, -jnp.inf, work)
    val = jnp.concatenate(vals, axis=0)
    g_ref[...] = val / jnp.sum(val, axis=0, keepdims=True) * ROUTED_SCALE
    e_ref[...] = jnp.concatenate(firsts, axis=0).astype(jnp.int32)
    onehot = picks[0].astype(F32)
    for pick in picks[1:]:
        onehot = onehot + pick.astype(F32)
    before = jnp.dot(onehot.astype(BF16), tri_ref[...], preferred_element_type=F32) + carry_scr[...]
    r_ref[...] = jnp.concatenate([jnp.sum(jnp.where(pick, before, 0.0), axis=0, keepdims=True)
                                  for pick in picks], axis=0).astype(jnp.int32)
    carry = carry_scr[...] + jnp.sum(onehot, axis=1, keepdims=True)
    carry_scr[...] = carry
    cnt_ref[...] = jnp.broadcast_to(carry, cnt_ref.shape)


def _router(xf, router_w, router_b, tm=512):
    n, d = xf.shape
    tri = (jnp.arange(tm)[:, None] < jnp.arange(tm)[None, :]).astype(BF16)
    col = lambda i: (0, i)
    fixed = lambda i: (0, 0)
    return pl.pallas_call(
        functools.partial(_router_body, tm=tm), grid=(n // tm,),
        in_specs=[pl.BlockSpec((tm, d), lambda i: (i, 0)), pl.BlockSpec((N_EXPERTS, d), fixed),
                  pl.BlockSpec((N_EXPERTS, 1), fixed), pl.BlockSpec((tm, tm), fixed)],
        out_specs=[pl.BlockSpec((TOP_K, tm), col), pl.BlockSpec((TOP_K, tm), col),
                   pl.BlockSpec((TOP_K, tm), col), pl.BlockSpec((N_EXPERTS, LANE), fixed)],
        out_shape=[jax.ShapeDtypeStruct((TOP_K, n), jnp.int32), jax.ShapeDtypeStruct((TOP_K, n), F32),
                   jax.ShapeDtypeStruct((TOP_K, n), jnp.int32), jax.ShapeDtypeStruct((N_EXPERTS, LANE), F32)],
        scratch_shapes=[pltpu.VMEM((N_EXPERTS, 1), F32)],
        compiler_params=_cparams(1), name="router",
    )(xf, router_w.T, router_b.astype(F32)[:, None], tri)


def _pack_rows(x):
    w = x.shape[-1] // 2
    lo = lax.bitcast_convert_type(x[:, :w].astype(BF16).astype(F32), jnp.uint32)
    hi = lax.bitcast_convert_type(x[:, w:].astype(BF16).astype(F32), jnp.uint32)
    return (lo >> 16) | (hi & jnp.uint32(0xFFFF0000))


def _unpack_rows(p):
    lo = lax.bitcast_convert_type(p << 16, F32)
    hi = lax.bitcast_convert_type(p & jnp.uint32(0xFFFF0000), F32)
    return lo, hi


def _dispatch_body(fill_ref, pos_ref, x_ref, z_ref, xs_hbm, xp_scr, sem, *, tm, n_blocks):
    i = pl.program_id(0)
    slot = i % 2
    xp_scr[slot] = _pack_rows(x_ref[...]).reshape(tm // SUBLANES, SUBLANES, -1)
    zsem = sem.at[2]

    def row_copy(s, grp, j, dst):
        return pltpu.make_async_copy(xp_scr.at[s, grp, pl.ds(j, 1)], xs_hbm.at[pl.ds(dst, 1)], sem.at[s])

    def zero_rows(dst, size):
        return pltpu.make_async_copy(z_ref.at[pl.ds(0, size)], xs_hbm.at[pl.ds(dst, size)], zsem)

    def zero_block(blk):
        return pltpu.make_async_copy(z_ref, xs_hbm.at[pl.ds(blk * MOE_BM, MOE_BM)], zsem)

    def issue(grp, c):
        for j in range(SUBLANES):
            for k in range(TOP_K):
                row_copy(slot, grp, j, pos_ref[k, grp * SUBLANES + j]).start(priority=k % 2)
        return c

    def drain(s):
        def body(grp, c):
            for _ in range(SUBLANES * TOP_K):
                row_copy(s, 0, 0, 0).wait()
            return c

        lax.fori_loop(0, tm // SUBLANES, body, 0)

    lax.fori_loop(0, tm // SUBLANES, issue, 0)

    @pl.when(i > 0)
    def _():
        drain(1 - slot)

    @pl.when(i == pl.num_programs(0) - 1)
    def _():
        drain(slot)

    @pl.when(i == 0)
    def _():
        def fill(wait):
            def go(cp):
                if wait:
                    cp.wait()
                else:
                    cp.start()

            def body(e, c):
                lo, cnt = fill_ref[2 * e], fill_ref[2 * e + 1] - fill_ref[2 * e]

                def one_row(r, cc):
                    go(zero_rows(0 if wait else r, 1))
                    return cc

                head = jnp.minimum((-lo) & (SUBLANES - 1), cnt)
                lax.fori_loop(lo, lo + head, one_row, 0)
                base, rem = lo + head, cnt - head
                for bit in reversed(range(SUBLANES.bit_length() - 1, MOE_BM.bit_length() - 1)):
                    size = 1 << bit

                    @pl.when((rem & size) != 0)
                    def _():
                        off = base + ((rem >> (bit + 1)) << (bit + 1))
                        go(zero_rows(0 if wait else pl.multiple_of(off, SUBLANES), size))

                lax.fori_loop(base + (rem & -SUBLANES), base + rem, one_row, 0)
                return c

            lax.fori_loop(0, N_EXPERTS, body, 0)

        fill(False)
        fill(True)

        def tail_start(blk, c):
            zero_block(blk).start()
            return c

        def tail_wait(blk, c):
            zero_block(0).wait()
            return c

        lax.fori_loop(fill_ref[2 * N_EXPERTS], n_blocks, tail_start, 0)
        lax.fori_loop(fill_ref[2 * N_EXPERTS], n_blocks, tail_wait, 0)


def _dispatch(xb, pos, fill, p, tm=128):
    n, d = xb.shape
    grid_spec = pltpu.PrefetchScalarGridSpec(
        num_scalar_prefetch=1, grid=(n // tm,),
        in_specs=[pl.BlockSpec((TOP_K, tm), lambda i, fl: (0, i), memory_space=pltpu.SMEM),
                  pl.BlockSpec((tm, d), lambda i, fl: (i, 0)),
                  pl.BlockSpec((MOE_BM, d // 2), lambda i, fl: (0, 0))],
        out_specs=pl.BlockSpec(memory_space=pl.ANY),
        scratch_shapes=[pltpu.VMEM((2, tm // SUBLANES, SUBLANES, d // 2), jnp.uint32),
                        pltpu.SemaphoreType.DMA((3,))])
    return pl.pallas_call(
        functools.partial(_dispatch_body, tm=tm, n_blocks=p // MOE_BM), grid_spec=grid_spec,
        out_shape=jax.ShapeDtypeStruct((p, d // 2), jnp.uint32),
        compiler_params=_cparams(1), name="dispatch",
    )(fill, pos, xb, jnp.zeros((MOE_BM, d // 2), jnp.uint32))


def _moe_ffn_body(be_ref, nb_ref, x_ref, wg_ref, wu_ref, wd_ref, o_ref, wg_scr, wu_scr, wd_scr):
    i = pl.program_id(0)
    half = x_ref.shape[1]

    @pl.when((i == 0) | (be_ref[i] != be_ref[jnp.maximum(i - 1, 0)]))
    def _():
        wg_scr[...] = wg_ref[0, 0].astype(BF16)
        wu_scr[...] = wu_ref[0, 0].astype(BF16)
        wd_scr[...] = wd_ref[0, 0].astype(BF16)

    @pl.when(i < nb_ref[0])
    def _():
        lo, hi = _unpack_rows(x_ref[...])
        lo, hi = lo.astype(BF16), hi.astype(BF16)
        hg = (jnp.dot(lo, wg_scr[:half], preferred_element_type=F32)
              + jnp.dot(hi, wg_scr[half:], preferred_element_type=F32))
        hu = (jnp.dot(lo, wu_scr[:half], preferred_element_type=F32)
              + jnp.dot(hi, wu_scr[half:], preferred_element_type=F32))
        hb = (hg * jax.nn.sigmoid(hg) * hu).astype(BF16)
        o_ref[...] = _pack_rows(jnp.dot(hb, wd_scr[...], preferred_element_type=F32))

    @pl.when(i >= nb_ref[0])
    def _():
        o_ref[...] = jnp.zeros(o_ref.shape, o_ref.dtype)


def _moe_ffn(xs, blk_e, n_used, wg, wu, wd, layer):
    p, half = xs.shape
    d = 2 * half
    n_blocks = p // MOE_BM
    grid_spec = pltpu.PrefetchScalarGridSpec(
        num_scalar_prefetch=2, grid=(n_blocks,),
        in_specs=[pl.BlockSpec((MOE_BM, half), lambda i, be, nb: (i, 0)),
                  pl.BlockSpec((1, 1, d, D_EXPERT), lambda i, be, nb: (layer, be[i], 0, 0)),
                  pl.BlockSpec((1, 1, d, D_EXPERT), lambda i, be, nb: (layer, be[i], 0, 0)),
                  pl.BlockSpec((1, 1, D_EXPERT, d), lambda i, be, nb: (layer, be[i], 0, 0))],
        out_specs=pl.BlockSpec((MOE_BM, half), lambda i, be, nb: (i, 0)),
        scratch_shapes=[pltpu.VMEM((d, D_EXPERT), BF16), pltpu.VMEM((d, D_EXPERT), BF16),
                        pltpu.VMEM((D_EXPERT, d), BF16)])
    return pl.pallas_call(
        _moe_ffn_body, grid_spec=grid_spec, out_shape=jax.ShapeDtypeStruct((p, half), jnp.uint32),
        compiler_params=_cparams(1), name="expert_ffn",
    )(blk_e, n_used, xs, wg, wu, wd)


def _combine_body(pos_ref, posn_ref, gate_ref, xb_ref, xf_ref, y_hbm, sg_ref, su_ref, sd_ref,
                  g_ref, b_ref, of_ref, ob_ref, buf, sem, *, tm):
    i = pl.program_id(0)
    slot = i % 2

    def row_copy(s, k, grp, j, src):
        return pltpu.make_async_copy(y_hbm.at[pl.ds(src, 1)], buf.at[s, k, grp, pl.ds(j, 1)], sem.at[s])

    def issue_tile(pr, s):
        def issue(grp, c):
            for j in range(SUBLANES):
                for k in range(TOP_K):
                    row_copy(s, k, grp, j, pr[k, grp * SUBLANES + j]).start(priority=k % 2)
            return c

        lax.fori_loop(0, tm // SUBLANES, issue, 0)

    @pl.when(i == 0)
    def _():
        issue_tile(pos_ref, 0)

    @pl.when(i + 1 < pl.num_programs(0))
    def _():
        issue_tile(posn_ref, 1 - slot)

    x = xb_ref[...]
    hg = jnp.dot(x, sg_ref[...], preferred_element_type=F32)
    hu = jnp.dot(x, su_ref[...], preferred_element_type=F32)
    hb = (hg * jax.nn.sigmoid(hg) * hu).astype(BF16)
    y = jnp.dot(hb, sd_ref[...], preferred_element_type=F32)

    def drain(grp, c):
        for _ in range(SUBLANES * TOP_K):
            row_copy(slot, 0, 0, 0, 0).wait()
        return c

    lax.fori_loop(0, tm // SUBLANES, drain, 0)
    gate = gate_ref[...]
    half = buf.shape[-1]
    y_lo, y_hi = y[:, :half], y[:, half:]
    for k in range(TOP_K):
        lo, hi = _unpack_rows(buf[slot, k].reshape(tm, half))
        y_lo = y_lo + gate[:, k:k + 1] * lo
        y_hi = y_hi + gate[:, k:k + 1] * hi
    out = _layer_norm(ALPHA * xf_ref[...] + jnp.concatenate([y_lo, y_hi], axis=-1), g_ref[...], b_ref[...])
    of_ref[...] = out
    ob_ref[...] = out.astype(BF16)


def _combine(pos, gate, xb, xf, y, sg, su, sd, g, b, tm=128):
    n, d = xf.shape
    n_tiles = n // tm
    row = lambda i: (i, 0)
    fixed = lambda i: (0, 0)
    smem_col = pl.BlockSpec((TOP_K, tm), lambda i: (0, i), memory_space=pltpu.SMEM)
    smem_next = pl.BlockSpec((TOP_K, tm), lambda i: (0, jnp.minimum(i + 1, n_tiles - 1)),
                             memory_space=pltpu.SMEM)
    grid_spec = pltpu.PrefetchScalarGridSpec(
        num_scalar_prefetch=0, grid=(n_tiles,),
        in_specs=[smem_col, smem_next, pl.BlockSpec((tm, TOP_K), row),
                  pl.BlockSpec((tm, d), row), pl.BlockSpec((tm, d), row), pl.BlockSpec(memory_space=pl.ANY),
                  pl.BlockSpec(sg.shape, fixed), pl.BlockSpec(su.shape, fixed), pl.BlockSpec(sd.shape, fixed),
                  pl.BlockSpec((1, d), fixed), pl.BlockSpec((1, d), fixed)],
        out_specs=[pl.BlockSpec((tm, d), row), pl.BlockSpec((tm, d), row)],
        scratch_shapes=[pltpu.VMEM((2, TOP_K, tm // SUBLANES, SUBLANES, d // 2), jnp.uint32),
                        pltpu.SemaphoreType.DMA((2,))])
    return pl.pallas_call(
        functools.partial(_combine_body, tm=tm), grid_spec=grid_spec,
        out_shape=[jax.ShapeDtypeStruct((n, d), F32), jax.ShapeDtypeStruct((n, d), BF16)],
        compiler_params=_cparams(1), name="combine",
    )(pos, pos, gate, xb, xf, y, sg, su, sd, g, b)


def _moe(xf, xb, router_w, router_b, w_gate, w_up, w_down, sh_gate, sh_up, sh_down, ln_g, ln_b, layer=0):
    n, d = xf.shape
    e_idx, gate, rank, cnt = _router(xf, router_w, router_b)
    counts = cnt[:, 0].astype(jnp.int32)
    padded = (counts + MOE_BM - 1) // MOE_BM * MOE_BM
    pad_end = jnp.cumsum(padded)
    pad_start = (pad_end - padded).astype(jnp.int32)
    n_blocks = (n * TOP_K + N_EXPERTS * (MOE_BM - 1) + MOE_BM - 1) // MOE_BM
    blk_e = jnp.minimum(jnp.sum(pad_end[None, :] <= (jnp.arange(n_blocks) * MOE_BM)[:, None], axis=1),
                        N_EXPERTS - 1).astype(jnp.int32)
    n_used = (pad_end[-1] // MOE_BM).astype(jnp.int32).reshape(1)
    fill = jnp.concatenate([jnp.stack([pad_start + counts, pad_end], axis=1).reshape(-1), n_used]).astype(jnp.int32)
    experts = jnp.arange(N_EXPERTS, dtype=jnp.int32)[:, None, None]
    pos = rank + jnp.sum(jnp.where(e_idx[None] == experts, pad_start[:, None, None], 0), axis=0)
    xs = _dispatch(xb, pos, fill, n_blocks * MOE_BM)
    y = _moe_ffn(xs, blk_e, n_used, w_gate, w_up, w_down, layer)
    return _combine(pos, gate.T, xb, xf, y, sh_gate, sh_up, sh_down, ln_g[None], ln_b[None])


def _rope_tables(seq):
    half = ROPE_DIM // 2
    freqs = ROPE_THETA ** (-jnp.arange(half, dtype=F32) / half)
    ang = jnp.arange(seq).astype(F32)[:, None] * freqs
    cos = jnp.concatenate([jnp.cos(ang)] * 2, -1)
    sin = jnp.concatenate([jnp.sin(ang)] * 2, -1)
    z = lambda w: jnp.zeros((seq, w), F32)
    pad = LANE - NOPE - ROPE_DIM
    cos_q = jnp.concatenate([jnp.ones((seq, NOPE), F32), cos, z(pad)], -1)
    sin_q = jnp.concatenate([z(NOPE), sin, z(pad)], -1)
    cos_k = jnp.concatenate([z(NOPE), cos, z(pad)], -1)
    return cos_q, sin_q, cos_k, sin_q


def _rot_cols(w):
    half = w.shape[-1] // 2
    return jnp.concatenate([-w[..., half:], w[..., :half]], -1)


def _mixer_ab(xb, xf, bsz, seq, w_in, q_norm, w_uq, kv_norm, w_ukv, w_out, ln_g, ln_b, rope_tabs, dil_bias):
    n = bsz * seq
    d = w_in.shape[0]
    c0 = Q_LORA + KV_LORA
    w_kr = w_in[:, c0:c0 + ROPE_DIM]
    zc = lambda w: jnp.zeros((d, w), F32)
    pad = LANE - NOPE - ROPE_DIM
    w1 = jnp.concatenate([w_in[:, :c0], zc(NOPE), w_kr, zc(pad), zc(NOPE), _rot_cols(w_kr), zc(pad)], 1)
    h1 = _mm(xb, w1.astype(BF16), F32, tn=w1.shape[1])
    gw = H_B_GROUP * HD_B
    w_b = w_in[:, c0 + ROPE_DIM:].reshape(d, 3, len(DIL_PAIRS), gw).transpose(0, 2, 1, 3).reshape(d, -1)
    h2 = _proj_dilated(xb, w_b.astype(BF16), bsz, seq)

    wq = w_uq.reshape(Q_LORA, H_A, NOPE + ROPE_DIM)
    zq = jnp.zeros((Q_LORA, H_A, pad), F32)
    wq_main = jnp.concatenate([wq, zq], -1).reshape(Q_LORA, H_A * LANE)
    wq_rot = jnp.concatenate([jnp.zeros((Q_LORA, H_A, NOPE), F32), _rot_cols(wq[..., NOPE:]), zq], -1)
    wq_rot = wq_rot.reshape(Q_LORA, H_A * LANE)
    wkv = w_ukv.reshape(KV_LORA, H_A, NOPE + MLA_V)
    wk = jnp.concatenate([wkv[..., :NOPE], jnp.zeros((KV_LORA, H_A, LANE - NOPE), F32)], -1)
    wk = wk.reshape(KV_LORA, H_A * LANE)
    wv = wkv[..., NOPE:].reshape(KV_LORA, H_A * MLA_V)
    q_a, k_a, v_a = _mla_up(h1, q_norm[None], wq_main.astype(BF16), wq_rot.astype(BF16), kv_norm[None],
                            wk.astype(BF16), wv.astype(BF16), *rope_tabs, seq)
    o_a = _flash(q_a.reshape(bsz, seq, -1), k_a.reshape(bsz, seq, -1), v_a.reshape(bsz, seq, -1), name="mla_attn",
                 n_outer=H_A // 2, T=min(ATT_TQ, seq), TK=ATT_TK, dq=LANE, dv=MLA_V,
                 q_col=lambda g: g, k_col=lambda g: g, v_col=lambda g: g,
                 q_offs=(0, LANE), k_offs=(0, LANE), v_offs=(0, MLA_V),
                 scale=(NOPE + ROPE_DIM) ** -0.5, out_cols=H_A * MLA_V, out_col=lambda g: g)

    gb = gw // LANE
    outs, lses = [], []
    for gi, (window, dil) in enumerate(DIL_PAIRS):
        L = seq // dil
        t = h2[gi]
        o, lse = _flash(t, t, t, name="dilated_attn", n_outer=2, T=min(256, L), dq=HD_B, dv=HD_B,
                        q_col=lambda g: g, k_col=lambda g: gb + g, v_col=lambda g: 2 * gb + g,
                        q_offs=(0, HD_B), k_offs=(0, HD_B), v_offs=(0, HD_B), scale=HD_B ** -0.5,
                        out_cols=dil * gw, out_col=lambda g: g, bias=dil_bias[gi], bias_mode="pair",
                        bias_idx=lambda g: g, want_lse=True, nback=1,
                        rep=dil, rep_in=3 * gb, rep_out=gb)
        outs.append(o.reshape(bsz, seq, H_B_GROUP, HD_B).astype(F32))
        lses.append(lse.reshape(bsz, dil, H_B_GROUP, L).transpose(0, 3, 1, 2).reshape(bsz, seq, H_B_GROUP))
    w = jax.nn.softmax(jnp.stack(lses), axis=0)
    o_b = jnp.sum(w[..., None] * jnp.stack(outs), axis=0).astype(BF16).reshape(n, gw)
    na = H_A * MLA_V
    return _out_ln([o_a.reshape(n, na)], o_b, w_out[:na].astype(BF16), w_out[na:].astype(BF16),
                   xf, ln_g[None], ln_b[None])


def _mixer_cd(xb, xf, bsz, seq, w_in, pos_k, k_w1, k_w2, pos_v, v_w1, v_w2, lq1, lk1, lq2, lk2, d_norm,
              w_out, ln_g, ln_b, lam_init, tabs):
    n = bsz * seq
    qc_w = H_C * DK_C
    kv_w = G_C * DK_C
    off = qc_w
    kvs = []
    for _ in range(3):
        wk_ = w_in[:, off:off + kv_w].reshape(-1, G_C, DK_C)
        wv_ = w_in[:, off + kv_w:off + 2 * kv_w].reshape(-1, G_C, DK_C)
        kvs.append(jnp.concatenate([wk_, wv_], -1).reshape(-1, 2 * kv_w))
        off += 2 * kv_w
    g_off = off
    d_off = off + 3 * H_C
    w_main = jnp.concatenate([w_in[:, :qc_w]] + kvs + [w_in[:, d_off:]], 1)
    h = _mm(xb, w_main.astype(BF16), BF16, tn=w_main.shape[1] // 2).reshape(bsz, seq, -1)
    w_g = jnp.repeat(w_in[:, g_off:d_off], DK_C, axis=1)
    gates = _mm(xb, w_g.astype(BF16), F32, tn=w_g.shape[1] // 2, act="sigmoid").reshape(bsz, seq, -1)

    ncp = seq // CMP_STRIDE
    half = CMP_STRIDE * DK_C
    kv_cmp = h[:, :, qc_w:qc_w + 2 * kv_w].reshape(bsz, ncp, CMP_STRIDE, G_C, 2, DK_C)
    u = kv_cmp.transpose(0, 3, 4, 1, 2, 5).reshape(bsz, G_C, 2, ncp, half)
    pe = jnp.stack([pos_k.reshape(2, 1, half), pos_v.reshape(2, 1, half)])
    w1 = jnp.stack([k_w1.reshape(2, half, CMP_HID), v_w1.reshape(2, half, CMP_HID)])
    w2 = jnp.stack([k_w2, v_w2])
    kvc = _compress(u, pe, w1, w2)
    o_cmp, sel = _cmp_attn(h, kvc, tabs["bias_c"], tabs["overlap"], gates)

    cb = qc_w // LANE
    scale = DK_C ** -0.5
    n_pairs = H_C // 2
    nsa = dict(n_outer=n_pairs, dq=DK_C, dv=DK_C, q_col=lambda g: g, q_offs=(0, DK_C), k_offs=(0, 0),
               v_offs=(DK_C, DK_C), scale=scale, out_cols=qc_w, out_col=lambda g: g, bias_mode="pair",
               bias_idx=lambda g: g, gate=gates, vmem=VMEM_LIMIT, k_w=LANE, v_w=LANE)
    o_sel = _flash(h, h, h, name="nsa_sel_attn", T=ATT_TQ, TK=ATT_TK, k_col=lambda g: cb + 2 + g // 2, v_col=lambda g: cb + 2 + g // 2,
                   bias=tabs["bias_sel"], sel=sel, sel_idx=lambda g: g // 2,
                   gate_col=lambda g: n_pairs + g, **nsa)
    o_win = _flash(h, h, h, name="nsa_win_attn", T=ATT_TQ, TK=ATT_TK, k_col=lambda g: cb + 4 + g // 2,
                   v_col=lambda g: cb + 4 + g // 2, bias=tabs["bias_win"], gate_col=lambda g: 2 * n_pairs + g,
                   nback=-(-(WIN - 1) // ATT_TK), **nsa)

    lam = (jnp.exp(jnp.sum(lq1.astype(F32) * lk1.astype(F32)))
           - jnp.exp(jnp.sum(lq2.astype(F32) * lk2.astype(F32))) + lam_init).reshape(1, 1)
    db = cb + 6
    o_d = _flash(h, h, h, name="diff_attn", n_outer=H_D, T=ATT_TQ, TK=ATT_TK, dq=DD, dv=2 * DD,
                 q_col=lambda g: db + g, k_col=lambda g: db + H_D + g, v_col=lambda g: db + 2 * H_D + g,
                 q_offs=(0, DD), k_offs=(0, DD), v_offs=(0, 0), scale=DD ** -0.5,
                 out_cols=H_D * 2 * DD, out_col=lambda g: g, bias=tabs["bias_d"], bias_mode="shared",
                 bias_idx=lambda g: g, lam=lam, dnorm=d_norm[:, None], lam_init=lam_init, epilogue="diff",
                 vmem=VMEM_LIMIT)
    r2 = lambda a: a.reshape(n, -1)
    return _out_ln([r2(o_cmp), r2(o_sel), r2(o_win)], r2(o_d), w_out[:qc_w].astype(BF16),
                   w_out[qc_w:].astype(BF16), xf, ln_g[None], ln_b[None])


def _nsa_tables(rel_bias, seq):
    tab_c = rel_bias[:, H_B:H_B + H_C]
    tab_d = rel_bias[:, H_B + H_C:H_B + H_C + H_D]
    ncp = seq // CMP_STRIDE
    n_sel = seq // SEL_BLOCK
    pos = jnp.arange(seq)
    x = jnp.arange(2 * ncp)
    c_minus_a = jnp.where(x < ncp, x, x - 2 * ncp)
    dist = -CMP_STRIDE * c_minus_a[None, :] + jnp.arange(CMP_STRIDE)[:, None] - (CMP_LEN - 1)
    w = jnp.moveaxis(tab_c[_rel_bucket(dist)].astype(F32), -1, 0)
    bias_c = _toeplitz(w, ncp, ncp).transpose(0, 2, 1, 3).reshape(H_C, seq, ncp)
    c0 = jnp.arange(ncp) * CMP_STRIDE
    s0 = jnp.arange(n_sel) * SEL_BLOCK
    overlap = jnp.maximum(jnp.minimum(c0[:, None] + CMP_LEN, s0[None, :] + SEL_BLOCK)
                          - jnp.maximum(c0[:, None], s0[None, :]), 0).astype(F32) / CMP_LEN
    return {
        "bias_c": bias_c, "overlap": overlap.T,
        "bias_sel": _toeplitz_bias(tab_c, ATT_TQ, seq // ATT_TK, 1, seq, ATT_TK),
        "bias_win": _toeplitz_bias(tab_c, ATT_TQ, ATT_TQ // ATT_TK + -(-(WIN - 1) // ATT_TK), 1, WIN - 1, ATT_TK),
        "bias_d": _toeplitz_bias(tab_d, ATT_TQ, seq // ATT_TK, 1, seq, ATT_TK),
    }


def kernel(x, rel_bias, ab_w_in, mla_q_norm, mla_w_uq, mla_kv_norm, mla_w_ukv, ab_w_out, cd_w_in, nsa_cmp_pos_k, nsa_cmp_k_w1, nsa_cmp_k_w2, nsa_cmp_pos_v, nsa_cmp_v_w1, nsa_cmp_v_w2, diff_lambda_q1, diff_lambda_k1, diff_lambda_q2, diff_lambda_k2, diff_norm, cd_w_out, ln1_g, ln1_b, ln2_g, ln2_b, router_w, router_b, exp_w_gate, exp_w_up, exp_w_down, sh_w_gate, sh_w_up, sh_w_down):
    bsz, seq, d = x.shape
    n = bsz * seq
    depth = ln1_g.shape[0]
    rope_tabs = _rope_tables(seq)
    dil_bias = [_toeplitz_bias(rel_bias[:, gi * H_B_GROUP:(gi + 1) * H_B_GROUP], min(256, seq // dil), 2, dil,
                               window // dil) for gi, (window, dil) in enumerate(DIL_PAIRS)]
    nsa_tabs = _nsa_tables(rel_bias, seq)
    xf = x.reshape(n, d)
    xb = xf.astype(BF16)
    for l in range(depth):
        i = l // 2
        if l % 2 == 0:
            xf, xb = _mixer_ab(xb, xf, bsz, seq, ab_w_in[i], mla_q_norm[i], mla_w_uq[i], mla_kv_norm[i],
                               mla_w_ukv[i], ab_w_out[i], ln1_g[l], ln1_b[l], rope_tabs, dil_bias)
        else:
            lam_init = 0.8 - 0.6 * math.exp(-0.3 * l)
            xf, xb = _mixer_cd(xb, xf, bsz, seq, cd_w_in[i], nsa_cmp_pos_k[i], nsa_cmp_k_w1[i],
                               nsa_cmp_k_w2[i], nsa_cmp_pos_v[i], nsa_cmp_v_w1[i], nsa_cmp_v_w2[i],
                               diff_lambda_q1[i], diff_lambda_k1[i], diff_lambda_q2[i], diff_lambda_k2[i],
                               diff_norm[i], cd_w_out[i], ln1_g[l], ln1_b[l], lam_init, nsa_tabs)
        xf, xb = _moe(xf, xb, router_w[l], router_b[l], exp_w_gate, exp_w_up, exp_w_down,
                      sh_w_gate[l].astype(BF16), sh_w_up[l].astype(BF16), sh_w_down[l].astype(BF16),
                      ln2_g[l], ln2_b[l], layer=l)
    return xf.reshape(bsz, seq, d)
```

```python
import functools
import math

import jax
import jax.numpy as jnp
from jax import lax
from jax.experimental import pallas as pl
from jax.experimental.pallas import tpu as pltpu

F32 = jnp.float32
BF16 = jnp.bfloat16
HI = lax.Precision.HIGHEST

DEPTH = 4
NEG = -1e30
BIG = 1e9
LN_EPS = 1e-5
RMS_EPS = 1e-6
ALPHA = (2 * DEPTH) ** 0.25

N_BUCKETS = 32
REL_MAX_DIST = 2048

H_A = 12
NOPE = 64
ROPE_DIM = 32
MLA_V = 64
Q_LORA = 256
KV_LORA = 128
ROPE_THETA = 10000.0

DIL_PAIRS = ((128, 1), (512, 4), (2048, 16))
H_B_GROUP = 4
H_B = 12
HD_B = 64

H_C = 8
G_C = 2
R_C = 4
DK_C = 64
CMP_LEN = 32
CMP_STRIDE = 16
CMP_HID = 64
SEL_BLOCK = 64
SEL_TOP = 16
WIN = 512

H_D = 4
DD = 64

N_EXPERTS = 64
TOP_K = 8
N_EXPERT_GROUPS = 8
TOPK_GROUPS = 4
D_EXPERT = 256
ROUTED_SCALE = 2.5

LANE = 128
SUBLANES = 8
MOE_BM = 512
ATT_TQ = 512
ATT_TK = 256
VMEM_LIMIT = 56 * 1024 * 1024

_NT = (((1,), (1,)), ((), ()))


def _cparams(n_axes, vmem=None):
    return pltpu.CompilerParams(dimension_semantics=("arbitrary",) * n_axes, vmem_limit_bytes=vmem)


def _mm_body(x_ref, w_ref, o_ref, *, act, precision):
    y = jnp.dot(x_ref[...], w_ref[...], preferred_element_type=F32, precision=precision)
    if act == "sigmoid":
        y = jax.nn.sigmoid(y)
    o_ref[...] = y.astype(o_ref.dtype)


def _mm(x, w, out_dtype, tn, tm=1024, act=None, precision=None):
    m, k = x.shape
    nc = w.shape[1]
    tm = min(tm, m)
    return pl.pallas_call(
        functools.partial(_mm_body, act=act, precision=precision),
        grid=(m // tm, nc // tn),
        in_specs=[pl.BlockSpec((tm, k), lambda i, j: (i, 0)),
                  pl.BlockSpec((k, tn), lambda i, j: (0, j))],
        out_specs=pl.BlockSpec((tm, tn), lambda i, j: (i, j)),
        out_shape=jax.ShapeDtypeStruct((m, nc), out_dtype),
        compiler_params=_cparams(2, VMEM_LIMIT), name="proj",
    )(x, w)


def _proj_dilated_body(x_ref, w_ref, *rest, tm, gw3):
    o_refs, scr = rest[:-1], rest[-1]
    for gi, (_, dil) in enumerate(DIL_PAIRS):
        y = jnp.dot(x_ref[...], w_ref[:, gi * gw3:(gi + 1) * gw3], preferred_element_type=F32)
        rows = tm // dil
        for c in range(gw3 // LANE):
            scr[c] = y[:, c * LANE:(c + 1) * LANE]
        for r in range(dil):
            for c in range(gw3 // LANE):
                col = r * gw3 + c * LANE
                o_refs[gi][0, :, col:col + LANE] = scr[c, pl.ds(r, rows, stride=dil), :].astype(BF16)


def _proj_dilated(xb, w, bsz, seq, tm=512):
    n, k = xb.shape
    gw3 = w.shape[1] // len(DIL_PAIRS)
    spt = seq // tm
    return pl.pallas_call(
        functools.partial(_proj_dilated_body, tm=tm, gw3=gw3), grid=(n // tm,),
        in_specs=[pl.BlockSpec((tm, k), lambda i: (i, 0)), pl.BlockSpec(w.shape, lambda i: (0, 0))],
        out_specs=[pl.BlockSpec((1, tm // dil, dil * gw3), lambda i: (i // spt, i % spt, 0))
                   for _, dil in DIL_PAIRS],
        out_shape=[jax.ShapeDtypeStruct((bsz, seq // dil, dil * gw3), BF16) for _, dil in DIL_PAIRS],
        scratch_shapes=[pltpu.VMEM((gw3 // LANE, tm, LANE), F32)],
        compiler_params=_cparams(1, VMEM_LIMIT), name="proj_dilated",
    )(xb, w)


def _layer_norm(z, g, b):
    mu = jnp.mean(z, axis=-1, keepdims=True)
    zc = z - mu
    var = jnp.mean(zc * zc, axis=-1, keepdims=True)
    return zc * lax.rsqrt(var + LN_EPS) * g + b


def _out_ln_body(*refs, n_sum):
    a0 = refs[0][...].astype(F32)
    for r in refs[1:n_sum]:
        a0 = a0 + r[...].astype(F32)
    a1_ref, w0_ref, w1_ref, r_ref, g_ref, b_ref, of_ref, ob_ref = refs[n_sum:]
    y = jnp.dot(a0.astype(BF16), w0_ref[...], preferred_element_type=F32)
    y = y + jnp.dot(a1_ref[...], w1_ref[...], preferred_element_type=F32)
    out = _layer_norm(ALPHA * r_ref[...] + y, g_ref[...], b_ref[...])
    of_ref[...] = out
    ob_ref[...] = out.astype(BF16)


def _out_ln(a0s, a1, w0, w1, resid, g, b, tm=512):
    n, d = resid.shape
    k0, k1 = w0.shape[0], w1.shape[0]
    row = lambda i: (i, 0)
    fixed = lambda i: (0, 0)
    return pl.pallas_call(
        functools.partial(_out_ln_body, n_sum=len(a0s)),
        grid=(n // tm,),
        in_specs=[pl.BlockSpec((tm, k0), row)] * len(a0s) + [
            pl.BlockSpec((tm, k1), row), pl.BlockSpec((k0, d), fixed), pl.BlockSpec((k1, d), fixed),
            pl.BlockSpec((tm, d), row), pl.BlockSpec((1, d), fixed), pl.BlockSpec((1, d), fixed)],
        out_specs=[pl.BlockSpec((tm, d), row), pl.BlockSpec((tm, d), row)],
        out_shape=[jax.ShapeDtypeStruct((n, d), F32), jax.ShapeDtypeStruct((n, d), BF16)],
        compiler_params=_cparams(1), name="out_proj_ln",
    )(*a0s, a1, w0, w1, resid, g, b)


def _rms(x, g):
    return x * lax.rsqrt(jnp.mean(x * x, axis=-1, keepdims=True) + RMS_EPS) * g


def _mla_q_body(c_ref, g_ref, w_ref, wr_ref, cos_ref, sin_ref, o_ref):
    cn = _rms(c_ref[...], g_ref[...]).astype(BF16)
    a = jnp.dot(cn, w_ref[...], preferred_element_type=F32)
    r = jnp.dot(cn, wr_ref[...], preferred_element_type=F32)
    cos, sin = cos_ref[...], sin_ref[...]
    for h in range(H_A):
        sl = slice(h * LANE, (h + 1) * LANE)
        o_ref[:, sl] = (a[:, sl] * cos + r[:, sl] * sin).astype(o_ref.dtype)


def _mla_kv_body(c_ref, kr_ref, krr_ref, g_ref, wk_ref, wv_ref, cos_ref, sin_ref, k_ref, v_ref):
    cn = _rms(c_ref[...], g_ref[...]).astype(BF16)
    kn = jnp.dot(cn, wk_ref[...], preferred_element_type=F32)
    rope = kr_ref[...] * cos_ref[...] + krr_ref[...] * sin_ref[...]
    for h in range(H_A):
        sl = slice(h * LANE, (h + 1) * LANE)
        k_ref[:, sl] = (kn[:, sl] + rope).astype(k_ref.dtype)
    v_ref[...] = jnp.dot(cn, wv_ref[...], preferred_element_type=F32).astype(v_ref.dtype)


def _mla_up(h1, q_norm, wq, wq_rot, kv_norm, wk, wv, cos_q, sin_q, cos_k, sin_k, seq, tm=512):
    n = h1.shape[0]
    spt = seq // tm
    row = lambda c: (lambda i: (i, c))
    pos = lambda i: (i % spt, 0)
    fixed = lambda i: (0, 0)
    q_a = pl.pallas_call(
        _mla_q_body, grid=(n // tm,),
        in_specs=[pl.BlockSpec((tm, Q_LORA), row(0)), pl.BlockSpec((1, Q_LORA), fixed),
                  pl.BlockSpec(wq.shape, fixed), pl.BlockSpec(wq_rot.shape, fixed),
                  pl.BlockSpec((tm, LANE), pos), pl.BlockSpec((tm, LANE), pos)],
        out_specs=pl.BlockSpec((tm, H_A * LANE), row(0)),
        out_shape=jax.ShapeDtypeStruct((n, H_A * LANE), BF16),
        compiler_params=_cparams(1), name="mla_q_up",
    )(h1, q_norm, wq, wq_rot, cos_q, sin_q)
    k_a, v_a = pl.pallas_call(
        _mla_kv_body, grid=(n // tm,),
        in_specs=[pl.BlockSpec((tm, LANE), row(2)), pl.BlockSpec((tm, LANE), row(3)),
                  pl.BlockSpec((tm, LANE), row(4)), pl.BlockSpec((1, KV_LORA), fixed),
                  pl.BlockSpec(wk.shape, fixed), pl.BlockSpec(wv.shape, fixed),
                  pl.BlockSpec((tm, LANE), pos), pl.BlockSpec((tm, LANE), pos)],
        out_specs=[pl.BlockSpec((tm, H_A * LANE), row(0)), pl.BlockSpec((tm, H_A * MLA_V), row(0))],
        out_shape=[jax.ShapeDtypeStruct((n, H_A * LANE), BF16), jax.ShapeDtypeStruct((n, H_A * MLA_V), BF16)],
        compiler_params=_cparams(1), name="mla_kv_up",
    )(h1, h1, h1, kv_norm, wk, wv, cos_k, sin_k)
    return q_a, k_a, v_a


def _flash_body(*refs, T, TK, dq, dv, q_offs, k_offs, v_offs, scale, bias_mode, has_sel, has_gate,
                epilogue, want_lse, nback, lam_init, unroll2):
    R = T // TK
    it = iter(refs)
    q_ref, k_ref, v_ref = next(it), next(it), next(it)
    bias_ref = next(it) if bias_mode else None
    sel_ref = next(it) if has_sel else None
    gate_ref = next(it) if has_gate else None
    lam_ref, dn_ref = (next(it), next(it)) if epilogue == "diff" else (None, None)
    o_ref = next(it)
    lse_ref = next(it) if want_lse else None
    vt_scr = next(it)

    qi = pl.program_id(2)
    seq = v_ref.shape[1]

    @pl.when(qi == 0)
    def _():
        for c in range(seq // TK):
            vt_scr[:, c * TK:(c + 1) * TK] = v_ref[0, c * TK:(c + 1) * TK, :].astype(F32).T.astype(BF16)

    qfull = q_ref[0].astype(F32)
    fold_scale = math.frexp(scale)[0] == 0.5
    qts = [(qfull[:, off:off + dq] * (scale if fold_scale else 1.0)).T.astype(BF16) for off in q_offs]

    def qk(kc):
        kfull = k_ref[0, pl.ds(pl.multiple_of(kc * TK, TK), TK), :]
        return tuple(jnp.dot(kfull[:, k_offs[u]:k_offs[u] + dq], qts[u], preferred_element_type=F32)
                     for u in range(2))

    def update(kc, state, scores, diag):
        start = pl.multiple_of(kc * TK, TK)
        new_state = []
        sel_add = None
        if has_sel:
            per = TK // SEL_BLOCK
            rows = [sel_ref[0, 0, pl.ds(kc * per + a, 1), :] for a in range(per)]
            sel_add = jnp.concatenate([jnp.broadcast_to((r - 1.0) * (-NEG), (SEL_BLOCK, T)) for r in rows], axis=0)
        for u in range(2):
            vt = vt_scr[v_offs[u]:v_offs[u] + dv, pl.ds(start, TK)]
            s = scores[u]
            if not fold_scale:
                s = s * scale
            if bias_mode:
                s = s + bias_ref[u if bias_mode == "pair" else 0, R * qi - kc + (R - 1)]
            elif diag is not None:
                key = lax.broadcasted_iota(jnp.int32, (TK, T), 0) + diag * TK
                qry = lax.broadcasted_iota(jnp.int32, (TK, T), 1)
                s = jnp.where(key <= qry, s, NEG)
            if has_sel:
                s = s + sel_add
            m_prev, l_prev, acc_prev = state[u]
            m_new = jnp.maximum(m_prev, jnp.max(s, axis=0, keepdims=True))
            alpha = jnp.exp(m_prev - m_new)
            p = jnp.exp(s - m_new)
            l_new = alpha * l_prev + jnp.sum(p, axis=0, keepdims=True)
            acc_new = alpha * acc_prev + jnp.dot(vt, p.astype(BF16), preferred_element_type=F32)
            new_state.append((m_new, l_new, acc_new))
        return tuple(new_state)

    init = tuple((jnp.full((1, T), NEG, F32), jnp.zeros((1, T), F32), jnp.zeros((dv, T), F32)) for _ in range(2))
    lo = 0 if nback is None else jnp.maximum(R * qi - nback, 0)

    def step(kc, carry):
        state, scores = carry
        nxt = qk(kc + 1)
        return update(kc, state, scores, None), nxt

    if unroll2:
        assert R % 2 == 0 and nback is None
        state, scores = lax.fori_loop(0, (R * qi - lo) // 2,
                                      lambda j, c: step(lo + 2 * j + 1, step(lo + 2 * j, c)), (init, qk(lo)))
    else:
        state, scores = lax.fori_loop(lo, R * qi, step, (init, qk(lo)))
    for a in range(R):
        nxt = qk(R * qi + a + 1) if a + 1 < R else None
        state = update(R * qi + a, state, scores, a)
        scores = nxt

    outs = [acc / l for _, l, acc in state]
    if epilogue == "diff":
        a = outs[0] - lam_ref[0, 0] * outs[1]
        rinv = lax.rsqrt(jnp.mean(a * a, axis=0, keepdims=True) + RMS_EPS)
        o = (a * rinv * dn_ref[...] * (1.0 - lam_init)).T
    else:
        o = jnp.concatenate(outs, axis=0).T
        if has_gate:
            o = o * gate_ref[0]
    o_ref[0] = o.astype(o_ref.dtype)
    if want_lse:
        lse_ref[0, 0] = jnp.concatenate([m + jnp.log(l) for m, l, _ in state], axis=0)


def _flash(q, k, v, *, n_outer, T, dq, dv, q_col, k_col, v_col, q_offs, k_offs, v_offs, scale,
           out_cols, out_col, bias=None, bias_mode=None, bias_idx=None, sel=None, sel_idx=None,
           gate=None, gate_col=None, lam=None, dnorm=None, lam_init=0.0, epilogue="plain", want_lse=False,
           nback=None, vmem=None, k_w=None, v_w=None, name="flash", TK=None, unroll2=False,
           rep=1, rep_in=0, rep_out=0):
    bsz, seq, _ = q.shape
    TK = TK or T
    assert T % TK == 0
    nq = seq // T
    qw = max(o + dq for o in q_offs)
    kw = k_w or max(o + dq for o in k_offs)
    vw = v_w or max(o + dv for o in v_offs)
    ow = dv if epilogue == "diff" else 2 * dv
    in_specs = [pl.BlockSpec((1, T, qw), lambda g, b, i: (b // rep, i, q_col(g) + (b % rep) * rep_in)),
                pl.BlockSpec((1, seq, kw), lambda g, b, i: (b // rep, 0, k_col(g) + (b % rep) * rep_in)),
                pl.BlockSpec((1, seq, vw), lambda g, b, i: (b // rep, 0, v_col(g) + (b % rep) * rep_in))]
    args = [q, k, v]
    if bias_mode:
        nb = 2 if bias_mode == "pair" else 1
        in_specs.append(pl.BlockSpec((nb,) + bias.shape[1:], lambda g, b, i: (bias_idx(g), 0, 0, 0)))
        args.append(bias)
    if sel is not None:
        in_specs.append(pl.BlockSpec((1, 1, sel.shape[2], T), lambda g, b, i: (b, sel_idx(g), 0, i)))
        args.append(sel)
    if gate is not None:
        in_specs.append(pl.BlockSpec((1, T, ow), lambda g, b, i: (b, i, gate_col(g))))
        args.append(gate)
    if epilogue == "diff":
        in_specs.append(pl.BlockSpec(memory_space=pltpu.SMEM))
        in_specs.append(pl.BlockSpec((dv, 1), lambda g, b, i: (0, 0)))
        args += [lam, dnorm]
    out_specs = [pl.BlockSpec((1, T, ow), lambda g, b, i: (b // rep, i, out_col(g) + (b % rep) * rep_out))]
    out_shape = [jax.ShapeDtypeStruct((bsz, seq, out_cols), BF16)]
    if want_lse:
        out_specs.append(pl.BlockSpec((1, 1, 2, T), lambda g, b, i: (b, g, 0, i)))
        out_shape.append(jax.ShapeDtypeStruct((bsz * rep, n_outer, 2, seq), F32))
    body = functools.partial(
        _flash_body, T=T, TK=TK, dq=dq, dv=dv, q_offs=q_offs, k_offs=k_offs, v_offs=v_offs, scale=scale,
        bias_mode=bias_mode, has_sel=sel is not None, has_gate=gate is not None, epilogue=epilogue,
        want_lse=want_lse, nback=nback, lam_init=lam_init, unroll2=unroll2)
    res = pl.pallas_call(
        body, grid=(n_outer, bsz * rep, nq), in_specs=in_specs, out_specs=out_specs, out_shape=out_shape,
        scratch_shapes=[pltpu.VMEM((vw, seq), BF16)],
        compiler_params=_cparams(3, vmem), name=name,
    )(*args)
    return res if want_lse else res[0]


def _rel_bucket(dist):
    n = jnp.maximum(dist, 0)
    exact = N_BUCKETS // 2
    log_ratio = jnp.log(jnp.maximum(n, 1).astype(F32) / exact) / math.log(REL_MAX_DIST / exact)
    large = exact + (log_ratio * (N_BUCKETS - exact)).astype(jnp.int32)
    return jnp.where(n < exact, n, jnp.minimum(large, N_BUCKETS - 1))


def _toeplitz_bias(tab, T, n_d, dist_scale, max_dist, TK=None):
    TK = TK or T
    R = T // TK
    wlen = T + TK
    x = jnp.arange(wlen)
    dist = (jnp.arange(n_d)[:, None] - (R - 1)) * TK + jnp.where(x < T, x, x - wlen)[None, :]
    w = tab[_rel_bucket(dist * dist_scale)].astype(F32)
    w = jnp.where(((dist >= 0) & (dist <= max_dist))[..., None], w, NEG)
    return _toeplitz(jnp.moveaxis(w, -1, 0), TK, T)


def _toeplitz_body(w_ref, o_ref, *, rows, cols):
    x = jnp.broadcast_to(w_ref[0], (rows, w_ref.shape[-1]))
    o_ref[0] = pltpu.roll(x, 0, 1, stride=1, stride_axis=0)[:, :cols]


def _toeplitz(w, rows, cols):
    wlen = w.shape[-1]
    w2 = w.reshape(-1, 1, wlen)
    out = pl.pallas_call(
        functools.partial(_toeplitz_body, rows=rows, cols=cols), grid=(w2.shape[0],),
        in_specs=[pl.BlockSpec((1, 1, wlen), lambda i: (i, 0, 0))],
        out_specs=pl.BlockSpec((1, rows, cols), lambda i: (i, 0, 0)),
        out_shape=jax.ShapeDtypeStruct((w2.shape[0], rows, cols), w.dtype),
        compiler_params=_cparams(1), name="toeplitz",
    )(w2)
    return out.reshape(w.shape[:-1] + (rows, cols))


def _compress_body(u_ref, pe_ref, w1_ref, w2_ref, o_ref, *, ncp):
    outs = []
    for a in range(2):
        u = u_ref[0, 0, a].astype(F32)
        p1 = jnp.dot(u + pe_ref[a, 0], w1_ref[a, 0], preferred_element_type=F32, precision=HI)
        p2 = jnp.dot(u + pe_ref[a, 1], w1_ref[a, 1], preferred_element_type=F32, precision=HI)
        hid = jax.nn.gelu(p1 + pltpu.roll(p2, ncp - 1, 0))
        outs.append(jnp.dot(hid, w2_ref[a], preferred_element_type=F32, precision=HI))
    o_ref[0, 0] = jnp.concatenate(outs, axis=-1)


def _compress(u, pe, w1, w2):
    bsz, g, _, ncp, width = u.shape
    return pl.pallas_call(
        functools.partial(_compress_body, ncp=ncp), grid=(bsz, g),
        in_specs=[pl.BlockSpec((1, 1, 2, ncp, width), lambda b, gg: (b, gg, 0, 0, 0)),
                  pl.BlockSpec(pe.shape, lambda b, gg: (0, 0, 0, 0)),
                  pl.BlockSpec(w1.shape, lambda b, gg: (0, 0, 0, 0)),
                  pl.BlockSpec(w2.shape, lambda b, gg: (0, 0, 0))],
        out_specs=pl.BlockSpec((1, 1, ncp, 2 * DK_C), lambda b, gg: (b, gg, 0, 0)),
        out_shape=jax.ShapeDtypeStruct((bsz, g, ncp, 2 * DK_C), F32),
        compiler_params=_cparams(2), name="nsa_compress",
    )(u, pe, w1, w2)


def _cmp_attn_body(q_ref, kv_ref, bias_ref, ov_ref, gate_ref, o_ref, sel_ref, *, T, ncp, n_sel, n_top, scale):
    qi = pl.program_id(2)
    kc = kv_ref[0, 0, :, :DK_C]
    vc = kv_ref[0, 0, :, DK_C:]
    t = qi * T + lax.broadcasted_iota(jnp.int32, (T, ncp), 0)
    c = lax.broadcasted_iota(jnp.int32, (T, ncp), 1)
    valid = t >= c * CMP_STRIDE + (CMP_LEN - 1)
    validf = valid.astype(F32)
    psum = jnp.zeros((T, ncp), F32)
    outs = []
    for r in range(R_C):
        q = q_ref[0, :, r * DK_C:(r + 1) * DK_C].astype(F32)
        s = lax.dot_general(q, kc, _NT, preferred_element_type=F32, precision=HI) * scale + bias_ref[r]
        s = jnp.where(valid, s, NEG)
        e = jnp.exp(s - jnp.max(s, axis=-1, keepdims=True)) * validf
        p = e / jnp.maximum(jnp.sum(e, axis=-1, keepdims=True), 1e-30)
        outs.append(jnp.dot(p, vc, preferred_element_type=F32, precision=HI))
        psum = psum + p
    o_ref[0] = (jnp.concatenate(outs, axis=-1) * gate_ref[0]).astype(o_ref.dtype)

    imp = jnp.dot(ov_ref[...], psum.T, preferred_element_type=F32, precision=HI)
    tq = qi * T + lax.broadcasted_iota(jnp.int32, (n_sel, T), 1)
    j = lax.broadcasted_iota(jnp.int32, (n_sel, T), 0)
    forced = (j == tq // SEL_BLOCK) | (j == 0)
    work = jnp.where(forced, BIG, jnp.where(j * SEL_BLOCK <= tq, imp, -BIG))
    sel = jnp.zeros((n_sel, T), F32)
    jf = j.astype(F32)
    for _ in range(n_top):
        _, _, pick = _first_max(work, jf, n_sel)
        sel = jnp.where(pick, 1.0, sel)
        work = jnp.where(pick, -jnp.inf, work)
    sel_ref[0, 0] = sel


def _cmp_attn(h, kvc, bias_c, overlap, gates, T=256):
    bsz, seq, _ = h.shape
    ncp = kvc.shape[2]
    n_sel = seq // SEL_BLOCK
    n_top = min(SEL_TOP, n_sel)
    qw = R_C * DK_C
    return pl.pallas_call(
        functools.partial(_cmp_attn_body, T=T, ncp=ncp, n_sel=n_sel, n_top=n_top, scale=DK_C ** -0.5),
        grid=(G_C, bsz, seq // T),
        in_specs=[pl.BlockSpec((1, T, qw), lambda g, b, i: (b, i, g)),
                  pl.BlockSpec((1, 1, ncp, 2 * DK_C), lambda g, b, i: (b, g, 0, 0)),
                  pl.BlockSpec((R_C, T, ncp), lambda g, b, i: (g, i, 0)),
                  pl.BlockSpec(overlap.shape, lambda g, b, i: (0, 0)),
                  pl.BlockSpec((1, T, qw), lambda g, b, i: (b, i, g))],
        out_specs=[pl.BlockSpec((1, T, qw), lambda g, b, i: (b, i, g)),
                   pl.BlockSpec((1, 1, n_sel, T), lambda g, b, i: (b, g, 0, i))],
        out_shape=[jax.ShapeDtypeStruct((bsz, seq, H_C * DK_C), BF16),
                   jax.ShapeDtypeStruct((bsz, G_C, n_sel, seq), F32)],
        compiler_params=_cparams(3), name="nsa_cmp_attn",
    )(h, kvc, bias_c, overlap, gates)


def _first_max(work, idx, n):
    mx = jnp.max(work, axis=0, keepdims=True)
    first = jnp.min(jnp.where(work == mx, idx, float(n)), axis=0, keepdims=True)
    return mx, first, idx == first


def _router_body(x_ref, wt_ref, b_ref, tri_ref, e_ref, g_ref, r_ref, cnt_ref, carry_scr, *, tm):
    i = pl.program_id(0)

    @pl.when(i == 0)
    def _():
        carry_scr[...] = jnp.zeros(carry_scr.shape, F32)

    st = lax.dot_general(wt_ref[...], x_ref[...], _NT, preferred_element_type=F32, precision=HI)
    scores = jax.nn.sigmoid(st)
    sel = scores + b_ref[...]
    per = N_EXPERTS // N_EXPERT_GROUPS
    fiota = lambda rows: lax.broadcasted_iota(jnp.int32, (rows, tm), 0).astype(F32)
    i_per, i_grp, i_exp = fiota(per), fiota(N_EXPERT_GROUPS), fiota(N_EXPERTS)
    grp_scores = []
    for g in range(N_EXPERT_GROUPS):
        blk = sel[g * per:(g + 1) * per]
        m1, _, pick = _first_max(blk, i_per, per)
        grp_scores.append(m1 + jnp.max(jnp.where(pick, -jnp.inf, blk), axis=0, keepdims=True))
    work = jnp.concatenate(grp_scores, axis=0)
    gmask = jnp.zeros((N_EXPERT_GROUPS, tm), F32)
    for _ in range(TOPK_GROUPS):
        _, _, pick = _first_max(work, i_grp, N_EXPERT_GROUPS)
        gmask = jnp.where(pick, 1.0, gmask)
        work = jnp.where(pick, -jnp.inf, work)
    work = jnp.concatenate([jnp.where(gmask[g:g + 1] > 0.5, sel[g * per:(g + 1) * per], NEG)
                            for g in range(N_EXPERT_GROUPS)], axis=0)
    picks, firsts, vals = [], [], []
    for _ in range(TOP_K):
        _, first, pick = _first_max(work, i_exp, N_EXPERTS)
        picks.append(pick)
        firsts.append(first)
        vals.append(jnp.sum(jnp.where(pick, scores, 0.0), axis=0, keepdims=True))
        work = jnp.where(pick, -jnp.inf, work)
    val = jnp.concatenate(vals, axis=0)
    g_ref[...] = val / jnp.sum(val, axis=0, keepdims=True) * ROUTED_SCALE
    e_ref[...] = jnp.concatenate(firsts, axis=0).astype(jnp.int32)
    onehot = picks[0].astype(F32)
    for pick in picks[1:]:
        onehot = onehot + pick.astype(F32)
    before = jnp.dot(onehot.astype(BF16), tri_ref[...], preferred_element_type=F32) + carry_scr[...]
    r_ref[...] = jnp.concatenate([jnp.sum(jnp.where(pick, before, 0.0), axis=0, keepdims=True)
                                  for pick in picks], axis=0).astype(jnp.int32)
    carry = carry_scr[...] + jnp.sum(onehot, axis=1, keepdims=True)
    carry_scr[...] = carry
    cnt_ref[...] = jnp.broadcast_to(carry, cnt_ref.shape)


def _router(xf, router_w, router_b, tm=512):
    n, d = xf.shape
    tri = (jnp.arange(tm)[:, None] < jnp.arange(tm)[None, :]).astype(BF16)
    col = lambda i: (0, i)
    fixed = lambda i: (0, 0)
    return pl.pallas_call(
        functools.partial(_router_body, tm=tm), grid=(n // tm,),
        in_specs=[pl.BlockSpec((tm, d), lambda i: (i, 0)), pl.BlockSpec((N_EXPERTS, d), fixed),
                  pl.BlockSpec((N_EXPERTS, 1), fixed), pl.BlockSpec((tm, tm), fixed)],
        out_specs=[pl.BlockSpec((TOP_K, tm), col), pl.BlockSpec((TOP_K, tm), col),
                   pl.BlockSpec((TOP_K, tm), col), pl.BlockSpec((N_EXPERTS, LANE), fixed)],
        out_shape=[jax.ShapeDtypeStruct((TOP_K, n), jnp.int32), jax.ShapeDtypeStruct((TOP_K, n), F32),
                   jax.ShapeDtypeStruct((TOP_K, n), jnp.int32), jax.ShapeDtypeStruct((N_EXPERTS, LANE), F32)],
        scratch_shapes=[pltpu.VMEM((N_EXPERTS, 1), F32)],
        compiler_params=_cparams(1), name="router",
    )(xf, router_w.T, router_b.astype(F32)[:, None], tri)


def _pack_rows(x):
    w = x.shape[-1] // 2
    lo = lax.bitcast_convert_type(x[:, :w].astype(BF16).astype(F32), jnp.uint32)
    hi = lax.bitcast_convert_type(x[:, w:].astype(BF16).astype(F32), jnp.uint32)
    return (lo >> 16) | (hi & jnp.uint32(0xFFFF0000))


def _unpack_rows(p):
    lo = lax.bitcast_convert_type(p << 16, F32)
    hi = lax.bitcast_convert_type(p & jnp.uint32(0xFFFF0000), F32)
    return lo, hi


def _dispatch_body(fill_ref, pos_ref, x_ref, z_ref, xs_hbm, xp_scr, sem, *, tm, n_blocks):
    i = pl.program_id(0)
    slot = i % 2
    xp_scr[slot] = _pack_rows(x_ref[...]).reshape(tm // SUBLANES, SUBLANES, -1)
    zsem = sem.at[2]

    def row_copy(s, grp, j, dst):
        return pltpu.make_async_copy(xp_scr.at[s, grp, pl.ds(j, 1)], xs_hbm.at[pl.ds(dst, 1)], sem.at[s])

    def zero_rows(dst, size):
        return pltpu.make_async_copy(z_ref.at[pl.ds(0, size)], xs_hbm.at[pl.ds(dst, size)], zsem)

    def zero_block(blk):
        return pltpu.make_async_copy(z_ref, xs_hbm.at[pl.ds(blk * MOE_BM, MOE_BM)], zsem)

    def issue(grp, c):
        for j in range(SUBLANES):
            for k in range(TOP_K):
                row_copy(slot, grp, j, pos_ref[k, grp * SUBLANES + j]).start(priority=k % 2)
        return c

    def drain(s):
        def body(grp, c):
            for _ in range(SUBLANES * TOP_K):
                row_copy(s, 0, 0, 0).wait()
            return c

        lax.fori_loop(0, tm // SUBLANES, body, 0)

    lax.fori_loop(0, tm // SUBLANES, issue, 0)

    @pl.when(i > 0)
    def _():
        drain(1 - slot)

    @pl.when(i == pl.num_programs(0) - 1)
    def _():
        drain(slot)

    @pl.when(i == 0)
    def _():
        def fill(wait):
            def go(cp):
                if wait:
                    cp.wait()
                else:
                    cp.start()

            def body(e, c):
                lo, cnt = fill_ref[2 * e], fill_ref[2 * e + 1] - fill_ref[2 * e]

                def one_row(r, cc):
                    go(zero_rows(0 if wait else r, 1))
                    return cc

                head = jnp.minimum((-lo) & (SUBLANES - 1), cnt)
                lax.fori_loop(lo, lo + head, one_row, 0)
                base, rem = lo + head, cnt - head
                for bit in reversed(range(SUBLANES.bit_length() - 1, MOE_BM.bit_length() - 1)):
                    size = 1 << bit

                    @pl.when((rem & size) != 0)
                    def _():
                        off = base + ((rem >> (bit + 1)) << (bit + 1))
                        go(zero_rows(0 if wait else pl.multiple_of(off, SUBLANES), size))

                lax.fori_loop(base + (rem & -SUBLANES), base + rem, one_row, 0)
                return c

            lax.fori_loop(0, N_EXPERTS, body, 0)

        fill(False)
        fill(True)

        def tail_start(blk, c):
            zero_block(blk).start()
            return c

        def tail_wait(blk, c):
            zero_block(0).wait()
            return c

        lax.fori_loop(fill_ref[2 * N_EXPERTS], n_blocks, tail_start, 0)
        lax.fori_loop(fill_ref[2 * N_EXPERTS], n_blocks, tail_wait, 0)


def _dispatch(xb, pos, fill, p, tm=128):
    n, d = xb.shape
    grid_spec = pltpu.PrefetchScalarGridSpec(
        num_scalar_prefetch=1, grid=(n // tm,),
        in_specs=[pl.BlockSpec((TOP_K, tm), lambda i, fl: (0, i), memory_space=pltpu.SMEM),
                  pl.BlockSpec((tm, d), lambda i, fl: (i, 0)),
                  pl.BlockSpec((MOE_BM, d // 2), lambda i, fl: (0, 0))],
        out_specs=pl.BlockSpec(memory_space=pl.ANY),
        scratch_shapes=[pltpu.VMEM((2, tm // SUBLANES, SUBLANES, d // 2), jnp.uint32),
                        pltpu.SemaphoreType.DMA((3,))])
    return pl.pallas_call(
        functools.partial(_dispatch_body, tm=tm, n_blocks=p // MOE_BM), grid_spec=grid_spec,
        out_shape=jax.ShapeDtypeStruct((p, d // 2), jnp.uint32),
        compiler_params=_cparams(1), name="dispatch",
    )(fill, pos, xb, jnp.zeros((MOE_BM, d // 2), jnp.uint32))


def _moe_ffn_body(be_ref, nb_ref, x_ref, wg_ref, wu_ref, wd_ref, o_ref, wg_scr, wu_scr, wd_scr):
    i = pl.program_id(0)
    half = x_ref.shape[1]

    @pl.when((i == 0) | (be_ref[i] != be_ref[jnp.maximum(i - 1, 0)]))
    def _():
        wg_scr[...] = wg_ref[0, 0].astype(BF16)
        wu_scr[...] = wu_ref[0, 0].astype(BF16)
        wd_scr[...] = wd_ref[0, 0].astype(BF16)

    @pl.when(i < nb_ref[0])
    def _():
        lo, hi = _unpack_rows(x_ref[...])
        lo, hi = lo.astype(BF16), hi.astype(BF16)
        hg = (jnp.dot(lo, wg_scr[:half], preferred_element_type=F32)
              + jnp.dot(hi, wg_scr[half:], preferred_element_type=F32))
        hu = (jnp.dot(lo, wu_scr[:half], preferred_element_type=F32)
              + jnp.dot(hi, wu_scr[half:], preferred_element_type=F32))
        hb = (hg * jax.nn.sigmoid(hg) * hu).astype(BF16)
        o_ref[...] = _pack_rows(jnp.dot(hb, wd_scr[...], preferred_element_type=F32))

    @pl.when(i >= nb_ref[0])
    def _():
        o_ref[...] = jnp.zeros(o_ref.shape, o_ref.dtype)


def _moe_ffn(xs, blk_e, n_used, wg, wu, wd, layer):
    p, half = xs.shape
    d = 2 * half
    n_blocks = p // MOE_BM
    grid_spec = pltpu.PrefetchScalarGridSpec(
        num_scalar_prefetch=2, grid=(n_blocks,),
        in_specs=[pl.BlockSpec((MOE_BM, half), lambda i, be, nb: (i, 0)),
                  pl.BlockSpec((1, 1, d, D_EXPERT), lambda i, be, nb: (layer, be[i], 0, 0)),
                  pl.BlockSpec((1, 1, d, D_EXPERT), lambda i, be, nb: (layer, be[i], 0, 0)),
                  pl.BlockSpec((1, 1, D_EXPERT, d), lambda i, be, nb: (layer, be[i], 0, 0))],
        out_specs=pl.BlockSpec((MOE_BM, half), lambda i, be, nb: (i, 0)),
        scratch_shapes=[pltpu.VMEM((d, D_EXPERT), BF16), pltpu.VMEM((d, D_EXPERT), BF16),
                        pltpu.VMEM((D_EXPERT, d), BF16)])
    return pl.pallas_call(
        _moe_ffn_body, grid_spec=grid_spec, out_shape=jax.ShapeDtypeStruct((p, half), jnp.uint32),
        compiler_params=_cparams(1), name="expert_ffn",
    )(blk_e, n_used, xs, wg, wu, wd)


def _combine_body(pos_ref, posn_ref, gate_ref, xb_ref, xf_ref, y_hbm, sg_ref, su_ref, sd_ref,
                  g_ref, b_ref, of_ref, ob_ref, buf, sem, *, tm):
    i = pl.program_id(0)
    slot = i % 2

    def row_copy(s, k, grp, j, src):
        return pltpu.make_async_copy(y_hbm.at[pl.ds(src, 1)], buf.at[s, k, grp, pl.ds(j, 1)], sem.at[s])

    def issue_tile(pr, s):
        def issue(grp, c):
            for j in range(SUBLANES):
                for k in range(TOP_K):
                    row_copy(s, k, grp, j, pr[k, grp * SUBLANES + j]).start(priority=k % 2)
            return c

        lax.fori_loop(0, tm // SUBLANES, issue, 0)

    @pl.when(i == 0)
    def _():
        issue_tile(pos_ref, 0)

    @pl.when(i + 1 < pl.num_programs(0))
    def _():
        issue_tile(posn_ref, 1 - slot)

    x = xb_ref[...]
    hg = jnp.dot(x, sg_ref[...], preferred_element_type=F32)
    hu = jnp.dot(x, su_ref[...], preferred_element_type=F32)
    hb = (hg * jax.nn.sigmoid(hg) * hu).astype(BF16)
    y = jnp.dot(hb, sd_ref[...], preferred_element_type=F32)

    def drain(grp, c):
        for _ in range(SUBLANES * TOP_K):
            row_copy(slot, 0, 0, 0, 0).wait()
        return c

    lax.fori_loop(0, tm // SUBLANES, drain, 0)
    gate = gate_ref[...]
    half = buf.shape[-1]
    y_lo, y_hi = y[:, :half], y[:, half:]
    for k in range(TOP_K):
        lo, hi = _unpack_rows(buf[slot, k].reshape(tm, half))
        y_lo = y_lo + gate[:, k:k + 1] * lo
        y_hi = y_hi + gate[:, k:k + 1] * hi
    out = _layer_norm(ALPHA * xf_ref[...] + jnp.concatenate([y_lo, y_hi], axis=-1), g_ref[...], b_ref[...])
    of_ref[...] = out
    ob_ref[...] = out.astype(BF16)


def _combine(pos, gate, xb, xf, y, sg, su, sd, g, b, tm=128):
    n, d = xf.shape
    n_tiles = n // tm
    row = lambda i: (i, 0)
    fixed = lambda i: (0, 0)
    smem_col = pl.BlockSpec((TOP_K, tm), lambda i: (0, i), memory_space=pltpu.SMEM)
    smem_next = pl.BlockSpec((TOP_K, tm), lambda i: (0, jnp.minimum(i + 1, n_tiles - 1)),
                             memory_space=pltpu.SMEM)
    grid_spec = pltpu.PrefetchScalarGridSpec(
        num_scalar_prefetch=0, grid=(n_tiles,),
        in_specs=[smem_col, smem_next, pl.BlockSpec((tm, TOP_K), row),
                  pl.BlockSpec((tm, d), row), pl.BlockSpec((tm, d), row), pl.BlockSpec(memory_space=pl.ANY),
                  pl.BlockSpec(sg.shape, fixed), pl.BlockSpec(su.shape, fixed), pl.BlockSpec(sd.shape, fixed),
                  pl.BlockSpec((1, d), fixed), pl.BlockSpec((1, d), fixed)],
        out_specs=[pl.BlockSpec((tm, d), row), pl.BlockSpec((tm, d), row)],
        scratch_shapes=[pltpu.VMEM((2, TOP_K, tm // SUBLANES, SUBLANES, d // 2), jnp.uint32),
                        pltpu.SemaphoreType.DMA((2,))])
    return pl.pallas_call(
        functools.partial(_combine_body, tm=tm), grid_spec=grid_spec,
        out_shape=[jax.ShapeDtypeStruct((n, d), F32), jax.ShapeDtypeStruct((n, d), BF16)],
        compiler_params=_cparams(1), name="combine",
    )(pos, pos, gate, xb, xf, y, sg, su, sd, g, b)


def _moe(xf, xb, router_w, router_b, w_gate, w_up, w_down, sh_gate, sh_up, sh_down, ln_g, ln_b, layer=0):
    n, d = xf.shape
    e_idx, gate, rank, cnt = _router(xf, router_w, router_b)
    counts = cnt[:, 0].astype(jnp.int32)
    padded = (counts + MOE_BM - 1) // MOE_BM * MOE_BM
    pad_end = jnp.cumsum(padded)
    pad_start = (pad_end - padded).astype(jnp.int32)
    n_blocks = (n * TOP_K + N_EXPERTS * (MOE_BM - 1) + MOE_BM - 1) // MOE_BM
    blk_e = jnp.minimum(jnp.sum(pad_end[None, :] <= (jnp.arange(n_blocks) * MOE_BM)[:, None], axis=1),
                        N_EXPERTS - 1).astype(jnp.int32)
    n_used = (pad_end[-1] // MOE_BM).astype(jnp.int32).reshape(1)
    fill = jnp.concatenate([jnp.stack([pad_start + counts, pad_end], axis=1).reshape(-1), n_used]).astype(jnp.int32)
    experts = jnp.arange(N_EXPERTS, dtype=jnp.int32)[:, None, None]
    pos = rank + jnp.sum(jnp.where(e_idx[None] == experts, pad_start[:, None, None], 0), axis=0)
    xs = _dispatch(xb, pos, fill, n_blocks * MOE_BM)
    y = _moe_ffn(xs, blk_e, n_used, w_gate, w_up, w_down, layer)
    return _combine(pos, gate.T, xb, xf, y, sh_gate, sh_up, sh_down, ln_g[None], ln_b[None])


def _rope_tables(seq):
    half = ROPE_DIM // 2
    freqs = ROPE_THETA ** (-jnp.arange(half, dtype=F32) / half)
    ang = jnp.arange(seq).astype(F32)[:, None] * freqs
    cos = jnp.concatenate([jnp.cos(ang)] * 2, -1)
    sin = jnp.concatenate([jnp.sin(ang)] * 2, -1)
    z = lambda w: jnp.zeros((seq, w), F32)
    pad = LANE - NOPE - ROPE_DIM
    cos_q = jnp.concatenate([jnp.ones((seq, NOPE), F32), cos, z(pad)], -1)
    sin_q = jnp.concatenate([z(NOPE), sin, z(pad)], -1)
    cos_k = jnp.concatenate([z(NOPE), cos, z(pad)], -1)
    return cos_q, sin_q, cos_k, sin_q


def _rot_cols(w):
    half = w.shape[-1] // 2
    return jnp.concatenate([-w[..., half:], w[..., :half]], -1)


def _mixer_ab(xb, xf, bsz, seq, w_in, q_norm, w_uq, kv_norm, w_ukv, w_out, ln_g, ln_b, rope_tabs, dil_bias):
    n = bsz * seq
    d = w_in.shape[0]
    c0 = Q_LORA + KV_LORA
    w_kr = w_in[:, c0:c0 + ROPE_DIM]
    zc = lambda w: jnp.zeros((d, w), F32)
    pad = LANE - NOPE - ROPE_DIM
    w1 = jnp.concatenate([w_in[:, :c0], zc(NOPE), w_kr, zc(pad), zc(NOPE), _rot_cols(w_kr), zc(pad)], 1)
    h1 = _mm(xb, w1.astype(BF16), F32, tn=w1.shape[1])
    gw = H_B_GROUP * HD_B
    w_b = w_in[:, c0 + ROPE_DIM:].reshape(d, 3, len(DIL_PAIRS), gw).transpose(0, 2, 1, 3).reshape(d, -1)
    h2 = _proj_dilated(xb, w_b.astype(BF16), bsz, seq)

    wq = w_uq.reshape(Q_LORA, H_A, NOPE + ROPE_DIM)
    zq = jnp.zeros((Q_LORA, H_A, pad), F32)
    wq_main = jnp.concatenate([wq, zq], -1).reshape(Q_LORA, H_A * LANE)
    wq_rot = jnp.concatenate([jnp.zeros((Q_LORA, H_A, NOPE), F32), _rot_cols(wq[..., NOPE:]), zq], -1)
    wq_rot = wq_rot.reshape(Q_LORA, H_A * LANE)
    wkv = w_ukv.reshape(KV_LORA, H_A, NOPE + MLA_V)
    wk = jnp.concatenate([wkv[..., :NOPE], jnp.zeros((KV_LORA, H_A, LANE - NOPE), F32)], -1)
    wk = wk.reshape(KV_LORA, H_A * LANE)
    wv = wkv[..., NOPE:].reshape(KV_LORA, H_A * MLA_V)
    q_a, k_a, v_a = _mla_up(h1, q_norm[None], wq_main.astype(BF16), wq_rot.astype(BF16), kv_norm[None],
                            wk.astype(BF16), wv.astype(BF16), *rope_tabs, seq)
    o_a = _flash(q_a.reshape(bsz, seq, -1), k_a.reshape(bsz, seq, -1), v_a.reshape(bsz, seq, -1), name="mla_attn",
                 n_outer=H_A // 2, T=min(ATT_TQ, seq), TK=ATT_TK, unroll2=True, dq=LANE, dv=MLA_V,
                 q_col=lambda g: g, k_col=lambda g: g, v_col=lambda g: g,
                 q_offs=(0, LANE), k_offs=(0, LANE), v_offs=(0, MLA_V),
                 scale=(NOPE + ROPE_DIM) ** -0.5, out_cols=H_A * MLA_V, out_col=lambda g: g)

    gb = gw // LANE
    outs, lses = [], []
    for gi, (window, dil) in enumerate(DIL_PAIRS):
        L = seq // dil
        t = h2[gi]
        o, lse = _flash(t, t, t, name="dilated_attn", n_outer=2, T=min(256, L), dq=HD_B, dv=HD_B,
                        q_col=lambda g: g, k_col=lambda g: gb + g, v_col=lambda g: 2 * gb + g,
                        q_offs=(0, HD_B), k_offs=(0, HD_B), v_offs=(0, HD_B), scale=HD_B ** -0.5,
                        out_cols=dil * gw, out_col=lambda g: g, bias=dil_bias[gi], bias_mode="pair",
                        bias_idx=lambda g: g, want_lse=True, nback=1,
                        rep=dil, rep_in=3 * gb, rep_out=gb)
        outs.append(o.reshape(bsz, seq, H_B_GROUP, HD_B).astype(F32))
        lses.append(lse.reshape(bsz, dil, H_B_GROUP, L).transpose(0, 3, 1, 2).reshape(bsz, seq, H_B_GROUP))
    w = jax.nn.softmax(jnp.stack(lses), axis=0)
    o_b = jnp.sum(w[..., None] * jnp.stack(outs), axis=0).astype(BF16).reshape(n, gw)
    na = H_A * MLA_V
    return _out_ln([o_a.reshape(n, na)], o_b, w_out[:na].astype(BF16), w_out[na:].astype(BF16),
                   xf, ln_g[None], ln_b[None])


def _mixer_cd(xb, xf, bsz, seq, w_in, pos_k, k_w1, k_w2, pos_v, v_w1, v_w2, lq1, lk1, lq2, lk2, d_norm,
              w_out, ln_g, ln_b, lam_init, tabs):
    n = bsz * seq
    qc_w = H_C * DK_C
    kv_w = G_C * DK_C
    off = qc_w
    kvs = []
    for _ in range(3):
        wk_ = w_in[:, off:off + kv_w].reshape(-1, G_C, DK_C)
        wv_ = w_in[:, off + kv_w:off + 2 * kv_w].reshape(-1, G_C, DK_C)
        kvs.append(jnp.concatenate([wk_, wv_], -1).reshape(-1, 2 * kv_w))
        off += 2 * kv_w
    g_off = off
    d_off = off + 3 * H_C
    w_main = jnp.concatenate([w_in[:, :qc_w]] + kvs + [w_in[:, d_off:]], 1)
    h = _mm(xb, w_main.astype(BF16), BF16, tn=w_main.shape[1] // 2).reshape(bsz, seq, -1)
    w_g = jnp.repeat(w_in[:, g_off:d_off], DK_C, axis=1)
    gates = _mm(xb, w_g.astype(BF16), F32, tn=w_g.shape[1] // 2, act="sigmoid").reshape(bsz, seq, -1)

    ncp = seq // CMP_STRIDE
    half = CMP_STRIDE * DK_C
    kv_cmp = h[:, :, qc_w:qc_w + 2 * kv_w].reshape(bsz, ncp, CMP_STRIDE, G_C, 2, DK_C)
    u = kv_cmp.transpose(0, 3, 4, 1, 2, 5).reshape(bsz, G_C, 2, ncp, half)
    pe = jnp.stack([pos_k.reshape(2, 1, half), pos_v.reshape(2, 1, half)])
    w1 = jnp.stack([k_w1.reshape(2, half, CMP_HID), v_w1.reshape(2, half, CMP_HID)])
    w2 = jnp.stack([k_w2, v_w2])
    kvc = _compress(u, pe, w1, w2)
    o_cmp, sel = _cmp_attn(h, kvc, tabs["bias_c"], tabs["overlap"], gates)

    cb = qc_w // LANE
    scale = DK_C ** -0.5
    n_pairs = H_C // 2
    nsa = dict(n_outer=n_pairs, dq=DK_C, dv=DK_C, q_col=lambda g: g, q_offs=(0, DK_C), k_offs=(0, 0),
               v_offs=(DK_C, DK_C), scale=scale, out_cols=qc_w, out_col=lambda g: g, bias_mode="pair",
               bias_idx=lambda g: g, gate=gates, vmem=VMEM_LIMIT, k_w=LANE, v_w=LANE)
    o_sel = _flash(h, h, h, name="nsa_sel_attn", T=ATT_TQ, TK=ATT_TK, k_col=lambda g: cb + 2 + g // 2, v_col=lambda g: cb + 2 + g // 2,
                   bias=tabs["bias_sel"], sel=sel, sel_idx=lambda g: g // 2,
                   gate_col=lambda g: n_pairs + g, **nsa)
    o_win = _flash(h, h, h, name="nsa_win_attn", T=ATT_TQ, TK=ATT_TK, k_col=lambda g: cb + 4 + g // 2,
                   v_col=lambda g: cb + 4 + g // 2, bias=tabs["bias_win"], gate_col=lambda g: 2 * n_pairs + g,
                   nback=-(-(WIN - 1) // ATT_TK), **nsa)

    lam = (jnp.exp(jnp.sum(lq1.astype(F32) * lk1.astype(F32)))
           - jnp.exp(jnp.sum(lq2.astype(F32) * lk2.astype(F32))) + lam_init).reshape(1, 1)
    db = cb + 6
    o_d = _flash(h, h, h, name="diff_attn", n_outer=H_D, T=ATT_TQ, TK=ATT_TK, dq=DD, dv=2 * DD,
                 q_col=lambda g: db + g, k_col=lambda g: db + H_D + g, v_col=lambda g: db + 2 * H_D + g,
                 q_offs=(0, DD), k_offs=(0, DD), v_offs=(0, 0), scale=DD ** -0.5,
                 out_cols=H_D * 2 * DD, out_col=lambda g: g, bias=tabs["bias_d"], bias_mode="shared",
                 bias_idx=lambda g: g, lam=lam, dnorm=d_norm[:, None], lam_init=lam_init, epilogue="diff",
                 vmem=VMEM_LIMIT)
    r2 = lambda a: a.reshape(n, -1)
    return _out_ln([r2(o_cmp), r2(o_sel), r2(o_win)], r2(o_d), w_out[:qc_w].astype(BF16),
                   w_out[qc_w:].astype(BF16), xf, ln_g[None], ln_b[None])


def _nsa_tables(rel_bias, seq):
    tab_c = rel_bias[:, H_B:H_B + H_C]
    tab_d = rel_bias[:, H_B + H_C:H_B + H_C + H_D]
    ncp = seq // CMP_STRIDE
    n_sel = seq // SEL_BLOCK
    pos = jnp.arange(seq)
    x = jnp.arange(2 * ncp)
    c_minus_a = jnp.where(x < ncp, x, x - 2 * ncp)
    dist = -CMP_STRIDE * c_minus_a[None, :] + jnp.arange(CMP_STRIDE)[:, None] - (CMP_LEN - 1)
    w = jnp.moveaxis(tab_c[_rel_bucket(dist)].astype(F32), -1, 0)
    bias_c = _toeplitz(w, ncp, ncp).transpose(0, 2, 1, 3).reshape(H_C, seq, ncp)
    c0 = jnp.arange(ncp) * CMP_STRIDE
    s0 = jnp.arange(n_sel) * SEL_BLOCK
    overlap = jnp.maximum(jnp.minimum(c0[:, None] + CMP_LEN, s0[None, :] + SEL_BLOCK)
                          - jnp.maximum(c0[:, None], s0[None, :]), 0).astype(F32) / CMP_LEN
    return {
        "bias_c": bias_c, "overlap": overlap.T,
        "bias_sel": _toeplitz_bias(tab_c, ATT_TQ, seq // ATT_TK, 1, seq, ATT_TK),
        "bias_win": _toeplitz_bias(tab_c, ATT_TQ, ATT_TQ // ATT_TK + -(-(WIN - 1) // ATT_TK), 1, WIN - 1, ATT_TK),
        "bias_d": _toeplitz_bias(tab_d, ATT_TQ, seq // ATT_TK, 1, seq, ATT_TK),
    }


def kernel(x, rel_bias, ab_w_in, mla_q_norm, mla_w_uq, mla_kv_norm, mla_w_ukv, ab_w_out, cd_w_in, nsa_cmp_pos_k, nsa_cmp_k_w1, nsa_cmp_k_w2, nsa_cmp_pos_v, nsa_cmp_v_w1, nsa_cmp_v_w2, diff_lambda_q1, diff_lambda_k1, diff_lambda_q2, diff_lambda_k2, diff_norm, cd_w_out, ln1_g, ln1_b, ln2_g, ln2_b, router_w, router_b, exp_w_gate, exp_w_up, exp_w_down, sh_w_gate, sh_w_up, sh_w_down):
    bsz, seq, d = x.shape
    n = bsz * seq
    depth = ln1_g.shape[0]
    rope_tabs = _rope_tables(seq)
    dil_bias = [_toeplitz_bias(rel_bias[:, gi * H_B_GROUP:(gi + 1) * H_B_GROUP], min(256, seq // dil), 2, dil,
                               window // dil) for gi, (window, dil) in enumerate(DIL_PAIRS)]
    nsa_tabs = _nsa_tables(rel_bias, seq)
    xf = x.reshape(n, d)
    xb = xf.astype(BF16)
    for l in range(depth):
        i = l // 2
        if l % 2 == 0:
            xf, xb = _mixer_ab(xb, xf, bsz, seq, ab_w_in[i], mla_q_norm[i], mla_w_uq[i], mla_kv_norm[i],
                               mla_w_ukv[i], ab_w_out[i], ln1_g[l], ln1_b[l], rope_tabs, dil_bias)
        else:
            lam_init = 0.8 - 0.6 * math.exp(-0.3 * l)
            xf, xb = _mixer_cd(xb, xf, bsz, seq, cd_w_in[i], nsa_cmp_pos_k[i], nsa_cmp_k_w1[i],
                               nsa_cmp_k_w2[i], nsa_cmp_pos_v[i], nsa_cmp_v_w1[i], nsa_cmp_v_w2[i],
                               diff_lambda_q1[i], diff_lambda_k1[i], diff_lambda_q2[i], diff_lambda_k2[i],
                               diff_norm[i], cd_w_out[i], ln1_g[l], ln1_b[l], lam_init, nsa_tabs)
        xf, xb = _moe(xf, xb, router_w[l], router_b[l], exp_w_gate, exp_w_up, exp_w_down,
                      sh_w_gate[l].astype(BF16), sh_w_up[l].astype(BF16), sh_w_down[l].astype(BF16),
                      ln2_g[l], ln2_b[l], layer=l)
    return xf.reshape(bsz, seq, d)
```

```python
import functools
import math

import jax
import jax.numpy as jnp
from jax import lax
from jax.experimental import pallas as pl
from jax.experimental.pallas import tpu as pltpu

F32 = jnp.float32
BF16 = jnp.bfloat16
HI = lax.Precision.HIGHEST

DEPTH = 4
NEG = -1e30
BIG = 1e9
LN_EPS = 1e-5
RMS_EPS = 1e-6
ALPHA = (2 * DEPTH) ** 0.25

N_BUCKETS = 32
REL_MAX_DIST = 2048

H_A = 12
NOPE = 64
ROPE_DIM = 32
MLA_V = 64
Q_LORA = 256
KV_LORA = 128
ROPE_THETA = 10000.0

DIL_PAIRS = ((128, 1), (512, 4), (2048, 16))
H_B_GROUP = 4
H_B = 12
HD_B = 64

H_C = 8
G_C = 2
R_C = 4
DK_C = 64
CMP_LEN = 32
CMP_STRIDE = 16
CMP_HID = 64
SEL_BLOCK = 64
SEL_TOP = 16
WIN = 512

H_D = 4
DD = 64

N_EXPERTS = 64
TOP_K = 8
N_EXPERT_GROUPS = 8
TOPK_GROUPS = 4
D_EXPERT = 256
ROUTED_SCALE = 2.5

LANE = 128
SUBLANES = 8
MOE_BM = 512
ATT_TQ = 512
ATT_TK = 256
VMEM_LIMIT = 56 * 1024 * 1024

_NT = (((1,), (1,)), ((), ()))


def _cparams(n_axes, vmem=None):
    return pltpu.CompilerParams(dimension_semantics=("arbitrary",) * n_axes, vmem_limit_bytes=vmem)


def _mm_body(x_ref, w_ref, o_ref, *, act, precision):
    y = jnp.dot(x_ref[...], w_ref[...], preferred_element_type=F32, precision=precision)
    if act == "sigmoid":
        y = jax.nn.sigmoid(y)
    o_ref[...] = y.astype(o_ref.dtype)


def _mm(x, w, out_dtype, tn, tm=1024, act=None, precision=None):
    m, k = x.shape
    nc = w.shape[1]
    tm = min(tm, m)
    return pl.pallas_call(
        functools.partial(_mm_body, act=act, precision=precision),
        grid=(m // tm, nc // tn),
        in_specs=[pl.BlockSpec((tm, k), lambda i, j: (i, 0)),
                  pl.BlockSpec((k, tn), lambda i, j: (0, j))],
        out_specs=pl.BlockSpec((tm, tn), lambda i, j: (i, j)),
        out_shape=jax.ShapeDtypeStruct((m, nc), out_dtype),
        compiler_params=_cparams(2, VMEM_LIMIT), name="proj",
    )(x, w)


def _proj_dilated_body(x_ref, w_ref, *rest, tm, gw3):
    o_refs, scr = rest[:-1], rest[-1]
    for gi, (_, dil) in enumerate(DIL_PAIRS):
        y = jnp.dot(x_ref[...], w_ref[:, gi * gw3:(gi + 1) * gw3], preferred_element_type=F32)
        rows = tm // dil
        for c in range(gw3 // LANE):
            scr[c] = y[:, c * LANE:(c + 1) * LANE]
        for r in range(dil):
            for c in range(gw3 // LANE):
                col = r * gw3 + c * LANE
                o_refs[gi][0, :, col:col + LANE] = scr[c, pl.ds(r, rows, stride=dil), :].astype(BF16)


def _proj_dilated(xb, w, bsz, seq, tm=512):
    n, k = xb.shape
    gw3 = w.shape[1] // len(DIL_PAIRS)
    spt = seq // tm
    return pl.pallas_call(
        functools.partial(_proj_dilated_body, tm=tm, gw3=gw3), grid=(n // tm,),
        in_specs=[pl.BlockSpec((tm, k), lambda i: (i, 0)), pl.BlockSpec(w.shape, lambda i: (0, 0))],
        out_specs=[pl.BlockSpec((1, tm // dil, dil * gw3), lambda i: (i // spt, i % spt, 0))
                   for _, dil in DIL_PAIRS],
        out_shape=[jax.ShapeDtypeStruct((bsz, seq // dil, dil * gw3), BF16) for _, dil in DIL_PAIRS],
        scratch_shapes=[pltpu.VMEM((gw3 // LANE, tm, LANE), F32)],
        compiler_params=_cparams(1, VMEM_LIMIT), name="proj_dilated",
    )(xb, w)


def _layer_norm(z, g, b):
    mu = jnp.mean(z, axis=-1, keepdims=True)
    zc = z - mu
    var = jnp.mean(zc * zc, axis=-1, keepdims=True)
    return zc * lax.rsqrt(var + LN_EPS) * g + b


def _out_ln_body(*refs, n_sum):
    a0 = refs[0][...].astype(F32)
    for r in refs[1:n_sum]:
        a0 = a0 + r[...].astype(F32)
    a1_ref, w0_ref, w1_ref, r_ref, g_ref, b_ref, of_ref, ob_ref = refs[n_sum:]
    y = jnp.dot(a0.astype(BF16), w0_ref[...], preferred_element_type=F32)
    y = y + jnp.dot(a1_ref[...], w1_ref[...], preferred_element_type=F32)
    out = _layer_norm(ALPHA * r_ref[...] + y, g_ref[...], b_ref[...])
    of_ref[...] = out
    ob_ref[...] = out.astype(BF16)


def _out_ln(a0s, a1, w0, w1, resid, g, b, tm=512):
    n, d = resid.shape
    k0, k1 = w0.shape[0], w1.shape[0]
    row = lambda i: (i, 0)
    fixed = lambda i: (0, 0)
    return pl.pallas_call(
        functools.partial(_out_ln_body, n_sum=len(a0s)),
        grid=(n // tm,),
        in_specs=[pl.BlockSpec((tm, k0), row)] * len(a0s) + [
            pl.BlockSpec((tm, k1), row), pl.BlockSpec((k0, d), fixed), pl.BlockSpec((k1, d), fixed),
            pl.BlockSpec((tm, d), row), pl.BlockSpec((1, d), fixed), pl.BlockSpec((1, d), fixed)],
        out_specs=[pl.BlockSpec((tm, d), row), pl.BlockSpec((tm, d), row)],
        out_shape=[jax.ShapeDtypeStruct((n, d), F32), jax.ShapeDtypeStruct((n, d), BF16)],
        compiler_params=_cparams(1), name="out_proj_ln",
    )(*a0s, a1, w0, w1, resid, g, b)


def _rms(x, g):
    return x * lax.rsqrt(jnp.mean(x * x, axis=-1, keepdims=True) + RMS_EPS) * g


def _mla_q_body(c_ref, g_ref, w_ref, wr_ref, cos_ref, sin_ref, o_ref):
    cn = _rms(c_ref[...], g_ref[...]).astype(BF16)
    a = jnp.dot(cn, w_ref[...], preferred_element_type=F32)
    r = jnp.dot(cn, wr_ref[...], preferred_element_type=F32)
    cos, sin = cos_ref[...], sin_ref[...]
    for h in range(H_A):
        sl = slice(h * LANE, (h + 1) * LANE)
        o_ref[:, sl] = (a[:, sl] * cos + r[:, sl] * sin).astype(o_ref.dtype)


def _mla_kv_body(c_ref, kr_ref, krr_ref, g_ref, wk_ref, wv_ref, cos_ref, sin_ref, k_ref, v_ref):
    cn = _rms(c_ref[...], g_ref[...]).astype(BF16)
    kn = jnp.dot(cn, wk_ref[...], preferred_element_type=F32)
    rope = kr_ref[...] * cos_ref[...] + krr_ref[...] * sin_ref[...]
    for h in range(H_A):
        sl = slice(h * LANE, (h + 1) * LANE)
        k_ref[:, sl] = (kn[:, sl] + rope).astype(k_ref.dtype)
    v_ref[...] = jnp.dot(cn, wv_ref[...], preferred_element_type=F32).astype(v_ref.dtype)


def _mla_up(h1, q_norm, wq, wq_rot, kv_norm, wk, wv, cos_q, sin_q, cos_k, sin_k, seq, tm=512):
    n = h1.shape[0]
    spt = seq // tm
    row = lambda c: (lambda i: (i, c))
    pos = lambda i: (i % spt, 0)
    fixed = lambda i: (0, 0)
    q_a = pl.pallas_call(
        _mla_q_body, grid=(n // tm,),
        in_specs=[pl.BlockSpec((tm, Q_LORA), row(0)), pl.BlockSpec((1, Q_LORA), fixed),
                  pl.BlockSpec(wq.shape, fixed), pl.BlockSpec(wq_rot.shape, fixed),
                  pl.BlockSpec((tm, LANE), pos), pl.BlockSpec((tm, LANE), pos)],
        out_specs=pl.BlockSpec((tm, H_A * LANE), row(0)),
        out_shape=jax.ShapeDtypeStruct((n, H_A * LANE), BF16),
        compiler_params=_cparams(1), name="mla_q_up",
    )(h1, q_norm, wq, wq_rot, cos_q, sin_q)
    k_a, v_a = pl.pallas_call(
        _mla_kv_body, grid=(n // tm,),
        in_specs=[pl.BlockSpec((tm, LANE), row(2)), pl.BlockSpec((tm, LANE), row(3)),
                  pl.BlockSpec((tm, LANE), row(4)), pl.BlockSpec((1, KV_LORA), fixed),
                  pl.BlockSpec(wk.shape, fixed), pl.BlockSpec(wv.shape, fixed),
                  pl.BlockSpec((tm, LANE), pos), pl.BlockSpec((tm, LANE), pos)],
        out_specs=[pl.BlockSpec((tm, H_A * LANE), row(0)), pl.BlockSpec((tm, H_A * MLA_V), row(0))],
        out_shape=[jax.ShapeDtypeStruct((n, H_A * LANE), BF16), jax.ShapeDtypeStruct((n, H_A * MLA_V), BF16)],
        compiler_params=_cparams(1), name="mla_kv_up",
    )(h1, h1, h1, kv_norm, wk, wv, cos_k, sin_k)
    return q_a, k_a, v_a


def _flash_body(*refs, T, TK, dq, dv, q_offs, k_offs, v_offs, scale, bias_mode, has_sel, has_gate,
                epilogue, want_lse, nback, lam_init, unroll2):
    R = T // TK
    it = iter(refs)
    q_ref, k_ref, v_ref = next(it), next(it), next(it)
    bias_ref = next(it) if bias_mode else None
    sel_ref = next(it) if has_sel else None
    gate_ref = next(it) if has_gate else None
    lam_ref, dn_ref = (next(it), next(it)) if epilogue == "diff" else (None, None)
    o_ref = next(it)
    lse_ref = next(it) if want_lse else None
    vt_scr = next(it)

    qi = pl.program_id(2)
    seq = v_ref.shape[1]

    @pl.when(qi == 0)
    def _():
        for c in range(seq // TK):
            vt_scr[:, c * TK:(c + 1) * TK] = v_ref[0, c * TK:(c + 1) * TK, :].astype(F32).T.astype(BF16)

    qfull = q_ref[0].astype(F32)
    fold_scale = math.frexp(scale)[0] == 0.5
    qts = [(qfull[:, off:off + dq] * (scale if fold_scale else 1.0)).T.astype(BF16) for off in q_offs]

    def qk(kc):
        kfull = k_ref[0, pl.ds(pl.multiple_of(kc * TK, TK), TK), :]
        return tuple(jnp.dot(kfull[:, k_offs[u]:k_offs[u] + dq], qts[u], preferred_element_type=F32)
                     for u in range(2))

    def update(kc, state, scores, diag):
        start = pl.multiple_of(kc * TK, TK)
        new_state = []
        sel_add = None
        if has_sel:
            per = TK // SEL_BLOCK
            rows = [sel_ref[0, 0, pl.ds(kc * per + a, 1), :] for a in range(per)]
            sel_add = jnp.concatenate([jnp.broadcast_to((r - 1.0) * (-NEG), (SEL_BLOCK, T)) for r in rows], axis=0)
        for u in range(2):
            vt = vt_scr[v_offs[u]:v_offs[u] + dv, pl.ds(start, TK)]
            s = scores[u]
            if not fold_scale:
                s = s * scale
            if bias_mode:
                s = s + bias_ref[u if bias_mode == "pair" else 0, R * qi - kc + (R - 1)]
            elif diag is not None:
                key = lax.broadcasted_iota(jnp.int32, (TK, T), 0) + diag * TK
                qry = lax.broadcasted_iota(jnp.int32, (TK, T), 1)
                s = jnp.where(key <= qry, s, NEG)
            if has_sel:
                s = s + sel_add
            m_prev, l_prev, acc_prev = state[u]
            m_new = jnp.maximum(m_prev, jnp.max(s, axis=0, keepdims=True))
            alpha = jnp.exp(m_prev - m_new)
            p = jnp.exp(s - m_new)
            l_new = alpha * l_prev + jnp.sum(p, axis=0, keepdims=True)
            acc_new = alpha * acc_prev + jnp.dot(vt, p.astype(BF16), preferred_element_type=F32)
            new_state.append((m_new, l_new, acc_new))
        return tuple(new_state)

    init = tuple((jnp.full((1, T), NEG, F32), jnp.zeros((1, T), F32), jnp.zeros((dv, T), F32)) for _ in range(2))
    lo = 0 if nback is None else jnp.maximum(R * qi - nback, 0)

    def step(kc, carry):
        state, scores = carry
        nxt = qk(kc + 1)
        return update(kc, state, scores, None), nxt

    if unroll2:
        assert R % 2 == 0 and nback is None
        state, scores = lax.fori_loop(0, (R * qi - lo) // 2,
                                      lambda j, c: step(lo + 2 * j + 1, step(lo + 2 * j, c)), (init, qk(lo)))
    else:
        state, scores = lax.fori_loop(lo, R * qi, step, (init, qk(lo)))
    for a in range(R):
        nxt = qk(R * qi + a + 1) if a + 1 < R else None
        state = update(R * qi + a, state, scores, a)
        scores = nxt

    outs = [acc / l for _, l, acc in state]
    if epilogue == "diff":
        a = outs[0] - lam_ref[0, 0] * outs[1]
        rinv = lax.rsqrt(jnp.mean(a * a, axis=0, keepdims=True) + RMS_EPS)
        o = (a * rinv * dn_ref[...] * (1.0 - lam_init)).T
    else:
        o = jnp.concatenate(outs, axis=0).T
        if has_gate:
            o = o * gate_ref[0]
    o_ref[0] = o.astype(o_ref.dtype)
    if want_lse:
        lse_ref[0, 0] = jnp.concatenate([m + jnp.log(l) for m, l, _ in state], axis=0)


def _flash(q, k, v, *, n_outer, T, dq, dv, q_col, k_col, v_col, q_offs, k_offs, v_offs, scale,
           out_cols, out_col, bias=None, bias_mode=None, bias_idx=None, sel=None, sel_idx=None,
           gate=None, gate_col=None, lam=None, dnorm=None, lam_init=0.0, epilogue="plain", want_lse=False,
           nback=None, vmem=None, k_w=None, v_w=None, name="flash", TK=None, unroll2=False,
           rep=1, rep_in=0, rep_out=0):
    bsz, seq, _ = q.shape
    TK = TK or T
    assert T % TK == 0
    nq = seq // T
    qw = max(o + dq for o in q_offs)
    kw = k_w or max(o + dq for o in k_offs)
    vw = v_w or max(o + dv for o in v_offs)
    ow = dv if epilogue == "diff" else 2 * dv
    in_specs = [pl.BlockSpec((1, T, qw), lambda g, b, i: (b // rep, i, q_col(g) + (b % rep) * rep_in)),
                pl.BlockSpec((1, seq, kw), lambda g, b, i: (b // rep, 0, k_col(g) + (b % rep) * rep_in)),
                pl.BlockSpec((1, seq, vw), lambda g, b, i: (b // rep, 0, v_col(g) + (b % rep) * rep_in))]
    args = [q, k, v]
    if bias_mode:
        nb = 2 if bias_mode == "pair" else 1
        in_specs.append(pl.BlockSpec((nb,) + bias.shape[1:], lambda g, b, i: (bias_idx(g), 0, 0, 0)))
        args.append(bias)
    if sel is not None:
        in_specs.append(pl.BlockSpec((1, 1, sel.shape[2], T), lambda g, b, i: (b, sel_idx(g), 0, i)))
        args.append(sel)
    if gate is not None:
        in_specs.append(pl.BlockSpec((1, T, ow), lambda g, b, i: (b, i, gate_col(g))))
        args.append(gate)
    if epilogue == "diff":
        in_specs.append(pl.BlockSpec(memory_space=pltpu.SMEM))
        in_specs.append(pl.BlockSpec((dv, 1), lambda g, b, i: (0, 0)))
        args += [lam, dnorm]
    out_specs = [pl.BlockSpec((1, T, ow), lambda g, b, i: (b // rep, i, out_col(g) + (b % rep) * rep_out))]
    out_shape = [jax.ShapeDtypeStruct((bsz, seq, out_cols), BF16)]
    if want_lse:
        out_specs.append(pl.BlockSpec((1, 1, 2, T), lambda g, b, i: (b, g, 0, i)))
        out_shape.append(jax.ShapeDtypeStruct((bsz * rep, n_outer, 2, seq), F32))
    body = functools.partial(
        _flash_body, T=T, TK=TK, dq=dq, dv=dv, q_offs=q_offs, k_offs=k_offs, v_offs=v_offs, scale=scale,
        bias_mode=bias_mode, has_sel=sel is not None, has_gate=gate is not None, epilogue=epilogue,
        want_lse=want_lse, nback=nback, lam_init=lam_init, unroll2=unroll2)
    res = pl.pallas_call(
        body, grid=(n_outer, bsz * rep, nq), in_specs=in_specs, out_specs=out_specs, out_shape=out_shape,
        scratch_shapes=[pltpu.VMEM((vw, seq), BF16)],
        compiler_params=_cparams(3, vmem), name=name,
    )(*args)
    return res if want_lse else res[0]


def _rel_bucket(dist):
    n = jnp.maximum(dist, 0)
    exact = N_BUCKETS // 2
    log_ratio = jnp.log(jnp.maximum(n, 1).astype(F32) / exact) / math.log(REL_MAX_DIST / exact)
    large = exact + (log_ratio * (N_BUCKETS - exact)).astype(jnp.int32)
    return jnp.where(n < exact, n, jnp.minimum(large, N_BUCKETS - 1))


def _toeplitz_bias(tab, T, n_d, dist_scale, max_dist, TK=None):
    TK = TK or T
    R = T // TK
    wlen = T + TK
    x = jnp.arange(wlen)
    dist = (jnp.arange(n_d)[:, None] - (R - 1)) * TK + jnp.where(x < T, x, x - wlen)[None, :]
    w = tab[_rel_bucket(dist * dist_scale)].astype(F32)
    w = jnp.where(((dist >= 0) & (dist <= max_dist))[..., None], w, NEG)
    return _toeplitz(jnp.moveaxis(w, -1, 0), TK, T)


def _toeplitz_body(w_ref, o_ref, *, rows, cols):
    x = jnp.broadcast_to(w_ref[0], (rows, w_ref.shape[-1]))
    o_ref[0] = pltpu.roll(x, 0, 1, stride=1, stride_axis=0)[:, :cols]


def _toeplitz(w, rows, cols):
    wlen = w.shape[-1]
    w2 = w.reshape(-1, 1, wlen)
    out = pl.pallas_call(
        functools.partial(_toeplitz_body, rows=rows, cols=cols), grid=(w2.shape[0],),
        in_specs=[pl.BlockSpec((1, 1, wlen), lambda i: (i, 0, 0))],
        out_specs=pl.BlockSpec((1, rows, cols), lambda i: (i, 0, 0)),
        out_shape=jax.ShapeDtypeStruct((w2.shape[0], rows, cols), w.dtype),
        compiler_params=_cparams(1), name="toeplitz",
    )(w2)
    return out.reshape(w.shape[:-1] + (rows, cols))


def _compress_body(u_ref, pe_ref, w1_ref, w2_ref, o_ref, *, ncp):
    outs = []
    for a in range(2):
        u = u_ref[0, 0, a].astype(F32)
        p1 = jnp.dot(u + pe_ref[a, 0], w1_ref[a, 0], preferred_element_type=F32, precision=HI)
        p2 = jnp.dot(u + pe_ref[a, 1], w1_ref[a, 1], preferred_element_type=F32, precision=HI)
        hid = jax.nn.gelu(p1 + pltpu.roll(p2, ncp - 1, 0))
        outs.append(jnp.dot(hid, w2_ref[a], preferred_element_type=F32, precision=HI))
    o_ref[0, 0] = jnp.concatenate(outs, axis=-1)


def _compress(u, pe, w1, w2):
    bsz, g, _, ncp, width = u.shape
    return pl.pallas_call(
        functools.partial(_compress_body, ncp=ncp), grid=(bsz, g),
        in_specs=[pl.BlockSpec((1, 1, 2, ncp, width), lambda b, gg: (b, gg, 0, 0, 0)),
                  pl.BlockSpec(pe.shape, lambda b, gg: (0, 0, 0, 0)),
                  pl.BlockSpec(w1.shape, lambda b, gg: (0, 0, 0, 0)),
                  pl.BlockSpec(w2.shape, lambda b, gg: (0, 0, 0))],
        out_specs=pl.BlockSpec((1, 1, ncp, 2 * DK_C), lambda b, gg: (b, gg, 0, 0)),
        out_shape=jax.ShapeDtypeStruct((bsz, g, ncp, 2 * DK_C), F32),
        compiler_params=_cparams(2), name="nsa_compress",
    )(u, pe, w1, w2)


def _cmp_attn_body(q_ref, kv_ref, bias_ref, ov_ref, gate_ref, o_ref, sel_ref, *, T, ncp, n_sel, n_top, scale):
    qi = pl.program_id(2)
    kc = kv_ref[0, 0, :, :DK_C]
    vc = kv_ref[0, 0, :, DK_C:]
    t = qi * T + lax.broadcasted_iota(jnp.int32, (T, ncp), 0)
    c = lax.broadcasted_iota(jnp.int32, (T, ncp), 1)
    valid = t >= c * CMP_STRIDE + (CMP_LEN - 1)
    validf = valid.astype(F32)
    psum = jnp.zeros((T, ncp), F32)
    outs = []
    for r in range(R_C):
        q = q_ref[0, :, r * DK_C:(r + 1) * DK_C].astype(F32)
        s = lax.dot_general(q, kc, _NT, preferred_element_type=F32, precision=HI) * scale + bias_ref[r]
        s = jnp.where(valid, s, NEG)
        e = jnp.exp(s - jnp.max(s, axis=-1, keepdims=True)) * validf
        p = e / jnp.maximum(jnp.sum(e, axis=-1, keepdims=True), 1e-30)
        outs.append(jnp.dot(p, vc, preferred_element_type=F32, precision=HI))
        psum = psum + p
    o_ref[0] = (jnp.concatenate(outs, axis=-1) * gate_ref[0]).astype(o_ref.dtype)

    imp = jnp.dot(ov_ref[...], psum.T, preferred_element_type=F32, precision=HI)
    tq = qi * T + lax.broadcasted_iota(jnp.int32, (n_sel, T), 1)
    j = lax.broadcasted_iota(jnp.int32, (n_sel, T), 0)
    forced = (j == tq // SEL_BLOCK) | (j == 0)
    work = jnp.where(forced, BIG, jnp.where(j * SEL_BLOCK <= tq, imp, -BIG))
    sel = jnp.zeros((n_sel, T), F32)
    jf = j.astype(F32)
    for _ in range(n_top):
        _, _, pick = _first_max(work, jf, n_sel)
        sel = jnp.where(pick, 1.0, sel)
        work = jnp.where(pick, -jnp.inf, work)
    sel_ref[0, 0] = sel


def _cmp_attn(h, kvc, bias_c, overlap, gates, T=256):
    bsz, seq, _ = h.shape
    ncp = kvc.shape[2]
    n_sel = seq // SEL_BLOCK
    n_top = min(SEL_TOP, n_sel)
    qw = R_C * DK_C
    return pl.pallas_call(
        functools.partial(_cmp_attn_body, T=T, ncp=ncp, n_sel=n_sel, n_top=n_top, scale=DK_C ** -0.5),
        grid=(G_C, bsz, seq // T),
        in_specs=[pl.BlockSpec((1, T, qw), lambda g, b, i: (b, i, g)),
                  pl.BlockSpec((1, 1, ncp, 2 * DK_C), lambda g, b, i: (b, g, 0, 0)),
                  pl.BlockSpec((R_C, T, ncp), lambda g, b, i: (g, i, 0)),
                  pl.BlockSpec(overlap.shape, lambda g, b, i: (0, 0)),
                  pl.BlockSpec((1, T, qw), lambda g, b, i: (b, i, g))],
        out_specs=[pl.BlockSpec((1, T, qw), lambda g, b, i: (b, i, g)),
                   pl.BlockSpec((1, 1, n_sel, T), lambda g, b, i: (b, g, 0, i))],
        out_shape=[jax.ShapeDtypeStruct((bsz, seq, H_C * DK_C), BF16),
                   jax.ShapeDtypeStruct((bsz, G_C, n_sel, seq), F32)],
        compiler_params=_cparams(3), name="nsa_cmp_attn",
    )(h, kvc, bias_c, overlap, gates)


def _first_max(work, idx, n):
    mx = jnp.max(work, axis=0, keepdims=True)
    first = jnp.min(jnp.where(work == mx, idx, float(n)), axis=0, keepdims=True)
    return mx, first, idx == first


def _router_body(x_ref, wt_ref, b_ref, tri_ref, e_ref, g_ref, r_ref, cnt_ref, carry_scr, *, tm):
    i = pl.program_id(0)

    @pl.when(i == 0)
    def _():
        carry_scr[...] = jnp.zeros(carry_scr.shape, F32)

    st = lax.dot_general(wt_ref[...], x_ref[...], _NT, preferred_element_type=F32, precision=HI)
    scores = jax.nn.sigmoid(st)
    sel = scores + b_ref[...]
    per = N_EXPERTS // N_EXPERT_GROUPS
    fiota = lambda rows: lax.broadcasted_iota(jnp.int32, (rows, tm), 0).astype(F32)
    i_per, i_grp, i_exp = fiota(per), fiota(N_EXPERT_GROUPS), fiota(N_EXPERTS)
    grp_scores = []
    for g in range(N_EXPERT_GROUPS):
        blk = sel[g * per:(g + 1) * per]
        m1, _, pick = _first_max(blk, i_per, per)
        grp_scores.append(m1 + jnp.max(jnp.where(pick, -jnp.inf, blk), axis=0, keepdims=True))
    work = jnp.concatenate(grp_scores, axis=0)
    gmask = jnp.zeros((N_EXPERT_GROUPS, tm), F32)
    for _ in range(TOPK_GROUPS):
        _, _, pick = _first_max(work, i_grp, N_EXPERT_GROUPS)
        gmask = jnp.where(pick, 1.0, gmask)
        work = jnp.where(pick, -jnp.inf, work)
    work = jnp.concatenate([jnp.where(gmask[g:g + 1] > 0.5, sel[g * per:(g + 1) * per], NEG)
                            for g in range(N_EXPERT_GROUPS)], axis=0)
    picks, firsts, vals = [], [], []
    for _ in range(TOP_K):
        _, first, pick = _first_max(work, i_exp, N_EXPERTS)
        picks.append(pick)
        firsts.append(first)
        vals.append(jnp.sum(jnp.where(pick, scores, 0.0), axis=0, keepdims=True))
        work = jnp.where(pick, -jnp.inf, work)
    val = jnp.concatenate(vals, axis=0)
    g_ref[...] = val / jnp.sum(val, axis=0, keepdims=True) * ROUTED_SCALE
    e_ref[...] = jnp.concatenate(firsts, axis=0).astype(jnp.int32)
    onehot = picks[0].astype(F32)
    for pick in picks[1:]:
        onehot = onehot + pick.astype(F32)
    before = jnp.dot(onehot.astype(BF16), tri_ref[...], preferred_element_type=F32) + carry_scr[...]
    r_ref[...] = jnp.concatenate([jnp.sum(jnp.where(pick, before, 0.0), axis=0, keepdims=True)
                                  for pick in picks], axis=0).astype(jnp.int32)
    carry = carry_scr[...] + jnp.sum(onehot, axis=1, keepdims=True)
    carry_scr[...] = carry
    cnt_ref[...] = jnp.broadcast_to(carry, cnt_ref.shape)


def _router(xf, router_w, router_b, tm=512):
    n, d = xf.shape
    tri = (jnp.arange(tm)[:, None] < jnp.arange(tm)[None, :]).astype(BF16)
    col = lambda i: (0, i)
    fixed = lambda i: (0, 0)
    return pl.pallas_call(
        functools.partial(_router_body, tm=tm), grid=(n // tm,),
        in_specs=[pl.BlockSpec((tm, d), lambda i: (i, 0)), pl.BlockSpec((N_EXPERTS, d), fixed),
                  pl.BlockSpec((N_EXPERTS, 1), fixed), pl.BlockSpec((tm, tm), fixed)],
        out_specs=[pl.BlockSpec((TOP_K, tm), col), pl.BlockSpec((TOP_K, tm), col),
                   pl.BlockSpec((TOP_K, tm), col), pl.BlockSpec((N_EXPERTS, LANE), fixed)],
        out_shape=[jax.ShapeDtypeStruct((TOP_K, n), jnp.int32), jax.ShapeDtypeStruct((TOP_K, n), F32),
                   jax.ShapeDtypeStruct((TOP_K, n), jnp.int32), jax.ShapeDtypeStruct((N_EXPERTS, LANE), F32)],
        scratch_shapes=[pltpu.VMEM((N_EXPERTS, 1), F32)],
        compiler_params=_cparams(1), name="router",
    )(xf, router_w.T, router_b.astype(F32)[:, None], tri)


def _pack_rows(x):
    w = x.shape[-1] // 2
    lo = lax.bitcast_convert_type(x[:, :w].astype(BF16).astype(F32), jnp.uint32)
    hi = lax.bitcast_convert_type(x[:, w:].astype(BF16).astype(F32), jnp.uint32)
    return (lo >> 16) | (hi & jnp.uint32(0xFFFF0000))


def _unpack_rows(p):
    lo = lax.bitcast_convert_type(p << 16, F32)
    hi = lax.bitcast_convert_type(p & jnp.uint32(0xFFFF0000), F32)
    return lo, hi


def _dispatch_body(fill_ref, pos_ref, x_ref, z_ref, xs_hbm, xp_scr, sem, *, tm, n_blocks):
    i = pl.program_id(0)
    slot = i % 2
    xp_scr[slot] = _pack_rows(x_ref[...]).reshape(tm // SUBLANES, SUBLANES, -1)
    zsem = sem.at[2]

    def row_copy(s, grp, j, dst):
        return pltpu.make_async_copy(xp_scr.at[s, grp, pl.ds(j, 1)], xs_hbm.at[pl.ds(dst, 1)], sem.at[s])

    def zero_rows(dst, size):
        return pltpu.make_async_copy(z_ref.at[pl.ds(0, size)], xs_hbm.at[pl.ds(dst, size)], zsem)

    def zero_block(blk):
        return pltpu.make_async_copy(z_ref, xs_hbm.at[pl.ds(blk * MOE_BM, MOE_BM)], zsem)

    def issue(grp, c):
        for j in range(SUBLANES):
            for k in range(TOP_K):
                row_copy(slot, grp, j, pos_ref[k, grp * SUBLANES + j]).start(priority=k % 2)
        return c

    def drain(s):
        def body(grp, c):
            for _ in range(SUBLANES * TOP_K):
                row_copy(s, 0, 0, 0).wait()
            return c

        lax.fori_loop(0, tm // SUBLANES, body, 0)

    lax.fori_loop(0, tm // SUBLANES, issue, 0)

    @pl.when(i > 0)
    def _():
        drain(1 - slot)

    @pl.when(i == pl.num_programs(0) - 1)
    def _():
        drain(slot)

    @pl.when(i == 0)
    def _():
        def fill(wait):
            def go(cp):
                if wait:
                    cp.wait()
                else:
                    cp.start()

            def body(e, c):
                lo, cnt = fill_ref[2 * e], fill_ref[2 * e + 1] - fill_ref[2 * e]

                def one_row(r, cc):
                    go(zero_rows(0 if wait else r, 1))
                    return cc

                head = jnp.minimum((-lo) & (SUBLANES - 1), cnt)
                lax.fori_loop(lo, lo + head, one_row, 0)
                base, rem = lo + head, cnt - head
                for bit in reversed(range(SUBLANES.bit_length() - 1, MOE_BM.bit_length() - 1)):
                    size = 1 << bit

                    @pl.when((rem & size) != 0)
                    def _():
                        off = base + ((rem >> (bit + 1)) << (bit + 1))
                        go(zero_rows(0 if wait else pl.multiple_of(off, SUBLANES), size))

                lax.fori_loop(base + (rem & -SUBLANES), base + rem, one_row, 0)
                return c

            lax.fori_loop(0, N_EXPERTS, body, 0)

        fill(False)
        fill(True)

        def tail_start(blk, c):
            zero_block(blk).start()
            return c

        def tail_wait(blk, c):
            zero_block(0).wait()
            return c

        lax.fori_loop(fill_ref[2 * N_EXPERTS], n_blocks, tail_start, 0)
        lax.fori_loop(fill_ref[2 * N_EXPERTS], n_blocks, tail_wait, 0)


def _dispatch(xb, pos, fill, p, tm=128):
    n, d = xb.shape
    grid_spec = pltpu.PrefetchScalarGridSpec(
        num_scalar_prefetch=1, grid=(n // tm,),
        in_specs=[pl.BlockSpec((TOP_K, tm), lambda i, fl: (0, i), memory_space=pltpu.SMEM),
                  pl.BlockSpec((tm, d), lambda i, fl: (i, 0)),
                  pl.BlockSpec((MOE_BM, d // 2), lambda i, fl: (0, 0))],
        out_specs=pl.BlockSpec(memory_space=pl.ANY),
        scratch_shapes=[pltpu.VMEM((2, tm // SUBLANES, SUBLANES, d // 2), jnp.uint32),
                        pltpu.SemaphoreType.DMA((3,))])
    return pl.pallas_call(
        functools.partial(_dispatch_body, tm=tm, n_blocks=p // MOE_BM), grid_spec=grid_spec,
        out_shape=jax.ShapeDtypeStruct((p, d // 2), jnp.uint32),
        compiler_params=_cparams(1), name="dispatch",
    )(fill, pos, xb, jnp.zeros((MOE_BM, d // 2), jnp.uint32))


def _moe_ffn_body(be_ref, nb_ref, x_ref, wg_ref, wu_ref, wd_ref, o_ref, wg_scr, wu_scr, wd_scr):
    i = pl.program_id(0)
    half = x_ref.shape[1]

    @pl.when((i == 0) | (be_ref[i] != be_ref[jnp.maximum(i - 1, 0)]))
    def _():
        wg_scr[...] = wg_ref[0, 0].astype(BF16)
        wu_scr[...] = wu_ref[0, 0].astype(BF16)
        wd_scr[...] = wd_ref[0, 0].astype(BF16)

    @pl.when(i < nb_ref[0])
    def _():
        lo, hi = _unpack_rows(x_ref[...])
        lo, hi = lo.astype(BF16), hi.astype(BF16)
        hg = (jnp.dot(lo, wg_scr[:half], preferred_element_type=F32)
              + jnp.dot(hi, wg_scr[half:], preferred_element_type=F32))
        hu = (jnp.dot(lo, wu_scr[:half], preferred_element_type=F32)
              + jnp.dot(hi, wu_scr[half:], preferred_element_type=F32))
        hb = (hg * jax.nn.sigmoid(hg) * hu).astype(BF16)
        o_ref[...] = _pack_rows(jnp.dot(hb, wd_scr[...], preferred_element_type=F32))

    @pl.when(i >= nb_ref[0])
    def _():
        o_ref[...] = jnp.zeros(o_ref.shape, o_ref.dtype)


def _moe_ffn(xs, blk_e, n_used, wg, wu, wd, layer):
    p, half = xs.shape
    d = 2 * half
    n_blocks = p // MOE_BM
    grid_spec = pltpu.PrefetchScalarGridSpec(
        num_scalar_prefetch=2, grid=(n_blocks,),
        in_specs=[pl.BlockSpec((MOE_BM, half), lambda i, be, nb: (i, 0)),
                  pl.BlockSpec((1, 1, d, D_EXPERT), lambda i, be, nb: (layer, be[i], 0, 0)),
                  pl.BlockSpec((1, 1, d, D_EXPERT), lambda i, be, nb: (layer, be[i], 0, 0)),
                  pl.BlockSpec((1, 1, D_EXPERT, d), lambda i, be, nb: (layer, be[i], 0, 0))],
        out_specs=pl.BlockSpec((MOE_BM, half), lambda i, be, nb: (i, 0)),
        scratch_shapes=[pltpu.VMEM((d, D_EXPERT), BF16), pltpu.VMEM((d, D_EXPERT), BF16),
                        pltpu.VMEM((D_EXPERT, d), BF16)])
    return pl.pallas_call(
        _moe_ffn_body, grid_spec=grid_spec, out_shape=jax.ShapeDtypeStruct((p, half), jnp.uint32),
        compiler_params=_cparams(1), name="expert_ffn",
    )(blk_e, n_used, xs, wg, wu, wd)


def _combine_body(pos_ref, posn_ref, gate_ref, xb_ref, xf_ref, y_hbm, sg_ref, su_ref, sd_ref,
                  g_ref, b_ref, of_ref, ob_ref, buf, sem, *, tm):
    i = pl.program_id(0)
    slot = i % 2

    def row_copy(s, k, grp, j, src):
        return pltpu.make_async_copy(y_hbm.at[pl.ds(src, 1)], buf.at[s, k, grp, pl.ds(j, 1)], sem.at[s])

    def issue_tile(pr, s):
        def issue(grp, c):
            for j in range(SUBLANES):
                for k in range(TOP_K):
                    row_copy(s, k, grp, j, pr[k, grp * SUBLANES + j]).start(priority=k % 2)
            return c

        lax.fori_loop(0, tm // SUBLANES, issue, 0)

    @pl.when(i == 0)
    def _():
        issue_tile(pos_ref, 0)

    @pl.when(i + 1 < pl.num_programs(0))
    def _():
        issue_tile(posn_ref, 1 - slot)

    x = xb_ref[...]
    hg = jnp.dot(x, sg_ref[...], preferred_element_type=F32)
    hu = jnp.dot(x, su_ref[...], preferred_element_type=F32)
    hb = (hg * jax.nn.sigmoid(hg) * hu).astype(BF16)
    y = jnp.dot(hb, sd_ref[...], preferred_element_type=F32)

    def drain(grp, c):
        for _ in range(SUBLANES * TOP_K):
            row_copy(slot, 0, 0, 0, 0).wait()
        return c

    lax.fori_loop(0, tm // SUBLANES, drain, 0)
    gate = gate_ref[...]
    half = buf.shape[-1]
    y_lo, y_hi = y[:, :half], y[:, half:]
    for k in range(TOP_K):
        lo, hi = _unpack_rows(buf[slot, k].reshape(tm, half))
        y_lo = y_lo + gate[:, k:k + 1] * lo
        y_hi = y_hi + gate[:, k:k + 1] * hi
    out = _layer_norm(ALPHA * xf_ref[...] + jnp.concatenate([y_lo, y_hi], axis=-1), g_ref[...], b_ref[...])
    of_ref[...] = out
    ob_ref[...] = out.astype(BF16)


def _combine(pos, gate, xb, xf, y, sg, su, sd, g, b, tm=128):
    n, d = xf.shape
    n_tiles = n // tm
    row = lambda i: (i, 0)
    fixed = lambda i: (0, 0)
    smem_col = pl.BlockSpec((TOP_K, tm), lambda i: (0, i), memory_space=pltpu.SMEM)
    smem_next = pl.BlockSpec((TOP_K, tm), lambda i: (0, jnp.minimum(i + 1, n_tiles - 1)),
                             memory_space=pltpu.SMEM)
    grid_spec = pltpu.PrefetchScalarGridSpec(
        num_scalar_prefetch=0, grid=(n_tiles,),
        in_specs=[smem_col, smem_next, pl.BlockSpec((tm, TOP_K), row),
                  pl.BlockSpec((tm, d), row), pl.BlockSpec((tm, d), row), pl.BlockSpec(memory_space=pl.ANY),
                  pl.BlockSpec(sg.shape, fixed), pl.BlockSpec(su.shape, fixed), pl.BlockSpec(sd.shape, fixed),
                  pl.BlockSpec((1, d), fixed), pl.BlockSpec((1, d), fixed)],
        out_specs=[pl.BlockSpec((tm, d), row), pl.BlockSpec((tm, d), row)],
        scratch_shapes=[pltpu.VMEM((2, TOP_K, tm // SUBLANES, SUBLANES, d // 2), jnp.uint32),
                        pltpu.SemaphoreType.DMA((2,))])
    return pl.pallas_call(
        functools.partial(_combine_body, tm=tm), grid_spec=grid_spec,
        out_shape=[jax.ShapeDtypeStruct((n, d), F32), jax.ShapeDtypeStruct((n, d), BF16)],
        compiler_params=_cparams(1), name="combine",
    )(pos, pos, gate, xb, xf, y, sg, su, sd, g, b)


def _moe(xf, xb, router_w, router_b, w_gate, w_up, w_down, sh_gate, sh_up, sh_down, ln_g, ln_b, layer=0):
    n, d = xf.shape
    e_idx, gate, rank, cnt = _router(xf, router_w, router_b)
    counts = cnt[:, 0].astype(jnp.int32)
    padded = (counts + MOE_BM - 1) // MOE_BM * MOE_BM
    pad_end = jnp.cumsum(padded)
    pad_start = (pad_end - padded).astype(jnp.int32)
    n_blocks = (n * TOP_K + N_EXPERTS * (MOE_BM - 1) + MOE_BM - 1) // MOE_BM
    blk_e = jnp.minimum(jnp.sum(pad_end[None, :] <= (jnp.arange(n_blocks) * MOE_BM)[:, None], axis=1),
                        N_EXPERTS - 1).astype(jnp.int32)
    n_used = (pad_end[-1] // MOE_BM).astype(jnp.int32).reshape(1)
    fill = jnp.concatenate([jnp.stack([pad_start + counts, pad_end], axis=1).reshape(-1), n_used]).astype(jnp.int32)
    experts = jnp.arange(N_EXPERTS, dtype=jnp.int32)[:, None, None]
    pos = rank + jnp.sum(jnp.where(e_idx[None] == experts, pad_start[:, None, None], 0), axis=0)
    xs = _dispatch(xb, pos, fill, n_blocks * MOE_BM)
    y = _moe_ffn(xs, blk_e, n_used, w_gate, w_up, w_down, layer)
    return _combine(pos, gate.T, xb, xf, y, sh_gate, sh_up, sh_down, ln_g[None], ln_b[None])


def _rope_tables(seq):
    half = ROPE_DIM // 2
    freqs = ROPE_THETA ** (-jnp.arange(half, dtype=F32) / half)
    ang = jnp.arange(seq).astype(F32)[:, None] * freqs
    cos = jnp.concatenate([jnp.cos(ang)] * 2, -1)
    sin = jnp.concatenate([jnp.sin(ang)] * 2, -1)
    z = lambda w: jnp.zeros((seq, w), F32)
    pad = LANE - NOPE - ROPE_DIM
    cos_q = jnp.concatenate([jnp.ones((seq, NOPE), F32), cos, z(pad)], -1)
    sin_q = jnp.concatenate([z(NOPE), sin, z(pad)], -1)
    cos_k = jnp.concatenate([z(NOPE), cos, z(pad)], -1)
    return cos_q, sin_q, cos_k, sin_q


def _rot_cols(w):
    half = w.shape[-1] // 2
    return jnp.concatenate([-w[..., half:], w[..., :half]], -1)


def _mixer_ab(xb, xf, bsz, seq, w_in, q_norm, w_uq, kv_norm, w_ukv, w_out, ln_g, ln_b, rope_tabs, dil_bias):
    n = bsz * seq
    d = w_in.shape[0]
    c0 = Q_LORA + KV_LORA
    w_kr = w_in[:, c0:c0 + ROPE_DIM]
    zc = lambda w: jnp.zeros((d, w), F32)
    pad = LANE - NOPE - ROPE_DIM
    w1 = jnp.concatenate([w_in[:, :c0], zc(NOPE), w_kr, zc(pad), zc(NOPE), _rot_cols(w_kr), zc(pad)], 1)
    h1 = _mm(xb, w1.astype(BF16), F32, tn=w1.shape[1])
    gw = H_B_GROUP * HD_B
    w_b = w_in[:, c0 + ROPE_DIM:].reshape(d, 3, len(DIL_PAIRS), gw).transpose(0, 2, 1, 3).reshape(d, -1)
    h2 = _proj_dilated(xb, w_b.astype(BF16), bsz, seq)

    wq = w_uq.reshape(Q_LORA, H_A, NOPE + ROPE_DIM)
    zq = jnp.zeros((Q_LORA, H_A, pad), F32)
    wq_main = jnp.concatenate([wq, zq], -1).reshape(Q_LORA, H_A * LANE)
    wq_rot = jnp.concatenate([jnp.zeros((Q_LORA, H_A, NOPE), F32), _rot_cols(wq[..., NOPE:]), zq], -1)
    wq_rot = wq_rot.reshape(Q_LORA, H_A * LANE)
    wkv = w_ukv.reshape(KV_LORA, H_A, NOPE + MLA_V)
    wk = jnp.concatenate([wkv[..., :NOPE], jnp.zeros((KV_LORA, H_A, LANE - NOPE), F32)], -1)
    wk = wk.reshape(KV_LORA, H_A * LANE)
    wv = wkv[..., NOPE:].reshape(KV_LORA, H_A * MLA_V)
    q_a, k_a, v_a = _mla_up(h1, q_norm[None], wq_main.astype(BF16), wq_rot.astype(BF16), kv_norm[None],
                            wk.astype(BF16), wv.astype(BF16), *rope_tabs, seq)
    o_a = _flash(q_a.reshape(bsz, seq, -1), k_a.reshape(bsz, seq, -1), v_a.reshape(bsz, seq, -1), name="mla_attn",
                 n_outer=H_A // 2, T=min(ATT_TQ, seq), TK=ATT_TK, unroll2=True, dq=LANE, dv=MLA_V,
                 q_col=lambda g: g, k_col=lambda g: g, v_col=lambda g: g,
                 q_offs=(0, LANE), k_offs=(0, LANE), v_offs=(0, MLA_V),
                 scale=(NOPE + ROPE_DIM) ** -0.5, out_cols=H_A * MLA_V, out_col=lambda g: g)

    gb = gw // LANE
    outs, lses = [], []
    for gi, (window, dil) in enumerate(DIL_PAIRS):
        L = seq // dil
        t = h2[gi]
        o, lse = _flash(t, t, t, name="dilated_attn", n_outer=2, T=min(ATT_TQ, L), TK=min(ATT_TK, L),
                        dq=HD_B, dv=HD_B,
                        q_col=lambda g: g, k_col=lambda g: gb + g, v_col=lambda g: 2 * gb + g,
                        q_offs=(0, HD_B), k_offs=(0, HD_B), v_offs=(0, HD_B), scale=HD_B ** -0.5,
                        out_cols=dil * gw, out_col=lambda g: g, bias=dil_bias[gi], bias_mode="pair",
                        bias_idx=lambda g: g, want_lse=True, nback=-(-(window // dil) // min(ATT_TK, L)),
                        rep=dil, rep_in=3 * gb, rep_out=gb)
        outs.append(o.reshape(bsz, seq, H_B_GROUP, HD_B).astype(F32))
        lses.append(lse.reshape(bsz, dil, H_B_GROUP, L).transpose(0, 3, 1, 2).reshape(bsz, seq, H_B_GROUP))
    w = jax.nn.softmax(jnp.stack(lses), axis=0)
    o_b = jnp.sum(w[..., None] * jnp.stack(outs), axis=0).astype(BF16).reshape(n, gw)
    na = H_A * MLA_V
    return _out_ln([o_a.reshape(n, na)], o_b, w_out[:na].astype(BF16), w_out[na:].astype(BF16),
                   xf, ln_g[None], ln_b[None])


def _mixer_cd(xb, xf, bsz, seq, w_in, pos_k, k_w1, k_w2, pos_v, v_w1, v_w2, lq1, lk1, lq2, lk2, d_norm,
              w_out, ln_g, ln_b, lam_init, tabs):
    n = bsz * seq
    qc_w = H_C * DK_C
    kv_w = G_C * DK_C
    off = qc_w
    kvs = []
    for _ in range(3):
        wk_ = w_in[:, off:off + kv_w].reshape(-1, G_C, DK_C)
        wv_ = w_in[:, off + kv_w:off + 2 * kv_w].reshape(-1, G_C, DK_C)
        kvs.append(jnp.concatenate([wk_, wv_], -1).reshape(-1, 2 * kv_w))
        off += 2 * kv_w
    g_off = off
    d_off = off + 3 * H_C
    w_main = jnp.concatenate([w_in[:, :qc_w]] + kvs + [w_in[:, d_off:]], 1)
    h = _mm(xb, w_main.astype(BF16), BF16, tn=w_main.shape[1] // 2).reshape(bsz, seq, -1)
    w_g = jnp.repeat(w_in[:, g_off:d_off], DK_C, axis=1)
    gates = _mm(xb, w_g.astype(BF16), F32, tn=w_g.shape[1] // 2, act="sigmoid").reshape(bsz, seq, -1)

    ncp = seq // CMP_STRIDE
    half = CMP_STRIDE * DK_C
    kv_cmp = h[:, :, qc_w:qc_w + 2 * kv_w].reshape(bsz, ncp, CMP_STRIDE, G_C, 2, DK_C)
    u = kv_cmp.transpose(0, 3, 4, 1, 2, 5).reshape(bsz, G_C, 2, ncp, half)
    pe = jnp.stack([pos_k.reshape(2, 1, half), pos_v.reshape(2, 1, half)])
    w1 = jnp.stack([k_w1.reshape(2, half, CMP_HID), v_w1.reshape(2, half, CMP_HID)])
    w2 = jnp.stack([k_w2, v_w2])
    kvc = _compress(u, pe, w1, w2)
    o_cmp, sel = _cmp_attn(h, kvc, tabs["bias_c"], tabs["overlap"], gates)

    cb = qc_w // LANE
    scale = DK_C ** -0.5
    n_pairs = H_C // 2
    nsa = dict(n_outer=n_pairs, dq=DK_C, dv=DK_C, q_col=lambda g: g, q_offs=(0, DK_C), k_offs=(0, 0),
               v_offs=(DK_C, DK_C), scale=scale, out_cols=qc_w, out_col=lambda g: g, bias_mode="pair",
               bias_idx=lambda g: g, gate=gates, vmem=VMEM_LIMIT, k_w=LANE, v_w=LANE)
    o_sel = _flash(h, h, h, name="nsa_sel_attn", T=ATT_TQ, TK=ATT_TK, k_col=lambda g: cb + 2 + g // 2, v_col=lambda g: cb + 2 + g // 2,
                   bias=tabs["bias_sel"], sel=sel, sel_idx=lambda g: g // 2,
                   gate_col=lambda g: n_pairs + g, **nsa)
    o_win = _flash(h, h, h, name="nsa_win_attn", T=ATT_TQ, TK=ATT_TK, k_col=lambda g: cb + 4 + g // 2,
                   v_col=lambda g: cb + 4 + g // 2, bias=tabs["bias_win"], gate_col=lambda g: 2 * n_pairs + g,
                   nback=-(-(WIN - 1) // ATT_TK), **nsa)

    lam = (jnp.exp(jnp.sum(lq1.astype(F32) * lk1.astype(F32)))
           - jnp.exp(jnp.sum(lq2.astype(F32) * lk2.astype(F32))) + lam_init).reshape(1, 1)
    db = cb + 6
    o_d = _flash(h, h, h, name="diff_attn", n_outer=H_D, T=ATT_TQ, TK=ATT_TK, dq=DD, dv=2 * DD,
                 q_col=lambda g: db + g, k_col=lambda g: db + H_D + g, v_col=lambda g: db + 2 * H_D + g,
                 q_offs=(0, DD), k_offs=(0, DD), v_offs=(0, 0), scale=DD ** -0.5,
                 out_cols=H_D * 2 * DD, out_col=lambda g: g, bias=tabs["bias_d"], bias_mode="shared",
                 bias_idx=lambda g: g, lam=lam, dnorm=d_norm[:, None], lam_init=lam_init, epilogue="diff",
                 vmem=VMEM_LIMIT)
    r2 = lambda a: a.reshape(n, -1)
    return _out_ln([r2(o_cmp), r2(o_sel), r2(o_win)], r2(o_d), w_out[:qc_w].astype(BF16),
                   w_out[qc_w:].astype(BF16), xf, ln_g[None], ln_b[None])


def _nsa_tables(rel_bias, seq):
    tab_c = rel_bias[:, H_B:H_B + H_C]
    tab_d = rel_bias[:, H_B + H_C:H_B + H_C + H_D]
    ncp = seq // CMP_STRIDE
    n_sel = seq // SEL_BLOCK
    x = jnp.arange(2 * ncp)
    c_minus_a = jnp.where(x < ncp, x, x - 2 * ncp)
    dist = -CMP_STRIDE * c_minus_a[None, :] + jnp.arange(CMP_STRIDE)[:, None] - (CMP_LEN - 1)
    w = jnp.moveaxis(tab_c[_rel_bucket(dist)].astype(F32), -1, 0)
    bias_c = _toeplitz(w, ncp, ncp).transpose(0, 2, 1, 3).reshape(H_C, seq, ncp)
    c0 = jnp.arange(ncp) * CMP_STRIDE
    s0 = jnp.arange(n_sel) * SEL_BLOCK
    overlap = jnp.maximum(jnp.minimum(c0[:, None] + CMP_LEN, s0[None, :] + SEL_BLOCK)
                          - jnp.maximum(c0[:, None], s0[None, :]), 0).astype(F32) / CMP_LEN
    return {
        "bias_c": bias_c, "overlap": overlap.T,
        "bias_sel": _toeplitz_bias(tab_c, ATT_TQ, seq // ATT_TK, 1, seq, ATT_TK),
        "bias_win": _toeplitz_bias(tab_c, ATT_TQ, ATT_TQ // ATT_TK + -(-(WIN - 1) // ATT_TK), 1, WIN - 1, ATT_TK),
        "bias_d": _toeplitz_bias(tab_d, ATT_TQ, seq // ATT_TK, 1, seq, ATT_TK),
    }


def kernel(x, rel_bias, ab_w_in, mla_q_norm, mla_w_uq, mla_kv_norm, mla_w_ukv, ab_w_out, cd_w_in, nsa_cmp_pos_k, nsa_cmp_k_w1, nsa_cmp_k_w2, nsa_cmp_pos_v, nsa_cmp_v_w1, nsa_cmp_v_w2, diff_lambda_q1, diff_lambda_k1, diff_lambda_q2, diff_lambda_k2, diff_norm, cd_w_out, ln1_g, ln1_b, ln2_g, ln2_b, router_w, router_b, exp_w_gate, exp_w_up, exp_w_down, sh_w_gate, sh_w_up, sh_w_down):
    bsz, seq, d = x.shape
    n = bsz * seq
    depth = ln1_g.shape[0]
    rope_tabs = _rope_tables(seq)
    dil_bias = []
    for gi, (window, dil) in enumerate(DIL_PAIRS):
        tq, tk = min(ATT_TQ, seq // dil), min(ATT_TK, seq // dil)
        dil_bias.append(_toeplitz_bias(rel_bias[:, gi * H_B_GROUP:(gi + 1) * H_B_GROUP], tq,
                                       tq // tk + -(-(window // dil) // tk), dil, window // dil, tk))
    nsa_tabs = _nsa_tables(rel_bias, seq)
    xf = x.reshape(n, d)
    xb = xf.astype(BF16)
    for l in range(depth):
        i = l // 2
        if l % 2 == 0:
            xf, xb = _mixer_ab(xb, xf, bsz, seq, ab_w_in[i], mla_q_norm[i], mla_w_uq[i], mla_kv_norm[i],
                               mla_w_ukv[i], ab_w_out[i], ln1_g[l], ln1_b[l], rope_tabs, dil_bias)
        else:
            lam_init = 0.8 - 0.6 * math.exp(-0.3 * l)
            xf, xb = _mixer_cd(xb, xf, bsz, seq, cd_w_in[i], nsa_cmp_pos_k[i], nsa_cmp_k_w1[i],
                               nsa_cmp_k_w2[i], nsa_cmp_pos_v[i], nsa_cmp_v_w1[i], nsa_cmp_v_w2[i],
                               diff_lambda_q1[i], diff_lambda_k1[i], diff_lambda_q2[i], diff_lambda_k2[i],
                               diff_norm[i], cd_w_out[i], ln1_g[l], ln1_b[l], lam_init, nsa_tabs)
        xf, xb = _moe(xf, xb, router_w[l], router_b[l], exp_w_gate, exp_w_up, exp_w_down,
                      sh_w_gate[l].astype(BF16), sh_w_up[l].astype(BF16), sh_w_down[l].astype(BF16),
                      ln2_g[l], ln2_b[l], layer=l)
    return xf.reshape(bsz, seq, d)
```

```python
import functools
import math

import jax
import jax.numpy as jnp
from jax import lax
from jax.experimental import pallas as pl
from jax.experimental.pallas import tpu as pltpu

F32 = jnp.float32
BF16 = jnp.bfloat16
HI = lax.Precision.HIGHEST

DEPTH = 4
NEG = -1e30
BIG = 1e9
LN_EPS = 1e-5
RMS_EPS = 1e-6
ALPHA = (2 * DEPTH) ** 0.25

N_BUCKETS = 32
REL_MAX_DIST = 2048

H_A = 12
NOPE = 64
ROPE_DIM = 32
MLA_V = 64
Q_LORA = 256
KV_LORA = 128
ROPE_THETA = 10000.0

DIL_PAIRS = ((128, 1), (512, 4), (2048, 16))
H_B_GROUP = 4
H_B = 12
HD_B = 64

H_C = 8
G_C = 2
R_C = 4
DK_C = 64
CMP_LEN = 32
CMP_STRIDE = 16
CMP_HID = 64
SEL_BLOCK = 64
SEL_TOP = 16
WIN = 512

H_D = 4
DD = 64

N_EXPERTS = 64
TOP_K = 8
N_EXPERT_GROUPS = 8
TOPK_GROUPS = 4
D_EXPERT = 256
ROUTED_SCALE = 2.5

LANE = 128
SUBLANES = 8
MOE_BM = 512
ATT_TQ = 512
ATT_TK = 256
VMEM_LIMIT = 56 * 1024 * 1024

_NT = (((1,), (1,)), ((), ()))


def _cparams(n_axes, vmem=None):
    return pltpu.CompilerParams(dimension_semantics=("arbitrary",) * n_axes, vmem_limit_bytes=vmem)


def _mm_body(x_ref, w_ref, o_ref, *, act, precision):
    y = jnp.dot(x_ref[...], w_ref[...], preferred_element_type=F32, precision=precision)
    if act == "sigmoid":
        y = jax.nn.sigmoid(y)
    o_ref[...] = y.astype(o_ref.dtype)


def _mm(x, w, out_dtype, tn, tm=1024, act=None, precision=None):
    m, k = x.shape
    nc = w.shape[1]
    tm = min(tm, m)
    return pl.pallas_call(
        functools.partial(_mm_body, act=act, precision=precision),
        grid=(m // tm, nc // tn),
        in_specs=[pl.BlockSpec((tm, k), lambda i, j: (i, 0)),
                  pl.BlockSpec((k, tn), lambda i, j: (0, j))],
        out_specs=pl.BlockSpec((tm, tn), lambda i, j: (i, j)),
        out_shape=jax.ShapeDtypeStruct((m, nc), out_dtype),
        compiler_params=_cparams(2, VMEM_LIMIT), name="proj",
    )(x, w)


def _proj_dilated_body(x_ref, w_ref, *rest, tm, gw3):
    o_refs, scr = rest[:-1], rest[-1]
    for gi, (_, dil) in enumerate(DIL_PAIRS):
        y = jnp.dot(x_ref[...], w_ref[:, gi * gw3:(gi + 1) * gw3], preferred_element_type=F32)
        rows = tm // dil
        for c in range(gw3 // LANE):
            scr[c] = y[:, c * LANE:(c + 1) * LANE]
        for r in range(dil):
            for c in range(gw3 // LANE):
                col = r * gw3 + c * LANE
                o_refs[gi][0, :, col:col + LANE] = scr[c, pl.ds(r, rows, stride=dil), :].astype(BF16)


def _proj_dilated(xb, w, bsz, seq, tm=512):
    n, k = xb.shape
    gw3 = w.shape[1] // len(DIL_PAIRS)
    spt = seq // tm
    return pl.pallas_call(
        functools.partial(_proj_dilated_body, tm=tm, gw3=gw3), grid=(n // tm,),
        in_specs=[pl.BlockSpec((tm, k), lambda i: (i, 0)), pl.BlockSpec(w.shape, lambda i: (0, 0))],
        out_specs=[pl.BlockSpec((1, tm // dil, dil * gw3), lambda i: (i // spt, i % spt, 0))
                   for _, dil in DIL_PAIRS],
        out_shape=[jax.ShapeDtypeStruct((bsz, seq // dil, dil * gw3), BF16) for _, dil in DIL_PAIRS],
        scratch_shapes=[pltpu.VMEM((gw3 // LANE, tm, LANE), F32)],
        compiler_params=_cparams(1, VMEM_LIMIT), name="proj_dilated",
    )(xb, w)


def _layer_norm(z, g, b):
    mu = jnp.mean(z, axis=-1, keepdims=True)
    zc = z - mu
    var = jnp.mean(zc * zc, axis=-1, keepdims=True)
    return zc * lax.rsqrt(var + LN_EPS) * g + b


def _out_ln_body(*refs, n_sum):
    a0 = refs[0][...].astype(F32)
    for r in refs[1:n_sum]:
        a0 = a0 + r[...].astype(F32)
    a1_ref, w0_ref, w1_ref, r_ref, g_ref, b_ref, of_ref, ob_ref = refs[n_sum:]
    y = jnp.dot(a0.astype(BF16), w0_ref[...], preferred_element_type=F32)
    y = y + jnp.dot(a1_ref[...], w1_ref[...], preferred_element_type=F32)
    out = _layer_norm(ALPHA * r_ref[...] + y, g_ref[...], b_ref[...])
    of_ref[...] = out
    ob_ref[...] = out.astype(BF16)


def _out_ln(a0s, a1, w0, w1, resid, g, b, tm=512):
    n, d = resid.shape
    k0, k1 = w0.shape[0], w1.shape[0]
    row = lambda i: (i, 0)
    fixed = lambda i: (0, 0)
    return pl.pallas_call(
        functools.partial(_out_ln_body, n_sum=len(a0s)),
        grid=(n // tm,),
        in_specs=[pl.BlockSpec((tm, k0), row)] * len(a0s) + [
            pl.BlockSpec((tm, k1), row), pl.BlockSpec((k0, d), fixed), pl.BlockSpec((k1, d), fixed),
            pl.BlockSpec((tm, d), row), pl.BlockSpec((1, d), fixed), pl.BlockSpec((1, d), fixed)],
        out_specs=[pl.BlockSpec((tm, d), row), pl.BlockSpec((tm, d), row)],
        out_shape=[jax.ShapeDtypeStruct((n, d), F32), jax.ShapeDtypeStruct((n, d), BF16)],
        compiler_params=_cparams(1), name="out_proj_ln",
    )(*a0s, a1, w0, w1, resid, g, b)


def _rms(x, g):
    return x * lax.rsqrt(jnp.mean(x * x, axis=-1, keepdims=True) + RMS_EPS) * g


def _mla_q_body(c_ref, g_ref, w_ref, wr_ref, cos_ref, sin_ref, o_ref):
    cn = _rms(c_ref[...], g_ref[...]).astype(BF16)
    a = jnp.dot(cn, w_ref[...], preferred_element_type=F32)
    r = jnp.dot(cn, wr_ref[...], preferred_element_type=F32)
    cos, sin = cos_ref[...], sin_ref[...]
    for h in range(H_A):
        sl = slice(h * LANE, (h + 1) * LANE)
        o_ref[:, sl] = (a[:, sl] * cos + r[:, sl] * sin).astype(o_ref.dtype)


def _mla_kv_body(c_ref, kr_ref, krr_ref, g_ref, wk_ref, wv_ref, cos_ref, sin_ref, k_ref, v_ref):
    cn = _rms(c_ref[...], g_ref[...]).astype(BF16)
    kn = jnp.dot(cn, wk_ref[...], preferred_element_type=F32)
    rope = kr_ref[...] * cos_ref[...] + krr_ref[...] * sin_ref[...]
    for h in range(H_A):
        sl = slice(h * LANE, (h + 1) * LANE)
        k_ref[:, sl] = (kn[:, sl] + rope).astype(k_ref.dtype)
    v_ref[...] = jnp.dot(cn, wv_ref[...], preferred_element_type=F32).astype(v_ref.dtype)


def _mla_up(h1, q_norm, wq, wq_rot, kv_norm, wk, wv, cos_q, sin_q, cos_k, sin_k, seq, tm=512):
    n = h1.shape[0]
    spt = seq // tm
    row = lambda c: (lambda i: (i, c))
    pos = lambda i: (i % spt, 0)
    fixed = lambda i: (0, 0)
    q_a = pl.pallas_call(
        _mla_q_body, grid=(n // tm,),
        in_specs=[pl.BlockSpec((tm, Q_LORA), row(0)), pl.BlockSpec((1, Q_LORA), fixed),
                  pl.BlockSpec(wq.shape, fixed), pl.BlockSpec(wq_rot.shape, fixed),
                  pl.BlockSpec((tm, LANE), pos), pl.BlockSpec((tm, LANE), pos)],
        out_specs=pl.BlockSpec((tm, H_A * LANE), row(0)),
        out_shape=jax.ShapeDtypeStruct((n, H_A * LANE), BF16),
        compiler_params=_cparams(1), name="mla_q_up",
    )(h1, q_norm, wq, wq_rot, cos_q, sin_q)
    k_a, v_a = pl.pallas_call(
        _mla_kv_body, grid=(n // tm,),
        in_specs=[pl.BlockSpec((tm, LANE), row(2)), pl.BlockSpec((tm, LANE), row(3)),
                  pl.BlockSpec((tm, LANE), row(4)), pl.BlockSpec((1, KV_LORA), fixed),
                  pl.BlockSpec(wk.shape, fixed), pl.BlockSpec(wv.shape, fixed),
                  pl.BlockSpec((tm, LANE), pos), pl.BlockSpec((tm, LANE), pos)],
        out_specs=[pl.BlockSpec((tm, H_A * LANE), row(0)), pl.BlockSpec((tm, H_A * MLA_V), row(0))],
        out_shape=[jax.ShapeDtypeStruct((n, H_A * LANE), BF16), jax.ShapeDtypeStruct((n, H_A * MLA_V), BF16)],
        compiler_params=_cparams(1), name="mla_kv_up",
    )(h1, h1, h1, kv_norm, wk, wv, cos_k, sin_k)
    return q_a, k_a, v_a


def _flash_body(*refs, T, TK, dq, dv, q_offs, k_offs, v_offs, scale, bias_mode, has_sel, has_gate,
                epilogue, want_lse, nback, lam_init, unroll2):
    R = T // TK
    it = iter(refs)
    q_ref, k_ref, v_ref = next(it), next(it), next(it)
    bias_ref = next(it) if bias_mode else None
    sel_ref = next(it) if has_sel else None
    gate_ref = next(it) if has_gate else None
    lam_ref, dn_ref = (next(it), next(it)) if epilogue == "diff" else (None, None)
    o_ref = next(it)
    lse_ref = next(it) if want_lse else None
    vt_scr = next(it)

    qi = pl.program_id(2)
    seq = v_ref.shape[1]

    @pl.when(qi == 0)
    def _():
        for c in range(seq // TK):
            vt_scr[:, c * TK:(c + 1) * TK] = v_ref[0, c * TK:(c + 1) * TK, :].astype(F32).T.astype(BF16)

    qfull = q_ref[0].astype(F32)
    fold_scale = math.frexp(scale)[0] == 0.5
    qts = [(qfull[:, off:off + dq] * (scale if fold_scale else 1.0)).T.astype(BF16) for off in q_offs]

    def qk(kc):
        kfull = k_ref[0, pl.ds(pl.multiple_of(kc * TK, TK), TK), :]
        return tuple(jnp.dot(kfull[:, k_offs[u]:k_offs[u] + dq], qts[u], preferred_element_type=F32)
                     for u in range(2))

    def update(kc, state, scores, diag):
        start = pl.multiple_of(kc * TK, TK)
        new_state = []
        sel_add = None
        if has_sel:
            per = TK // SEL_BLOCK
            rows = [sel_ref[0, 0, pl.ds(kc * per + a, 1), :] for a in range(per)]
            sel_add = jnp.concatenate([jnp.broadcast_to((r - 1.0) * (-NEG), (SEL_BLOCK, T)) for r in rows], axis=0)
        for u in range(2):
            vt = vt_scr[v_offs[u]:v_offs[u] + dv, pl.ds(start, TK)]
            s = scores[u]
            if not fold_scale:
                s = s * scale
            if bias_mode:
                s = s + bias_ref[u if bias_mode == "pair" else 0, R * qi - kc + (R - 1)]
            elif diag is not None:
                key = lax.broadcasted_iota(jnp.int32, (TK, T), 0) + diag * TK
                qry = lax.broadcasted_iota(jnp.int32, (TK, T), 1)
                s = jnp.where(key <= qry, s, NEG)
            if has_sel:
                s = s + sel_add
            m_prev, l_prev, acc_prev = state[u]
            m_new = jnp.maximum(m_prev, jnp.max(s, axis=0, keepdims=True))
            alpha = jnp.exp(m_prev - m_new)
            p = jnp.exp(s - m_new)
            l_new = alpha * l_prev + jnp.sum(p, axis=0, keepdims=True)
            acc_new = alpha * acc_prev + jnp.dot(vt, p.astype(BF16), preferred_element_type=F32)
            new_state.append((m_new, l_new, acc_new))
        return tuple(new_state)

    init = tuple((jnp.full((1, T), NEG, F32), jnp.zeros((1, T), F32), jnp.zeros((dv, T), F32)) for _ in range(2))
    lo = 0 if nback is None else jnp.maximum(R * qi - nback, 0)

    def step(kc, carry):
        state, scores = carry
        nxt = qk(kc + 1)
        return update(kc, state, scores, None), nxt

    if unroll2:
        assert R % 2 == 0 and nback is None
        state, scores = lax.fori_loop(0, (R * qi - lo) // 2,
                                      lambda j, c: step(lo + 2 * j + 1, step(lo + 2 * j, c)), (init, qk(lo)))
    else:
        state, scores = lax.fori_loop(lo, R * qi, step, (init, qk(lo)))
    for a in range(R):
        nxt = qk(R * qi + a + 1) if a + 1 < R else None
        state = update(R * qi + a, state, scores, a)
        scores = nxt

    outs = [acc / l for _, l, acc in state]
    if epilogue == "diff":
        a = outs[0] - lam_ref[0, 0] * outs[1]
        rinv = lax.rsqrt(jnp.mean(a * a, axis=0, keepdims=True) + RMS_EPS)
        o = (a * rinv * dn_ref[...] * (1.0 - lam_init)).T
    else:
        o = jnp.concatenate(outs, axis=0).T
        if has_gate:
            o = o * gate_ref[0]
    o_ref[0] = o.astype(o_ref.dtype)
    if want_lse:
        lse_ref[0, 0] = jnp.concatenate([m + jnp.log(l) for m, l, _ in state], axis=0)


def _flash(q, k, v, *, n_outer, T, dq, dv, q_col, k_col, v_col, q_offs, k_offs, v_offs, scale,
           out_cols, out_col, bias=None, bias_mode=None, bias_idx=None, sel=None, sel_idx=None,
           gate=None, gate_col=None, lam=None, dnorm=None, lam_init=0.0, epilogue="plain", want_lse=False,
           nback=None, vmem=None, k_w=None, v_w=None, name="flash", TK=None, unroll2=False,
           rep=1, rep_in=0, rep_out=0):
    bsz, seq, _ = q.shape
    TK = TK or T
    assert T % TK == 0
    nq = seq // T
    qw = max(o + dq for o in q_offs)
    kw = k_w or max(o + dq for o in k_offs)
    vw = v_w or max(o + dv for o in v_offs)
    ow = dv if epilogue == "diff" else 2 * dv
    in_specs = [pl.BlockSpec((1, T, qw), lambda g, b, i: (b // rep, i, q_col(g) + (b % rep) * rep_in)),
                pl.BlockSpec((1, seq, kw), lambda g, b, i: (b // rep, 0, k_col(g) + (b % rep) * rep_in)),
                pl.BlockSpec((1, seq, vw), lambda g, b, i: (b // rep, 0, v_col(g) + (b % rep) * rep_in))]
    args = [q, k, v]
    if bias_mode:
        nb = 2 if bias_mode == "pair" else 1
        in_specs.append(pl.BlockSpec((nb,) + bias.shape[1:], lambda g, b, i: (bias_idx(g), 0, 0, 0)))
        args.append(bias)
    if sel is not None:
        in_specs.append(pl.BlockSpec((1, 1, sel.shape[2], T), lambda g, b, i: (b, sel_idx(g), 0, i)))
        args.append(sel)
    if gate is not None:
        in_specs.append(pl.BlockSpec((1, T, ow), lambda g, b, i: (b, i, gate_col(g))))
        args.append(gate)
    if epilogue == "diff":
        in_specs.append(pl.BlockSpec(memory_space=pltpu.SMEM))
        in_specs.append(pl.BlockSpec((dv, 1), lambda g, b, i: (0, 0)))
        args += [lam, dnorm]
    out_specs = [pl.BlockSpec((1, T, ow), lambda g, b, i: (b // rep, i, out_col(g) + (b % rep) * rep_out))]
    out_shape = [jax.ShapeDtypeStruct((bsz, seq, out_cols), BF16)]
    if want_lse:
        out_specs.append(pl.BlockSpec((1, 1, 2, T), lambda g, b, i: (b, g, 0, i)))
        out_shape.append(jax.ShapeDtypeStruct((bsz * rep, n_outer, 2, seq), F32))
    body = functools.partial(
        _flash_body, T=T, TK=TK, dq=dq, dv=dv, q_offs=q_offs, k_offs=k_offs, v_offs=v_offs, scale=scale,
        bias_mode=bias_mode, has_sel=sel is not None, has_gate=gate is not None, epilogue=epilogue,
        want_lse=want_lse, nback=nback, lam_init=lam_init, unroll2=unroll2)
    res = pl.pallas_call(
        body, grid=(n_outer, bsz * rep, nq), in_specs=in_specs, out_specs=out_specs, out_shape=out_shape,
        scratch_shapes=[pltpu.VMEM((vw, seq), BF16)],
        compiler_params=_cparams(3, vmem), name=name,
    )(*args)
    return res if want_lse else res[0]


def _rel_bucket(dist):
    n = jnp.maximum(dist, 0)
    exact = N_BUCKETS // 2
    log_ratio = jnp.log(jnp.maximum(n, 1).astype(F32) / exact) / math.log(REL_MAX_DIST / exact)
    large = exact + (log_ratio * (N_BUCKETS - exact)).astype(jnp.int32)
    return jnp.where(n < exact, n, jnp.minimum(large, N_BUCKETS - 1))


def _toeplitz_bias(tab, T, n_d, dist_scale, max_dist, TK=None):
    TK = TK or T
    R = T // TK
    wlen = T + TK
    x = jnp.arange(wlen)
    dist = (jnp.arange(n_d)[:, None] - (R - 1)) * TK + jnp.where(x < T, x, x - wlen)[None, :]
    w = tab[_rel_bucket(dist * dist_scale)].astype(F32)
    w = jnp.where(((dist >= 0) & (dist <= max_dist))[..., None], w, NEG)
    return _toeplitz(jnp.moveaxis(w, -1, 0), TK, T)


def _toeplitz_body(w_ref, o_ref, *, rows, cols):
    x = jnp.broadcast_to(w_ref[0], (rows, w_ref.shape[-1]))
    o_ref[0] = pltpu.roll(x, 0, 1, stride=1, stride_axis=0)[:, :cols]


def _toeplitz(w, rows, cols):
    wlen = w.shape[-1]
    w2 = w.reshape(-1, 1, wlen)
    out = pl.pallas_call(
        functools.partial(_toeplitz_body, rows=rows, cols=cols), grid=(w2.shape[0],),
        in_specs=[pl.BlockSpec((1, 1, wlen), lambda i: (i, 0, 0))],
        out_specs=pl.BlockSpec((1, rows, cols), lambda i: (i, 0, 0)),
        out_shape=jax.ShapeDtypeStruct((w2.shape[0], rows, cols), w.dtype),
        compiler_params=_cparams(1), name="toeplitz",
    )(w2)
    return out.reshape(w.shape[:-1] + (rows, cols))


def _compress_body(u_ref, pe_ref, w1_ref, w2_ref, o_ref, *, ncp):
    outs = []
    for a in range(2):
        u = u_ref[0, 0, a].astype(F32)
        p1 = jnp.dot(u + pe_ref[a, 0], w1_ref[a, 0], preferred_element_type=F32, precision=HI)
        p2 = jnp.dot(u + pe_ref[a, 1], w1_ref[a, 1], preferred_element_type=F32, precision=HI)
        hid = jax.nn.gelu(p1 + pltpu.roll(p2, ncp - 1, 0))
        outs.append(jnp.dot(hid, w2_ref[a], preferred_element_type=F32, precision=HI))
    o_ref[0, 0] = jnp.concatenate(outs, axis=-1)


def _compress(u, pe, w1, w2):
    bsz, g, _, ncp, width = u.shape
    return pl.pallas_call(
        functools.partial(_compress_body, ncp=ncp), grid=(bsz, g),
        in_specs=[pl.BlockSpec((1, 1, 2, ncp, width), lambda b, gg: (b, gg, 0, 0, 0)),
                  pl.BlockSpec(pe.shape, lambda b, gg: (0, 0, 0, 0)),
                  pl.BlockSpec(w1.shape, lambda b, gg: (0, 0, 0, 0)),
                  pl.BlockSpec(w2.shape, lambda b, gg: (0, 0, 0))],
        out_specs=pl.BlockSpec((1, 1, ncp, 2 * DK_C), lambda b, gg: (b, gg, 0, 0)),
        out_shape=jax.ShapeDtypeStruct((bsz, g, ncp, 2 * DK_C), F32),
        compiler_params=_cparams(2), name="nsa_compress",
    )(u, pe, w1, w2)


def _cmp_attn_body(q_ref, kv_ref, bias_ref, ov_ref, gate_ref, o_ref, sel_ref, *, T, ncp, n_sel, n_top, scale):
    qi = pl.program_id(2)
    kc = kv_ref[0, 0, :, :DK_C]
    vc = kv_ref[0, 0, :, DK_C:]
    t = qi * T + lax.broadcasted_iota(jnp.int32, (T, ncp), 0)
    c = lax.broadcasted_iota(jnp.int32, (T, ncp), 1)
    valid = t >= c * CMP_STRIDE + (CMP_LEN - 1)
    validf = valid.astype(F32)
    psum = jnp.zeros((T, ncp), F32)
    outs = []
    for r in range(R_C):
        q = q_ref[0, :, r * DK_C:(r + 1) * DK_C].astype(F32)
        s = lax.dot_general(q, kc, _NT, preferred_element_type=F32, precision=HI) * scale + bias_ref[r]
        s = jnp.where(valid, s, NEG)
        e = jnp.exp(s - jnp.max(s, axis=-1, keepdims=True)) * validf
        p = e / jnp.maximum(jnp.sum(e, axis=-1, keepdims=True), 1e-30)
        outs.append(jnp.dot(p, vc, preferred_element_type=F32, precision=HI))
        psum = psum + p
    o_ref[0] = (jnp.concatenate(outs, axis=-1) * gate_ref[0]).astype(o_ref.dtype)

    imp = jnp.dot(ov_ref[...], psum.T, preferred_element_type=F32, precision=HI)
    tq = qi * T + lax.broadcasted_iota(jnp.int32, (n_sel, T), 1)
    j = lax.broadcasted_iota(jnp.int32, (n_sel, T), 0)
    forced = (j == tq // SEL_BLOCK) | (j == 0)
    work = jnp.where(forced, BIG, jnp.where(j * SEL_BLOCK <= tq, imp, -BIG))
    sel = jnp.zeros((n_sel, T), F32)
    jf = j.astype(F32)
    for _ in range(n_top):
        _, _, pick = _first_max(work, jf, n_sel)
        sel = jnp.where(pick, 1.0, sel)
        work = jnp.where(pick, -jnp.inf, work)
    sel_ref[0, 0] = sel


def _cmp_attn(h, kvc, bias_c, overlap, gates, T=256):
    bsz, seq, _ = h.shape
    ncp = kvc.shape[2]
    n_sel = seq // SEL_BLOCK
    n_top = min(SEL_TOP, n_sel)
    qw = R_C * DK_C
    return pl.pallas_call(
        functools.partial(_cmp_attn_body, T=T, ncp=ncp, n_sel=n_sel, n_top=n_top, scale=DK_C ** -0.5),
        grid=(G_C, bsz, seq // T),
        in_specs=[pl.BlockSpec((1, T, qw), lambda g, b, i: (b, i, g)),
                  pl.BlockSpec((1, 1, ncp, 2 * DK_C), lambda g, b, i: (b, g, 0, 0)),
                  pl.BlockSpec((R_C, T, ncp), lambda g, b, i: (g, i, 0)),
                  pl.BlockSpec(overlap.shape, lambda g, b, i: (0, 0)),
                  pl.BlockSpec((1, T, qw), lambda g, b, i: (b, i, g))],
        out_specs=[pl.BlockSpec((1, T, qw), lambda g, b, i: (b, i, g)),
                   pl.BlockSpec((1, 1, n_sel, T), lambda g, b, i: (b, g, 0, i))],
        out_shape=[jax.ShapeDtypeStruct((bsz, seq, H_C * DK_C), BF16),
                   jax.ShapeDtypeStruct((bsz, G_C, n_sel, seq), F32)],
        compiler_params=_cparams(3), name="nsa_cmp_attn",
    )(h, kvc, bias_c, overlap, gates)


def _first_max(work, idx, n):
    mx = jnp.max(work, axis=0, keepdims=True)
    first = jnp.min(jnp.where(work == mx, idx, float(n)), axis=0, keepdims=True)
    return mx, first, idx == first


def _router_body(x_ref, wt_ref, b_ref, tri_ref, e_ref, g_ref, r_ref, cnt_ref, carry_scr, *, tm):
    i = pl.program_id(0)

    @pl.when(i == 0)
    def _():
        carry_scr[...] = jnp.zeros(carry_scr.shape, F32)

    st = lax.dot_general(wt_ref[...], x_ref[...], _NT, preferred_element_type=F32, precision=HI)
    scores = jax.nn.sigmoid(st)
    sel = scores + b_ref[...]
    per = N_EXPERTS // N_EXPERT_GROUPS
    fiota = lambda rows: lax.broadcasted_iota(jnp.int32, (rows, tm), 0).astype(F32)
    i_per, i_grp, i_exp = fiota(per), fiota(N_EXPERT_GROUPS), fiota(N_EXPERTS)
    grp_scores = []
    for g in range(N_EXPERT_GROUPS):
        blk = sel[g * per:(g + 1) * per]
        m1, _, pick = _first_max(blk, i_per, per)
        grp_scores.append(m1 + jnp.max(jnp.where(pick, -jnp.inf, blk), axis=0, keepdims=True))
    work = jnp.concatenate(grp_scores, axis=0)
    gmask = jnp.zeros((N_EXPERT_GROUPS, tm), F32)
    for _ in range(TOPK_GROUPS):
        _, _, pick = _first_max(work, i_grp, N_EXPERT_GROUPS)
        gmask = jnp.where(pick, 1.0, gmask)
        work = jnp.where(pick, -jnp.inf, work)
    work = jnp.concatenate([jnp.where(gmask[g:g + 1] > 0.5, sel[g * per:(g + 1) * per], NEG)
                            for g in range(N_EXPERT_GROUPS)], axis=0)
    picks, firsts, vals = [], [], []
    for _ in range(TOP_K):
        _, first, pick = _first_max(work, i_exp, N_EXPERTS)
        picks.append(pick)
        firsts.append(first)
        vals.append(jnp.sum(jnp.where(pick, scores, 0.0), axis=0, keepdims=True))
        work = jnp.where(pick, -jnp.inf, work)
    val = jnp.concatenate(vals, axis=0)
    g_ref[...] = val / jnp.sum(val, axis=0, keepdims=True) * ROUTED_SCALE
    e_ref[...] = jnp.concatenate(firsts, axis=0).astype(jnp.int32)
    onehot = picks[0].astype(F32)
    for pick in picks[1:]:
        onehot = onehot + pick.astype(F32)
    before = jnp.dot(onehot.astype(BF16), tri_ref[...], preferred_element_type=F32) + carry_scr[...]
    r_ref[...] = jnp.concatenate([jnp.sum(jnp.where(pick, before, 0.0), axis=0, keepdims=True)
                                  for pick in picks], axis=0).astype(jnp.int32)
    carry = carry_scr[...] + jnp.sum(onehot, axis=1, keepdims=True)
    carry_scr[...] = carry
    cnt_ref[...] = jnp.broadcast_to(carry, cnt_ref.shape)


def _router(xf, router_w, router_b, tm=512):
    n, d = xf.shape
    tri = (jnp.arange(tm)[:, None] < jnp.arange(tm)[None, :]).astype(BF16)
    col = lambda i: (0, i)
    fixed = lambda i: (0, 0)
    return pl.pallas_call(
        functools.partial(_router_body, tm=tm), grid=(n // tm,),
        in_specs=[pl.BlockSpec((tm, d), lambda i: (i, 0)), pl.BlockSpec((N_EXPERTS, d), fixed),
                  pl.BlockSpec((N_EXPERTS, 1), fixed), pl.BlockSpec((tm, tm), fixed)],
        out_specs=[pl.BlockSpec((TOP_K, tm), col), pl.BlockSpec((TOP_K, tm), col),
                   pl.BlockSpec((TOP_K, tm), col), pl.BlockSpec((N_EXPERTS, LANE), fixed)],
        out_shape=[jax.ShapeDtypeStruct((TOP_K, n), jnp.int32), jax.ShapeDtypeStruct((TOP_K, n), F32),
                   jax.ShapeDtypeStruct((TOP_K, n), jnp.int32), jax.ShapeDtypeStruct((N_EXPERTS, LANE), F32)],
        scratch_shapes=[pltpu.VMEM((N_EXPERTS, 1), F32)],
        compiler_params=_cparams(1), name="router",
    )(xf, router_w.T, router_b.astype(F32)[:, None], tri)


def _pack_rows(x):
    w = x.shape[-1] // 2
    lo = lax.bitcast_convert_type(x[:, :w].astype(BF16).astype(F32), jnp.uint32)
    hi = lax.bitcast_convert_type(x[:, w:].astype(BF16).astype(F32), jnp.uint32)
    return (lo >> 16) | (hi & jnp.uint32(0xFFFF0000))


def _unpack_rows(p):
    lo = lax.bitcast_convert_type(p << 16, F32)
    hi = lax.bitcast_convert_type(p & jnp.uint32(0xFFFF0000), F32)
    return lo, hi


def _dispatch_body(fill_ref, pos_ref, x_ref, z_ref, xs_hbm, xp_scr, sem, *, tm, n_blocks):
    i = pl.program_id(0)
    slot = i % 2
    xp_scr[slot] = _pack_rows(x_ref[...]).reshape(tm // SUBLANES, SUBLANES, -1)
    zsem = sem.at[2]

    def row_copy(s, grp, j, dst):
        return pltpu.make_async_copy(xp_scr.at[s, grp, pl.ds(j, 1)], xs_hbm.at[pl.ds(dst, 1)], sem.at[s])

    def zero_rows(dst, size):
        return pltpu.make_async_copy(z_ref.at[pl.ds(0, size)], xs_hbm.at[pl.ds(dst, size)], zsem)

    def zero_block(blk):
        return pltpu.make_async_copy(z_ref, xs_hbm.at[pl.ds(blk * MOE_BM, MOE_BM)], zsem)

    def issue(grp, c):
        for j in range(SUBLANES):
            for k in range(TOP_K):
                row_copy(slot, grp, j, pos_ref[k, grp * SUBLANES + j]).start(priority=k % 2)
        return c

    def drain(s):
        def body(grp, c):
            for _ in range(SUBLANES * TOP_K):
                row_copy(s, 0, 0, 0).wait()
            return c

        lax.fori_loop(0, tm // SUBLANES, body, 0)

    lax.fori_loop(0, tm // SUBLANES, issue, 0)

    @pl.when(i > 0)
    def _():
        drain(1 - slot)

    @pl.when(i == pl.num_programs(0) - 1)
    def _():
        drain(slot)

    @pl.when(i == 0)
    def _():
        def fill(wait):
            def go(cp):
                if wait:
                    cp.wait()
                else:
                    cp.start()

            def body(e, c):
                lo, cnt = fill_ref[2 * e], fill_ref[2 * e + 1] - fill_ref[2 * e]

                def one_row(r, cc):
                    go(zero_rows(0 if wait else r, 1))
                    return cc

                head = jnp.minimum((-lo) & (SUBLANES - 1), cnt)
                lax.fori_loop(lo, lo + head, one_row, 0)
                base, rem = lo + head, cnt - head
                for bit in reversed(range(SUBLANES.bit_length() - 1, MOE_BM.bit_length() - 1)):
                    size = 1 << bit

                    @pl.when((rem & size) != 0)
                    def _():
                        off = base + ((rem >> (bit + 1)) << (bit + 1))
                        go(zero_rows(0 if wait else pl.multiple_of(off, SUBLANES), size))

                lax.fori_loop(base + (rem & -SUBLANES), base + rem, one_row, 0)
                return c

            lax.fori_loop(0, N_EXPERTS, body, 0)

        fill(False)
        fill(True)

        def tail_start(blk, c):
            zero_block(blk).start()
            return c

        def tail_wait(blk, c):
            zero_block(0).wait()
            return c

        lax.fori_loop(fill_ref[2 * N_EXPERTS], n_blocks, tail_start, 0)
        lax.fori_loop(fill_ref[2 * N_EXPERTS], n_blocks, tail_wait, 0)


def _dispatch(xb, pos, fill, p, tm=256):
    n, d = xb.shape
    grid_spec = pltpu.PrefetchScalarGridSpec(
        num_scalar_prefetch=1, grid=(n // tm,),
        in_specs=[pl.BlockSpec((TOP_K, tm), lambda i, fl: (0, i), memory_space=pltpu.SMEM),
                  pl.BlockSpec((tm, d), lambda i, fl: (i, 0)),
                  pl.BlockSpec((MOE_BM, d // 2), lambda i, fl: (0, 0))],
        out_specs=pl.BlockSpec(memory_space=pl.ANY),
        scratch_shapes=[pltpu.VMEM((2, tm // SUBLANES, SUBLANES, d // 2), jnp.uint32),
                        pltpu.SemaphoreType.DMA((3,))])
    return pl.pallas_call(
        functools.partial(_dispatch_body, tm=tm, n_blocks=p // MOE_BM), grid_spec=grid_spec,
        out_shape=jax.ShapeDtypeStruct((p, d // 2), jnp.uint32),
        compiler_params=_cparams(1), name="dispatch",
    )(fill, pos, xb, jnp.zeros((MOE_BM, d // 2), jnp.uint32))


def _moe_ffn_body(be_ref, nb_ref, x_ref, wg_ref, wu_ref, wd_ref, o_ref, wg_scr, wu_scr, wd_scr):
    i = pl.program_id(0)
    half = x_ref.shape[1]

    @pl.when((i == 0) | (be_ref[i] != be_ref[jnp.maximum(i - 1, 0)]))
    def _():
        wg_scr[...] = wg_ref[0, 0].astype(BF16)
        wu_scr[...] = wu_ref[0, 0].astype(BF16)
        wd_scr[...] = wd_ref[0, 0].astype(BF16)

    @pl.when(i < nb_ref[0])
    def _():
        lo, hi = _unpack_rows(x_ref[...])
        lo, hi = lo.astype(BF16), hi.astype(BF16)
        hg = (jnp.dot(lo, wg_scr[:half], preferred_element_type=F32)
              + jnp.dot(hi, wg_scr[half:], preferred_element_type=F32))
        hu = (jnp.dot(lo, wu_scr[:half], preferred_element_type=F32)
              + jnp.dot(hi, wu_scr[half:], preferred_element_type=F32))
        hb = (hg * jax.nn.sigmoid(hg) * hu).astype(BF16)
        o_ref[...] = _pack_rows(jnp.dot(hb, wd_scr[...], preferred_element_type=F32))

    @pl.when(i >= nb_ref[0])
    def _():
        o_ref[...] = jnp.zeros(o_ref.shape, o_ref.dtype)


def _moe_ffn(xs, blk_e, n_used, wg, wu, wd, layer):
    p, half = xs.shape
    d = 2 * half
    n_blocks = p // MOE_BM
    grid_spec = pltpu.PrefetchScalarGridSpec(
        num_scalar_prefetch=2, grid=(n_blocks,),
        in_specs=[pl.BlockSpec((MOE_BM, half), lambda i, be, nb: (i, 0)),
                  pl.BlockSpec((1, 1, d, D_EXPERT), lambda i, be, nb: (layer, be[i], 0, 0)),
                  pl.BlockSpec((1, 1, d, D_EXPERT), lambda i, be, nb: (layer, be[i], 0, 0)),
                  pl.BlockSpec((1, 1, D_EXPERT, d), lambda i, be, nb: (layer, be[i], 0, 0))],
        out_specs=pl.BlockSpec((MOE_BM, half), lambda i, be, nb: (i, 0)),
        scratch_shapes=[pltpu.VMEM((d, D_EXPERT), BF16), pltpu.VMEM((d, D_EXPERT), BF16),
                        pltpu.VMEM((D_EXPERT, d), BF16)])
    return pl.pallas_call(
        _moe_ffn_body, grid_spec=grid_spec, out_shape=jax.ShapeDtypeStruct((p, half), jnp.uint32),
        compiler_params=_cparams(1), name="expert_ffn",
    )(blk_e, n_used, xs, wg, wu, wd)


def _combine_body(pos_ref, posn_ref, gate_ref, xb_ref, xf_ref, y_hbm, sg_ref, su_ref, sd_ref,
                  g_ref, b_ref, of_ref, ob_ref, buf, sem, *, tm):
    i = pl.program_id(0)
    slot = i % 2

    def row_copy(s, k, grp, j, src):
        return pltpu.make_async_copy(y_hbm.at[pl.ds(src, 1)], buf.at[s, k, grp, pl.ds(j, 1)], sem.at[s])

    def issue_tile(pr, s):
        def issue(grp, c):
            for j in range(SUBLANES):
                for k in range(TOP_K):
                    row_copy(s, k, grp, j, pr[k, grp * SUBLANES + j]).start(priority=k % 2)
            return c

        lax.fori_loop(0, tm // SUBLANES, issue, 0)

    @pl.when(i == 0)
    def _():
        issue_tile(pos_ref, 0)

    @pl.when(i + 1 < pl.num_programs(0))
    def _():
        issue_tile(posn_ref, 1 - slot)

    x = xb_ref[...]
    hg = jnp.dot(x, sg_ref[...], preferred_element_type=F32)
    hu = jnp.dot(x, su_ref[...], preferred_element_type=F32)
    hb = (hg * jax.nn.sigmoid(hg) * hu).astype(BF16)
    y = jnp.dot(hb, sd_ref[...], preferred_element_type=F32)

    def drain(grp, c):
        for _ in range(SUBLANES * TOP_K):
            row_copy(slot, 0, 0, 0, 0).wait()
        return c

    lax.fori_loop(0, tm // SUBLANES, drain, 0)
    gate = gate_ref[...]
    half = buf.shape[-1]
    y_lo, y_hi = y[:, :half], y[:, half:]
    for k in range(TOP_K):
        lo, hi = _unpack_rows(buf[slot, k].reshape(tm, half))
        y_lo = y_lo + gate[:, k:k + 1] * lo
        y_hi = y_hi + gate[:, k:k + 1] * hi
    out = _layer_norm(ALPHA * xf_ref[...] + jnp.concatenate([y_lo, y_hi], axis=-1), g_ref[...], b_ref[...])
    of_ref[...] = out
    ob_ref[...] = out.astype(BF16)


def _combine(pos, gate, xb, xf, y, sg, su, sd, g, b, tm=256):
    n, d = xf.shape
    n_tiles = n // tm
    row = lambda i: (i, 0)
    fixed = lambda i: (0, 0)
    smem_col = pl.BlockSpec((TOP_K, tm), lambda i: (0, i), memory_space=pltpu.SMEM)
    smem_next = pl.BlockSpec((TOP_K, tm), lambda i: (0, jnp.minimum(i + 1, n_tiles - 1)),
                             memory_space=pltpu.SMEM)
    grid_spec = pltpu.PrefetchScalarGridSpec(
        num_scalar_prefetch=0, grid=(n_tiles,),
        in_specs=[smem_col, smem_next, pl.BlockSpec((tm, TOP_K), row),
                  pl.BlockSpec((tm, d), row), pl.BlockSpec((tm, d), row), pl.BlockSpec(memory_space=pl.ANY),
                  pl.BlockSpec(sg.shape, fixed), pl.BlockSpec(su.shape, fixed), pl.BlockSpec(sd.shape, fixed),
                  pl.BlockSpec((1, d), fixed), pl.BlockSpec((1, d), fixed)],
        out_specs=[pl.BlockSpec((tm, d), row), pl.BlockSpec((tm, d), row)],
        scratch_shapes=[pltpu.VMEM((2, TOP_K, tm // SUBLANES, SUBLANES, d // 2), jnp.uint32),
                        pltpu.SemaphoreType.DMA((2,))])
    return pl.pallas_call(
        functools.partial(_combine_body, tm=tm), grid_spec=grid_spec,
        out_shape=[jax.ShapeDtypeStruct((n, d), F32), jax.ShapeDtypeStruct((n, d), BF16)],
        compiler_params=_cparams(1), name="combine",
    )(pos, pos, gate, xb, xf, y, sg, su, sd, g, b)


def _moe(xf, xb, router_w, router_b, w_gate, w_up, w_down, sh_gate, sh_up, sh_down, ln_g, ln_b, layer=0):
    n, d = xf.shape
    e_idx, gate, rank, cnt = _router(xf, router_w, router_b)
    counts = cnt[:, 0].astype(jnp.int32)
    padded = (counts + MOE_BM - 1) // MOE_BM * MOE_BM
    pad_end = jnp.cumsum(padded)
    pad_start = (pad_end - padded).astype(jnp.int32)
    n_blocks = (n * TOP_K + N_EXPERTS * (MOE_BM - 1) + MOE_BM - 1) // MOE_BM
    blk_e = jnp.minimum(jnp.sum(pad_end[None, :] <= (jnp.arange(n_blocks) * MOE_BM)[:, None], axis=1),
                        N_EXPERTS - 1).astype(jnp.int32)
    n_used = (pad_end[-1] // MOE_BM).astype(jnp.int32).reshape(1)
    fill = jnp.concatenate([jnp.stack([pad_start + counts, pad_end], axis=1).reshape(-1), n_used]).astype(jnp.int32)
    experts = jnp.arange(N_EXPERTS, dtype=jnp.int32)[:, None, None]
    pos = rank + jnp.sum(jnp.where(e_idx[None] == experts, pad_start[:, None, None], 0), axis=0)
    xs = _dispatch(xb, pos, fill, n_blocks * MOE_BM)
    y = _moe_ffn(xs, blk_e, n_used, w_gate, w_up, w_down, layer)
    return _combine(pos, gate.T, xb, xf, y, sh_gate, sh_up, sh_down, ln_g[None], ln_b[None])


def _rope_tables(seq):
    half = ROPE_DIM // 2
    freqs = ROPE_THETA ** (-jnp.arange(half, dtype=F32) / half)
    ang = jnp.arange(seq).astype(F32)[:, None] * freqs
    cos = jnp.concatenate([jnp.cos(ang)] * 2, -1)
    sin = jnp.concatenate([jnp.sin(ang)] * 2, -1)
    z = lambda w: jnp.zeros((seq, w), F32)
    pad = LANE - NOPE - ROPE_DIM
    cos_q = jnp.concatenate([jnp.ones((seq, NOPE), F32), cos, z(pad)], -1)
    sin_q = jnp.concatenate([z(NOPE), sin, z(pad)], -1)
    cos_k = jnp.concatenate([z(NOPE), cos, z(pad)], -1)
    return cos_q, sin_q, cos_k, sin_q


def _rot_cols(w):
    half = w.shape[-1] // 2
    return jnp.concatenate([-w[..., half:], w[..., :half]], -1)


def _mixer_ab(xb, xf, bsz, seq, w_in, q_norm, w_uq, kv_norm, w_ukv, w_out, ln_g, ln_b, rope_tabs, dil_bias):
    n = bsz * seq
    d = w_in.shape[0]
    c0 = Q_LORA + KV_LORA
    w_kr = w_in[:, c0:c0 + ROPE_DIM]
    zc = lambda w: jnp.zeros((d, w), F32)
    pad = LANE - NOPE - ROPE_DIM
    w1 = jnp.concatenate([w_in[:, :c0], zc(NOPE), w_kr, zc(pad), zc(NOPE), _rot_cols(w_kr), zc(pad)], 1)
    h1 = _mm(xb, w1.astype(BF16), F32, tn=w1.shape[1])
    gw = H_B_GROUP * HD_B
    w_b = w_in[:, c0 + ROPE_DIM:].reshape(d, 3, len(DIL_PAIRS), gw).transpose(0, 2, 1, 3).reshape(d, -1)
    h2 = _proj_dilated(xb, w_b.astype(BF16), bsz, seq)

    wq = w_uq.reshape(Q_LORA, H_A, NOPE + ROPE_DIM)
    zq = jnp.zeros((Q_LORA, H_A, pad), F32)
    wq_main = jnp.concatenate([wq, zq], -1).reshape(Q_LORA, H_A * LANE)
    wq_rot = jnp.concatenate([jnp.zeros((Q_LORA, H_A, NOPE), F32), _rot_cols(wq[..., NOPE:]), zq], -1)
    wq_rot = wq_rot.reshape(Q_LORA, H_A * LANE)
    wkv = w_ukv.reshape(KV_LORA, H_A, NOPE + MLA_V)
    wk = jnp.concatenate([wkv[..., :NOPE], jnp.zeros((KV_LORA, H_A, LANE - NOPE), F32)], -1)
    wk = wk.reshape(KV_LORA, H_A * LANE)
    wv = wkv[..., NOPE:].reshape(KV_LORA, H_A * MLA_V)
    q_a, k_a, v_a = _mla_up(h1, q_norm[None], wq_main.astype(BF16), wq_rot.astype(BF16), kv_norm[None],
                            wk.astype(BF16), wv.astype(BF16), *rope_tabs, seq)
    o_a = _flash(q_a.reshape(bsz, seq, -1), k_a.reshape(bsz, seq, -1), v_a.reshape(bsz, seq, -1), name="mla_attn",
                 n_outer=H_A // 2, T=min(ATT_TQ, seq), TK=ATT_TK, unroll2=True, dq=LANE, dv=MLA_V,
                 q_col=lambda g: g, k_col=lambda g: g, v_col=lambda g: g,
                 q_offs=(0, LANE), k_offs=(0, LANE), v_offs=(0, MLA_V),
                 scale=(NOPE + ROPE_DIM) ** -0.5, out_cols=H_A * MLA_V, out_col=lambda g: g)

    gb = gw // LANE
    outs, lses = [], []
    for gi, (window, dil) in enumerate(DIL_PAIRS):
        L = seq // dil
        t = h2[gi]
        o, lse = _flash(t, t, t, name="dilated_attn", n_outer=2, T=min(ATT_TQ, L), TK=min(ATT_TK, L),
                        dq=HD_B, dv=HD_B,
                        q_col=lambda g: g, k_col=lambda g: gb + g, v_col=lambda g: 2 * gb + g,
                        q_offs=(0, HD_B), k_offs=(0, HD_B), v_offs=(0, HD_B), scale=HD_B ** -0.5,
                        out_cols=dil * gw, out_col=lambda g: g, bias=dil_bias[gi], bias_mode="pair",
                        bias_idx=lambda g: g, want_lse=True, nback=-(-(window // dil) // min(ATT_TK, L)),
                        rep=dil, rep_in=3 * gb, rep_out=gb)
        outs.append(o.reshape(bsz, seq, H_B_GROUP, HD_B).astype(F32))
        lses.append(lse.reshape(bsz, dil, H_B_GROUP, L).transpose(0, 3, 1, 2).reshape(bsz, seq, H_B_GROUP))
    w = jax.nn.softmax(jnp.stack(lses), axis=0)
    o_b = jnp.sum(w[..., None] * jnp.stack(outs), axis=0).astype(BF16).reshape(n, gw)
    na = H_A * MLA_V
    return _out_ln([o_a.reshape(n, na)], o_b, w_out[:na].astype(BF16), w_out[na:].astype(BF16),
                   xf, ln_g[None], ln_b[None])


def _mixer_cd(xb, xf, bsz, seq, w_in, pos_k, k_w1, k_w2, pos_v, v_w1, v_w2, lq1, lk1, lq2, lk2, d_norm,
              w_out, ln_g, ln_b, lam_init, tabs):
    n = bsz * seq
    qc_w = H_C * DK_C
    kv_w = G_C * DK_C
    off = qc_w
    kvs = []
    for _ in range(3):
        wk_ = w_in[:, off:off + kv_w].reshape(-1, G_C, DK_C)
        wv_ = w_in[:, off + kv_w:off + 2 * kv_w].reshape(-1, G_C, DK_C)
        kvs.append(jnp.concatenate([wk_, wv_], -1).reshape(-1, 2 * kv_w))
        off += 2 * kv_w
    g_off = off
    d_off = off + 3 * H_C
    w_main = jnp.concatenate([w_in[:, :qc_w]] + kvs + [w_in[:, d_off:]], 1)
    h = _mm(xb, w_main.astype(BF16), BF16, tn=w_main.shape[1] // 2).reshape(bsz, seq, -1)
    w_g = jnp.repeat(w_in[:, g_off:d_off], DK_C, axis=1)
    gates = _mm(xb, w_g.astype(BF16), F32, tn=w_g.shape[1] // 2, act="sigmoid").reshape(bsz, seq, -1)

    ncp = seq // CMP_STRIDE
    half = CMP_STRIDE * DK_C
    kv_cmp = h[:, :, qc_w:qc_w + 2 * kv_w].reshape(bsz, ncp, CMP_STRIDE, G_C, 2, DK_C)
    u = kv_cmp.transpose(0, 3, 4, 1, 2, 5).reshape(bsz, G_C, 2, ncp, half)
    pe = jnp.stack([pos_k.reshape(2, 1, half), pos_v.reshape(2, 1, half)])
    w1 = jnp.stack([k_w1.reshape(2, half, CMP_HID), v_w1.reshape(2, half, CMP_HID)])
    w2 = jnp.stack([k_w2, v_w2])
    kvc = _compress(u, pe, w1, w2)
    o_cmp, sel = _cmp_attn(h, kvc, tabs["bias_c"], tabs["overlap"], gates)

    cb = qc_w // LANE
    scale = DK_C ** -0.5
    n_pairs = H_C // 2
    nsa = dict(n_outer=n_pairs, dq=DK_C, dv=DK_C, q_col=lambda g: g, q_offs=(0, DK_C), k_offs=(0, 0),
               v_offs=(DK_C, DK_C), scale=scale, out_cols=qc_w, out_col=lambda g: g, bias_mode="pair",
               bias_idx=lambda g: g, gate=gates, vmem=VMEM_LIMIT, k_w=LANE, v_w=LANE)
    o_sel = _flash(h, h, h, name="nsa_sel_attn", T=ATT_TQ, TK=ATT_TK, k_col=lambda g: cb + 2 + g // 2, v_col=lambda g: cb + 2 + g // 2,
                   bias=tabs["bias_sel"], sel=sel, sel_idx=lambda g: g // 2,
                   gate_col=lambda g: n_pairs + g, **nsa)
    o_win = _flash(h, h, h, name="nsa_win_attn", T=ATT_TQ, TK=ATT_TK, k_col=lambda g: cb + 4 + g // 2,
                   v_col=lambda g: cb + 4 + g // 2, bias=tabs["bias_win"], gate_col=lambda g: 2 * n_pairs + g,
                   nback=-(-(WIN - 1) // ATT_TK), **nsa)

    lam = (jnp.exp(jnp.sum(lq1.astype(F32) * lk1.astype(F32)))
           - jnp.exp(jnp.sum(lq2.astype(F32) * lk2.astype(F32))) + lam_init).reshape(1, 1)
    db = cb + 6
    o_d = _flash(h, h, h, name="diff_attn", n_outer=H_D, T=ATT_TQ, TK=ATT_TK, dq=DD, dv=2 * DD,
                 q_col=lambda g: db + g, k_col=lambda g: db + H_D + g, v_col=lambda g: db + 2 * H_D + g,
                 q_offs=(0, DD), k_offs=(0, DD), v_offs=(0, 0), scale=DD ** -0.5,
                 out_cols=H_D * 2 * DD, out_col=lambda g: g, bias=tabs["bias_d"], bias_mode="shared",
                 bias_idx=lambda g: g, lam=lam, dnorm=d_norm[:, None], lam_init=lam_init, epilogue="diff",
                 vmem=VMEM_LIMIT)
    r2 = lambda a: a.reshape(n, -1)
    return _out_ln([r2(o_cmp), r2(o_sel), r2(o_win)], r2(o_d), w_out[:qc_w].astype(BF16),
                   w_out[qc_w:].astype(BF16), xf, ln_g[None], ln_b[None])


def _nsa_tables(rel_bias, seq):
    tab_c = rel_bias[:, H_B:H_B + H_C]
    tab_d = rel_bias[:, H_B + H_C:H_B + H_C + H_D]
    ncp = seq // CMP_STRIDE
    n_sel = seq // SEL_BLOCK
    x = jnp.arange(2 * ncp)
    c_minus_a = jnp.where(x < ncp, x, x - 2 * ncp)
    dist = -CMP_STRIDE * c_minus_a[None, :] + jnp.arange(CMP_STRIDE)[:, None] - (CMP_LEN - 1)
    w = jnp.moveaxis(tab_c[_rel_bucket(dist)].astype(F32), -1, 0)
    bias_c = _toeplitz(w, ncp, ncp).transpose(0, 2, 1, 3).reshape(H_C, seq, ncp)
    c0 = jnp.arange(ncp) * CMP_STRIDE
    s0 = jnp.arange(n_sel) * SEL_BLOCK
    overlap = jnp.maximum(jnp.minimum(c0[:, None] + CMP_LEN, s0[None, :] + SEL_BLOCK)
                          - jnp.maximum(c0[:, None], s0[None, :]), 0).astype(F32) / CMP_LEN
    return {
        "bias_c": bias_c, "overlap": overlap.T,
        "bias_sel": _toeplitz_bias(tab_c, ATT_TQ, seq // ATT_TK, 1, seq, ATT_TK),
        "bias_win": _toeplitz_bias(tab_c, ATT_TQ, ATT_TQ // ATT_TK + -(-(WIN - 1) // ATT_TK), 1, WIN - 1, ATT_TK),
        "bias_d": _toeplitz_bias(tab_d, ATT_TQ, seq // ATT_TK, 1, seq, ATT_TK),
    }


def kernel(x, rel_bias, ab_w_in, mla_q_norm, mla_w_uq, mla_kv_norm, mla_w_ukv, ab_w_out, cd_w_in, nsa_cmp_pos_k, nsa_cmp_k_w1, nsa_cmp_k_w2, nsa_cmp_pos_v, nsa_cmp_v_w1, nsa_cmp_v_w2, diff_lambda_q1, diff_lambda_k1, diff_lambda_q2, diff_lambda_k2, diff_norm, cd_w_out, ln1_g, ln1_b, ln2_g, ln2_b, router_w, router_b, exp_w_gate, exp_w_up, exp_w_down, sh_w_gate, sh_w_up, sh_w_down):
    bsz, seq, d = x.shape
    n = bsz * seq
    depth = ln1_g.shape[0]
    rope_tabs = _rope_tables(seq)
    dil_bias = []
    for gi, (window, dil) in enumerate(DIL_PAIRS):
        tq, tk = min(ATT_TQ, seq // dil), min(ATT_TK, seq // dil)
        dil_bias.append(_toeplitz_bias(rel_bias[:, gi * H_B_GROUP:(gi + 1) * H_B_GROUP], tq,
                                       tq // tk + -(-(window // dil) // tk), dil, window // dil, tk))
    nsa_tabs = _nsa_tables(rel_bias, seq)
    xf = x.reshape(n, d)
    xb = xf.astype(BF16)
    for l in range(depth):
        i = l // 2
        if l % 2 == 0:
            xf, xb = _mixer_ab(xb, xf, bsz, seq, ab_w_in[i], mla_q_norm[i], mla_w_uq[i], mla_kv_norm[i],
                               mla_w_ukv[i], ab_w_out[i], ln1_g[l], ln1_b[l], rope_tabs, dil_bias)
        else:
            lam_init = 0.8 - 0.6 * math.exp(-0.3 * l)
            xf, xb = _mixer_cd(xb, xf, bsz, seq, cd_w_in[i], nsa_cmp_pos_k[i], nsa_cmp_k_w1[i],
                               nsa_cmp_k_w2[i], nsa_cmp_pos_v[i], nsa_cmp_v_w1[i], nsa_cmp_v_w2[i],
                               diff_lambda_q1[i], diff_lambda_k1[i], diff_lambda_q2[i], diff_lambda_k2[i],
                               diff_norm[i], cd_w_out[i], ln1_g[l], ln1_b[l], lam_init, nsa_tabs)
        xf, xb = _moe(xf, xb, router_w[l], router_b[l], exp_w_gate, exp_w_up, exp_w_down,
                      sh_w_gate[l].astype(BF16), sh_w_up[l].astype(BF16), sh_w_down[l].astype(BF16),
                      ln2_g[l], ln2_b[l], layer=l)
    return xf.reshape(bsz, seq, d)
```

```python
import functools
import math

import jax
import jax.numpy as jnp
from jax import lax
from jax.experimental import pallas as pl
from jax.experimental.pallas import tpu as pltpu

F32 = jnp.float32
BF16 = jnp.bfloat16
HI = lax.Precision.HIGHEST

DEPTH = 4
NEG = -1e30
BIG = 1e9
LN_EPS = 1e-5
RMS_EPS = 1e-6
ALPHA = (2 * DEPTH) ** 0.25

N_BUCKETS = 32
REL_MAX_DIST = 2048

H_A = 12
NOPE = 64
ROPE_DIM = 32
MLA_V = 64
Q_LORA = 256
KV_LORA = 128
ROPE_THETA = 10000.0

DIL_PAIRS = ((128, 1), (512, 4), (2048, 16))
H_B_GROUP = 4
H_B = 12
HD_B = 64

H_C = 8
G_C = 2
R_C = 4
DK_C = 64
CMP_LEN = 32
CMP_STRIDE = 16
CMP_HID = 64
SEL_BLOCK = 64
SEL_TOP = 16
WIN = 512

H_D = 4
DD = 64

N_EXPERTS = 64
TOP_K = 8
N_EXPERT_GROUPS = 8
TOPK_GROUPS = 4
D_EXPERT = 256
ROUTED_SCALE = 2.5

LANE = 128
SUBLANES = 8
MOE_BM = 512
ATT_TQ = 512
ATT_TK = 256
VMEM_LIMIT = 56 * 1024 * 1024

_NT = (((1,), (1,)), ((), ()))


def _cparams(n_axes, vmem=None):
    return pltpu.CompilerParams(dimension_semantics=("arbitrary",) * n_axes, vmem_limit_bytes=vmem)


def _mm_body(x_ref, w_ref, o_ref, *, act, precision):
    y = jnp.dot(x_ref[...], w_ref[...], preferred_element_type=F32, precision=precision)
    if act == "sigmoid":
        y = jax.nn.sigmoid(y)
    o_ref[...] = y.astype(o_ref.dtype)


def _mm(x, w, out_dtype, tn, tm=1024, act=None, precision=None):
    m, k = x.shape
    nc = w.shape[1]
    tm = min(tm, m)
    return pl.pallas_call(
        functools.partial(_mm_body, act=act, precision=precision),
        grid=(m // tm, nc // tn),
        in_specs=[pl.BlockSpec((tm, k), lambda i, j: (i, 0)),
                  pl.BlockSpec((k, tn), lambda i, j: (0, j))],
        out_specs=pl.BlockSpec((tm, tn), lambda i, j: (i, j)),
        out_shape=jax.ShapeDtypeStruct((m, nc), out_dtype),
        compiler_params=_cparams(2, VMEM_LIMIT), name="proj",
    )(x, w)


def _proj_dilated_body(x_ref, w_ref, *rest, tm, gw3):
    o_refs, scr = rest[:-1], rest[-1]
    for gi, (_, dil) in enumerate(DIL_PAIRS):
        y = jnp.dot(x_ref[...], w_ref[:, gi * gw3:(gi + 1) * gw3], preferred_element_type=F32)
        rows = tm // dil
        for c in range(gw3 // LANE):
            scr[c] = y[:, c * LANE:(c + 1) * LANE]
        for r in range(dil):
            for c in range(gw3 // LANE):
                col = r * gw3 + c * LANE
                o_refs[gi][0, :, col:col + LANE] = scr[c, pl.ds(r, rows, stride=dil), :].astype(BF16)


def _proj_dilated(xb, w, bsz, seq, tm=512):
    n, k = xb.shape
    gw3 = w.shape[1] // len(DIL_PAIRS)
    spt = seq // tm
    return pl.pallas_call(
        functools.partial(_proj_dilated_body, tm=tm, gw3=gw3), grid=(n // tm,),
        in_specs=[pl.BlockSpec((tm, k), lambda i: (i, 0)), pl.BlockSpec(w.shape, lambda i: (0, 0))],
        out_specs=[pl.BlockSpec((1, tm // dil, dil * gw3), lambda i: (i // spt, i % spt, 0))
                   for _, dil in DIL_PAIRS],
        out_shape=[jax.ShapeDtypeStruct((bsz, seq // dil, dil * gw3), BF16) for _, dil in DIL_PAIRS],
        scratch_shapes=[pltpu.VMEM((gw3 // LANE, tm, LANE), F32)],
        compiler_params=_cparams(1, VMEM_LIMIT), name="proj_dilated",
    )(xb, w)


def _layer_norm(z, g, b):
    mu = jnp.mean(z, axis=-1, keepdims=True)
    zc = z - mu
    var = jnp.mean(zc * zc, axis=-1, keepdims=True)
    return zc * lax.rsqrt(var + LN_EPS) * g + b


def _out_ln_body(*refs, n_sum):
    a0 = refs[0][...].astype(F32)
    for r in refs[1:n_sum]:
        a0 = a0 + r[...].astype(F32)
    a1_ref, w0_ref, w1_ref, r_ref, g_ref, b_ref, of_ref, ob_ref = refs[n_sum:]
    y = jnp.dot(a0.astype(BF16), w0_ref[...], preferred_element_type=F32)
    y = y + jnp.dot(a1_ref[...], w1_ref[...], preferred_element_type=F32)
    out = _layer_norm(ALPHA * r_ref[...] + y, g_ref[...], b_ref[...])
    of_ref[...] = out
    ob_ref[...] = out.astype(BF16)


def _out_ln(a0s, a1, w0, w1, resid, g, b, tm=512):
    n, d = resid.shape
    k0, k1 = w0.shape[0], w1.shape[0]
    row = lambda i: (i, 0)
    fixed = lambda i: (0, 0)
    return pl.pallas_call(
        functools.partial(_out_ln_body, n_sum=len(a0s)),
        grid=(n // tm,),
        in_specs=[pl.BlockSpec((tm, k0), row)] * len(a0s) + [
            pl.BlockSpec((tm, k1), row), pl.BlockSpec((k0, d), fixed), pl.BlockSpec((k1, d), fixed),
            pl.BlockSpec((tm, d), row), pl.BlockSpec((1, d), fixed), pl.BlockSpec((1, d), fixed)],
        out_specs=[pl.BlockSpec((tm, d), row), pl.BlockSpec((tm, d), row)],
        out_shape=[jax.ShapeDtypeStruct((n, d), F32), jax.ShapeDtypeStruct((n, d), BF16)],
        compiler_params=_cparams(1), name="out_proj_ln",
    )(*a0s, a1, w0, w1, resid, g, b)


def _rms(x, g):
    return x * lax.rsqrt(jnp.mean(x * x, axis=-1, keepdims=True) + RMS_EPS) * g


def _mla_q_body(c_ref, g_ref, w_ref, wr_ref, cos_ref, sin_ref, o_ref):
    cn = _rms(c_ref[...], g_ref[...]).astype(BF16)
    a = jnp.dot(cn, w_ref[...], preferred_element_type=F32)
    r = jnp.dot(cn, wr_ref[...], preferred_element_type=F32)
    cos, sin = cos_ref[...], sin_ref[...]
    for h in range(H_A):
        sl = slice(h * LANE, (h + 1) * LANE)
        o_ref[:, sl] = (a[:, sl] * cos + r[:, sl] * sin).astype(o_ref.dtype)


def _mla_kv_body(c_ref, kr_ref, krr_ref, g_ref, wk_ref, wv_ref, cos_ref, sin_ref, k_ref, v_ref):
    cn = _rms(c_ref[...], g_ref[...]).astype(BF16)
    kn = jnp.dot(cn, wk_ref[...], preferred_element_type=F32)
    rope = kr_ref[...] * cos_ref[...] + krr_ref[...] * sin_ref[...]
    for h in range(H_A):
        sl = slice(h * LANE, (h + 1) * LANE)
        k_ref[:, sl] = (kn[:, sl] + rope).astype(k_ref.dtype)
    v_ref[...] = jnp.dot(cn, wv_ref[...], preferred_element_type=F32).astype(v_ref.dtype)


def _mla_up(h1, q_norm, wq, wq_rot, kv_norm, wk, wv, cos_q, sin_q, cos_k, sin_k, seq, tm=512):
    n = h1.shape[0]
    spt = seq // tm
    row = lambda c: (lambda i: (i, c))
    pos = lambda i: (i % spt, 0)
    fixed = lambda i: (0, 0)
    q_a = pl.pallas_call(
        _mla_q_body, grid=(n // tm,),
        in_specs=[pl.BlockSpec((tm, Q_LORA), row(0)), pl.BlockSpec((1, Q_LORA), fixed),
                  pl.BlockSpec(wq.shape, fixed), pl.BlockSpec(wq_rot.shape, fixed),
                  pl.BlockSpec((tm, LANE), pos), pl.BlockSpec((tm, LANE), pos)],
        out_specs=pl.BlockSpec((tm, H_A * LANE), row(0)),
        out_shape=jax.ShapeDtypeStruct((n, H_A * LANE), BF16),
        compiler_params=_cparams(1), name="mla_q_up",
    )(h1, q_norm, wq, wq_rot, cos_q, sin_q)
    k_a, v_a = pl.pallas_call(
        _mla_kv_body, grid=(n // tm,),
        in_specs=[pl.BlockSpec((tm, LANE), row(2)), pl.BlockSpec((tm, LANE), row(3)),
                  pl.BlockSpec((tm, LANE), row(4)), pl.BlockSpec((1, KV_LORA), fixed),
                  pl.BlockSpec(wk.shape, fixed), pl.BlockSpec(wv.shape, fixed),
                  pl.BlockSpec((tm, LANE), pos), pl.BlockSpec((tm, LANE), pos)],
        out_specs=[pl.BlockSpec((tm, H_A * LANE), row(0)), pl.BlockSpec((tm, H_A * MLA_V), row(0))],
        out_shape=[jax.ShapeDtypeStruct((n, H_A * LANE), BF16), jax.ShapeDtypeStruct((n, H_A * MLA_V), BF16)],
        compiler_params=_cparams(1), name="mla_kv_up",
    )(h1, h1, h1, kv_norm, wk, wv, cos_k, sin_k)
    return q_a, k_a, v_a


def _flash_body(*refs, T, TK, dq, dv, q_offs, k_offs, v_offs, scale, bias_mode, has_sel, has_gate,
                epilogue, want_lse, nback, lam_init, unroll2):
    R = T // TK
    it = iter(refs)
    q_ref, k_ref, v_ref = next(it), next(it), next(it)
    bias_ref = next(it) if bias_mode else None
    sel_ref = next(it) if has_sel else None
    gate_ref = next(it) if has_gate else None
    lam_ref, dn_ref = (next(it), next(it)) if epilogue == "diff" else (None, None)
    o_ref = next(it)
    lse_ref = next(it) if want_lse else None
    vt_scr = next(it)

    qi = pl.program_id(2)
    seq = v_ref.shape[1]

    @pl.when(qi == 0)
    def _():
        for c in range(seq // TK):
            vt_scr[:, c * TK:(c + 1) * TK] = v_ref[0, c * TK:(c + 1) * TK, :].astype(F32).T.astype(BF16)

    qfull = q_ref[0].astype(F32)
    fold_scale = math.frexp(scale)[0] == 0.5
    qts = [(qfull[:, off:off + dq] * (scale if fold_scale else 1.0)).T.astype(BF16) for off in q_offs]

    def qk(kc):
        kfull = k_ref[0, pl.ds(pl.multiple_of(kc * TK, TK), TK), :]
        return tuple(jnp.dot(kfull[:, k_offs[u]:k_offs[u] + dq], qts[u], preferred_element_type=F32)
                     for u in range(2))

    def update(kc, state, scores, diag):
        start = pl.multiple_of(kc * TK, TK)
        new_state = []
        sel_add = None
        if has_sel:
            per = TK // SEL_BLOCK
            rows = [sel_ref[0, 0, pl.ds(kc * per + a, 1), :] for a in range(per)]
            sel_add = jnp.concatenate([jnp.broadcast_to((r - 1.0) * (-NEG), (SEL_BLOCK, T)) for r in rows], axis=0)
        for u in range(2):
            vt = vt_scr[v_offs[u]:v_offs[u] + dv, pl.ds(start, TK)]
            s = scores[u]
            if not fold_scale:
                s = s * scale
            if bias_mode:
                s = s + bias_ref[u if bias_mode == "pair" else 0, R * qi - kc + (R - 1)]
            elif diag is not None:
                key = lax.broadcasted_iota(jnp.int32, (TK, T), 0) + diag * TK
                qry = lax.broadcasted_iota(jnp.int32, (TK, T), 1)
                s = jnp.where(key <= qry, s, NEG)
            if has_sel:
                s = s + sel_add
            m_prev, l_prev, acc_prev = state[u]
            m_new = jnp.maximum(m_prev, jnp.max(s, axis=0, keepdims=True))
            alpha = jnp.exp(m_prev - m_new)
            p = jnp.exp(s - m_new)
            l_new = alpha * l_prev + jnp.sum(p, axis=0, keepdims=True)
            acc_new = alpha * acc_prev + jnp.dot(vt, p.astype(BF16), preferred_element_type=F32)
            new_state.append((m_new, l_new, acc_new))
        return tuple(new_state)

    init = tuple((jnp.full((1, T), NEG, F32), jnp.zeros((1, T), F32), jnp.zeros((dv, T), F32)) for _ in range(2))
    lo = 0 if nback is None else jnp.maximum(R * qi - nback, 0)

    def step(kc, carry):
        state, scores = carry
        nxt = qk(kc + 1)
        return update(kc, state, scores, None), nxt

    if unroll2:
        assert R % 2 == 0 and nback is None
        state, scores = lax.fori_loop(0, (R * qi - lo) // 2,
                                      lambda j, c: step(lo + 2 * j + 1, step(lo + 2 * j, c)), (init, qk(lo)))
    else:
        state, scores = lax.fori_loop(lo, R * qi, step, (init, qk(lo)))
    for a in range(R):
        nxt = qk(R * qi + a + 1) if a + 1 < R else None
        state = update(R * qi + a, state, scores, a)
        scores = nxt

    outs = [acc / l for _, l, acc in state]
    if epilogue == "diff":
        a = outs[0] - lam_ref[0, 0] * outs[1]
        rinv = lax.rsqrt(jnp.mean(a * a, axis=0, keepdims=True) + RMS_EPS)
        o = (a * rinv * dn_ref[...] * (1.0 - lam_init)).T
    else:
        o = jnp.concatenate(outs, axis=0).T
        if has_gate:
            o = o * gate_ref[0]
    o_ref[0] = o.astype(o_ref.dtype)
    if want_lse:
        lse_ref[0, 0] = jnp.concatenate([m + jnp.log(l) for m, l, _ in state], axis=0)


def _flash(q, k, v, *, n_outer, T, dq, dv, q_col, k_col, v_col, q_offs, k_offs, v_offs, scale,
           out_cols, out_col, bias=None, bias_mode=None, bias_idx=None, sel=None, sel_idx=None,
           gate=None, gate_col=None, lam=None, dnorm=None, lam_init=0.0, epilogue="plain", want_lse=False,
           nback=None, vmem=None, k_w=None, v_w=None, name="flash", TK=None, unroll2=False,
           rep=1, rep_in=0, rep_out=0):
    bsz, seq, _ = q.shape
    TK = TK or T
    assert T % TK == 0
    nq = seq // T
    qw = max(o + dq for o in q_offs)
    kw = k_w or max(o + dq for o in k_offs)
    vw = v_w or max(o + dv for o in v_offs)
    ow = dv if epilogue == "diff" else 2 * dv
    in_specs = [pl.BlockSpec((1, T, qw), lambda g, b, i: (b // rep, i, q_col(g) + (b % rep) * rep_in)),
                pl.BlockSpec((1, seq, kw), lambda g, b, i: (b // rep, 0, k_col(g) + (b % rep) * rep_in)),
                pl.BlockSpec((1, seq, vw), lambda g, b, i: (b // rep, 0, v_col(g) + (b % rep) * rep_in))]
    args = [q, k, v]
    if bias_mode:
        nb = 2 if bias_mode == "pair" else 1
        in_specs.append(pl.BlockSpec((nb,) + bias.shape[1:], lambda g, b, i: (bias_idx(g), 0, 0, 0)))
        args.append(bias)
    if sel is not None:
        in_specs.append(pl.BlockSpec((1, 1, sel.shape[2], T), lambda g, b, i: (b, sel_idx(g), 0, i)))
        args.append(sel)
    if gate is not None:
        in_specs.append(pl.BlockSpec((1, T, ow), lambda g, b, i: (b, i, gate_col(g))))
        args.append(gate)
    if epilogue == "diff":
        in_specs.append(pl.BlockSpec(memory_space=pltpu.SMEM))
        in_specs.append(pl.BlockSpec((dv, 1), lambda g, b, i: (0, 0)))
        args += [lam, dnorm]
    out_specs = [pl.BlockSpec((1, T, ow), lambda g, b, i: (b // rep, i, out_col(g) + (b % rep) * rep_out))]
    out_shape = [jax.ShapeDtypeStruct((bsz, seq, out_cols), BF16)]
    if want_lse:
        out_specs.append(pl.BlockSpec((1, 1, 2, T), lambda g, b, i: (b, g, 0, i)))
        out_shape.append(jax.ShapeDtypeStruct((bsz * rep, n_outer, 2, seq), F32))
    body = functools.partial(
        _flash_body, T=T, TK=TK, dq=dq, dv=dv, q_offs=q_offs, k_offs=k_offs, v_offs=v_offs, scale=scale,
        bias_mode=bias_mode, has_sel=sel is not None, has_gate=gate is not None, epilogue=epilogue,
        want_lse=want_lse, nback=nback, lam_init=lam_init, unroll2=unroll2)
    res = pl.pallas_call(
        body, grid=(n_outer, bsz * rep, nq), in_specs=in_specs, out_specs=out_specs, out_shape=out_shape,
        scratch_shapes=[pltpu.VMEM((vw, seq), BF16)],
        compiler_params=_cparams(3, vmem), name=name,
    )(*args)
    return res if want_lse else res[0]


def _rel_bucket(dist):
    n = jnp.maximum(dist, 0)
    exact = N_BUCKETS // 2
    log_ratio = jnp.log(jnp.maximum(n, 1).astype(F32) / exact) / math.log(REL_MAX_DIST / exact)
    large = exact + (log_ratio * (N_BUCKETS - exact)).astype(jnp.int32)
    return jnp.where(n < exact, n, jnp.minimum(large, N_BUCKETS - 1))


def _toeplitz_bias(tab, T, n_d, dist_scale, max_dist, TK=None):
    TK = TK or T
    R = T // TK
    wlen = T + TK
    x = jnp.arange(wlen)
    dist = (jnp.arange(n_d)[:, None] - (R - 1)) * TK + jnp.where(x < T, x, x - wlen)[None, :]
    w = tab[_rel_bucket(dist * dist_scale)].astype(F32)
    w = jnp.where(((dist >= 0) & (dist <= max_dist))[..., None], w, NEG)
    return _toeplitz(jnp.moveaxis(w, -1, 0), TK, T)


def _toeplitz_body(w_ref, o_ref, *, rows, cols):
    x = jnp.broadcast_to(w_ref[0], (rows, w_ref.shape[-1]))
    o_ref[0] = pltpu.roll(x, 0, 1, stride=1, stride_axis=0)[:, :cols]


def _toeplitz(w, rows, cols):
    wlen = w.shape[-1]
    w2 = w.reshape(-1, 1, wlen)
    out = pl.pallas_call(
        functools.partial(_toeplitz_body, rows=rows, cols=cols), grid=(w2.shape[0],),
        in_specs=[pl.BlockSpec((1, 1, wlen), lambda i: (i, 0, 0))],
        out_specs=pl.BlockSpec((1, rows, cols), lambda i: (i, 0, 0)),
        out_shape=jax.ShapeDtypeStruct((w2.shape[0], rows, cols), w.dtype),
        compiler_params=_cparams(1), name="toeplitz",
    )(w2)
    return out.reshape(w.shape[:-1] + (rows, cols))


def _compress_body(u_ref, pe_ref, w1_ref, w2_ref, o_ref, *, ncp):
    outs = []
    for a in range(2):
        u = u_ref[0, 0, a].astype(F32)
        p1 = jnp.dot(u + pe_ref[a, 0], w1_ref[a, 0], preferred_element_type=F32, precision=HI)
        p2 = jnp.dot(u + pe_ref[a, 1], w1_ref[a, 1], preferred_element_type=F32, precision=HI)
        hid = jax.nn.gelu(p1 + pltpu.roll(p2, ncp - 1, 0))
        outs.append(jnp.dot(hid, w2_ref[a], preferred_element_type=F32, precision=HI))
    o_ref[0, 0] = jnp.concatenate(outs, axis=-1)


def _compress(u, pe, w1, w2):
    bsz, g, _, ncp, width = u.shape
    return pl.pallas_call(
        functools.partial(_compress_body, ncp=ncp), grid=(bsz, g),
        in_specs=[pl.BlockSpec((1, 1, 2, ncp, width), lambda b, gg: (b, gg, 0, 0, 0)),
                  pl.BlockSpec(pe.shape, lambda b, gg: (0, 0, 0, 0)),
                  pl.BlockSpec(w1.shape, lambda b, gg: (0, 0, 0, 0)),
                  pl.BlockSpec(w2.shape, lambda b, gg: (0, 0, 0))],
        out_specs=pl.BlockSpec((1, 1, ncp, 2 * DK_C), lambda b, gg: (b, gg, 0, 0)),
        out_shape=jax.ShapeDtypeStruct((bsz, g, ncp, 2 * DK_C), F32),
        compiler_params=_cparams(2), name="nsa_compress",
    )(u, pe, w1, w2)


def _cmp_attn_body(q_ref, kv_ref, bias_ref, ov_ref, gate_ref, o_ref, sel_ref, *, T, ncp, n_sel, n_top, scale):
    qi = pl.program_id(2)
    kc = kv_ref[0, 0, :, :DK_C]
    vc = kv_ref[0, 0, :, DK_C:]
    t = qi * T + lax.broadcasted_iota(jnp.int32, (T, ncp), 0)
    c = lax.broadcasted_iota(jnp.int32, (T, ncp), 1)
    valid = t >= c * CMP_STRIDE + (CMP_LEN - 1)
    validf = valid.astype(F32)
    psum = jnp.zeros((T, ncp), F32)
    outs = []
    for r in range(R_C):
        q = q_ref[0, :, r * DK_C:(r + 1) * DK_C].astype(F32)
        s = lax.dot_general(q, kc, _NT, preferred_element_type=F32, precision=HI) * scale + bias_ref[r]
        s = jnp.where(valid, s, NEG)
        e = jnp.exp(s - jnp.max(s, axis=-1, keepdims=True)) * validf
        p = e / jnp.maximum(jnp.sum(e, axis=-1, keepdims=True), 1e-30)
        outs.append(jnp.dot(p, vc, preferred_element_type=F32, precision=HI))
        psum = psum + p
    o_ref[0] = (jnp.concatenate(outs, axis=-1) * gate_ref[0]).astype(o_ref.dtype)

    imp = jnp.dot(ov_ref[...], psum.T, preferred_element_type=F32, precision=HI)
    tq = qi * T + lax.broadcasted_iota(jnp.int32, (n_sel, T), 1)
    j = lax.broadcasted_iota(jnp.int32, (n_sel, T), 0)
    forced = (j == tq // SEL_BLOCK) | (j == 0)
    work = jnp.where(forced, BIG, jnp.where(j * SEL_BLOCK <= tq, imp, -BIG))
    sel = jnp.zeros((n_sel, T), F32)
    jf = j.astype(F32)
    for _ in range(n_top):
        _, _, pick = _first_max(work, jf, n_sel)
        sel = jnp.where(pick, 1.0, sel)
        work = jnp.where(pick, -jnp.inf, work)
    sel_ref[0, 0] = sel


def _cmp_attn(h, kvc, bias_c, overlap, gates, T=256):
    bsz, seq, _ = h.shape
    ncp = kvc.shape[2]
    n_sel = seq // SEL_BLOCK
    n_top = min(SEL_TOP, n_sel)
    qw = R_C * DK_C
    return pl.pallas_call(
        functools.partial(_cmp_attn_body, T=T, ncp=ncp, n_sel=n_sel, n_top=n_top, scale=DK_C ** -0.5),
        grid=(G_C, bsz, seq // T),
        in_specs=[pl.BlockSpec((1, T, qw), lambda g, b, i: (b, i, g)),
                  pl.BlockSpec((1, 1, ncp, 2 * DK_C), lambda g, b, i: (b, g, 0, 0)),
                  pl.BlockSpec((R_C, T, ncp), lambda g, b, i: (g, i, 0)),
                  pl.BlockSpec(overlap.shape, lambda g, b, i: (0, 0)),
                  pl.BlockSpec((1, T, qw), lambda g, b, i: (b, i, g))],
        out_specs=[pl.BlockSpec((1, T, qw), lambda g, b, i: (b, i, g)),
                   pl.BlockSpec((1, 1, n_sel, T), lambda g, b, i: (b, g, 0, i))],
        out_shape=[jax.ShapeDtypeStruct((bsz, seq, H_C * DK_C), BF16),
                   jax.ShapeDtypeStruct((bsz, G_C, n_sel, seq), F32)],
        compiler_params=_cparams(3), name="nsa_cmp_attn",
    )(h, kvc, bias_c, overlap, gates)


def _first_max(work, idx, n):
    mx = jnp.max(work, axis=0, keepdims=True)
    first = jnp.min(jnp.where(work == mx, idx, float(n)), axis=0, keepdims=True)
    return mx, first, idx == first


def _router_body(x_ref, wt_ref, b_ref, tri_ref, e_ref, g_ref, r_ref, cnt_ref, carry_scr, *, tm):
    i = pl.program_id(0)

    @pl.when(i == 0)
    def _():
        carry_scr[...] = jnp.zeros(carry_scr.shape, F32)

    st = lax.dot_general(wt_ref[...], x_ref[...], _NT, preferred_element_type=F32, precision=HI)
    scores = jax.nn.sigmoid(st)
    sel = scores + b_ref[...]
    per = N_EXPERTS // N_EXPERT_GROUPS
    fiota = lambda rows: lax.broadcasted_iota(jnp.int32, (rows, tm), 0).astype(F32)
    i_per, i_grp, i_exp = fiota(per), fiota(N_EXPERT_GROUPS), fiota(N_EXPERTS)
    grp_scores = []
    for g in range(N_EXPERT_GROUPS):
        blk = sel[g * per:(g + 1) * per]
        m1, _, pick = _first_max(blk, i_per, per)
        grp_scores.append(m1 + jnp.max(jnp.where(pick, -jnp.inf, blk), axis=0, keepdims=True))
    work = jnp.concatenate(grp_scores, axis=0)
    gmask = jnp.zeros((N_EXPERT_GROUPS, tm), F32)
    for _ in range(TOPK_GROUPS):
        _, _, pick = _first_max(work, i_grp, N_EXPERT_GROUPS)
        gmask = jnp.where(pick, 1.0, gmask)
        work = jnp.where(pick, -jnp.inf, work)
    work = jnp.concatenate([jnp.where(gmask[g:g + 1] > 0.5, sel[g * per:(g + 1) * per], NEG)
                            for g in range(N_EXPERT_GROUPS)], axis=0)
    picks, firsts, vals = [], [], []
    for _ in range(TOP_K):
        _, first, pick = _first_max(work, i_exp, N_EXPERTS)
        picks.append(pick)
        firsts.append(first)
        vals.append(jnp.sum(jnp.where(pick, scores, 0.0), axis=0, keepdims=True))
        work = jnp.where(pick, -jnp.inf, work)
    val = jnp.concatenate(vals, axis=0)
    g_ref[...] = val / jnp.sum(val, axis=0, keepdims=True) * ROUTED_SCALE
    e_ref[...] = jnp.concatenate(firsts, axis=0).astype(jnp.int32)
    onehot = picks[0].astype(F32)
    for pick in picks[1:]:
        onehot = onehot + pick.astype(F32)
    before = jnp.dot(onehot.astype(BF16), tri_ref[...], preferred_element_type=F32) + carry_scr[...]
    r_ref[...] = jnp.concatenate([jnp.sum(jnp.where(pick, before, 0.0), axis=0, keepdims=True)
                                  for pick in picks], axis=0).astype(jnp.int32)
    carry = carry_scr[...] + jnp.sum(onehot, axis=1, keepdims=True)
    carry_scr[...] = carry
    cnt_ref[...] = jnp.broadcast_to(carry, cnt_ref.shape)


def _router(xf, router_w, router_b, tm=512):
    n, d = xf.shape
    tri = (jnp.arange(tm)[:, None] < jnp.arange(tm)[None, :]).astype(BF16)
    col = lambda i: (0, i)
    fixed = lambda i: (0, 0)
    return pl.pallas_call(
        functools.partial(_router_body, tm=tm), grid=(n // tm,),
        in_specs=[pl.BlockSpec((tm, d), lambda i: (i, 0)), pl.BlockSpec((N_EXPERTS, d), fixed),
                  pl.BlockSpec((N_EXPERTS, 1), fixed), pl.BlockSpec((tm, tm), fixed)],
        out_specs=[pl.BlockSpec((TOP_K, tm), col), pl.BlockSpec((TOP_K, tm), col),
                   pl.BlockSpec((TOP_K, tm), col), pl.BlockSpec((N_EXPERTS, LANE), fixed)],
        out_shape=[jax.ShapeDtypeStruct((TOP_K, n), jnp.int32), jax.ShapeDtypeStruct((TOP_K, n), F32),
                   jax.ShapeDtypeStruct((TOP_K, n), jnp.int32), jax.ShapeDtypeStruct((N_EXPERTS, LANE), F32)],
        scratch_shapes=[pltpu.VMEM((N_EXPERTS, 1), F32)],
        compiler_params=_cparams(1), name="router",
    )(xf, router_w.T, router_b.astype(F32)[:, None], tri)


def _pack_rows(x):
    w = x.shape[-1] // 2
    lo = lax.bitcast_convert_type(x[:, :w].astype(BF16).astype(F32), jnp.uint32)
    hi = lax.bitcast_convert_type(x[:, w:].astype(BF16).astype(F32), jnp.uint32)
    return (lo >> 16) | (hi & jnp.uint32(0xFFFF0000))


def _unpack_rows(p):
    lo = lax.bitcast_convert_type(p << 16, F32)
    hi = lax.bitcast_convert_type(p & jnp.uint32(0xFFFF0000), F32)
    return lo, hi


def _dispatch_body(fill_ref, pos_ref, x_ref, z_ref, xs_hbm, xp_scr, sem, *, tm, n_blocks):
    i = pl.program_id(0)
    slot = i % 2
    xp_scr[slot] = _pack_rows(x_ref[...]).reshape(tm // SUBLANES, SUBLANES, -1)
    zsem = sem.at[2]

    def row_copy(s, grp, j, dst):
        return pltpu.make_async_copy(xp_scr.at[s, grp, pl.ds(j, 1)], xs_hbm.at[pl.ds(dst, 1)], sem.at[s])

    def zero_rows(dst, size):
        return pltpu.make_async_copy(z_ref.at[pl.ds(0, size)], xs_hbm.at[pl.ds(dst, size)], zsem)

    def zero_block(blk):
        return pltpu.make_async_copy(z_ref, xs_hbm.at[pl.ds(blk * MOE_BM, MOE_BM)], zsem)

    def issue(grp, c):
        for j in range(SUBLANES):
            for k in range(TOP_K):
                row_copy(slot, grp, j, pos_ref[k, grp * SUBLANES + j]).start(priority=k % 2)
        return c

    def drain(s):
        def body(grp, c):
            for _ in range(SUBLANES * TOP_K):
                row_copy(s, 0, 0, 0).wait()
            return c

        lax.fori_loop(0, tm // SUBLANES, body, 0)

    lax.fori_loop(0, tm // SUBLANES, issue, 0)

    @pl.when(i > 0)
    def _():
        drain(1 - slot)

    @pl.when(i == pl.num_programs(0) - 1)
    def _():
        drain(slot)

    @pl.when(i == 0)
    def _():
        def fill(wait):
            def go(cp):
                if wait:
                    cp.wait()
                else:
                    cp.start()

            def body(e, c):
                lo, cnt = fill_ref[2 * e], fill_ref[2 * e + 1] - fill_ref[2 * e]

                def one_row(r, cc):
                    go(zero_rows(0 if wait else r, 1))
                    return cc

                head = jnp.minimum((-lo) & (SUBLANES - 1), cnt)
                lax.fori_loop(lo, lo + head, one_row, 0)
                base, rem = lo + head, cnt - head
                for bit in reversed(range(SUBLANES.bit_length() - 1, MOE_BM.bit_length() - 1)):
                    size = 1 << bit

                    @pl.when((rem & size) != 0)
                    def _():
                        off = base + ((rem >> (bit + 1)) << (bit + 1))
                        go(zero_rows(0 if wait else pl.multiple_of(off, SUBLANES), size))

                lax.fori_loop(base + (rem & -SUBLANES), base + rem, one_row, 0)
                return c

            lax.fori_loop(0, N_EXPERTS, body, 0)

        fill(False)
        fill(True)

        def tail_start(blk, c):
            zero_block(blk).start()
            return c

        def tail_wait(blk, c):
            zero_block(0).wait()
            return c

        lax.fori_loop(fill_ref[2 * N_EXPERTS], n_blocks, tail_start, 0)
        lax.fori_loop(fill_ref[2 * N_EXPERTS], n_blocks, tail_wait, 0)


def _dispatch(xb, pos, fill, p, tm=256):
    n, d = xb.shape
    grid_spec = pltpu.PrefetchScalarGridSpec(
        num_scalar_prefetch=1, grid=(n // tm,),
        in_specs=[pl.BlockSpec((TOP_K, tm), lambda i, fl: (0, i), memory_space=pltpu.SMEM),
                  pl.BlockSpec((tm, d), lambda i, fl: (i, 0)),
                  pl.BlockSpec((MOE_BM, d // 2), lambda i, fl: (0, 0))],
        out_specs=pl.BlockSpec(memory_space=pl.ANY),
        scratch_shapes=[pltpu.VMEM((2, tm // SUBLANES, SUBLANES, d // 2), jnp.uint32),
                        pltpu.SemaphoreType.DMA((3,))])
    return pl.pallas_call(
        functools.partial(_dispatch_body, tm=tm, n_blocks=p // MOE_BM), grid_spec=grid_spec,
        out_shape=jax.ShapeDtypeStruct((p, d // 2), jnp.uint32),
        compiler_params=_cparams(1), name="dispatch",
    )(fill, pos, xb, jnp.zeros((MOE_BM, d // 2), jnp.uint32))


def _moe_ffn_body(be_ref, nb_ref, x_ref, wg_ref, wu_ref, wd_ref, o_ref, wg_scr, wu_scr, wd_scr):
    i = pl.program_id(0)
    half = x_ref.shape[1]

    @pl.when((i == 0) | (be_ref[i] != be_ref[jnp.maximum(i - 1, 0)]))
    def _():
        wg_scr[...] = wg_ref[0, 0].astype(BF16)
        wu_scr[...] = wu_ref[0, 0].astype(BF16)
        wd_scr[...] = wd_ref[0, 0].astype(BF16)

    @pl.when(i < nb_ref[0])
    def _():
        lo, hi = _unpack_rows(x_ref[...])
        lo, hi = lo.astype(BF16), hi.astype(BF16)
        hg = (jnp.dot(lo, wg_scr[:half], preferred_element_type=F32)
              + jnp.dot(hi, wg_scr[half:], preferred_element_type=F32))
        hu = (jnp.dot(lo, wu_scr[:half], preferred_element_type=F32)
              + jnp.dot(hi, wu_scr[half:], preferred_element_type=F32))
        hb = (hg * jax.nn.sigmoid(hg) * hu).astype(BF16)
        o_ref[...] = _pack_rows(jnp.dot(hb, wd_scr[...], preferred_element_type=F32))

    @pl.when(i >= nb_ref[0])
    def _():
        o_ref[...] = jnp.zeros(o_ref.shape, o_ref.dtype)


def _moe_ffn(xs, blk_e, n_used, wg, wu, wd, layer):
    p, half = xs.shape
    d = 2 * half
    n_blocks = p // MOE_BM
    grid_spec = pltpu.PrefetchScalarGridSpec(
        num_scalar_prefetch=2, grid=(n_blocks,),
        in_specs=[pl.BlockSpec((MOE_BM, half), lambda i, be, nb: (i, 0)),
                  pl.BlockSpec((1, 1, d, D_EXPERT), lambda i, be, nb: (layer, be[i], 0, 0)),
                  pl.BlockSpec((1, 1, d, D_EXPERT), lambda i, be, nb: (layer, be[i], 0, 0)),
                  pl.BlockSpec((1, 1, D_EXPERT, d), lambda i, be, nb: (layer, be[i], 0, 0))],
        out_specs=pl.BlockSpec((MOE_BM, half), lambda i, be, nb: (i, 0)),
        scratch_shapes=[pltpu.VMEM((d, D_EXPERT), BF16), pltpu.VMEM((d, D_EXPERT), BF16),
                        pltpu.VMEM((D_EXPERT, d), BF16)])
    return pl.pallas_call(
        _moe_ffn_body, grid_spec=grid_spec, out_shape=jax.ShapeDtypeStruct((p, half), jnp.uint32),
        compiler_params=_cparams(1), name="expert_ffn",
    )(blk_e, n_used, xs, wg, wu, wd)


def _combine_body(pos_ref, posn_ref, gate_ref, xb_ref, xf_ref, y_hbm, sg_ref, su_ref, sd_ref,
                  g_ref, b_ref, of_ref, ob_ref, buf, sem, *, tm):
    i = pl.program_id(0)
    slot = i % 2

    def row_copy(s, k, grp, j, src):
        return pltpu.make_async_copy(y_hbm.at[pl.ds(src, 1)], buf.at[s, k, grp, pl.ds(j, 1)], sem.at[s])

    def issue_tile(pr, s):
        def issue(grp, c):
            for j in range(SUBLANES):
                for k in range(TOP_K):
                    row_copy(s, k, grp, j, pr[k, grp * SUBLANES + j]).start(priority=k % 2)
            return c

        lax.fori_loop(0, tm // SUBLANES, issue, 0)

    @pl.when(i == 0)
    def _():
        issue_tile(pos_ref, 0)

    @pl.when(i + 1 < pl.num_programs(0))
    def _():
        issue_tile(posn_ref, 1 - slot)

    x = xb_ref[...]
    hg = jnp.dot(x, sg_ref[...], preferred_element_type=F32)
    hu = jnp.dot(x, su_ref[...], preferred_element_type=F32)
    hb = (hg * jax.nn.sigmoid(hg) * hu).astype(BF16)
    y = jnp.dot(hb, sd_ref[...], preferred_element_type=F32)

    def drain(grp, c):
        for _ in range(SUBLANES * TOP_K):
            row_copy(slot, 0, 0, 0, 0).wait()
        return c

    lax.fori_loop(0, tm // SUBLANES, drain, 0)
    gate = gate_ref[...]
    half = buf.shape[-1]
    y_lo, y_hi = y[:, :half], y[:, half:]
    for k in range(TOP_K):
        lo, hi = _unpack_rows(buf[slot, k].reshape(tm, half))
        y_lo = y_lo + gate[:, k:k + 1] * lo
        y_hi = y_hi + gate[:, k:k + 1] * hi
    out = _layer_norm(ALPHA * xf_ref[...] + jnp.concatenate([y_lo, y_hi], axis=-1), g_ref[...], b_ref[...])
    of_ref[...] = out
    ob_ref[...] = out.astype(BF16)


def _combine(pos, gate, xb, xf, y, sg, su, sd, g, b, tm=512):
    n, d = xf.shape
    n_tiles = n // tm
    row = lambda i: (i, 0)
    fixed = lambda i: (0, 0)
    smem_col = pl.BlockSpec((TOP_K, tm), lambda i: (0, i), memory_space=pltpu.SMEM)
    smem_next = pl.BlockSpec((TOP_K, tm), lambda i: (0, jnp.minimum(i + 1, n_tiles - 1)),
                             memory_space=pltpu.SMEM)
    grid_spec = pltpu.PrefetchScalarGridSpec(
        num_scalar_prefetch=0, grid=(n_tiles,),
        in_specs=[smem_col, smem_next, pl.BlockSpec((tm, TOP_K), row),
                  pl.BlockSpec((tm, d), row), pl.BlockSpec((tm, d), row), pl.BlockSpec(memory_space=pl.ANY),
                  pl.BlockSpec(sg.shape, fixed), pl.BlockSpec(su.shape, fixed), pl.BlockSpec(sd.shape, fixed),
                  pl.BlockSpec((1, d), fixed), pl.BlockSpec((1, d), fixed)],
        out_specs=[pl.BlockSpec((tm, d), row), pl.BlockSpec((tm, d), row)],
        scratch_shapes=[pltpu.VMEM((2, TOP_K, tm // SUBLANES, SUBLANES, d // 2), jnp.uint32),
                        pltpu.SemaphoreType.DMA((2,))])
    return pl.pallas_call(
        functools.partial(_combine_body, tm=tm), grid_spec=grid_spec,
        out_shape=[jax.ShapeDtypeStruct((n, d), F32), jax.ShapeDtypeStruct((n, d), BF16)],
        compiler_params=_cparams(1, VMEM_LIMIT), name="combine",
    )(pos, pos, gate, xb, xf, y, sg, su, sd, g, b)


def _moe(xf, xb, router_w, router_b, w_gate, w_up, w_down, sh_gate, sh_up, sh_down, ln_g, ln_b, layer=0):
    n, d = xf.shape
    e_idx, gate, rank, cnt = _router(xf, router_w, router_b)
    counts = cnt[:, 0].astype(jnp.int32)
    padded = (counts + MOE_BM - 1) // MOE_BM * MOE_BM
    pad_end = jnp.cumsum(padded)
    pad_start = (pad_end - padded).astype(jnp.int32)
    n_blocks = (n * TOP_K + N_EXPERTS * (MOE_BM - 1) + MOE_BM - 1) // MOE_BM
    blk_e = jnp.minimum(jnp.sum(pad_end[None, :] <= (jnp.arange(n_blocks) * MOE_BM)[:, None], axis=1),
                        N_EXPERTS - 1).astype(jnp.int32)
    n_used = (pad_end[-1] // MOE_BM).astype(jnp.int32).reshape(1)
    fill = jnp.concatenate([jnp.stack([pad_start + counts, pad_end], axis=1).reshape(-1), n_used]).astype(jnp.int32)
    experts = jnp.arange(N_EXPERTS, dtype=jnp.int32)[:, None, None]
    pos = rank + jnp.sum(jnp.where(e_idx[None] == experts, pad_start[:, None, None], 0), axis=0)
    xs = _dispatch(xb, pos, fill, n_blocks * MOE_BM)
    y = _moe_ffn(xs, blk_e, n_used, w_gate, w_up, w_down, layer)
    return _combine(pos, gate.T, xb, xf, y, sh_gate, sh_up, sh_down, ln_g[None], ln_b[None])


def _rope_tables(seq):
    half = ROPE_DIM // 2
    freqs = ROPE_THETA ** (-jnp.arange(half, dtype=F32) / half)
    ang = jnp.arange(seq).astype(F32)[:, None] * freqs
    cos = jnp.concatenate([jnp.cos(ang)] * 2, -1)
    sin = jnp.concatenate([jnp.sin(ang)] * 2, -1)
    z = lambda w: jnp.zeros((seq, w), F32)
    pad = LANE - NOPE - ROPE_DIM
    cos_q = jnp.concatenate([jnp.ones((seq, NOPE), F32), cos, z(pad)], -1)
    sin_q = jnp.concatenate([z(NOPE), sin, z(pad)], -1)
    cos_k = jnp.concatenate([z(NOPE), cos, z(pad)], -1)
    return cos_q, sin_q, cos_k, sin_q


def _rot_cols(w):
    half = w.shape[-1] // 2
    return jnp.concatenate([-w[..., half:], w[..., :half]], -1)


def _mixer_ab(xb, xf, bsz, seq, w_in, q_norm, w_uq, kv_norm, w_ukv, w_out, ln_g, ln_b, rope_tabs, dil_bias):
    n = bsz * seq
    d = w_in.shape[0]
    c0 = Q_LORA + KV_LORA
    w_kr = w_in[:, c0:c0 + ROPE_DIM]
    zc = lambda w: jnp.zeros((d, w), F32)
    pad = LANE - NOPE - ROPE_DIM
    w1 = jnp.concatenate([w_in[:, :c0], zc(NOPE), w_kr, zc(pad), zc(NOPE), _rot_cols(w_kr), zc(pad)], 1)
    h1 = _mm(xb, w1.astype(BF16), F32, tn=w1.shape[1])
    gw = H_B_GROUP * HD_B
    w_b = w_in[:, c0 + ROPE_DIM:].reshape(d, 3, len(DIL_PAIRS), gw).transpose(0, 2, 1, 3).reshape(d, -1)
    h2 = _proj_dilated(xb, w_b.astype(BF16), bsz, seq)

    wq = w_uq.reshape(Q_LORA, H_A, NOPE + ROPE_DIM)
    zq = jnp.zeros((Q_LORA, H_A, pad), F32)
    wq_main = jnp.concatenate([wq, zq], -1).reshape(Q_LORA, H_A * LANE)
    wq_rot = jnp.concatenate([jnp.zeros((Q_LORA, H_A, NOPE), F32), _rot_cols(wq[..., NOPE:]), zq], -1)
    wq_rot = wq_rot.reshape(Q_LORA, H_A * LANE)
    wkv = w_ukv.reshape(KV_LORA, H_A, NOPE + MLA_V)
    wk = jnp.concatenate([wkv[..., :NOPE], jnp.zeros((KV_LORA, H_A, LANE - NOPE), F32)], -1)
    wk = wk.reshape(KV_LORA, H_A * LANE)
    wv = wkv[..., NOPE:].reshape(KV_LORA, H_A * MLA_V)
    q_a, k_a, v_a = _mla_up(h1, q_norm[None], wq_main.astype(BF16), wq_rot.astype(BF16), kv_norm[None],
                            wk.astype(BF16), wv.astype(BF16), *rope_tabs, seq)
    o_a = _flash(q_a.reshape(bsz, seq, -1), k_a.reshape(bsz, seq, -1), v_a.reshape(bsz, seq, -1), name="mla_attn",
                 n_outer=H_A // 2, T=min(ATT_TQ, seq), TK=ATT_TK, unroll2=True, dq=LANE, dv=MLA_V,
                 q_col=lambda g: g, k_col=lambda g: g, v_col=lambda g: g,
                 q_offs=(0, LANE), k_offs=(0, LANE), v_offs=(0, MLA_V),
                 scale=(NOPE + ROPE_DIM) ** -0.5, out_cols=H_A * MLA_V, out_col=lambda g: g)

    gb = gw // LANE
    outs, lses = [], []
    for gi, (window, dil) in enumerate(DIL_PAIRS):
        L = seq // dil
        t = h2[gi]
        o, lse = _flash(t, t, t, name="dilated_attn", n_outer=2, T=min(ATT_TQ, L), TK=min(ATT_TK, L),
                        dq=HD_B, dv=HD_B,
                        q_col=lambda g: g, k_col=lambda g: gb + g, v_col=lambda g: 2 * gb + g,
                        q_offs=(0, HD_B), k_offs=(0, HD_B), v_offs=(0, HD_B), scale=HD_B ** -0.5,
                        out_cols=dil * gw, out_col=lambda g: g, bias=dil_bias[gi], bias_mode="pair",
                        bias_idx=lambda g: g, want_lse=True, nback=-(-(window // dil) // min(ATT_TK, L)),
                        rep=dil, rep_in=3 * gb, rep_out=gb)
        outs.append(o.reshape(bsz, seq, H_B_GROUP, HD_B).astype(F32))
        lses.append(lse.reshape(bsz, dil, H_B_GROUP, L).transpose(0, 3, 1, 2).reshape(bsz, seq, H_B_GROUP))
    w = jax.nn.softmax(jnp.stack(lses), axis=0)
    o_b = jnp.sum(w[..., None] * jnp.stack(outs), axis=0).astype(BF16).reshape(n, gw)
    na = H_A * MLA_V
    return _out_ln([o_a.reshape(n, na)], o_b, w_out[:na].astype(BF16), w_out[na:].astype(BF16),
                   xf, ln_g[None], ln_b[None])


def _mixer_cd(xb, xf, bsz, seq, w_in, pos_k, k_w1, k_w2, pos_v, v_w1, v_w2, lq1, lk1, lq2, lk2, d_norm,
              w_out, ln_g, ln_b, lam_init, tabs):
    n = bsz * seq
    qc_w = H_C * DK_C
    kv_w = G_C * DK_C
    off = qc_w
    kvs = []
    for _ in range(3):
        wk_ = w_in[:, off:off + kv_w].reshape(-1, G_C, DK_C)
        wv_ = w_in[:, off + kv_w:off + 2 * kv_w].reshape(-1, G_C, DK_C)
        kvs.append(jnp.concatenate([wk_, wv_], -1).reshape(-1, 2 * kv_w))
        off += 2 * kv_w
    g_off = off
    d_off = off + 3 * H_C
    w_main = jnp.concatenate([w_in[:, :qc_w]] + kvs + [w_in[:, d_off:]], 1)
    h = _mm(xb, w_main.astype(BF16), BF16, tn=w_main.shape[1] // 2).reshape(bsz, seq, -1)
    w_g = jnp.repeat(w_in[:, g_off:d_off], DK_C, axis=1)
    gates = _mm(xb, w_g.astype(BF16), F32, tn=w_g.shape[1] // 2, act="sigmoid").reshape(bsz, seq, -1)

    ncp = seq // CMP_STRIDE
    half = CMP_STRIDE * DK_C
    kv_cmp = h[:, :, qc_w:qc_w + 2 * kv_w].reshape(bsz, ncp, CMP_STRIDE, G_C, 2, DK_C)
    u = kv_cmp.transpose(0, 3, 4, 1, 2, 5).reshape(bsz, G_C, 2, ncp, half)
    pe = jnp.stack([pos_k.reshape(2, 1, half), pos_v.reshape(2, 1, half)])
    w1 = jnp.stack([k_w1.reshape(2, half, CMP_HID), v_w1.reshape(2, half, CMP_HID)])
    w2 = jnp.stack([k_w2, v_w2])
    kvc = _compress(u, pe, w1, w2)
    o_cmp, sel = _cmp_attn(h, kvc, tabs["bias_c"], tabs["overlap"], gates)

    cb = qc_w // LANE
    scale = DK_C ** -0.5
    n_pairs = H_C // 2
    nsa = dict(n_outer=n_pairs, dq=DK_C, dv=DK_C, q_col=lambda g: g, q_offs=(0, DK_C), k_offs=(0, 0),
               v_offs=(DK_C, DK_C), scale=scale, out_cols=qc_w, out_col=lambda g: g, bias_mode="pair",
               bias_idx=lambda g: g, gate=gates, vmem=VMEM_LIMIT, k_w=LANE, v_w=LANE)
    o_sel = _flash(h, h, h, name="nsa_sel_attn", T=ATT_TQ, TK=ATT_TK, k_col=lambda g: cb + 2 + g // 2, v_col=lambda g: cb + 2 + g // 2,
                   bias=tabs["bias_sel"], sel=sel, sel_idx=lambda g: g // 2,
                   gate_col=lambda g: n_pairs + g, **nsa)
    o_win = _flash(h, h, h, name="nsa_win_attn", T=ATT_TQ, TK=ATT_TK, k_col=lambda g: cb + 4 + g // 2,
                   v_col=lambda g: cb + 4 + g // 2, bias=tabs["bias_win"], gate_col=lambda g: 2 * n_pairs + g,
                   nback=-(-(WIN - 1) // ATT_TK), **nsa)

    lam = (jnp.exp(jnp.sum(lq1.astype(F32) * lk1.astype(F32)))
           - jnp.exp(jnp.sum(lq2.astype(F32) * lk2.astype(F32))) + lam_init).reshape(1, 1)
    db = cb + 6
    o_d = _flash(h, h, h, name="diff_attn", n_outer=H_D, T=ATT_TQ, TK=ATT_TK, dq=DD, dv=2 * DD,
                 q_col=lambda g: db + g, k_col=lambda g: db + H_D + g, v_col=lambda g: db + 2 * H_D + g,
                 q_offs=(0, DD), k_offs=(0, DD), v_offs=(0, 0), scale=DD ** -0.5,
                 out_cols=H_D * 2 * DD, out_col=lambda g: g, bias=tabs["bias_d"], bias_mode="shared",
                 bias_idx=lambda g: g, lam=lam, dnorm=d_norm[:, None], lam_init=lam_init, epilogue="diff",
                 vmem=VMEM_LIMIT)
    r2 = lambda a: a.reshape(n, -1)
    return _out_ln([r2(o_cmp), r2(o_sel), r2(o_win)], r2(o_d), w_out[:qc_w].astype(BF16),
                   w_out[qc_w:].astype(BF16), xf, ln_g[None], ln_b[None])


def _nsa_tables(rel_bias, seq):
    tab_c = rel_bias[:, H_B:H_B + H_C]
    tab_d = rel_bias[:, H_B + H_C:H_B + H_C + H_D]
    ncp = seq // CMP_STRIDE
    n_sel = seq // SEL_BLOCK
    x = jnp.arange(2 * ncp)
    c_minus_a = jnp.where(x < ncp, x, x - 2 * ncp)
    dist = -CMP_STRIDE * c_minus_a[None, :] + jnp.arange(CMP_STRIDE)[:, None] - (CMP_LEN - 1)
    w = jnp.moveaxis(tab_c[_rel_bucket(dist)].astype(F32), -1, 0)
    bias_c = _toeplitz(w, ncp, ncp).transpose(0, 2, 1, 3).reshape(H_C, seq, ncp)
    c0 = jnp.arange(ncp) * CMP_STRIDE
    s0 = jnp.arange(n_sel) * SEL_BLOCK
    overlap = jnp.maximum(jnp.minimum(c0[:, None] + CMP_LEN, s0[None, :] + SEL_BLOCK)
                          - jnp.maximum(c0[:, None], s0[None, :]), 0).astype(F32) / CMP_LEN
    return {
        "bias_c": bias_c, "overlap": overlap.T,
        "bias_sel": _toeplitz_bias(tab_c, ATT_TQ, seq // ATT_TK, 1, seq, ATT_TK),
        "bias_win": _toeplitz_bias(tab_c, ATT_TQ, ATT_TQ // ATT_TK + -(-(WIN - 1) // ATT_TK), 1, WIN - 1, ATT_TK),
        "bias_d": _toeplitz_bias(tab_d, ATT_TQ, seq // ATT_TK, 1, seq, ATT_TK),
    }


def kernel(x, rel_bias, ab_w_in, mla_q_norm, mla_w_uq, mla_kv_norm, mla_w_ukv, ab_w_out, cd_w_in, nsa_cmp_pos_k, nsa_cmp_k_w1, nsa_cmp_k_w2, nsa_cmp_pos_v, nsa_cmp_v_w1, nsa_cmp_v_w2, diff_lambda_q1, diff_lambda_k1, diff_lambda_q2, diff_lambda_k2, diff_norm, cd_w_out, ln1_g, ln1_b, ln2_g, ln2_b, router_w, router_b, exp_w_gate, exp_w_up, exp_w_down, sh_w_gate, sh_w_up, sh_w_down):
    bsz, seq, d = x.shape
    n = bsz * seq
    depth = ln1_g.shape[0]
    rope_tabs = _rope_tables(seq)
    dil_bias = []
    for gi, (window, dil) in enumerate(DIL_PAIRS):
        tq, tk = min(ATT_TQ, seq // dil), min(ATT_TK, seq // dil)
        dil_bias.append(_toeplitz_bias(rel_bias[:, gi * H_B_GROUP:(gi + 1) * H_B_GROUP], tq,
                                       tq // tk + -(-(window // dil) // tk), dil, window // dil, tk))
    nsa_tabs = _nsa_tables(rel_bias, seq)
    xf = x.reshape(n, d)
    xb = xf.astype(BF16)
    for l in range(depth):
        i = l // 2
        if l % 2 == 0:
            xf, xb = _mixer_ab(xb, xf, bsz, seq, ab_w_in[i], mla_q_norm[i], mla_w_uq[i], mla_kv_norm[i],
                               mla_w_ukv[i], ab_w_out[i], ln1_g[l], ln1_b[l], rope_tabs, dil_bias)
        else:
            lam_init = 0.8 - 0.6 * math.exp(-0.3 * l)
            xf, xb = _mixer_cd(xb, xf, bsz, seq, cd_w_in[i], nsa_cmp_pos_k[i], nsa_cmp_k_w1[i],
                               nsa_cmp_k_w2[i], nsa_cmp_pos_v[i], nsa_cmp_v_w1[i], nsa_cmp_v_w2[i],
                               diff_lambda_q1[i], diff_lambda_k1[i], diff_lambda_q2[i], diff_lambda_k2[i],
                               diff_norm[i], cd_w_out[i], ln1_g[l], ln1_b[l], lam_init, nsa_tabs)
        xf, xb = _moe(xf, xb, router_w[l], router_b[l], exp_w_gate, exp_w_up, exp_w_down,
                      sh_w_gate[l].astype(BF16), sh_w_up[l].astype(BF16), sh_w_down[l].astype(BF16),
                      ln2_g[l], ln2_b[l], layer=l)
    return xf.reshape(bsz, seq, d)
```
